```python
import math
import jax, jax.numpy as jnp
from jax import lax
import numpy as np

D_MODEL = 1024
BATCH = 8
SEQ = 4096
DEPTH = 1

POOL_WIDTH = D_MODEL // 2
POOL_WINDOWS = (2, 4, 8, 16)
POOL_GROUPS = len(POOL_WINDOWS)
POOL_GROUP_DIM = POOL_WIDTH // POOL_GROUPS
N_HEADS = 8
N_KV_GROUPS = 2
HEADS_PER_GROUP = N_HEADS // N_KV_GROUPS
HEAD_DIM = 64
Q_WIDTH = N_HEADS * HEAD_DIM
KV_WIDTH = N_KV_GROUPS * HEAD_DIM
CMP_BLOCK = 32
CMP_STRIDE = 16
CMP_HIDDEN = 128
SEL_BLOCK = 64
SEL_TOPK = 16
WINDOW = 512
Q_CHUNK = 64
FORCE_SCORE = 1000.0
N_BUCKETS = 32
MAX_DISTANCE = 128
PEER_HEADS = 8
PEER_KEYS = 128
PEER_EXPERTS = PEER_KEYS * PEER_KEYS
PEER_TOPK = 16
PEER_QDIM = 256
PEER_CHUNK = 128
IN_SIZES = (POOL_WIDTH, Q_WIDTH, KV_WIDTH, KV_WIDTH, KV_WIDTH, KV_WIDTH, KV_WIDTH, KV_WIDTH, N_HEADS * 3, 2 * D_MODEL)
IN_COLS = sum(IN_SIZES)
EPS = 1e-6
NEG_INF = -1e30

kernel_name = 'hybrid_pool_nsa_peer_block'


def rms_norm(x, g):
    xf = x.astype(jnp.float32)
    y = xf * lax.rsqrt(jnp.mean(xf * xf, axis=-1, keepdims=True) + EPS)
    return (y * g.astype(jnp.float32)).astype(x.dtype)


def masked_softmax(logits, valid):
    lf = jnp.where(valid, logits.astype(jnp.float32), NEG_INF)
    p = jax.nn.softmax(lf, axis=-1)
    return jnp.where(valid, p, 0.0)


def t5_bucket(rel):
    n = jnp.maximum(rel, 0)
    max_exact = N_BUCKETS // 2
    nf = jnp.maximum(n, 1).astype(jnp.float32)
    large = max_exact + (jnp.log(nf / max_exact) / math.log(MAX_DISTANCE / max_exact) * (N_BUCKETS - max_exact)).astype(jnp.int32)
    large = jnp.minimum(large, N_BUCKETS - 1)
    return jnp.where(n < max_exact, n, large)


def rel_bias_dense(table, rel):
    b = table[t5_bucket(rel)]
    return jnp.transpose(b, (2, 0, 1)).reshape(N_KV_GROUPS, HEADS_PER_GROUP, *rel.shape)


def sel_mapping(n_cmp, n_sel):
    m = np.zeros((n_cmp, n_sel), np.float32)
    pos = np.arange(n_cmp)[:, None] * CMP_STRIDE + np.arange(CMP_BLOCK)[None, :]
    np.add.at(m, (np.repeat(np.arange(n_cmp), CMP_BLOCK), (pos // SEL_BLOCK).ravel()), 1.0 / CMP_BLOCK)
    return jnp.asarray(m)


def pool_mixer(u, w_pool, pool_scale):
    b_, s_, c_ = u.shape
    uf = u.astype(jnp.float32)
    csp = jnp.concatenate([jnp.zeros((b_, 1, c_), jnp.float32), jnp.cumsum(uf, axis=1)], axis=1)
    t = jnp.arange(s_)
    outs = []
    for gi, w in enumerate(POOL_WINDOWS):
        sl = slice(gi * POOL_GROUP_DIM, (gi + 1) * POOL_GROUP_DIM)
        cg = csp[:, :, sl]
        upper = cg[:, 1:]
        lower = jnp.pad(cg[:, :s_ + 1 - w], ((0, 0), (w - 1, 0), (0, 0)))
        cnt = jnp.minimum(t + 1, w).astype(jnp.float32)[:, None]
        outs.append((upper - lower) / cnt - uf[:, :, sl])
    pooled = jnp.stack(outs, axis=2).astype(u.dtype)
    y = jnp.einsum('bsgc,gcd->bsgd', pooled, w_pool).reshape(b_, s_, POOL_WIDTH)
    return y * pool_scale


def compress_blocks(t, pe, w1, w2):
    b_, g_, s_, dh = t.shape
    ratio = CMP_BLOCK // CMP_STRIDE
    n_str = s_ // CMP_STRIDE
    n_cmp = n_str - ratio + 1
    pieces = t.reshape(b_, g_, n_str, CMP_STRIDE, dh)
    blk = jnp.concatenate([pieces[:, :, r:r + n_cmp] for r in range(ratio)], axis=3)
    blk = (blk + pe).reshape(b_, g_, n_cmp, CMP_BLOCK * dh)
    return jax.nn.gelu(blk @ w1) @ w2


def nsa_attention(q, kc, vc, ks, vs, kw, vw, gates, rel_bias):
    b_, g_, j_, s_, dh = q.shape
    scale = dh ** -0.5
    n_cmp = kc.shape[2]
    n_sel = s_ // SEL_BLOCK
    k_top = min(SEL_TOPK, n_sel)
    cmp_end = jnp.arange(n_cmp) * CMP_STRIDE + CMP_BLOCK - 1
    smap = sel_mapping(n_cmp, n_sel)
    ksb = ks.reshape(b_, g_, n_sel, SEL_BLOCK, dh)
    vsb = vs.reshape(b_, g_, n_sel, SEL_BLOCK, dh)
    kw_pad = jnp.pad(kw, ((0, 0), (0, 0), (WINDOW, 0), (0, 0)))
    vw_pad = jnp.pad(vw, ((0, 0), (0, 0), (WINDOW, 0), (0, 0)))
    tbl_g = rel_bias.reshape(N_BUCKETS, g_, j_)
    bi = jnp.arange(b_)[:, None, None, None]
    gi = jnp.arange(g_)[None, :, None, None]
    blk_id = jnp.arange(n_sel)

    def chunk(ci):
        s0 = ci * Q_CHUNK
        qc = lax.dynamic_slice_in_dim(q, s0, Q_CHUNK, axis=3)
        gc = lax.dynamic_slice_in_dim(gates, s0, Q_CHUNK, axis=3)
        tpos = s0 + jnp.arange(Q_CHUNK)
        rel_c = tpos[:, None] - cmp_end[None, :]
        lg_c = jnp.einsum('bgjqd,bgnd->bgjqn', qc, kc) * scale + rel_bias_dense(rel_bias, rel_c)
        p_c = masked_softmax(lg_c, rel_c >= 0)
        o_cmp = jnp.einsum('bgjqn,bgnd->bgjqd', p_c.astype(vc.dtype), vc)
        imp = jnp.einsum('bgjqn,ns->bgqs', p_c, smap)
        cur = tpos // SEL_BLOCK
        forced = (blk_id[None, :] == 0) | (blk_id[None, :] == cur[:, None]) | (blk_id[None, :] == cur[:, None] - 1)
        visible = blk_id[None, :] * SEL_BLOCK <= tpos[:, None]
        score = jnp.where(visible, imp + jnp.where(forced, FORCE_SCORE, 0.0), -1.0)
        _, idx = lax.top_k(score, k_top)
        k_g = ksb[bi, gi, idx].reshape(b_, g_, Q_CHUNK, k_top * SEL_BLOCK, dh)
        v_g = vsb[bi, gi, idx].reshape(b_, g_, Q_CHUNK, k_top * SEL_BLOCK, dh)
        kpos = (idx[..., None] * SEL_BLOCK + jnp.arange(SEL_BLOCK)).reshape(b_, g_, Q_CHUNK, k_top * SEL_BLOCK)
        rel_s = tpos[None, None, :, None] - kpos
        bias_s = jnp.moveaxis(tbl_g[t5_bucket(rel_s), gi], -1, 2)
        lg_s = jnp.einsum('bgjqd,bgqkd->bgjqk', qc, k_g) * scale + bias_s
        p_s = masked_softmax(lg_s, (rel_s >= 0)[:, :, None])
        o_sel = jnp.einsum('bgjqk,bgqkd->bgjqd', p_s.astype(v_g.dtype), v_g)
        kwc = lax.dynamic_slice_in_dim(kw_pad, s0, WINDOW + Q_CHUNK, axis=2)
        vwc = lax.dynamic_slice_in_dim(vw_pad, s0, WINDOW + Q_CHUNK, axis=2)
        kpos_w = s0 - WINDOW + jnp.arange(WINDOW + Q_CHUNK)
        rel_w = tpos[:, None] - kpos_w[None, :]
        valid_w = (rel_w >= 0) & (rel_w < WINDOW) & (kpos_w >= 0)[None, :]
        lg_w = jnp.einsum('bgjqd,bgkd->bgjqk', qc, kwc) * scale + rel_bias_dense(rel_bias, rel_w)
        p_w = masked_softmax(lg_w, valid_w)
        o_win = jnp.einsum('bgjqk,bgkd->bgjqd', p_w.astype(vwc.dtype), vwc)
        out = gc[..., 0:1] * o_cmp + gc[..., 1:2] * o_sel + gc[..., 2:3] * o_win
        return out.astype(q.dtype)

    outs = lax.map(chunk, jnp.arange(s_ // Q_CHUNK))
    return jnp.transpose(outs, (1, 0, 4, 2, 3, 5)).reshape(b_, s_, g_ * j_ * dh)


def peer_ffn(h, w_q, sub_keys, u, v):
    b_, s_, d_ = h.shape
    tokens = h.reshape(-1, PEER_CHUNK, d_)

    def chunk(xc):
        qv = (xc @ w_q).reshape(PEER_CHUNK, PEER_HEADS, 2, PEER_QDIM // 2)
        sc = jnp.einsum('chpd,hpkd->chpk', qv, sub_keys).astype(jnp.float32)
        s_half, i_half = lax.top_k(sc, PEER_TOPK)
        cand = (s_half[:, :, 0, :, None] + s_half[:, :, 1, None, :]).reshape(PEER_CHUNK, PEER_HEADS, PEER_TOPK * PEER_TOPK)
        cid = (i_half[:, :, 0, :, None] * PEER_KEYS + i_half[:, :, 1, None, :]).reshape(PEER_CHUNK, PEER_HEADS, PEER_TOPK * PEER_TOPK)
        top_s, top_j = lax.top_k(cand, PEER_TOPK)
        eid = jnp.take_along_axis(cid, top_j, axis=-1)
        g = jax.nn.softmax(top_s, axis=-1).astype(xc.dtype)
        ue = u[eid]
        ve = v[eid]
        a = jax.nn.gelu(jnp.einsum('cd,chkd->chk', xc, ue))
        return jnp.einsum('chk,chkd->cd', g * a, ve)

    return lax.map(chunk, tokens).reshape(b_, s_, d_)


def setup_inputs(seed: int = 0) -> dict:
    key = jax.random.key(seed)
    ks = jax.random.split(key, 25)
    f32 = jnp.float32
    L = DEPTH
    D = D_MODEL

    def nrm(k, shape, s):
        return jax.random.normal(k, shape, f32) * s

    return {
        'x': nrm(ks[0], (BATCH, SEQ, D), 1.0),
        'c': nrm(ks[1], (BATCH, D), 1.0),
        'rel_bias': nrm(ks[2], (N_BUCKETS, N_HEADS), 0.5),
        'ada_w': nrm(ks[3], (L, D, 6 * D), 0.5 * D ** -0.5),
        'ada_b': nrm(ks[4], (L, 6 * D), 0.01),
        'norm1_g': 1.0 + nrm(ks[5], (L, D), 0.05),
        'norm2_g': 1.0 + nrm(ks[6], (L, D), 0.05),
        'w_in': nrm(ks[7], (L, D, IN_COLS), D ** -0.5),
        'pool_w': nrm(ks[8], (L, POOL_GROUPS, POOL_GROUP_DIM, POOL_GROUP_DIM), POOL_GROUP_DIM ** -0.5),
        'pool_scale': 1.0 + nrm(ks[9], (L, POOL_WIDTH), 0.05),
        'cmp_pe_k': nrm(ks[10], (L, CMP_BLOCK, HEAD_DIM), 0.5),
        'cmp_w1_k': nrm(ks[11], (L, CMP_BLOCK * HEAD_DIM, CMP_HIDDEN), (CMP_BLOCK * HEAD_DIM) ** -0.5),
        'cmp_w2_k': nrm(ks[12], (L, CMP_HIDDEN, HEAD_DIM), CMP_HIDDEN ** -0.5),
        'cmp_pe_v': nrm(ks[13], (L, CMP_BLOCK, HEAD_DIM), 0.5),
        'cmp_w1_v': nrm(ks[14], (L, CMP_BLOCK * HEAD_DIM, CMP_HIDDEN), (CMP_BLOCK * HEAD_DIM) ** -0.5),
        'cmp_w2_v': nrm(ks[15], (L, CMP_HIDDEN, HEAD_DIM), CMP_HIDDEN ** -0.5),
        'q_norm_g': 1.0 + nrm(ks[16], (L, HEAD_DIM), 0.05),
        'k_norm_g': 1.0 + nrm(ks[17], (L, 3, HEAD_DIM), 0.05),
        'w_branch_pool': nrm(ks[18], (L, POOL_WIDTH, D), POOL_WIDTH ** -0.5),
        'w_branch_attn': nrm(ks[19], (L, Q_WIDTH, D), Q_WIDTH ** -0.5),
        'w_out': nrm(ks[20], (L, D, D), D ** -0.5),
        'peer_w_q': nrm(ks[21], (L, D, PEER_HEADS * PEER_QDIM), D ** -0.5),
        'peer_sub_keys': nrm(ks[22], (L, PEER_HEADS, 2, PEER_KEYS, PEER_QDIM // 2), (PEER_QDIM // 2) ** -0.5),
        'peer_u': nrm(ks[23], (L, PEER_EXPERTS, D), D ** -0.5),
        'peer_v': nrm(ks[24], (L, PEER_EXPERTS, D), PEER_HEADS ** -0.5),
    }


def reference(x, c, rel_bias, ada_w, ada_b, norm1_g, norm2_g, w_in, pool_w, pool_scale, cmp_pe_k, cmp_w1_k, cmp_w2_k, cmp_pe_v, cmp_w1_v, cmp_w2_v, q_norm_g, k_norm_g, w_branch_pool, w_branch_attn, w_out, peer_w_q, peer_sub_keys, peer_u, peer_v):
    b_, s_, d_ = x.shape
    split_at = np.cumsum(IN_SIZES)[:-1].tolist()

    def to_groups(t):
        return jnp.transpose(t.reshape(b_, s_, N_KV_GROUPS, HEAD_DIM), (0, 2, 1, 3))

    for l in range(DEPTH):
        ada = (jax.nn.silu(c) @ ada_w[l] + ada_b[l]).reshape(b_, 6, 1, d_)
        shift1, scale1, gate1 = ada[:, 0], ada[:, 1], ada[:, 2]
        shift2, scale2, gate2 = ada[:, 3], ada[:, 4], ada[:, 5]
        h = rms_norm(x, norm1_g[l]) * (1.0 + scale1) + shift1
        z = h @ w_in[l]
        z_pool, z_q, z_kc, z_vc, z_ks, z_vs, z_kw, z_vw, z_gate, z_merge = jnp.split(z, split_at, axis=-1)
        y_pool = pool_mixer(z_pool, pool_w[l], pool_scale[l])
        q = rms_norm(z_q.reshape(b_, s_, N_KV_GROUPS, HEADS_PER_GROUP, HEAD_DIM), q_norm_g[l])
        q = jnp.transpose(q, (0, 2, 3, 1, 4))
        kc = rms_norm(compress_blocks(to_groups(z_kc), cmp_pe_k[l], cmp_w1_k[l], cmp_w2_k[l]), k_norm_g[l, 0])
        vc = compress_blocks(to_groups(z_vc), cmp_pe_v[l], cmp_w1_v[l], cmp_w2_v[l])
        ksl = rms_norm(to_groups(z_ks), k_norm_g[l, 1])
        vsl = to_groups(z_vs)
        kwn = rms_norm(to_groups(z_kw), k_norm_g[l, 2])
        vwn = to_groups(z_vw)
        gates = jnp.transpose(jax.nn.sigmoid(z_gate.reshape(b_, s_, N_KV_GROUPS, HEADS_PER_GROUP, 3)), (0, 2, 3, 1, 4))
        y_attn = nsa_attention(q, kc, vc, ksl, vsl, kwn, vwn, gates, rel_bias)
        g_merge = jax.nn.sigmoid(z_merge).reshape(b_, s_, 2, d_)
        mixed = g_merge[:, :, 0] * (y_pool @ w_branch_pool[l]) + g_merge[:, :, 1] * (y_attn @ w_branch_attn[l])
        x = x + gate1 * (mixed @ w_out[l])
        h2 = rms_norm(x, norm2_g[l]) * (1.0 + scale2) + shift2
        x = x + gate2 * peer_ffn(h2, peer_w_q[l], peer_sub_keys[l], peer_u[l], peer_v[l])
    return x
```

```python
import math
from functools import partial

import jax
import jax.numpy as jnp
import numpy as np
from jax import lax
from jax.experimental import pallas as pl
from jax.experimental.pallas import tpu as pltpu

D_MODEL = 1024
POOL_WIDTH = 512
POOL_WINDOWS = (2, 4, 8, 16)
POOL_GROUP_DIM = 128
N_HEADS = 8
N_KV_GROUPS = 2
HEADS_PER_GROUP = 4
HEAD_DIM = 64
Q_WIDTH = 512
KV_WIDTH = 128
CMP_BLOCK = 32
CMP_STRIDE = 16
SEL_BLOCK = 64
SEL_TOPK = 16
WINDOW = 512
Q_CHUNK = 64
FORCE_SCORE = 1000.0
N_BUCKETS = 32
MAX_DISTANCE = 128
PEER_HEADS = 8
PEER_KEYS = 128
PEER_TOPK = 16
PEER_QDIM = 256
PEER_CHUNK = 128
GATE_COLS = N_HEADS * 3
EPS = 1e-6
NEG_INF = -1e30

VMEM_LIMIT = 48 * 1024 * 1024
ROW_TILE = 512


def _norm_proj_kernel(x_ref, g_ref, sc_ref, sh_ref, w_ref, h_ref, z_ref):
    x = x_ref[0]
    y = x * lax.rsqrt(jnp.mean(x * x, axis=-1, keepdims=True) + EPS)
    h = y * g_ref[...] * (1.0 + sc_ref[0]) + sh_ref[0]
    h_ref[0] = h
    z_ref[0] = jnp.dot(h.astype(jnp.bfloat16), w_ref[...], preferred_element_type=jnp.float32)


def _norm_proj(x, g, scale, shift, w_bf16):
    b_, s_, d_ = x.shape
    n = w_bf16.shape[1]
    return pl.pallas_call(
        _norm_proj_kernel,
        grid=(b_, s_ // ROW_TILE),
        in_specs=[
            pl.BlockSpec((1, ROW_TILE, d_), lambda b, i: (b, i, 0)),
            pl.BlockSpec((1, d_), lambda b, i: (0, 0)),
            pl.BlockSpec((1, 1, d_), lambda b, i: (b, 0, 0)),
            pl.BlockSpec((1, 1, d_), lambda b, i: (b, 0, 0)),
            pl.BlockSpec((d_, n), lambda b, i: (0, 0)),
        ],
        out_specs=[
            pl.BlockSpec((1, ROW_TILE, d_), lambda b, i: (b, i, 0)),
            pl.BlockSpec((1, ROW_TILE, n), lambda b, i: (b, i, 0)),
        ],
        out_shape=[
            jax.ShapeDtypeStruct((b_, s_, d_), jnp.float32),
            jax.ShapeDtypeStruct((b_, s_, n), jnp.float32),
        ],
        compiler_params=pltpu.CompilerParams(
            dimension_semantics=("parallel", "parallel"), vmem_limit_bytes=VMEM_LIMIT),
    )(x, g.reshape(1, d_), scale, shift, w_bf16)


def _merge_kernel(x_ref, yp_ref, ya_ref, zm_ref, g1_ref, wbp_ref, wba_ref, wo_ref, o_ref):
    d_ = x_ref.shape[-1]
    bp = jnp.dot(yp_ref[0].astype(jnp.bfloat16), wbp_ref[...], preferred_element_type=jnp.float32)
    ba = jnp.dot(ya_ref[0].astype(jnp.bfloat16), wba_ref[...], preferred_element_type=jnp.float32)
    zm = zm_ref[0]
    mixed = jax.nn.sigmoid(zm[:, :d_]) * bp + jax.nn.sigmoid(zm[:, d_:]) * ba
    proj = jnp.dot(mixed.astype(jnp.bfloat16), wo_ref[...], preferred_element_type=jnp.float32)
    o_ref[0] = x_ref[0] + g1_ref[0] * proj


def _merge(x, y_pool, y_attn, z_merge, gate1, wbp, wba, wo):
    b_, s_, d_ = x.shape
    row = lambda w: pl.BlockSpec((1, ROW_TILE, w), lambda b, i: (b, i, 0))
    full = lambda a: pl.BlockSpec(a.shape, lambda b, i: (0, 0))
    return pl.pallas_call(
        _merge_kernel,
        grid=(b_, s_ // ROW_TILE),
        in_specs=[row(d_), row(POOL_WIDTH), row(Q_WIDTH), row(2 * d_),
                  pl.BlockSpec((1, 1, d_), lambda b, i: (b, 0, 0)),
                  full(wbp), full(wba), full(wo)],
        out_specs=row(d_),
        out_shape=jax.ShapeDtypeStruct((b_, s_, d_), jnp.float32),
        compiler_params=pltpu.CompilerParams(
            dimension_semantics=("parallel", "parallel"), vmem_limit_bytes=VMEM_LIMIT),
    )(x, y_pool, y_attn, z_merge, gate1, wbp, wba, wo)


def _rms(x, g):
    return x * lax.rsqrt(jnp.mean(x * x, axis=-1, keepdims=True) + EPS) * g


def _masked_softmax(logits, valid):
    lf = jnp.where(valid, logits, NEG_INF)
    p = jax.nn.softmax(lf, axis=-1)
    return jnp.where(valid, p, 0.0)


def _t5_bucket(rel):
    n = jnp.maximum(rel, 0)
    max_exact = N_BUCKETS // 2
    nf = jnp.maximum(n, 1).astype(jnp.float32)
    large = max_exact + (jnp.log(nf / max_exact) / math.log(MAX_DISTANCE / max_exact) * (N_BUCKETS - max_exact)).astype(jnp.int32)
    large = jnp.minimum(large, N_BUCKETS - 1)
    return jnp.where(n < max_exact, n, large)


def _rel_bias_dense(table, rel):
    b = table[_t5_bucket(rel)]
    return jnp.transpose(b, (2, 0, 1)).reshape(N_KV_GROUPS, HEADS_PER_GROUP, *rel.shape)


def _sel_mapping(n_cmp, n_sel):
    m = np.zeros((n_cmp, n_sel), np.float32)
    pos = np.arange(n_cmp)[:, None] * CMP_STRIDE + np.arange(CMP_BLOCK)[None, :]
    np.add.at(m, (np.repeat(np.arange(n_cmp), CMP_BLOCK), (pos // SEL_BLOCK).ravel()), 1.0 / CMP_BLOCK)
    return jnp.asarray(m)


def _pool_mixer(u, w_pool, pool_scale):
    b_, s_, c_ = u.shape
    csp = jnp.concatenate([jnp.zeros((b_, 1, c_), jnp.float32), jnp.cumsum(u, axis=1)], axis=1)
    t = jnp.arange(s_)
    outs = []
    for gi, w in enumerate(POOL_WINDOWS):
        sl = slice(gi * POOL_GROUP_DIM, (gi + 1) * POOL_GROUP_DIM)
        cg = csp[:, :, sl]
        upper = cg[:, 1:]
        lower = jnp.pad(cg[:, :s_ + 1 - w], ((0, 0), (w - 1, 0), (0, 0)))
        cnt = jnp.minimum(t + 1, w).astype(jnp.float32)[:, None]
        outs.append((upper - lower) / cnt - u[:, :, sl])
    pooled = jnp.stack(outs, axis=2)
    y = jnp.einsum('bsgc,gcd->bsgd', pooled, w_pool).reshape(b_, s_, POOL_WIDTH)
    return y * pool_scale


def _compress_blocks(t, pe, w1, w2):
    b_, g_, s_, dh = t.shape
    ratio = CMP_BLOCK // CMP_STRIDE
    n_str = s_ // CMP_STRIDE
    n_cmp = n_str - ratio + 1
    pieces = t.reshape(b_, g_, n_str, CMP_STRIDE, dh)
    blk = jnp.concatenate([pieces[:, :, r:r + n_cmp] for r in range(ratio)], axis=3)
    blk = (blk + pe).reshape(b_, g_, n_cmp, CMP_BLOCK * dh)
    return jax.nn.gelu(blk @ w1) @ w2


def _nsa_attention(q, kc, vc, ks, vs, kw, vw, gates, rel_bias):
    b_, g_, j_, s_, dh = q.shape
    scale = dh ** -0.5
    n_cmp = kc.shape[2]
    n_sel = s_ // SEL_BLOCK
    k_top = min(SEL_TOPK, n_sel)
    cmp_end = jnp.arange(n_cmp) * CMP_STRIDE + CMP_BLOCK - 1
    smap = _sel_mapping(n_cmp, n_sel)
    ksb = ks.reshape(b_, g_, n_sel, SEL_BLOCK, dh)
    vsb = vs.reshape(b_, g_, n_sel, SEL_BLOCK, dh)
    kw_pad = jnp.pad(kw, ((0, 0), (0, 0), (WINDOW, 0), (0, 0)))
    vw_pad = jnp.pad(vw, ((0, 0), (0, 0), (WINDOW, 0), (0, 0)))
    tbl_g = rel_bias.reshape(N_BUCKETS, g_, j_)
    bi = jnp.arange(b_)[:, None, None, None]
    gi = jnp.arange(g_)[None, :, None, None]
    blk_id = jnp.arange(n_sel)

    def chunk(ci):
        s0 = ci * Q_CHUNK
        qc = lax.dynamic_slice_in_dim(q, s0, Q_CHUNK, axis=3)
        gc = lax.dynamic_slice_in_dim(gates, s0, Q_CHUNK, axis=3)
        tpos = s0 + jnp.arange(Q_CHUNK)
        rel_c = tpos[:, None] - cmp_end[None, :]
        lg_c = jnp.einsum('bgjqd,bgnd->bgjqn', qc, kc) * scale + _rel_bias_dense(rel_bias, rel_c)
        p_c = _masked_softmax(lg_c, rel_c >= 0)
        o_cmp = jnp.einsum('bgjqn,bgnd->bgjqd', p_c, vc)
        imp = jnp.einsum('bgjqn,ns->bgqs', p_c, smap)
        cur = tpos // SEL_BLOCK
        forced = (blk_id[None, :] == 0) | (blk_id[None, :] == cur[:, None]) | (blk_id[None, :] == cur[:, None] - 1)
        visible = blk_id[None, :] * SEL_BLOCK <= tpos[:, None]
        score = jnp.where(visible, imp + jnp.where(forced, FORCE_SCORE, 0.0), -1.0)
        _, idx = lax.top_k(score, k_top)
        k_g = ksb[bi, gi, idx].reshape(b_, g_, Q_CHUNK, k_top * SEL_BLOCK, dh)
        v_g = vsb[bi, gi, idx].reshape(b_, g_, Q_CHUNK, k_top * SEL_BLOCK, dh)
        kpos = (idx[..., None] * SEL_BLOCK + jnp.arange(SEL_BLOCK)).reshape(b_, g_, Q_CHUNK, k_top * SEL_BLOCK)
        rel_s = tpos[None, None, :, None] - kpos
        bias_s = jnp.moveaxis(tbl_g[_t5_bucket(rel_s), gi], -1, 2)
        lg_s = jnp.einsum('bgjqd,bgqkd->bgjqk', qc, k_g) * scale + bias_s
        p_s = _masked_softmax(lg_s, (rel_s >= 0)[:, :, None])
        o_sel = jnp.einsum('bgjqk,bgqkd->bgjqd', p_s, v_g)
        kwc = lax.dynamic_slice_in_dim(kw_pad, s0, WINDOW + Q_CHUNK, axis=2)
        vwc = lax.dynamic_slice_in_dim(vw_pad, s0, WINDOW + Q_CHUNK, axis=2)
        kpos_w = s0 - WINDOW + jnp.arange(WINDOW + Q_CHUNK)
        rel_w = tpos[:, None] - kpos_w[None, :]
        valid_w = (rel_w >= 0) & (rel_w < WINDOW) & (kpos_w >= 0)[None, :]
        lg_w = jnp.einsum('bgjqd,bgkd->bgjqk', qc, kwc) * scale + _rel_bias_dense(rel_bias, rel_w)
        p_w = _masked_softmax(lg_w, valid_w)
        o_win = jnp.einsum('bgjqk,bgkd->bgjqd', p_w, vwc)
        return gc[..., 0:1] * o_cmp + gc[..., 1:2] * o_sel + gc[..., 2:3] * o_win

    outs = lax.map(chunk, jnp.arange(s_ // Q_CHUNK))
    return jnp.transpose(outs, (1, 0, 4, 2, 3, 5)).reshape(b_, s_, g_ * j_ * dh)


def _peer_ffn(h, qv_all, sub_keys, u, v):
    b_, s_, d_ = h.shape
    tokens = h.reshape(-1, PEER_CHUNK, d_)
    qvs = qv_all.reshape(-1, PEER_CHUNK, PEER_HEADS, 2, PEER_QDIM // 2)

    def chunk(args):
        xc, qv = args
        sc = jnp.einsum('chpd,hpkd->chpk', qv, sub_keys)
        s_half, i_half = lax.top_k(sc, PEER_TOPK)
        cand = (s_half[:, :, 0, :, None] + s_half[:, :, 1, None, :]).reshape(PEER_CHUNK, PEER_HEADS, PEER_TOPK * PEER_TOPK)
        cid = (i_half[:, :, 0, :, None] * PEER_KEYS + i_half[:, :, 1, None, :]).reshape(PEER_CHUNK, PEER_HEADS, PEER_TOPK * PEER_TOPK)
        top_s, top_j = lax.top_k(cand, PEER_TOPK)
        eid = jnp.take_along_axis(cid, top_j, axis=-1)
        g = jax.nn.softmax(top_s, axis=-1)
        ue = u[eid]
        ve = v[eid]
        a = jax.nn.gelu(jnp.einsum('cd,chkd->chk', xc, ue))
        return jnp.einsum('chk,chkd->cd', g * a, ve)

    return lax.map(chunk, (tokens, qvs)).reshape(b_, s_, d_)


def kernel(x, c, rel_bias, ada_w, ada_b, norm1_g, norm2_g, w_in, pool_w, pool_scale, cmp_pe_k, cmp_w1_k, cmp_w2_k, cmp_pe_v, cmp_w1_v, cmp_w2_v, q_norm_g, k_norm_g, w_branch_pool, w_branch_attn, w_out, peer_w_q, peer_sub_keys, peer_u, peer_v):
    b_, s_, d_ = x.shape
    bf = jnp.bfloat16

    def to_groups(t):
        return jnp.transpose(t.reshape(b_, s_, N_KV_GROUPS, HEAD_DIM), (0, 2, 1, 3))

    l = 0
    ada = (jax.nn.silu(c) @ ada_w[l] + ada_b[l]).reshape(b_, 6, 1, d_)
    shift1, scale1, gate1 = ada[:, 0], ada[:, 1], ada[:, 2]
    shift2, scale2, gate2 = ada[:, 3], ada[:, 4], ada[:, 5]

    kv_end = POOL_WIDTH + Q_WIDTH + 6 * KV_WIDTH
    w = w_in[l]
    w_perm = jnp.concatenate(
        [w[:, :kv_end], w[:, kv_end + GATE_COLS:], w[:, kv_end:kv_end + GATE_COLS],
         jnp.zeros((d_, 128 - GATE_COLS), w.dtype)], axis=1).astype(bf)
    _, z = _norm_proj(x, norm1_g[l], scale1, shift1, w_perm)
    z_pool = z[..., :POOL_WIDTH]
    z_q = z[..., POOL_WIDTH:POOL_WIDTH + Q_WIDTH]
    o = POOL_WIDTH + Q_WIDTH
    z_kc, z_vc, z_ks, z_vs, z_kw, z_vw = [z[..., o + i * KV_WIDTH:o + (i + 1) * KV_WIDTH] for i in range(6)]
    z_merge = z[..., kv_end:kv_end + 2 * d_]
    z_gate = z[..., kv_end + 2 * d_:kv_end + 2 * d_ + GATE_COLS]

    y_pool = _pool_mixer(z_pool, pool_w[l], pool_scale[l])
    q = _rms(z_q.reshape(b_, s_, N_KV_GROUPS, HEADS_PER_GROUP, HEAD_DIM), q_norm_g[l])
    q = jnp.transpose(q, (0, 2, 3, 1, 4))
    kc = _rms(_compress_blocks(to_groups(z_kc), cmp_pe_k[l], cmp_w1_k[l], cmp_w2_k[l]), k_norm_g[l, 0])
    vc = _compress_blocks(to_groups(z_vc), cmp_pe_v[l], cmp_w1_v[l], cmp_w2_v[l])
    ksl = _rms(to_groups(z_ks), k_norm_g[l, 1])
    vsl = to_groups(z_vs)
    kwn = _rms(to_groups(z_kw), k_norm_g[l, 2])
    vwn = to_groups(z_vw)
    gates = jnp.transpose(jax.nn.sigmoid(z_gate.reshape(b_, s_, N_KV_GROUPS, HEADS_PER_GROUP, 3)), (0, 2, 3, 1, 4))
    y_attn = _nsa_attention(q, kc, vc, ksl, vsl, kwn, vwn, gates, rel_bias)

    x1 = _merge(x, y_pool, y_attn, z_merge, gate1,
                w_branch_pool[l].astype(bf), w_branch_attn[l].astype(bf), w_out[l].astype(bf))

    h2, qv = _norm_proj(x1, norm2_g[l], scale2, shift2, peer_w_q[l].astype(bf))
    ff = _peer_ffn(h2, qv, peer_sub_keys[l], peer_u[l], peer_v[l])
    return x1 + gate2 * ff
```

```python
import math
from functools import partial

import jax
import jax.numpy as jnp
import numpy as np
from jax import lax
from jax.experimental import pallas as pl
from jax.experimental.pallas import tpu as pltpu

D_MODEL = 1024
POOL_WIDTH = 512
POOL_WINDOWS = (2, 4, 8, 16)
POOL_GROUP_DIM = 128
N_HEADS = 8
N_KV_GROUPS = 2
HEADS_PER_GROUP = 4
HEAD_DIM = 64
Q_WIDTH = 512
KV_WIDTH = 128
CMP_BLOCK = 32
CMP_STRIDE = 16
CMP_HIDDEN = 128
SEL_BLOCK = 64
SEL_TOPK = 16
WINDOW = 512
FORCE_SCORE = 1000.0
N_BUCKETS = 32
MAX_DISTANCE = 128
PEER_HEADS = 8
PEER_KEYS = 128
PEER_TOPK = 16
PEER_QDIM = 256
PEER_CHUNK = 128
GATE_COLS = N_HEADS * 3
EPS = 1e-6
NEG_INF = -1e30

LANES = 128
VMEM_LIMIT = 48 * 1024 * 1024
ROW_TILE = 512
TQ = 128
SEL_FEATS = 64
CMP_BAND_LO = 10

BF16 = jnp.bfloat16
F32 = jnp.float32
_NT = (((1,), (1,)), ((), ()))


def _rms_rows(x, g):
    return x * lax.rsqrt(jnp.mean(x * x, axis=-1, keepdims=True) + EPS) * g


def _in_proj_kernel(x_ref, g_ref, sc_ref, sh_ref, w_ref, kg_ref,
                    zpool_ref, zq_ref, ksel_ref, vsel_ref, kwin_ref, vwin_ref, kcr_ref, vcr_ref,
                    zmerge_ref, zgate_ref):
    i = pl.program_id(1)
    x = x_ref[0]
    y = x * lax.rsqrt(jnp.mean(x * x, axis=-1, keepdims=True) + EPS)
    h = y * g_ref[...] * (1.0 + sc_ref[0]) + sh_ref[0]
    z = jnp.dot(h.astype(BF16), w_ref[...], preferred_element_type=F32)
    ts = x.shape[0]
    zpool_ref[0] = z[:, :POOL_WIDTH]
    o = POOL_WIDTH
    zq_ref[0] = z[:, o:o + Q_WIDTH]
    o += Q_WIDTH
    pos = i * ts + lax.broadcasted_iota(jnp.int32, (ts, SEL_FEATS), 0)
    onehot = (pos // SEL_BLOCK == lax.broadcasted_iota(jnp.int32, (ts, SEL_FEATS), 1)).astype(BF16)
    for g in range(N_KV_GROUPS):
        def col(k):
            return z[:, o + k * KV_WIDTH + g * HEAD_DIM:o + k * KV_WIDTH + (g + 1) * HEAD_DIM]
        kcr_ref[0, g] = col(0)
        vcr_ref[0, g] = col(1)
        ks = _rms_rows(col(2), kg_ref[1:2, :]).astype(BF16)
        ksel_ref[0, g] = jnp.concatenate([ks, onehot], axis=1)
        vsel_ref[0, g] = col(3).astype(BF16)
        kwin_ref[0, g] = _rms_rows(col(4), kg_ref[2:3, :]).astype(BF16)
        vwin_ref[0, g] = col(5).astype(BF16)
    o += 6 * KV_WIDTH
    zmerge_ref[0] = z[:, o:o + 2 * D_MODEL]
    o += 2 * D_MODEL
    zgate_ref[0] = jax.nn.sigmoid(z[:, o:o + N_KV_GROUPS * LANES])


def _in_proj(x, g, scale, shift, w_bf16, k_norm_g):
    b_, s_, d_ = x.shape
    n = w_bf16.shape[1]
    row = lambda w: pl.BlockSpec((1, ROW_TILE, w), lambda b, i: (b, i, 0))
    grp = lambda w: pl.BlockSpec((1, N_KV_GROUPS, ROW_TILE, w), lambda b, i: (b, 0, i, 0))
    gshape = lambda w, dt: jax.ShapeDtypeStruct((b_, N_KV_GROUPS, s_, w), dt)
    return pl.pallas_call(
        _in_proj_kernel,
        grid=(b_, s_ // ROW_TILE),
        in_specs=[
            row(d_),
            pl.BlockSpec((1, d_), lambda b, i: (0, 0)),
            pl.BlockSpec((1, 1, d_), lambda b, i: (b, 0, 0)),
            pl.BlockSpec((1, 1, d_), lambda b, i: (b, 0, 0)),
            pl.BlockSpec((d_, n), lambda b, i: (0, 0)),
            pl.BlockSpec((3, HEAD_DIM), lambda b, i: (0, 0)),
        ],
        out_specs=[row(POOL_WIDTH), row(Q_WIDTH), grp(HEAD_DIM + SEL_FEATS), grp(HEAD_DIM), grp(HEAD_DIM),
                   grp(HEAD_DIM), grp(HEAD_DIM), grp(HEAD_DIM), row(2 * d_), row(N_KV_GROUPS * LANES)],
        out_shape=[
            jax.ShapeDtypeStruct((b_, s_, POOL_WIDTH), F32),
            jax.ShapeDtypeStruct((b_, s_, Q_WIDTH), F32),
            gshape(HEAD_DIM + SEL_FEATS, BF16), gshape(HEAD_DIM, BF16), gshape(HEAD_DIM, BF16), gshape(HEAD_DIM, BF16),
            gshape(HEAD_DIM, F32), gshape(HEAD_DIM, F32),
            jax.ShapeDtypeStruct((b_, s_, 2 * d_), F32),
            jax.ShapeDtypeStruct((b_, s_, N_KV_GROUPS * LANES), F32),
        ],
        compiler_params=pltpu.CompilerParams(
            dimension_semantics=("parallel", "parallel"), vmem_limit_bytes=VMEM_LIMIT),
    )(x, g.reshape(1, d_), scale, shift, w_bf16, k_norm_g)


def _compress_kernel(t_ref, pe_ref, w1_ref, w2_ref, g_ref, o_ref, *, normalize):
    half = CMP_STRIDE * HEAD_DIM
    t = t_ref[0, 0].astype(BF16)
    w1 = w1_ref[...]
    a = jnp.dot(t, w1[:half], preferred_element_type=F32)
    b = jnp.dot(t, w1[half:], preferred_element_type=F32)
    pe = jnp.dot(pe_ref[...], w1, preferred_element_type=F32)
    n = a.shape[0]
    b_next = pltpu.roll(b, n - 1, axis=0)
    hid = jax.nn.gelu(a + b_next + pe)
    out = jnp.dot(hid.astype(BF16), w2_ref[...], preferred_element_type=F32)
    if normalize:
        out = _rms_rows(out, g_ref[...])
    o_ref[0, 0] = out.astype(BF16)


def _compress(t_raw, pe, w1, w2, g, normalize):
    b_, g_, s_, dh = t_raw.shape
    n_str = s_ // CMP_STRIDE
    t = t_raw.reshape(b_, g_, n_str, CMP_STRIDE * dh)
    return pl.pallas_call(
        partial(_compress_kernel, normalize=normalize),
        grid=(b_, g_),
        in_specs=[
            pl.BlockSpec((1, 1, n_str, CMP_STRIDE * dh), lambda b, g: (b, g, 0, 0)),
            pl.BlockSpec((1, CMP_BLOCK * dh), lambda b, g: (0, 0)),
            pl.BlockSpec((CMP_BLOCK * dh, CMP_HIDDEN), lambda b, g: (0, 0)),
            pl.BlockSpec((CMP_HIDDEN, dh), lambda b, g: (0, 0)),
            pl.BlockSpec((1, dh), lambda b, g: (0, 0)),
        ],
        out_specs=pl.BlockSpec((1, 1, n_str, dh), lambda b, g: (b, g, 0, 0)),
        out_shape=jax.ShapeDtypeStruct((b_, g_, n_str, dh), BF16),
        compiler_params=pltpu.CompilerParams(
            dimension_semantics=("parallel", "parallel"), vmem_limit_bytes=VMEM_LIMIT),
    )(t, pe.reshape(1, CMP_BLOCK * dh).astype(BF16), w1.astype(BF16), w2.astype(BF16), g.reshape(1, dh))


def _t5_bucket_np(rel):
    n = np.maximum(rel, 0)
    max_exact = N_BUCKETS // 2
    nf = np.maximum(n, 1).astype(np.float32)
    large = max_exact + (np.log(nf / max_exact) / math.log(MAX_DISTANCE / max_exact) * (N_BUCKETS - max_exact)).astype(np.int32)
    large = np.minimum(large, N_BUCKETS - 1)
    return np.where(n < max_exact, n, large)


def _bias_tiles(rel_bias, n_cmp_pad):
    far = rel_bias[N_BUCKETS - 1]
    ii = np.arange(TQ)[:, None]

    def lookup(dist):
        near = (dist >= 0) & (dist < MAX_DISTANCE)
        bucket = _t5_bucket_np(np.where(near, dist, MAX_DISTANCE))
        b = rel_bias[bucket] - far
        b = jnp.transpose(b, (2, 0, 1))
        return b.reshape(N_KV_GROUPS, HEADS_PER_GROUP * dist.shape[0], dist.shape[1])

    jj = np.arange(TQ)[None, :]
    bt = jnp.stack([lookup(ii - jj), lookup(TQ + ii - jj)], axis=1)
    cc = np.arange(n_cmp_pad)[None, :]
    dist_c = ii - CMP_STRIDE * (cc - CMP_BAND_LO) - (CMP_BLOCK - 1)
    dist_c = np.where(cc <= CMP_BAND_LO + TQ // CMP_STRIDE, dist_c, -1)
    bc = lookup(dist_c)
    return bt, bc


def _sel_mapping_t(n_cmp_pad, n_cmp):
    m = np.zeros((SEL_FEATS, n_cmp_pad), np.float32)
    pos = np.arange(n_cmp)[:, None] * CMP_STRIDE + np.arange(CMP_BLOCK)[None, :]
    np.add.at(m, ((pos // SEL_BLOCK).ravel(), np.repeat(np.arange(n_cmp), CMP_BLOCK)), 1.0 / CMP_BLOCK)
    return m


def _flash_step(q, k, v, m_sc, l_sc, acc_sc, bias=None, mask=None):
    s = lax.dot_general(q, k, _NT, preferred_element_type=F32)
    if bias is not None:
        s = s + bias
    if mask is not None:
        s = jnp.where(mask, s, NEG_INF)
    m_prev = m_sc[...]
    m_new = jnp.maximum(m_prev, jnp.max(s, axis=-1, keepdims=True))
    alpha = jnp.exp(m_prev - m_new)
    p = jnp.exp(s - m_new)
    l_sc[...] = alpha * l_sc[...] + jnp.sum(p, axis=-1, keepdims=True)
    acc_sc[...] = alpha * acc_sc[...] + jnp.dot(p.astype(BF16), v, preferred_element_type=F32)
    m_sc[...] = m_new


def _attn_kernel(zq_ref, gate_ref, qg_ref, kc_ref, vc_ref, ksel_ref, vsel_ref, kwin_ref, vwin_ref,
                 bt_ref, bc_ref, smap_ref, o_ref, m_sc, l_sc, acc_sc):
    i = pl.program_id(2)
    rows = HEADS_PER_GROUP * TQ
    ncp = kc_ref.shape[2]
    scale = HEAD_DIM ** -0.5

    zq = zq_ref[0]
    qs = jnp.concatenate([zq[:, j * HEAD_DIM:(j + 1) * HEAD_DIM] for j in range(HEADS_PER_GROUP)], axis=0)
    qn = _rms_rows(qs, qg_ref[...]) * scale
    qb = qn.astype(BF16)

    lc = lax.dot_general(qb, kc_ref[0, 0], _NT, preferred_element_type=F32)
    lc = lc + pltpu.roll(bc_ref[0], (i * (TQ // CMP_STRIDE) + ncp - CMP_BAND_LO) % ncp, axis=1)
    t_c = i * TQ + lax.broadcasted_iota(jnp.int32, (rows, ncp), 0) % TQ
    n_c = lax.broadcasted_iota(jnp.int32, (rows, ncp), 1)
    valid_c = n_c * CMP_STRIDE + (CMP_BLOCK - 1) <= t_c
    lm = jnp.where(valid_c, lc, NEG_INF)
    e = jnp.where(valid_c, jnp.exp(lm - jnp.max(lm, axis=-1, keepdims=True)), 0.0)
    den = jnp.sum(e, axis=-1, keepdims=True)
    p_c = e / jnp.where(den > 0.0, den, 1.0)
    o_cmp = jnp.dot(p_c.astype(BF16), vc_ref[0, 0], preferred_element_type=F32)

    p_sum = p_c[0:TQ] + p_c[TQ:2 * TQ] + p_c[2 * TQ:3 * TQ] + p_c[3 * TQ:4 * TQ]
    p_hi = p_sum.astype(BF16)
    r1 = p_sum - p_hi.astype(F32)
    p_mid = r1.astype(BF16)
    p_lo = (r1 - p_mid.astype(F32)).astype(BF16)
    smap = smap_ref[...]
    imp = (lax.dot_general(smap, p_hi, _NT, preferred_element_type=F32)
           + lax.dot_general(smap, p_mid, _NT, preferred_element_type=F32)
           + lax.dot_general(smap, p_lo, _NT, preferred_element_type=F32))
    blk = lax.broadcasted_iota(jnp.int32, (SEL_FEATS, TQ), 0)
    t_s = i * TQ + lax.broadcasted_iota(jnp.int32, (SEL_FEATS, TQ), 1)
    cur = t_s // SEL_BLOCK
    forced = (blk == 0) | (blk == cur) | (blk == cur - 1)
    visible = blk * SEL_BLOCK <= t_s
    score = jnp.where(visible, imp + jnp.where(forced, FORCE_SCORE, 0.0), -1.0)
    rank = jnp.zeros((SEL_FEATS, TQ), jnp.int32)
    for sp in range(SEL_FEATS):
        row = score[sp:sp + 1, :]
        beats = (row > score) | ((row == score) & (blk > sp))
        rank = rank + beats.astype(jnp.int32)
    pen_t = jnp.where(rank < SEL_TOPK, 0.0, NEG_INF)
    pen = jnp.transpose(pen_t).astype(BF16)
    q_aug = jnp.concatenate([qb, jnp.concatenate([pen] * HEADS_PER_GROUP, axis=0)], axis=1)

    ii = lax.broadcasted_iota(jnp.int32, (rows, TQ), 0) % TQ
    jj = lax.broadcasted_iota(jnp.int32, (rows, TQ), 1)
    causal = jj <= ii

    def reset():
        m_sc[...] = jnp.full((rows, 1), NEG_INF, F32)
        l_sc[...] = jnp.zeros((rows, 1), F32)
        acc_sc[...] = jnp.zeros((rows, HEAD_DIM), F32)

    def tile(ref, kt):
        return ref[0, 0, pl.ds(pl.multiple_of(kt * TQ, TQ), TQ), :]

    reset()

    def sel_body(kt, carry):
        _flash_step(q_aug, tile(ksel_ref, kt), tile(vsel_ref, kt), m_sc, l_sc, acc_sc)
        return carry

    lax.fori_loop(0, i - 1, sel_body, 0)

    @pl.when(i >= 1)
    def _():
        _flash_step(q_aug, tile(ksel_ref, i - 1), tile(vsel_ref, i - 1), m_sc, l_sc, acc_sc, bias=bt_ref[0, 1])

    _flash_step(q_aug, tile(ksel_ref, i), tile(vsel_ref, i), m_sc, l_sc, acc_sc, bias=bt_ref[0, 0], mask=causal)
    o_sel = acc_sc[...] / l_sc[...]

    reset()
    n_win = WINDOW // TQ

    @pl.when(i >= n_win)
    def _():
        _flash_step(qb, tile(kwin_ref, i - n_win), tile(vwin_ref, i - n_win), m_sc, l_sc, acc_sc, mask=jj > ii)

    for back in range(n_win - 1, 1, -1):
        @pl.when(i >= back)
        def _(back=back):
            _flash_step(qb, tile(kwin_ref, i - back), tile(vwin_ref, i - back), m_sc, l_sc, acc_sc)

    @pl.when(i >= 1)
    def _():
        _flash_step(qb, tile(kwin_ref, i - 1), tile(vwin_ref, i - 1), m_sc, l_sc, acc_sc, bias=bt_ref[0, 1])

    _flash_step(qb, tile(kwin_ref, i), tile(vwin_ref, i), m_sc, l_sc, acc_sc, bias=bt_ref[0, 0], mask=causal)
    o_win = acc_sc[...] / l_sc[...]

    gate = gate_ref[0]
    outs = []
    for j in range(HEADS_PER_GROUP):
        sl = slice(j * TQ, (j + 1) * TQ)
        outs.append(gate[:, 3 * j:3 * j + 1] * o_cmp[sl] + gate[:, 3 * j + 1:3 * j + 2] * o_sel[sl]
                    + gate[:, 3 * j + 2:3 * j + 3] * o_win[sl])
    o_ref[0] = jnp.concatenate(outs, axis=1)


def _attention(z_q, z_gate, q_norm_g, kc, vc, ksel, vsel, kwin, vwin, rel_bias):
    b_, s_, _ = z_q.shape
    ncp = kc.shape[2]
    n_cmp = ncp - CMP_BLOCK // CMP_STRIDE + 1
    assert s_ // SEL_BLOCK <= SEL_FEATS and s_ % TQ == 0
    bt, bc = _bias_tiles(rel_bias, ncp)
    smap_t = jnp.asarray(_sel_mapping_t(ncp, n_cmp), BF16)
    rows = HEADS_PER_GROUP * TQ
    gw = HEADS_PER_GROUP * HEAD_DIM
    kv = lambda w: pl.BlockSpec((1, 1, s_, w), lambda b, g, i: (b, g, 0, 0))
    cm = pl.BlockSpec((1, 1, ncp, HEAD_DIM), lambda b, g, i: (b, g, 0, 0))
    return pl.pallas_call(
        _attn_kernel,
        grid=(b_, N_KV_GROUPS, s_ // TQ),
        in_specs=[
            pl.BlockSpec((1, TQ, gw), lambda b, g, i: (b, i, g)),
            pl.BlockSpec((1, TQ, LANES), lambda b, g, i: (b, i, g)),
            pl.BlockSpec((1, HEAD_DIM), lambda b, g, i: (0, 0)),
            cm, cm, kv(HEAD_DIM + SEL_FEATS), kv(HEAD_DIM), kv(HEAD_DIM), kv(HEAD_DIM),
            pl.BlockSpec((1, 2, rows, TQ), lambda b, g, i: (g, 0, 0, 0)),
            pl.BlockSpec((1, rows, ncp), lambda b, g, i: (g, 0, 0)),
            pl.BlockSpec((SEL_FEATS, ncp), lambda b, g, i: (0, 0)),
        ],
        out_specs=pl.BlockSpec((1, TQ, gw), lambda b, g, i: (b, i, g)),
        out_shape=jax.ShapeDtypeStruct((b_, s_, N_KV_GROUPS * gw), F32),
        scratch_shapes=[pltpu.VMEM((rows, 1), F32), pltpu.VMEM((rows, 1), F32), pltpu.VMEM((rows, HEAD_DIM), F32)],
        compiler_params=pltpu.CompilerParams(
            dimension_semantics=("parallel", "parallel", "arbitrary"), vmem_limit_bytes=VMEM_LIMIT),
    )(z_q, z_gate, q_norm_g.reshape(1, HEAD_DIM), kc, vc, ksel, vsel, kwin, vwin, bt, bc, smap_t)


def _merge_kernel(x_ref, yp_ref, ya_ref, zm_ref, g1_ref, wbp_ref, wba_ref, wo_ref, o_ref):
    d_ = x_ref.shape[-1]
    bp = jnp.dot(yp_ref[0].astype(BF16), wbp_ref[...], preferred_element_type=F32)
    ba = jnp.dot(ya_ref[0].astype(BF16), wba_ref[...], preferred_element_type=F32)
    zm = zm_ref[0]
    mixed = jax.nn.sigmoid(zm[:, :d_]) * bp + jax.nn.sigmoid(zm[:, d_:]) * ba
    proj = jnp.dot(mixed.astype(BF16), wo_ref[...], preferred_element_type=F32)
    o_ref[0] = x_ref[0] + g1_ref[0] * proj


def _merge(x, y_pool, y_attn, z_merge, gate1, wbp, wba, wo):
    b_, s_, d_ = x.shape
    row = lambda w: pl.BlockSpec((1, ROW_TILE, w), lambda b, i: (b, i, 0))
    full = lambda a: pl.BlockSpec(a.shape, lambda b, i: (0, 0))
    return pl.pallas_call(
        _merge_kernel,
        grid=(b_, s_ // ROW_TILE),
        in_specs=[row(d_), row(POOL_WIDTH), row(Q_WIDTH), row(2 * d_),
                  pl.BlockSpec((1, 1, d_), lambda b, i: (b, 0, 0)),
                  full(wbp), full(wba), full(wo)],
        out_specs=row(d_),
        out_shape=jax.ShapeDtypeStruct((b_, s_, d_), F32),
        compiler_params=pltpu.CompilerParams(
            dimension_semantics=("parallel", "parallel"), vmem_limit_bytes=VMEM_LIMIT),
    )(x, y_pool, y_attn, z_merge, gate1, wbp, wba, wo)


def _norm_proj_kernel(x_ref, g_ref, sc_ref, sh_ref, w_ref, h_ref, z_ref):
    x = x_ref[0]
    y = x * lax.rsqrt(jnp.mean(x * x, axis=-1, keepdims=True) + EPS)
    h = y * g_ref[...] * (1.0 + sc_ref[0]) + sh_ref[0]
    h_ref[0] = h
    z_ref[0] = jnp.dot(h.astype(BF16), w_ref[...], preferred_element_type=F32)


def _norm_proj(x, g, scale, shift, w_bf16):
    b_, s_, d_ = x.shape
    n = w_bf16.shape[1]
    return pl.pallas_call(
        _norm_proj_kernel,
        grid=(b_, s_ // ROW_TILE),
        in_specs=[
            pl.BlockSpec((1, ROW_TILE, d_), lambda b, i: (b, i, 0)),
            pl.BlockSpec((1, d_), lambda b, i: (0, 0)),
            pl.BlockSpec((1, 1, d_), lambda b, i: (b, 0, 0)),
            pl.BlockSpec((1, 1, d_), lambda b, i: (b, 0, 0)),
            pl.BlockSpec((d_, n), lambda b, i: (0, 0)),
        ],
        out_specs=[
            pl.BlockSpec((1, ROW_TILE, d_), lambda b, i: (b, i, 0)),
            pl.BlockSpec((1, ROW_TILE, n), lambda b, i: (b, i, 0)),
        ],
        out_shape=[
            jax.ShapeDtypeStruct((b_, s_, d_), F32),
            jax.ShapeDtypeStruct((b_, s_, n), F32),
        ],
        compiler_params=pltpu.CompilerParams(
            dimension_semantics=("parallel", "parallel"), vmem_limit_bytes=VMEM_LIMIT),
    )(x, g.reshape(1, d_), scale, shift, w_bf16)


def _pool_mixer(u, w_pool, pool_scale):
    b_, s_, c_ = u.shape
    csp = jnp.concatenate([jnp.zeros((b_, 1, c_), jnp.float32), jnp.cumsum(u, axis=1)], axis=1)
    t = jnp.arange(s_)
    outs = []
    for gi, w in enumerate(POOL_WINDOWS):
        sl = slice(gi * POOL_GROUP_DIM, (gi + 1) * POOL_GROUP_DIM)
        cg = csp[:, :, sl]
        upper = cg[:, 1:]
        lower = jnp.pad(cg[:, :s_ + 1 - w], ((0, 0), (w - 1, 0), (0, 0)))
        cnt = jnp.minimum(t + 1, w).astype(jnp.float32)[:, None]
        outs.append((upper - lower) / cnt - u[:, :, sl])
    pooled = jnp.stack(outs, axis=2)
    y = jnp.einsum('bsgc,gcd->bsgd', pooled, w_pool).reshape(b_, s_, POOL_WIDTH)
    return y * pool_scale


def _peer_ffn(h, qv_all, sub_keys, u, v):
    b_, s_, d_ = h.shape
    tokens = h.reshape(-1, PEER_CHUNK, d_)
    qvs = qv_all.reshape(-1, PEER_CHUNK, PEER_HEADS, 2, PEER_QDIM // 2)

    def chunk(args):
        xc, qv = args
        sc = jnp.einsum('chpd,hpkd->chpk', qv, sub_keys)
        s_half, i_half = lax.top_k(sc, PEER_TOPK)
        cand = (s_half[:, :, 0, :, None] + s_half[:, :, 1, None, :]).reshape(PEER_CHUNK, PEER_HEADS, PEER_TOPK * PEER_TOPK)
        cid = (i_half[:, :, 0, :, None] * PEER_KEYS + i_half[:, :, 1, None, :]).reshape(PEER_CHUNK, PEER_HEADS, PEER_TOPK * PEER_TOPK)
        top_s, top_j = lax.top_k(cand, PEER_TOPK)
        eid = jnp.take_along_axis(cid, top_j, axis=-1)
        g = jax.nn.softmax(top_s, axis=-1)
        ue = u[eid]
        ve = v[eid]
        a = jax.nn.gelu(jnp.einsum('cd,chkd->chk', xc, ue))
        return jnp.einsum('chk,chkd->cd', g * a, ve)

    return lax.map(chunk, (tokens, qvs)).reshape(b_, s_, d_)


def _permute_w_in(w):
    d_ = w.shape[0]
    kv_end = POOL_WIDTH + Q_WIDTH + 6 * KV_WIDTH
    per_group = HEADS_PER_GROUP * 3
    parts = [w[:, :kv_end], w[:, kv_end + GATE_COLS:]]
    for g in range(N_KV_GROUPS):
        parts.append(w[:, kv_end + g * per_group:kv_end + (g + 1) * per_group])
        parts.append(jnp.zeros((d_, LANES - per_group), w.dtype))
    return jnp.concatenate(parts, axis=1).astype(BF16)


def kernel(x, c, rel_bias, ada_w, ada_b, norm1_g, norm2_g, w_in, pool_w, pool_scale, cmp_pe_k, cmp_w1_k, cmp_w2_k, cmp_pe_v, cmp_w1_v, cmp_w2_v, q_norm_g, k_norm_g, w_branch_pool, w_branch_attn, w_out, peer_w_q, peer_sub_keys, peer_u, peer_v):
    b_, s_, d_ = x.shape
    l = 0
    ada = (jax.nn.silu(c) @ ada_w[l] + ada_b[l]).reshape(b_, 6, 1, d_)
    shift1, scale1, gate1 = ada[:, 0], ada[:, 1], ada[:, 2]
    shift2, scale2, gate2 = ada[:, 3], ada[:, 4], ada[:, 5]

    (z_pool, z_q, ksel, vsel, kwin, vwin, kc_raw, vc_raw, z_merge, z_gate) = _in_proj(
        x, norm1_g[l], scale1, shift1, _permute_w_in(w_in[l]), k_norm_g[l])

    y_pool = _pool_mixer(z_pool, pool_w[l], pool_scale[l])
    kc = _compress(kc_raw, cmp_pe_k[l], cmp_w1_k[l], cmp_w2_k[l], k_norm_g[l, 0], True)
    vc = _compress(vc_raw, cmp_pe_v[l], cmp_w1_v[l], cmp_w2_v[l], k_norm_g[l, 0], False)
    y_attn = _attention(z_q, z_gate, q_norm_g[l], kc, vc, ksel, vsel, kwin, vwin, rel_bias)

    x1 = _merge(x, y_pool, y_attn, z_merge, gate1,
                w_branch_pool[l].astype(BF16), w_branch_attn[l].astype(BF16), w_out[l].astype(BF16))

    h2, qv = _norm_proj(x1, norm2_g[l], scale2, shift2, peer_w_q[l].astype(BF16))
    ff = _peer_ffn(h2, qv, peer_sub_keys[l], peer_u[l], peer_v[l])
    return x1 + gate2 * ff
```

```python
import math
from functools import partial

import jax
import jax.numpy as jnp
import numpy as np
from jax import lax
from jax.experimental import pallas as pl
from jax.experimental.pallas import tpu as pltpu

D_MODEL = 1024
POOL_WIDTH = 512
POOL_WINDOWS = (2, 4, 8, 16)
POOL_GROUP_DIM = 128
N_HEADS = 8
N_KV_GROUPS = 2
HEADS_PER_GROUP = 4
HEAD_DIM = 64
Q_WIDTH = 512
KV_WIDTH = 128
CMP_BLOCK = 32
CMP_STRIDE = 16
CMP_HIDDEN = 128
SEL_BLOCK = 64
SEL_TOPK = 16
WINDOW = 512
FORCE_SCORE = 1000.0
N_BUCKETS = 32
MAX_DISTANCE = 128
PEER_HEADS = 8
PEER_KEYS = 128
PEER_TOPK = 16
PEER_QDIM = 256
PEER_CHUNK = 128
GATE_COLS = N_HEADS * 3
EPS = 1e-6
NEG_INF = -1e30

LANES = 128
VMEM_LIMIT = 48 * 1024 * 1024
ROW_TILE = 512
TQ = 128
SEL_FEATS = 64
CMP_BAND_LO = 10
PEER_TILE = 256
PEER_TOKENS = 8

BF16 = jnp.bfloat16
F32 = jnp.float32
_NT = (((1,), (1,)), ((), ()))


def _rms_rows(x, g):
    return x * lax.rsqrt(jnp.mean(x * x, axis=-1, keepdims=True) + EPS) * g


def _in_proj_kernel(x_ref, g_ref, sc_ref, sh_ref, w_ref, kg_ref,
                    zpool_ref, zq_ref, ksel_ref, vsel_ref, kwin_ref, vwin_ref, kcr_ref, vcr_ref,
                    zmerge_ref, zgate_ref):
    i = pl.program_id(1)
    x = x_ref[0]
    y = x * lax.rsqrt(jnp.mean(x * x, axis=-1, keepdims=True) + EPS)
    h = y * g_ref[...] * (1.0 + sc_ref[0]) + sh_ref[0]
    z = jnp.dot(h.astype(BF16), w_ref[...], preferred_element_type=F32)
    ts = x.shape[0]
    zpool_ref[0] = z[:, :POOL_WIDTH]
    o = POOL_WIDTH
    zq_ref[0] = z[:, o:o + Q_WIDTH]
    o += Q_WIDTH
    pos = i * ts + lax.broadcasted_iota(jnp.int32, (ts, SEL_FEATS), 0)
    onehot = (pos // SEL_BLOCK == lax.broadcasted_iota(jnp.int32, (ts, SEL_FEATS), 1)).astype(BF16)
    for g in range(N_KV_GROUPS):
        def col(k):
            return z[:, o + k * KV_WIDTH + g * HEAD_DIM:o + k * KV_WIDTH + (g + 1) * HEAD_DIM]
        kcr_ref[0, g] = col(0)
        vcr_ref[0, g] = col(1)
        ks = _rms_rows(col(2), kg_ref[1:2, :]).astype(BF16)
        ksel_ref[0, g] = jnp.concatenate([ks, onehot], axis=1)
        vsel_ref[0, g] = col(3).astype(BF16)
        kwin_ref[0, g] = _rms_rows(col(4), kg_ref[2:3, :]).astype(BF16)
        vwin_ref[0, g] = col(5).astype(BF16)
    o += 6 * KV_WIDTH
    zmerge_ref[0] = z[:, o:o + 2 * D_MODEL]
    o += 2 * D_MODEL
    zgate_ref[0] = jax.nn.sigmoid(z[:, o:o + N_KV_GROUPS * LANES])


def _in_proj(x, g, scale, shift, w_bf16, k_norm_g):
    b_, s_, d_ = x.shape
    n = w_bf16.shape[1]
    row = lambda w: pl.BlockSpec((1, ROW_TILE, w), lambda b, i: (b, i, 0))
    grp = lambda w: pl.BlockSpec((1, N_KV_GROUPS, ROW_TILE, w), lambda b, i: (b, 0, i, 0))
    gshape = lambda w, dt: jax.ShapeDtypeStruct((b_, N_KV_GROUPS, s_, w), dt)
    return pl.pallas_call(
        _in_proj_kernel,
        grid=(b_, s_ // ROW_TILE),
        in_specs=[
            row(d_),
            pl.BlockSpec((1, d_), lambda b, i: (0, 0)),
            pl.BlockSpec((1, 1, d_), lambda b, i: (b, 0, 0)),
            pl.BlockSpec((1, 1, d_), lambda b, i: (b, 0, 0)),
            pl.BlockSpec((d_, n), lambda b, i: (0, 0)),
            pl.BlockSpec((3, HEAD_DIM), lambda b, i: (0, 0)),
        ],
        out_specs=[row(POOL_WIDTH), row(Q_WIDTH), grp(HEAD_DIM + SEL_FEATS), grp(HEAD_DIM), grp(HEAD_DIM),
                   grp(HEAD_DIM), grp(HEAD_DIM), grp(HEAD_DIM), row(2 * d_), row(N_KV_GROUPS * LANES)],
        out_shape=[
            jax.ShapeDtypeStruct((b_, s_, POOL_WIDTH), F32),
            jax.ShapeDtypeStruct((b_, s_, Q_WIDTH), F32),
            gshape(HEAD_DIM + SEL_FEATS, BF16), gshape(HEAD_DIM, BF16), gshape(HEAD_DIM, BF16), gshape(HEAD_DIM, BF16),
            gshape(HEAD_DIM, F32), gshape(HEAD_DIM, F32),
            jax.ShapeDtypeStruct((b_, s_, 2 * d_), F32),
            jax.ShapeDtypeStruct((b_, s_, N_KV_GROUPS * LANES), F32),
        ],
        compiler_params=pltpu.CompilerParams(
            dimension_semantics=("parallel", "parallel"), vmem_limit_bytes=VMEM_LIMIT),
    )(x, g.reshape(1, d_), scale, shift, w_bf16, k_norm_g)


def _compress_kernel(t_ref, pe_ref, w1_ref, w2_ref, g_ref, o_ref, *, normalize):
    half = CMP_STRIDE * HEAD_DIM
    t = t_ref[0, 0].astype(BF16)
    w1 = w1_ref[...]
    a = jnp.dot(t, w1[:half], preferred_element_type=F32)
    b = jnp.dot(t, w1[half:], preferred_element_type=F32)
    pe = jnp.dot(pe_ref[...], w1, preferred_element_type=F32)
    n = a.shape[0]
    b_next = pltpu.roll(b, n - 1, axis=0)
    hid = jax.nn.gelu(a + b_next + pe)
    out = jnp.dot(hid.astype(BF16), w2_ref[...], preferred_element_type=F32)
    if normalize:
        out = _rms_rows(out, g_ref[...])
    o_ref[0, 0] = out.astype(BF16)


def _compress(t_raw, pe, w1, w2, g, normalize):
    b_, g_, s_, dh = t_raw.shape
    n_str = s_ // CMP_STRIDE
    t = t_raw.reshape(b_, g_, n_str, CMP_STRIDE * dh)
    return pl.pallas_call(
        partial(_compress_kernel, normalize=normalize),
        grid=(b_, g_),
        in_specs=[
            pl.BlockSpec((1, 1, n_str, CMP_STRIDE * dh), lambda b, g: (b, g, 0, 0)),
            pl.BlockSpec((1, CMP_BLOCK * dh), lambda b, g: (0, 0)),
            pl.BlockSpec((CMP_BLOCK * dh, CMP_HIDDEN), lambda b, g: (0, 0)),
            pl.BlockSpec((CMP_HIDDEN, dh), lambda b, g: (0, 0)),
            pl.BlockSpec((1, dh), lambda b, g: (0, 0)),
        ],
        out_specs=pl.BlockSpec((1, 1, n_str, dh), lambda b, g: (b, g, 0, 0)),
        out_shape=jax.ShapeDtypeStruct((b_, g_, n_str, dh), BF16),
        compiler_params=pltpu.CompilerParams(
            dimension_semantics=("parallel", "parallel"), vmem_limit_bytes=VMEM_LIMIT),
    )(t, pe.reshape(1, CMP_BLOCK * dh).astype(BF16), w1.astype(BF16), w2.astype(BF16), g.reshape(1, dh))


def _t5_bucket_np(rel):
    n = np.maximum(rel, 0)
    max_exact = N_BUCKETS // 2
    nf = np.maximum(n, 1).astype(np.float32)
    large = max_exact + (np.log(nf / max_exact) / math.log(MAX_DISTANCE / max_exact) * (N_BUCKETS - max_exact)).astype(np.int32)
    large = np.minimum(large, N_BUCKETS - 1)
    return np.where(n < max_exact, n, large)


def _bias_tiles(rel_bias, n_cmp_pad):
    far = rel_bias[N_BUCKETS - 1]
    ii = np.arange(TQ)[:, None]

    def lookup(dist):
        near = (dist >= 0) & (dist < MAX_DISTANCE)
        bucket = _t5_bucket_np(np.where(near, dist, MAX_DISTANCE))
        b = rel_bias[bucket] - far
        b = jnp.transpose(b, (2, 0, 1))
        return b.reshape(N_KV_GROUPS, HEADS_PER_GROUP * dist.shape[0], dist.shape[1])

    jj = np.arange(TQ)[None, :]
    bt = jnp.stack([lookup(ii - jj), lookup(TQ + ii - jj)], axis=1)
    cc = np.arange(n_cmp_pad)[None, :]
    dist_c = ii - CMP_STRIDE * (cc - CMP_BAND_LO) - (CMP_BLOCK - 1)
    dist_c = np.where(cc <= CMP_BAND_LO + TQ // CMP_STRIDE, dist_c, -1)
    bc = lookup(dist_c)
    return bt, bc


def _sel_mapping_t(n_cmp_pad, n_cmp):
    m = np.zeros((SEL_FEATS, n_cmp_pad), np.float32)
    pos = np.arange(n_cmp)[:, None] * CMP_STRIDE + np.arange(CMP_BLOCK)[None, :]
    np.add.at(m, ((pos // SEL_BLOCK).ravel(), np.repeat(np.arange(n_cmp), CMP_BLOCK)), 1.0 / CMP_BLOCK)
    return m


def _flash_step(q, k, v, m_sc, l_sc, acc_sc, bias=None, mask=None):
    s = lax.dot_general(q, k, _NT, preferred_element_type=F32)
    if bias is not None:
        s = s + bias
    if mask is not None:
        s = jnp.where(mask, s, NEG_INF)
    m_prev = m_sc[...]
    m_new = jnp.maximum(m_prev, jnp.max(s, axis=-1, keepdims=True))
    alpha = jnp.exp(m_prev - m_new)
    p = jnp.exp(s - m_new)
    l_sc[...] = alpha * l_sc[...] + jnp.sum(p, axis=-1, keepdims=True)
    acc_sc[...] = alpha * acc_sc[...] + jnp.dot(p.astype(BF16), v, preferred_element_type=F32)
    m_sc[...] = m_new


def _attn_kernel(zq_ref, gate_ref, qg_ref, kc_ref, vc_ref, ksel_ref, vsel_ref, kwin_ref, vwin_ref,
                 bt_ref, bc_ref, smap_ref, o_ref, m_sc, l_sc, acc_sc):
    i = pl.program_id(2)
    rows = HEADS_PER_GROUP * TQ
    ncp = kc_ref.shape[2]
    scale = HEAD_DIM ** -0.5

    zq = zq_ref[0]
    qs = jnp.concatenate([zq[:, j * HEAD_DIM:(j + 1) * HEAD_DIM] for j in range(HEADS_PER_GROUP)], axis=0)
    qn = _rms_rows(qs, qg_ref[...]) * scale
    qb = qn.astype(BF16)

    lc = lax.dot_general(qb, kc_ref[0, 0], _NT, preferred_element_type=F32)
    lc = lc + pltpu.roll(bc_ref[0], (i * (TQ // CMP_STRIDE) + ncp - CMP_BAND_LO) % ncp, axis=1)
    t_c = i * TQ + lax.broadcasted_iota(jnp.int32, (rows, ncp), 0) % TQ
    n_c = lax.broadcasted_iota(jnp.int32, (rows, ncp), 1)
    valid_c = n_c * CMP_STRIDE + (CMP_BLOCK - 1) <= t_c
    lm = jnp.where(valid_c, lc, NEG_INF)
    e = jnp.where(valid_c, jnp.exp(lm - jnp.max(lm, axis=-1, keepdims=True)), 0.0)
    den = jnp.sum(e, axis=-1, keepdims=True)
    p_c = e / jnp.where(den > 0.0, den, 1.0)
    o_cmp = jnp.dot(p_c.astype(BF16), vc_ref[0, 0], preferred_element_type=F32)

    p_sum = p_c[0:TQ] + p_c[TQ:2 * TQ] + p_c[2 * TQ:3 * TQ] + p_c[3 * TQ:4 * TQ]
    p_hi = p_sum.astype(BF16)
    r1 = p_sum - p_hi.astype(F32)
    p_mid = r1.astype(BF16)
    p_lo = (r1 - p_mid.astype(F32)).astype(BF16)
    smap = smap_ref[...]
    imp = (lax.dot_general(smap, p_hi, _NT, preferred_element_type=F32)
           + lax.dot_general(smap, p_mid, _NT, preferred_element_type=F32)
           + lax.dot_general(smap, p_lo, _NT, preferred_element_type=F32))
    blk = lax.broadcasted_iota(jnp.int32, (SEL_FEATS, TQ), 0)
    t_s = i * TQ + lax.broadcasted_iota(jnp.int32, (SEL_FEATS, TQ), 1)
    cur = t_s // SEL_BLOCK
    forced = (blk == 0) | (blk == cur) | (blk == cur - 1)
    visible = blk * SEL_BLOCK <= t_s
    score = jnp.where(visible, imp + jnp.where(forced, FORCE_SCORE, 0.0), -1.0)
    rank = jnp.zeros((SEL_FEATS, TQ), jnp.int32)
    for sp in range(SEL_FEATS):
        row = score[sp:sp + 1, :]
        beats = (row > score) | ((row == score) & (blk > sp))
        rank = rank + beats.astype(jnp.int32)
    pen_t = jnp.where(rank < SEL_TOPK, 0.0, NEG_INF)
    pen = jnp.transpose(pen_t).astype(BF16)
    q_aug = jnp.concatenate([qb, jnp.concatenate([pen] * HEADS_PER_GROUP, axis=0)], axis=1)

    ii = lax.broadcasted_iota(jnp.int32, (rows, TQ), 0) % TQ
    jj = lax.broadcasted_iota(jnp.int32, (rows, TQ), 1)
    causal = jj <= ii

    def reset():
        m_sc[...] = jnp.full((rows, 1), NEG_INF, F32)
        l_sc[...] = jnp.zeros((rows, 1), F32)
        acc_sc[...] = jnp.zeros((rows, HEAD_DIM), F32)

    def tile(ref, kt):
        return ref[0, 0, pl.ds(pl.multiple_of(kt * TQ, TQ), TQ), :]

    reset()

    def sel_body(kt, carry):
        _flash_step(q_aug, tile(ksel_ref, kt), tile(vsel_ref, kt), m_sc, l_sc, acc_sc)
        return carry

    lax.fori_loop(0, i - 1, sel_body, 0)

    @pl.when(i >= 1)
    def _():
        _flash_step(q_aug, tile(ksel_ref, i - 1), tile(vsel_ref, i - 1), m_sc, l_sc, acc_sc, bias=bt_ref[0, 1])

    _flash_step(q_aug, tile(ksel_ref, i), tile(vsel_ref, i), m_sc, l_sc, acc_sc, bias=bt_ref[0, 0], mask=causal)
    o_sel = acc_sc[...] / l_sc[...]

    reset()
    n_win = WINDOW // TQ

    @pl.when(i >= n_win)
    def _():
        _flash_step(qb, tile(kwin_ref, i - n_win), tile(vwin_ref, i - n_win), m_sc, l_sc, acc_sc, mask=jj > ii)

    for back in range(n_win - 1, 1, -1):
        @pl.when(i >= back)
        def _(back=back):
            _flash_step(qb, tile(kwin_ref, i - back), tile(vwin_ref, i - back), m_sc, l_sc, acc_sc)

    @pl.when(i >= 1)
    def _():
        _flash_step(qb, tile(kwin_ref, i - 1), tile(vwin_ref, i - 1), m_sc, l_sc, acc_sc, bias=bt_ref[0, 1])

    _flash_step(qb, tile(kwin_ref, i), tile(vwin_ref, i), m_sc, l_sc, acc_sc, bias=bt_ref[0, 0], mask=causal)
    o_win = acc_sc[...] / l_sc[...]

    gate = gate_ref[0]
    outs = []
    for j in range(HEADS_PER_GROUP):
        sl = slice(j * TQ, (j + 1) * TQ)
        outs.append(gate[:, 3 * j:3 * j + 1] * o_cmp[sl] + gate[:, 3 * j + 1:3 * j + 2] * o_sel[sl]
                    + gate[:, 3 * j + 2:3 * j + 3] * o_win[sl])
    o_ref[0] = jnp.concatenate(outs, axis=1)


def _attention(z_q, z_gate, q_norm_g, kc, vc, ksel, vsel, kwin, vwin, rel_bias):
    b_, s_, _ = z_q.shape
    ncp = kc.shape[2]
    n_cmp = ncp - CMP_BLOCK // CMP_STRIDE + 1
    assert s_ // SEL_BLOCK <= SEL_FEATS and s_ % TQ == 0
    bt, bc = _bias_tiles(rel_bias, ncp)
    smap_t = jnp.asarray(_sel_mapping_t(ncp, n_cmp), BF16)
    rows = HEADS_PER_GROUP * TQ
    gw = HEADS_PER_GROUP * HEAD_DIM
    kv = lambda w: pl.BlockSpec((1, 1, s_, w), lambda b, g, i: (b, g, 0, 0))
    cm = pl.BlockSpec((1, 1, ncp, HEAD_DIM), lambda b, g, i: (b, g, 0, 0))
    return pl.pallas_call(
        _attn_kernel,
        grid=(b_, N_KV_GROUPS, s_ // TQ),
        in_specs=[
            pl.BlockSpec((1, TQ, gw), lambda b, g, i: (b, i, g)),
            pl.BlockSpec((1, TQ, LANES), lambda b, g, i: (b, i, g)),
            pl.BlockSpec((1, HEAD_DIM), lambda b, g, i: (0, 0)),
            cm, cm, kv(HEAD_DIM + SEL_FEATS), kv(HEAD_DIM), kv(HEAD_DIM), kv(HEAD_DIM),
            pl.BlockSpec((1, 2, rows, TQ), lambda b, g, i: (g, 0, 0, 0)),
            pl.BlockSpec((1, rows, ncp), lambda b, g, i: (g, 0, 0)),
            pl.BlockSpec((SEL_FEATS, ncp), lambda b, g, i: (0, 0)),
        ],
        out_specs=pl.BlockSpec((1, TQ, gw), lambda b, g, i: (b, i, g)),
        out_shape=jax.ShapeDtypeStruct((b_, s_, N_KV_GROUPS * gw), F32),
        scratch_shapes=[pltpu.VMEM((rows, 1), F32), pltpu.VMEM((rows, 1), F32), pltpu.VMEM((rows, HEAD_DIM), F32)],
        compiler_params=pltpu.CompilerParams(
            dimension_semantics=("parallel", "parallel", "arbitrary"), vmem_limit_bytes=VMEM_LIMIT),
    )(z_q, z_gate, q_norm_g.reshape(1, HEAD_DIM), kc, vc, ksel, vsel, kwin, vwin, bt, bc, smap_t)


def _merge_kernel(x_ref, yp_ref, ya_ref, zm_ref, g1_ref, wbp_ref, wba_ref, wo_ref, o_ref):
    d_ = x_ref.shape[-1]
    bp = jnp.dot(yp_ref[0].astype(BF16), wbp_ref[...], preferred_element_type=F32)
    ba = jnp.dot(ya_ref[0].astype(BF16), wba_ref[...], preferred_element_type=F32)
    zm = zm_ref[0]
    mixed = jax.nn.sigmoid(zm[:, :d_]) * bp + jax.nn.sigmoid(zm[:, d_:]) * ba
    proj = jnp.dot(mixed.astype(BF16), wo_ref[...], preferred_element_type=F32)
    o_ref[0] = x_ref[0] + g1_ref[0] * proj


def _merge(x, y_pool, y_attn, z_merge, gate1, wbp, wba, wo):
    b_, s_, d_ = x.shape
    row = lambda w: pl.BlockSpec((1, ROW_TILE, w), lambda b, i: (b, i, 0))
    full = lambda a: pl.BlockSpec(a.shape, lambda b, i: (0, 0))
    return pl.pallas_call(
        _merge_kernel,
        grid=(b_, s_ // ROW_TILE),
        in_specs=[row(d_), row(POOL_WIDTH), row(Q_WIDTH), row(2 * d_),
                  pl.BlockSpec((1, 1, d_), lambda b, i: (b, 0, 0)),
                  full(wbp), full(wba), full(wo)],
        out_specs=row(d_),
        out_shape=jax.ShapeDtypeStruct((b_, s_, d_), F32),
        compiler_params=pltpu.CompilerParams(
            dimension_semantics=("parallel", "parallel"), vmem_limit_bytes=VMEM_LIMIT),
    )(x, y_pool, y_attn, z_merge, gate1, wbp, wba, wo)


_CAND_ROWS = PEER_TOPK + 7 * 8 + 8


def _topk_rows(x, k):
    n = x.shape[0]
    rid = lax.broadcasted_iota(jnp.int32, x.shape, 0)
    vals, idxs = [], []
    for _ in range(k):
        m = jnp.max(x, axis=0, keepdims=True)
        idx = jnp.min(jnp.where(x == m, rid, n), axis=0, keepdims=True)
        vals.append(m)
        idxs.append(idx)
        x = jnp.where(rid == idx, -jnp.inf, x)
    return jnp.concatenate(vals, axis=0), jnp.concatenate(idxs, axis=0)


def _pair_grid(r0, r1, combine):
    parts = [combine(r0[0:1], r1)]
    parts += [combine(r0[a:a + 1], r1[0:8]) for a in range(1, 8)]
    parts.append(combine(r0[8:16], r1[0:1]))
    return jnp.concatenate(parts, axis=0)


def _peer_route_kernel(x_ref, g_ref, sc_ref, sh_ref, w_ref, keys_ref, h_ref, eid_ref, gw_ref):
    x = x_ref[0]
    y = x * lax.rsqrt(jnp.mean(x * x, axis=-1, keepdims=True) + EPS)
    h = y * g_ref[...] * (1.0 + sc_ref[0]) + sh_ref[0]
    h_ref[0] = h
    qv = jnp.dot(h.astype(BF16), w_ref[...], preferred_element_type=F32).astype(BF16)
    half = PEER_QDIM // 2
    eids, gws = [], []
    for hd in range(PEER_HEADS):
        tops = []
        for p in range(2):
            c = hd * 2 + p
            sc = lax.dot_general(keys_ref[c], qv[:, c * half:(c + 1) * half], _NT,
                                 preferred_element_type=F32)
            tops.append(_topk_rows(sc, PEER_TOPK))
        (s0, i0), (s1, i1) = tops
        cand = _pair_grid(s0, s1, lambda a, b: a + b)
        cid = _pair_grid(i0, i1, lambda a, b: a * PEER_KEYS + b)
        rid = lax.broadcasted_iota(jnp.int32, cand.shape, 0)
        top_s, top_e = [], []
        for _ in range(PEER_TOPK):
            m = jnp.max(cand, axis=0, keepdims=True)
            idx = jnp.min(jnp.where(cand == m, rid, _CAND_ROWS), axis=0, keepdims=True)
            hit = rid == idx
            top_s.append(m)
            top_e.append(jnp.sum(jnp.where(hit, cid, 0), axis=0, keepdims=True))
            cand = jnp.where(hit, -jnp.inf, cand)
        top_s = jnp.concatenate(top_s, axis=0)
        e = jnp.exp(top_s - top_s[0:1])
        gws.append(e / jnp.sum(e, axis=0, keepdims=True))
        eids.append(jnp.concatenate(top_e, axis=0))
    eid_ref[0] = jnp.transpose(jnp.concatenate(eids, axis=0))
    gw_ref[0] = jnp.transpose(jnp.concatenate(gws, axis=0))


def _peer_route(x, g, scale, shift, w_bf16, keys_bf16):
    b_, s_, d_ = x.shape
    n = w_bf16.shape[1]
    nk = PEER_HEADS * PEER_TOPK
    row = lambda w: pl.BlockSpec((1, PEER_TILE, w), lambda b, i: (b, i, 0))
    return pl.pallas_call(
        _peer_route_kernel,
        grid=(b_, s_ // PEER_TILE),
        in_specs=[
            row(d_),
            pl.BlockSpec((1, d_), lambda b, i: (0, 0)),
            pl.BlockSpec((1, 1, d_), lambda b, i: (b, 0, 0)),
            pl.BlockSpec((1, 1, d_), lambda b, i: (b, 0, 0)),
            pl.BlockSpec((d_, n), lambda b, i: (0, 0)),
            pl.BlockSpec(keys_bf16.shape, lambda b, i: (0, 0, 0)),
        ],
        out_specs=[row(d_), row(nk), row(nk)],
        out_shape=[
            jax.ShapeDtypeStruct((b_, s_, d_), F32),
            jax.ShapeDtypeStruct((b_, s_, nk), jnp.int32),
            jax.ShapeDtypeStruct((b_, s_, nk), F32),
        ],
        compiler_params=pltpu.CompilerParams(
            dimension_semantics=("parallel", "parallel"), vmem_limit_bytes=VMEM_LIMIT),
    )(x, g.reshape(1, d_), scale, shift, w_bf16, keys_bf16)


def _fold8(p, sub):
    m4, m2, m1 = (sub & 4) == 0, (sub & 2) == 0, (sub & 1) == 0
    t = [jnp.where(m4, p[i], p[i + 4]) + pltpu.roll(jnp.where(m4, p[i + 4], p[i]), 4, axis=0) for i in range(4)]
    u = [jnp.where(m2, t[i], t[i + 2]) + jnp.where(m2, pltpu.roll(t[i], 6, axis=0), pltpu.roll(t[i + 2], 2, axis=0))
         for i in range(2)]
    return jnp.where(m1, u[0], u[1]) + jnp.where(m1, pltpu.roll(u[0], 7, axis=0), pltpu.roll(u[1], 1, axis=0))


def _peer_expert_kernel(eid_cur_ref, eid_nxt_ref, gw_ref, h_ref, x_ref, g2_ref, uv_ref, o_ref,
                        idx_smem, buf, wb_ref, sem, isem):
    j = pl.program_id(0)
    nsteps = pl.num_programs(0)
    nk = PEER_HEADS * PEER_TOPK
    rows = PEER_TOKENS * nk
    sub_rows = D_MODEL // LANES

    def issue(eid_vmem_ref, slot):
        cp = pltpu.make_async_copy(eid_vmem_ref, idx_smem, isem)
        cp.start()
        cp.wait()
        for t in range(PEER_TOKENS):
            for k in range(nk):
                pltpu.make_async_copy(uv_ref.at[idx_smem[t, k]], buf.at[slot, t * nk + k], sem.at[slot]).start()

    @pl.when(j == 0)
    def _():
        issue(eid_cur_ref, 0)

    slot = j % 2
    pltpu.make_async_copy(uv_ref.at[pl.ds(0, rows)], buf.at[slot], sem.at[slot]).wait()

    @pl.when(j + 1 < nsteps)
    def _():
        issue(eid_nxt_ref, 1 - slot)

    sub = lax.broadcasted_iota(jnp.int32, (sub_rows, LANES), 0)
    gw_t = jnp.transpose(gw_ref[...])
    for t in range(PEER_TOKENS):
        xt = h_ref[t]
        folded = []
        for gi in range(nk // sub_rows):
            prods = [buf[slot, t * nk + gi * sub_rows + i, 0:sub_rows, :] * xt for i in range(sub_rows)]
            folded.append(_fold8(prods, sub))
        a = jnp.sum(jnp.concatenate(folded, axis=0), axis=-1, keepdims=True)
        w = jax.nn.gelu(a) * gw_t[:, t:t + 1]
        wb_ref[...] = jnp.broadcast_to(w, (nk, LANES))
        acc = jnp.zeros((sub_rows, LANES), F32)
        for k in range(nk):
            acc = acc + wb_ref[k:k + 1, :] * buf[slot, t * nk + k, sub_rows:2 * sub_rows, :]
        o_ref[t] = x_ref[t] + g2_ref[0] * acc


def _peer_experts(eid, gw, h2, x1, gate2, uv):
    b_, s_, d_ = x1.shape
    n_tok = b_ * s_
    nk = PEER_HEADS * PEER_TOPK
    sub_rows = d_ // LANES
    nsteps = n_tok // PEER_TOKENS
    steps_per_batch = s_ // PEER_TOKENS
    tok = lambda: pl.BlockSpec((PEER_TOKENS, sub_rows, LANES), lambda j: (j, 0, 0))
    out = pl.pallas_call(
        _peer_expert_kernel,
        grid=(nsteps,),
        in_specs=[
            pl.BlockSpec((PEER_TOKENS, nk), lambda j: (j, 0)),
            pl.BlockSpec((PEER_TOKENS, nk), lambda j: (jnp.minimum(j + 1, nsteps - 1), 0)),
            pl.BlockSpec((PEER_TOKENS, nk), lambda j: (j, 0)),
            tok(), tok(),
            pl.BlockSpec((1, sub_rows, LANES), lambda j: (j // steps_per_batch, 0, 0)),
            pl.BlockSpec(memory_space=pl.ANY),
        ],
        out_specs=tok(),
        out_shape=jax.ShapeDtypeStruct((n_tok, sub_rows, LANES), F32),
        scratch_shapes=[
            pltpu.SMEM((PEER_TOKENS, nk), jnp.int32),
            pltpu.VMEM((2, PEER_TOKENS * nk, 2 * sub_rows, LANES), F32),
            pltpu.VMEM((nk, LANES), F32),
            pltpu.SemaphoreType.DMA((2,)),
            pltpu.SemaphoreType.DMA,
        ],
        compiler_params=pltpu.CompilerParams(
            dimension_semantics=("arbitrary",), vmem_limit_bytes=VMEM_LIMIT),
    )(eid.reshape(n_tok, nk), eid.reshape(n_tok, nk), gw.reshape(n_tok, nk),
      h2.reshape(n_tok, sub_rows, LANES), x1.reshape(n_tok, sub_rows, LANES),
      gate2.reshape(b_, sub_rows, LANES), uv)
    return out.reshape(b_, s_, d_)


def _pool_mixer(u, w_pool, pool_scale):
    b_, s_, c_ = u.shape
    csp = jnp.concatenate([jnp.zeros((b_, 1, c_), jnp.float32), jnp.cumsum(u, axis=1)], axis=1)
    t = jnp.arange(s_)
    outs = []
    for gi, w in enumerate(POOL_WINDOWS):
        sl = slice(gi * POOL_GROUP_DIM, (gi + 1) * POOL_GROUP_DIM)
        cg = csp[:, :, sl]
        upper = cg[:, 1:]
        lower = jnp.pad(cg[:, :s_ + 1 - w], ((0, 0), (w - 1, 0), (0, 0)))
        cnt = jnp.minimum(t + 1, w).astype(jnp.float32)[:, None]
        outs.append((upper - lower) / cnt - u[:, :, sl])
    pooled = jnp.stack(outs, axis=2)
    y = jnp.einsum('bsgc,gcd->bsgd', pooled, w_pool).reshape(b_, s_, POOL_WIDTH)
    return y * pool_scale


def _permute_w_in(w):
    d_ = w.shape[0]
    kv_end = POOL_WIDTH + Q_WIDTH + 6 * KV_WIDTH
    per_group = HEADS_PER_GROUP * 3
    parts = [w[:, :kv_end], w[:, kv_end + GATE_COLS:]]
    for g in range(N_KV_GROUPS):
        parts.append(w[:, kv_end + g * per_group:kv_end + (g + 1) * per_group])
        parts.append(jnp.zeros((d_, LANES - per_group), w.dtype))
    return jnp.concatenate(parts, axis=1).astype(BF16)


def kernel(x, c, rel_bias, ada_w, ada_b, norm1_g, norm2_g, w_in, pool_w, pool_scale, cmp_pe_k, cmp_w1_k, cmp_w2_k, cmp_pe_v, cmp_w1_v, cmp_w2_v, q_norm_g, k_norm_g, w_branch_pool, w_branch_attn, w_out, peer_w_q, peer_sub_keys, peer_u, peer_v):
    b_, s_, d_ = x.shape
    l = 0
    ada = (jax.nn.silu(c) @ ada_w[l] + ada_b[l]).reshape(b_, 6, 1, d_)
    shift1, scale1, gate1 = ada[:, 0], ada[:, 1], ada[:, 2]
    shift2, scale2, gate2 = ada[:, 3], ada[:, 4], ada[:, 5]

    (z_pool, z_q, ksel, vsel, kwin, vwin, kc_raw, vc_raw, z_merge, z_gate) = _in_proj(
        x, norm1_g[l], scale1, shift1, _permute_w_in(w_in[l]), k_norm_g[l])

    y_pool = _pool_mixer(z_pool, pool_w[l], pool_scale[l])
    kc = _compress(kc_raw, cmp_pe_k[l], cmp_w1_k[l], cmp_w2_k[l], k_norm_g[l, 0], True)
    vc = _compress(vc_raw, cmp_pe_v[l], cmp_w1_v[l], cmp_w2_v[l], k_norm_g[l, 0], False)
    y_attn = _attention(z_q, z_gate, q_norm_g[l], kc, vc, ksel, vsel, kwin, vwin, rel_bias)

    x1 = _merge(x, y_pool, y_attn, z_merge, gate1,
                w_branch_pool[l].astype(BF16), w_branch_attn[l].astype(BF16), w_out[l].astype(BF16))

    keys = peer_sub_keys[l].reshape(PEER_HEADS * 2, PEER_KEYS, PEER_QDIM // 2).astype(BF16)
    h2, eid, gw = _peer_route(x1, norm2_g[l], scale2, shift2, peer_w_q[l].astype(BF16), keys)
    uv = jnp.concatenate([peer_u[l], peer_v[l]], axis=1).reshape(-1, 2 * d_ // LANES, LANES)
    return _peer_experts(eid, gw, h2, x1, gate2, uv)
```

```python
import math
from functools import partial

import jax
import jax.numpy as jnp
import numpy as np
from jax import lax
from jax.experimental import pallas as pl
from jax.experimental.pallas import tpu as pltpu

D_MODEL = 1024
POOL_WIDTH = 512
POOL_WINDOWS = (2, 4, 8, 16)
POOL_GROUP_DIM = 128
N_HEADS = 8
N_KV_GROUPS = 2
HEADS_PER_GROUP = 4
HEAD_DIM = 64
Q_WIDTH = 512
KV_WIDTH = 128
CMP_BLOCK = 32
CMP_STRIDE = 16
CMP_HIDDEN = 128
SEL_BLOCK = 64
SEL_TOPK = 16
WINDOW = 512
FORCE_SCORE = 1000.0
N_BUCKETS = 32
MAX_DISTANCE = 128
PEER_HEADS = 8
PEER_KEYS = 128
PEER_TOPK = 16
PEER_QDIM = 256
PEER_CHUNK = 128
GATE_COLS = N_HEADS * 3
EPS = 1e-6
NEG_INF = -1e30

LANES = 128
VMEM_LIMIT = 48 * 1024 * 1024
ROW_TILE = 512
TQ = 128
SEL_FEATS = 64
CMP_BAND_LO = 10
PEER_TILE = 256
PEER_EXPERT_TILE = 128
PEER_VMEM_LIMIT = 56 * 1024 * 1024

BF16 = jnp.bfloat16
F32 = jnp.float32
_NT = (((1,), (1,)), ((), ()))


def _rms_rows(x, g):
    return x * lax.rsqrt(jnp.mean(x * x, axis=-1, keepdims=True) + EPS) * g


def _in_proj_kernel(x_ref, g_ref, sc_ref, sh_ref, w_ref, kg_ref,
                    zpool_ref, zq_ref, ksel_ref, vsel_ref, kwin_ref, vwin_ref, kcr_ref, vcr_ref,
                    zmerge_ref, zgate_ref):
    i = pl.program_id(1)
    x = x_ref[0]
    y = x * lax.rsqrt(jnp.mean(x * x, axis=-1, keepdims=True) + EPS)
    h = y * g_ref[...] * (1.0 + sc_ref[0]) + sh_ref[0]
    z = jnp.dot(h.astype(BF16), w_ref[...], preferred_element_type=F32)
    ts = x.shape[0]
    zpool_ref[0] = z[:, :POOL_WIDTH]
    o = POOL_WIDTH
    zq_ref[0] = z[:, o:o + Q_WIDTH]
    o += Q_WIDTH
    pos = i * ts + lax.broadcasted_iota(jnp.int32, (ts, SEL_FEATS), 0)
    onehot = (pos // SEL_BLOCK == lax.broadcasted_iota(jnp.int32, (ts, SEL_FEATS), 1)).astype(BF16)
    for g in range(N_KV_GROUPS):
        def col(k):
            return z[:, o + k * KV_WIDTH + g * HEAD_DIM:o + k * KV_WIDTH + (g + 1) * HEAD_DIM]
        kcr_ref[0, g] = col(0)
        vcr_ref[0, g] = col(1)
        ks = _rms_rows(col(2), kg_ref[1:2, :]).astype(BF16)
        ksel_ref[0, g] = jnp.concatenate([ks, onehot], axis=1)
        vsel_ref[0, g] = col(3).astype(BF16)
        kwin_ref[0, g] = _rms_rows(col(4), kg_ref[2:3, :]).astype(BF16)
        vwin_ref[0, g] = col(5).astype(BF16)
    o += 6 * KV_WIDTH
    zmerge_ref[0] = z[:, o:o + 2 * D_MODEL]
    o += 2 * D_MODEL
    zgate_ref[0] = jax.nn.sigmoid(z[:, o:o + N_KV_GROUPS * LANES])


def _in_proj(x, g, scale, shift, w_bf16, k_norm_g):
    b_, s_, d_ = x.shape
    n = w_bf16.shape[1]
    row = lambda w: pl.BlockSpec((1, ROW_TILE, w), lambda b, i: (b, i, 0))
    grp = lambda w: pl.BlockSpec((1, N_KV_GROUPS, ROW_TILE, w), lambda b, i: (b, 0, i, 0))
    gshape = lambda w, dt: jax.ShapeDtypeStruct((b_, N_KV_GROUPS, s_, w), dt)
    return pl.pallas_call(
        _in_proj_kernel,
        grid=(b_, s_ // ROW_TILE),
        in_specs=[
            row(d_),
            pl.BlockSpec((1, d_), lambda b, i: (0, 0)),
            pl.BlockSpec((1, 1, d_), lambda b, i: (b, 0, 0)),
            pl.BlockSpec((1, 1, d_), lambda b, i: (b, 0, 0)),
            pl.BlockSpec((d_, n), lambda b, i: (0, 0)),
            pl.BlockSpec((3, HEAD_DIM), lambda b, i: (0, 0)),
        ],
        out_specs=[row(POOL_WIDTH), row(Q_WIDTH), grp(HEAD_DIM + SEL_FEATS), grp(HEAD_DIM), grp(HEAD_DIM),
                   grp(HEAD_DIM), grp(HEAD_DIM), grp(HEAD_DIM), row(2 * d_), row(N_KV_GROUPS * LANES)],
        out_shape=[
            jax.ShapeDtypeStruct((b_, s_, POOL_WIDTH), F32),
            jax.ShapeDtypeStruct((b_, s_, Q_WIDTH), F32),
            gshape(HEAD_DIM + SEL_FEATS, BF16), gshape(HEAD_DIM, BF16), gshape(HEAD_DIM, BF16), gshape(HEAD_DIM, BF16),
            gshape(HEAD_DIM, F32), gshape(HEAD_DIM, F32),
            jax.ShapeDtypeStruct((b_, s_, 2 * d_), F32),
            jax.ShapeDtypeStruct((b_, s_, N_KV_GROUPS * LANES), F32),
        ],
        compiler_params=pltpu.CompilerParams(
            dimension_semantics=("parallel", "parallel"), vmem_limit_bytes=VMEM_LIMIT),
    )(x, g.reshape(1, d_), scale, shift, w_bf16, k_norm_g)


def _compress_kernel(t_ref, pe_ref, w1_ref, w2_ref, g_ref, o_ref, *, normalize):
    half = CMP_STRIDE * HEAD_DIM
    t = t_ref[0, 0].astype(BF16)
    w1 = w1_ref[...]
    a = jnp.dot(t, w1[:half], preferred_element_type=F32)
    b = jnp.dot(t, w1[half:], preferred_element_type=F32)
    pe = jnp.dot(pe_ref[...], w1, preferred_element_type=F32)
    n = a.shape[0]
    b_next = pltpu.roll(b, n - 1, axis=0)
    hid = jax.nn.gelu(a + b_next + pe)
    out = jnp.dot(hid.astype(BF16), w2_ref[...], preferred_element_type=F32)
    if normalize:
        out = _rms_rows(out, g_ref[...])
    o_ref[0, 0] = out.astype(BF16)


def _compress(t_raw, pe, w1, w2, g, normalize):
    b_, g_, s_, dh = t_raw.shape
    n_str = s_ // CMP_STRIDE
    t = t_raw.reshape(b_, g_, n_str, CMP_STRIDE * dh)
    return pl.pallas_call(
        partial(_compress_kernel, normalize=normalize),
        grid=(b_, g_),
        in_specs=[
            pl.BlockSpec((1, 1, n_str, CMP_STRIDE * dh), lambda b, g: (b, g, 0, 0)),
            pl.BlockSpec((1, CMP_BLOCK * dh), lambda b, g: (0, 0)),
            pl.BlockSpec((CMP_BLOCK * dh, CMP_HIDDEN), lambda b, g: (0, 0)),
            pl.BlockSpec((CMP_HIDDEN, dh), lambda b, g: (0, 0)),
            pl.BlockSpec((1, dh), lambda b, g: (0, 0)),
        ],
        out_specs=pl.BlockSpec((1, 1, n_str, dh), lambda b, g: (b, g, 0, 0)),
        out_shape=jax.ShapeDtypeStruct((b_, g_, n_str, dh), BF16),
        compiler_params=pltpu.CompilerParams(
            dimension_semantics=("parallel", "parallel"), vmem_limit_bytes=VMEM_LIMIT),
    )(t, pe.reshape(1, CMP_BLOCK * dh).astype(BF16), w1.astype(BF16), w2.astype(BF16), g.reshape(1, dh))


def _t5_bucket_np(rel):
    n = np.maximum(rel, 0)
    max_exact = N_BUCKETS // 2
    nf = np.maximum(n, 1).astype(np.float32)
    large = max_exact + (np.log(nf / max_exact) / math.log(MAX_DISTANCE / max_exact) * (N_BUCKETS - max_exact)).astype(np.int32)
    large = np.minimum(large, N_BUCKETS - 1)
    return np.where(n < max_exact, n, large)


def _bias_tiles(rel_bias, n_cmp_pad):
    far = rel_bias[N_BUCKETS - 1]
    ii = np.arange(TQ)[:, None]

    def lookup(dist):
        near = (dist >= 0) & (dist < MAX_DISTANCE)
        bucket = _t5_bucket_np(np.where(near, dist, MAX_DISTANCE))
        b = rel_bias[bucket] - far
        b = jnp.transpose(b, (2, 0, 1))
        return b.reshape(N_KV_GROUPS, HEADS_PER_GROUP * dist.shape[0], dist.shape[1])

    jj = np.arange(TQ)[None, :]
    bt = jnp.stack([lookup(ii - jj), lookup(TQ + ii - jj)], axis=1)
    cc = np.arange(n_cmp_pad)[None, :]
    dist_c = ii - CMP_STRIDE * (cc - CMP_BAND_LO) - (CMP_BLOCK - 1)
    dist_c = np.where(cc <= CMP_BAND_LO + TQ // CMP_STRIDE, dist_c, -1)
    bc = lookup(dist_c)
    return bt, bc


def _sel_mapping_t(n_cmp_pad, n_cmp):
    m = np.zeros((SEL_FEATS, n_cmp_pad), np.float32)
    pos = np.arange(n_cmp)[:, None] * CMP_STRIDE + np.arange(CMP_BLOCK)[None, :]
    np.add.at(m, ((pos // SEL_BLOCK).ravel(), np.repeat(np.arange(n_cmp), CMP_BLOCK)), 1.0 / CMP_BLOCK)
    return m


def _flash_step(q, k, v, m_sc, l_sc, acc_sc, bias=None, mask=None):
    s = lax.dot_general(q, k, _NT, preferred_element_type=F32)
    if bias is not None:
        s = s + bias
    if mask is not None:
        s = jnp.where(mask, s, NEG_INF)
    m_prev = m_sc[...]
    m_new = jnp.maximum(m_prev, jnp.max(s, axis=-1, keepdims=True))
    alpha = jnp.exp(m_prev - m_new)
    p = jnp.exp(s - m_new)
    l_sc[...] = alpha * l_sc[...] + jnp.sum(p, axis=-1, keepdims=True)
    acc_sc[...] = alpha * acc_sc[...] + jnp.dot(p.astype(BF16), v, preferred_element_type=F32)
    m_sc[...] = m_new


def _attn_kernel(zq_ref, gate_ref, qg_ref, kc_ref, vc_ref, ksel_ref, vsel_ref, kwin_ref, vwin_ref,
                 bt_ref, bc_ref, smap_ref, o_ref, m_sc, l_sc, acc_sc):
    i = pl.program_id(2)
    rows = HEADS_PER_GROUP * TQ
    ncp = kc_ref.shape[2]
    scale = HEAD_DIM ** -0.5

    zq = zq_ref[0]
    qs = jnp.concatenate([zq[:, j * HEAD_DIM:(j + 1) * HEAD_DIM] for j in range(HEADS_PER_GROUP)], axis=0)
    qn = _rms_rows(qs, qg_ref[...]) * scale
    qb = qn.astype(BF16)

    lc = lax.dot_general(qb, kc_ref[0, 0], _NT, preferred_element_type=F32)
    lc = lc + pltpu.roll(bc_ref[0], (i * (TQ // CMP_STRIDE) + ncp - CMP_BAND_LO) % ncp, axis=1)
    t_c = i * TQ + lax.broadcasted_iota(jnp.int32, (rows, ncp), 0) % TQ
    n_c = lax.broadcasted_iota(jnp.int32, (rows, ncp), 1)
    valid_c = n_c * CMP_STRIDE + (CMP_BLOCK - 1) <= t_c
    lm = jnp.where(valid_c, lc, NEG_INF)
    e = jnp.where(valid_c, jnp.exp(lm - jnp.max(lm, axis=-1, keepdims=True)), 0.0)
    den = jnp.sum(e, axis=-1, keepdims=True)
    p_c = e / jnp.where(den > 0.0, den, 1.0)
    o_cmp = jnp.dot(p_c.astype(BF16), vc_ref[0, 0], preferred_element_type=F32)

    p_sum = p_c[0:TQ] + p_c[TQ:2 * TQ] + p_c[2 * TQ:3 * TQ] + p_c[3 * TQ:4 * TQ]
    p_hi = p_sum.astype(BF16)
    r1 = p_sum - p_hi.astype(F32)
    p_mid = r1.astype(BF16)
    p_lo = (r1 - p_mid.astype(F32)).astype(BF16)
    smap = smap_ref[...]
    imp = (lax.dot_general(smap, p_hi, _NT, preferred_element_type=F32)
           + lax.dot_general(smap, p_mid, _NT, preferred_element_type=F32)
           + lax.dot_general(smap, p_lo, _NT, preferred_element_type=F32))
    blk = lax.broadcasted_iota(jnp.int32, (SEL_FEATS, TQ), 0)
    t_s = i * TQ + lax.broadcasted_iota(jnp.int32, (SEL_FEATS, TQ), 1)
    cur = t_s // SEL_BLOCK
    forced = (blk == 0) | (blk == cur) | (blk == cur - 1)
    visible = blk * SEL_BLOCK <= t_s
    score = jnp.where(visible, imp + jnp.where(forced, FORCE_SCORE, 0.0), -1.0)
    rank = jnp.zeros((SEL_FEATS, TQ), jnp.int32)
    for sp in range(SEL_FEATS):
        row = score[sp:sp + 1, :]
        beats = (row > score) | ((row == score) & (blk > sp))
        rank = rank + beats.astype(jnp.int32)
    pen_t = jnp.where(rank < SEL_TOPK, 0.0, NEG_INF)
    pen = jnp.transpose(pen_t).astype(BF16)
    q_aug = jnp.concatenate([qb, jnp.concatenate([pen] * HEADS_PER_GROUP, axis=0)], axis=1)

    ii = lax.broadcasted_iota(jnp.int32, (rows, TQ), 0) % TQ
    jj = lax.broadcasted_iota(jnp.int32, (rows, TQ), 1)
    causal = jj <= ii

    def reset():
        m_sc[...] = jnp.full((rows, 1), NEG_INF, F32)
        l_sc[...] = jnp.zeros((rows, 1), F32)
        acc_sc[...] = jnp.zeros((rows, HEAD_DIM), F32)

    def tile(ref, kt):
        return ref[0, 0, pl.ds(pl.multiple_of(kt * TQ, TQ), TQ), :]

    reset()

    def sel_body(kt, carry):
        _flash_step(q_aug, tile(ksel_ref, kt), tile(vsel_ref, kt), m_sc, l_sc, acc_sc)
        return carry

    lax.fori_loop(0, i - 1, sel_body, 0)

    @pl.when(i >= 1)
    def _():
        _flash_step(q_aug, tile(ksel_ref, i - 1), tile(vsel_ref, i - 1), m_sc, l_sc, acc_sc, bias=bt_ref[0, 1])

    _flash_step(q_aug, tile(ksel_ref, i), tile(vsel_ref, i), m_sc, l_sc, acc_sc, bias=bt_ref[0, 0], mask=causal)
    o_sel = acc_sc[...] / l_sc[...]

    reset()
    n_win = WINDOW // TQ

    @pl.when(i >= n_win)
    def _():
        _flash_step(qb, tile(kwin_ref, i - n_win), tile(vwin_ref, i - n_win), m_sc, l_sc, acc_sc, mask=jj > ii)

    for back in range(n_win - 1, 1, -1):
        @pl.when(i >= back)
        def _(back=back):
            _flash_step(qb, tile(kwin_ref, i - back), tile(vwin_ref, i - back), m_sc, l_sc, acc_sc)

    @pl.when(i >= 1)
    def _():
        _flash_step(qb, tile(kwin_ref, i - 1), tile(vwin_ref, i - 1), m_sc, l_sc, acc_sc, bias=bt_ref[0, 1])

    _flash_step(qb, tile(kwin_ref, i), tile(vwin_ref, i), m_sc, l_sc, acc_sc, bias=bt_ref[0, 0], mask=causal)
    o_win = acc_sc[...] / l_sc[...]

    gate = gate_ref[0]
    outs = []
    for j in range(HEADS_PER_GROUP):
        sl = slice(j * TQ, (j + 1) * TQ)
        outs.append(gate[:, 3 * j:3 * j + 1] * o_cmp[sl] + gate[:, 3 * j + 1:3 * j + 2] * o_sel[sl]
                    + gate[:, 3 * j + 2:3 * j + 3] * o_win[sl])
    o_ref[0] = jnp.concatenate(outs, axis=1)


def _attention(z_q, z_gate, q_norm_g, kc, vc, ksel, vsel, kwin, vwin, rel_bias):
    b_, s_, _ = z_q.shape
    ncp = kc.shape[2]
    n_cmp = ncp - CMP_BLOCK // CMP_STRIDE + 1
    assert s_ // SEL_BLOCK <= SEL_FEATS and s_ % TQ == 0
    bt, bc = _bias_tiles(rel_bias, ncp)
    smap_t = jnp.asarray(_sel_mapping_t(ncp, n_cmp), BF16)
    rows = HEADS_PER_GROUP * TQ
    gw = HEADS_PER_GROUP * HEAD_DIM
    kv = lambda w: pl.BlockSpec((1, 1, s_, w), lambda b, g, i: (b, g, 0, 0))
    cm = pl.BlockSpec((1, 1, ncp, HEAD_DIM), lambda b, g, i: (b, g, 0, 0))
    return pl.pallas_call(
        _attn_kernel,
        grid=(b_, N_KV_GROUPS, s_ // TQ),
        in_specs=[
            pl.BlockSpec((1, TQ, gw), lambda b, g, i: (b, i, g)),
            pl.BlockSpec((1, TQ, LANES), lambda b, g, i: (b, i, g)),
            pl.BlockSpec((1, HEAD_DIM), lambda b, g, i: (0, 0)),
            cm, cm, kv(HEAD_DIM + SEL_FEATS), kv(HEAD_DIM), kv(HEAD_DIM), kv(HEAD_DIM),
            pl.BlockSpec((1, 2, rows, TQ), lambda b, g, i: (g, 0, 0, 0)),
            pl.BlockSpec((1, rows, ncp), lambda b, g, i: (g, 0, 0)),
            pl.BlockSpec((SEL_FEATS, ncp), lambda b, g, i: (0, 0)),
        ],
        out_specs=pl.BlockSpec((1, TQ, gw), lambda b, g, i: (b, i, g)),
        out_shape=jax.ShapeDtypeStruct((b_, s_, N_KV_GROUPS * gw), F32),
        scratch_shapes=[pltpu.VMEM((rows, 1), F32), pltpu.VMEM((rows, 1), F32), pltpu.VMEM((rows, HEAD_DIM), F32)],
        compiler_params=pltpu.CompilerParams(
            dimension_semantics=("parallel", "parallel", "arbitrary"), vmem_limit_bytes=VMEM_LIMIT),
    )(z_q, z_gate, q_norm_g.reshape(1, HEAD_DIM), kc, vc, ksel, vsel, kwin, vwin, bt, bc, smap_t)


def _merge_kernel(x_ref, yp_ref, ya_ref, zm_ref, g1_ref, wbp_ref, wba_ref, wo_ref, o_ref):
    d_ = x_ref.shape[-1]
    bp = jnp.dot(yp_ref[0].astype(BF16), wbp_ref[...], preferred_element_type=F32)
    ba = jnp.dot(ya_ref[0].astype(BF16), wba_ref[...], preferred_element_type=F32)
    zm = zm_ref[0]
    mixed = jax.nn.sigmoid(zm[:, :d_]) * bp + jax.nn.sigmoid(zm[:, d_:]) * ba
    proj = jnp.dot(mixed.astype(BF16), wo_ref[...], preferred_element_type=F32)
    o_ref[0] = x_ref[0] + g1_ref[0] * proj


def _merge(x, y_pool, y_attn, z_merge, gate1, wbp, wba, wo):
    b_, s_, d_ = x.shape
    row = lambda w: pl.BlockSpec((1, ROW_TILE, w), lambda b, i: (b, i, 0))
    full = lambda a: pl.BlockSpec(a.shape, lambda b, i: (0, 0))
    return pl.pallas_call(
        _merge_kernel,
        grid=(b_, s_ // ROW_TILE),
        in_specs=[row(d_), row(POOL_WIDTH), row(Q_WIDTH), row(2 * d_),
                  pl.BlockSpec((1, 1, d_), lambda b, i: (b, 0, 0)),
                  full(wbp), full(wba), full(wo)],
        out_specs=row(d_),
        out_shape=jax.ShapeDtypeStruct((b_, s_, d_), F32),
        compiler_params=pltpu.CompilerParams(
            dimension_semantics=("parallel", "parallel"), vmem_limit_bytes=VMEM_LIMIT),
    )(x, y_pool, y_attn, z_merge, gate1, wbp, wba, wo)


_CAND_ROWS = PEER_TOPK + 7 * 8 + 8


def _topk_rows(x, k):
    n = x.shape[0]
    rid = lax.broadcasted_iota(jnp.int32, x.shape, 0)
    vals, idxs = [], []
    for _ in range(k):
        m = jnp.max(x, axis=0, keepdims=True)
        idx = jnp.min(jnp.where(x == m, rid, n), axis=0, keepdims=True)
        vals.append(m)
        idxs.append(idx)
        x = jnp.where(rid == idx, -jnp.inf, x)
    return jnp.concatenate(vals, axis=0), jnp.concatenate(idxs, axis=0)


def _pair_grid(r0, r1, combine):
    parts = [combine(r0[0:1], r1)]
    parts += [combine(r0[a:a + 1], r1[0:8]) for a in range(1, 8)]
    parts.append(combine(r0[8:16], r1[0:1]))
    return jnp.concatenate(parts, axis=0)


def _peer_route_kernel(x_ref, g_ref, sc_ref, sh_ref, w_ref, keys_ref, h_ref, eid_ref, gw_ref):
    x = x_ref[0]
    y = x * lax.rsqrt(jnp.mean(x * x, axis=-1, keepdims=True) + EPS)
    h = y * g_ref[...] * (1.0 + sc_ref[0]) + sh_ref[0]
    h_ref[0] = h
    qv = jnp.dot(h.astype(BF16), w_ref[...], preferred_element_type=F32).astype(BF16)
    half = PEER_QDIM // 2
    eids, gws = [], []
    for hd in range(PEER_HEADS):
        tops = []
        for p in range(2):
            c = hd * 2 + p
            sc = lax.dot_general(keys_ref[c], qv[:, c * half:(c + 1) * half], _NT,
                                 preferred_element_type=F32)
            tops.append(_topk_rows(sc, PEER_TOPK))
        (s0, i0), (s1, i1) = tops
        cand = _pair_grid(s0, s1, lambda a, b: a + b)
        cid = _pair_grid(i0, i1, lambda a, b: a * PEER_KEYS + b)
        rid = lax.broadcasted_iota(jnp.int32, cand.shape, 0)
        top_s, top_e = [], []
        for _ in range(PEER_TOPK):
            m = jnp.max(cand, axis=0, keepdims=True)
            idx = jnp.min(jnp.where(cand == m, rid, _CAND_ROWS), axis=0, keepdims=True)
            hit = rid == idx
            top_s.append(m)
            top_e.append(jnp.sum(jnp.where(hit, cid, 0), axis=0, keepdims=True))
            cand = jnp.where(hit, -jnp.inf, cand)
        top_s = jnp.concatenate(top_s, axis=0)
        e = jnp.exp(top_s - top_s[0:1])
        gws.append(e / jnp.sum(e, axis=0, keepdims=True))
        eids.append(jnp.concatenate(top_e, axis=0))
    eid_ref[0] = jnp.transpose(jnp.concatenate(eids, axis=0))
    gw_ref[0] = jnp.transpose(jnp.concatenate(gws, axis=0))


def _peer_route(x, g, scale, shift, w_bf16, keys_bf16):
    b_, s_, d_ = x.shape
    n = w_bf16.shape[1]
    nk = PEER_HEADS * PEER_TOPK
    row = lambda w: pl.BlockSpec((1, PEER_TILE, w), lambda b, i: (b, i, 0))
    return pl.pallas_call(
        _peer_route_kernel,
        grid=(b_, s_ // PEER_TILE),
        in_specs=[
            row(d_),
            pl.BlockSpec((1, d_), lambda b, i: (0, 0)),
            pl.BlockSpec((1, 1, d_), lambda b, i: (b, 0, 0)),
            pl.BlockSpec((1, 1, d_), lambda b, i: (b, 0, 0)),
            pl.BlockSpec((d_, n), lambda b, i: (0, 0)),
            pl.BlockSpec(keys_bf16.shape, lambda b, i: (0, 0, 0)),
        ],
        out_specs=[row(d_), row(nk), row(nk)],
        out_shape=[
            jax.ShapeDtypeStruct((b_, s_, d_), F32),
            jax.ShapeDtypeStruct((b_, s_, nk), jnp.int32),
            jax.ShapeDtypeStruct((b_, s_, nk), F32),
        ],
        compiler_params=pltpu.CompilerParams(
            dimension_semantics=("parallel", "parallel"), vmem_limit_bytes=VMEM_LIMIT),
    )(x, g.reshape(1, d_), scale, shift, w_bf16, keys_bf16)


_HI_MASK = -65536


def _pack_table(t):
    e_, d_ = t.shape
    bits = lax.bitcast_convert_type(t.astype(jnp.bfloat16), jnp.uint16).astype(jnp.uint32)
    words = (bits[:, d_ // 2:] << 16) | bits[:, :d_ // 2]
    return lax.bitcast_convert_type(words, jnp.int32).reshape(e_, d_ // 2 // LANES, LANES)


def _unpack(words):
    lo = lax.bitcast_convert_type(lax.shift_left(words, 16), F32)
    hi = lax.bitcast_convert_type(words & _HI_MASK, F32)
    return lo, hi


def _load_resident(j, tab_hbm, tab_vmem, eid_ref, idx_smem, sem):
    @pl.when(j == 0)
    def _():
        cp = pltpu.make_async_copy(tab_hbm, tab_vmem, sem)
        cp.start()
        cp.wait()

    cp = pltpu.make_async_copy(eid_ref, idx_smem, sem)
    cp.start()
    cp.wait()


def _fold_pairs(q, sub):
    m2, m1 = (sub & 2) == 0, (sub & 1) == 0
    u = [jnp.where(m2, q[i], q[i + 2]) + jnp.where(m2, pltpu.roll(q[i], 6, axis=0), pltpu.roll(q[i + 2], 2, axis=0))
         for i in range(2)]
    return jnp.where(m1, u[0], u[1]) + jnp.where(m1, pltpu.roll(u[0], 7, axis=0), pltpu.roll(u[1], 1, axis=0))


def _peer_u_kernel(eid_ref, gw_ref, h_ref, tab_hbm, w_ref, idx_smem, tab_vmem, acc_ref, sem):
    j = pl.program_id(0)
    nk = PEER_HEADS * PEER_TOPK
    half = D_MODEL // 2 // LANES
    _load_resident(j, tab_hbm, tab_vmem, eid_ref, idx_smem, sem)
    sub = lax.broadcasted_iota(jnp.int32, (2 * half, LANES), 0)
    lane = lax.broadcasted_iota(jnp.int32, (nk, PEER_EXPERT_TILE), 1)
    acc_ref[...] = jnp.zeros((nk, PEER_EXPERT_TILE), F32)

    def body(t, carry):
        x = h_ref[t]
        x_lo = jnp.concatenate([x[0:half], x[0:half]], axis=0)
        x_hi = jnp.concatenate([x[half:], x[half:]], axis=0)
        folded = []
        for gi in range(nk // 8):
            q = []
            for i in range(4):
                k = gi * 8 + i
                words = jnp.concatenate([tab_vmem[idx_smem[t, k]], tab_vmem[idx_smem[t, k + 4]]], axis=0)
                lo, hi = _unpack(words)
                q.append(lo * x_lo + hi * x_hi)
            folded.append(_fold_pairs(q, sub))
        a = jnp.sum(jnp.concatenate(folded, axis=0), axis=-1, keepdims=True)
        acc_ref[...] = jnp.where(lane == t, a, acc_ref[...])
        return carry

    lax.fori_loop(0, PEER_EXPERT_TILE, body, 0)
    w_ref[...] = jax.nn.gelu(jnp.transpose(acc_ref[...])) * gw_ref[...]


def _peer_v_kernel(eid_ref, w_ref, x_ref, g2_ref, tab_hbm, o_ref, idx_smem, wb_ref, tab_vmem, sem):
    j = pl.program_id(0)
    nk = PEER_HEADS * PEER_TOPK
    half = D_MODEL // 2 // LANES
    _load_resident(j, tab_hbm, tab_vmem, eid_ref, idx_smem, sem)
    n_acc = 4

    def body(t, carry):
        wb_ref[...] = jnp.transpose(jnp.broadcast_to(w_ref[pl.ds(t, 1), :], (nk, nk)))
        lo_acc = [jnp.zeros((half, LANES), F32) for _ in range(n_acc)]
        hi_acc = [jnp.zeros((half, LANES), F32) for _ in range(n_acc)]
        for k in range(nk):
            lo, hi = _unpack(tab_vmem[idx_smem[t, k]])
            wk = wb_ref[k:k + 1, :]
            lo_acc[k % n_acc] = lo_acc[k % n_acc] + wk * lo
            hi_acc[k % n_acc] = hi_acc[k % n_acc] + wk * hi
        ff = jnp.concatenate([(lo_acc[0] + lo_acc[1]) + (lo_acc[2] + lo_acc[3]),
                              (hi_acc[0] + hi_acc[1]) + (hi_acc[2] + hi_acc[3])], axis=0)
        o_ref[t] = x_ref[t] + g2_ref[0] * ff
        return carry

    lax.fori_loop(0, PEER_EXPERT_TILE, body, 0)


def _peer_experts(eid, gw, h2, x1, gate2, u_tab, v_tab):
    b_, s_, d_ = x1.shape
    n_tok = b_ * s_
    nk = PEER_HEADS * PEER_TOPK
    sub_rows = d_ // LANES
    tt = PEER_EXPERT_TILE
    nsteps = n_tok // tt
    steps_per_batch = s_ // tt
    tok = pl.BlockSpec((tt, sub_rows, LANES), lambda j: (j, 0, 0))
    sel = pl.BlockSpec((tt, nk), lambda j: (j, 0))
    params = pltpu.CompilerParams(dimension_semantics=("arbitrary",), vmem_limit_bytes=PEER_VMEM_LIMIT)
    eid2 = eid.reshape(n_tok, nk)
    w = pl.pallas_call(
        _peer_u_kernel,
        grid=(nsteps,),
        in_specs=[sel, sel, tok, pl.BlockSpec(memory_space=pl.ANY)],
        out_specs=sel,
        out_shape=jax.ShapeDtypeStruct((n_tok, nk), F32),
        scratch_shapes=[pltpu.SMEM((tt, nk), jnp.int32), pltpu.VMEM(u_tab.shape, jnp.int32),
                        pltpu.VMEM((nk, tt), F32), pltpu.SemaphoreType.DMA],
        compiler_params=params,
    )(eid2, gw.reshape(n_tok, nk), h2.reshape(n_tok, sub_rows, LANES), u_tab)
    out = pl.pallas_call(
        _peer_v_kernel,
        grid=(nsteps,),
        in_specs=[sel, sel, tok, pl.BlockSpec((1, sub_rows, LANES), lambda j: (j // steps_per_batch, 0, 0)),
                  pl.BlockSpec(memory_space=pl.ANY)],
        out_specs=tok,
        out_shape=jax.ShapeDtypeStruct((n_tok, sub_rows, LANES), F32),
        scratch_shapes=[pltpu.SMEM((tt, nk), jnp.int32), pltpu.VMEM((nk, LANES), F32),
                        pltpu.VMEM(v_tab.shape, jnp.int32), pltpu.SemaphoreType.DMA],
        compiler_params=params,
    )(eid2, w, x1.reshape(n_tok, sub_rows, LANES), gate2.reshape(b_, sub_rows, LANES), v_tab)
    return out.reshape(b_, s_, d_)


def _pool_mixer(u, w_pool, pool_scale):
    b_, s_, c_ = u.shape
    csp = jnp.concatenate([jnp.zeros((b_, 1, c_), jnp.float32), jnp.cumsum(u, axis=1)], axis=1)
    t = jnp.arange(s_)
    outs = []
    for gi, w in enumerate(POOL_WINDOWS):
        sl = slice(gi * POOL_GROUP_DIM, (gi + 1) * POOL_GROUP_DIM)
        cg = csp[:, :, sl]
        upper = cg[:, 1:]
        lower = jnp.pad(cg[:, :s_ + 1 - w], ((0, 0), (w - 1, 0), (0, 0)))
        cnt = jnp.minimum(t + 1, w).astype(jnp.float32)[:, None]
        outs.append((upper - lower) / cnt - u[:, :, sl])
    pooled = jnp.stack(outs, axis=2)
    y = jnp.einsum('bsgc,gcd->bsgd', pooled, w_pool).reshape(b_, s_, POOL_WIDTH)
    return y * pool_scale


def _permute_w_in(w):
    d_ = w.shape[0]
    kv_end = POOL_WIDTH + Q_WIDTH + 6 * KV_WIDTH
    per_group = HEADS_PER_GROUP * 3
    parts = [w[:, :kv_end], w[:, kv_end + GATE_COLS:]]
    for g in range(N_KV_GROUPS):
        parts.append(w[:, kv_end + g * per_group:kv_end + (g + 1) * per_group])
        parts.append(jnp.zeros((d_, LANES - per_group), w.dtype))
    return jnp.concatenate(parts, axis=1).astype(BF16)


def kernel(x, c, rel_bias, ada_w, ada_b, norm1_g, norm2_g, w_in, pool_w, pool_scale, cmp_pe_k, cmp_w1_k, cmp_w2_k, cmp_pe_v, cmp_w1_v, cmp_w2_v, q_norm_g, k_norm_g, w_branch_pool, w_branch_attn, w_out, peer_w_q, peer_sub_keys, peer_u, peer_v):
    b_, s_, d_ = x.shape
    l = 0
    ada = (jax.nn.silu(c) @ ada_w[l] + ada_b[l]).reshape(b_, 6, 1, d_)
    shift1, scale1, gate1 = ada[:, 0], ada[:, 1], ada[:, 2]
    shift2, scale2, gate2 = ada[:, 3], ada[:, 4], ada[:, 5]

    (z_pool, z_q, ksel, vsel, kwin, vwin, kc_raw, vc_raw, z_merge, z_gate) = _in_proj(
        x, norm1_g[l], scale1, shift1, _permute_w_in(w_in[l]), k_norm_g[l])

    y_pool = _pool_mixer(z_pool, pool_w[l], pool_scale[l])
    kc = _compress(kc_raw, cmp_pe_k[l], cmp_w1_k[l], cmp_w2_k[l], k_norm_g[l, 0], True)
    vc = _compress(vc_raw, cmp_pe_v[l], cmp_w1_v[l], cmp_w2_v[l], k_norm_g[l, 0], False)
    y_attn = _attention(z_q, z_gate, q_norm_g[l], kc, vc, ksel, vsel, kwin, vwin, rel_bias)

    x1 = _merge(x, y_pool, y_attn, z_merge, gate1,
                w_branch_pool[l].astype(BF16), w_branch_attn[l].astype(BF16), w_out[l].astype(BF16))

    keys = peer_sub_keys[l].reshape(PEER_HEADS * 2, PEER_KEYS, PEER_QDIM // 2).astype(BF16)
    h2, eid, gw = _peer_route(x1, norm2_g[l], scale2, shift2, peer_w_q[l].astype(BF16), keys)
    return _peer_experts(eid, gw, h2, x1, gate2, _pack_table(peer_u[l]), _pack_table(peer_v[l]))
```

```python
import math
from functools import partial

import jax
import jax.numpy as jnp
import numpy as np
from jax import lax
from jax.experimental import pallas as pl
from jax.experimental.pallas import tpu as pltpu

D_MODEL = 1024
POOL_WIDTH = 512
POOL_WINDOWS = (2, 4, 8, 16)
POOL_GROUP_DIM = 128
N_HEADS = 8
N_KV_GROUPS = 2
HEADS_PER_GROUP = 4
HEAD_DIM = 64
Q_WIDTH = 512
KV_WIDTH = 128
CMP_BLOCK = 32
CMP_STRIDE = 16
CMP_HIDDEN = 128
SEL_BLOCK = 64
SEL_TOPK = 16
WINDOW = 512
FORCE_SCORE = 1000.0
N_BUCKETS = 32
MAX_DISTANCE = 128
PEER_HEADS = 8
PEER_KEYS = 128
PEER_TOPK = 16
PEER_QDIM = 256
PEER_CHUNK = 128
GATE_COLS = N_HEADS * 3
EPS = 1e-6
NEG_INF = -1e30

LANES = 128
VMEM_LIMIT = 48 * 1024 * 1024
ROW_TILE = 512
TQ = 128
SEL_FEATS = 64
CMP_BAND_LO = 10
PEER_TILE = 256
PEER_EXPERT_TILE = 128
PEER_TOKEN_UNROLL = 4
PEER_VMEM_LIMIT = 56 * 1024 * 1024

BF16 = jnp.bfloat16
F32 = jnp.float32
_NT = (((1,), (1,)), ((), ()))


def _rms_rows(x, g):
    return x * lax.rsqrt(jnp.mean(x * x, axis=-1, keepdims=True) + EPS) * g


def _in_proj_kernel(x_ref, g_ref, sc_ref, sh_ref, w_ref, kg_ref,
                    zpool_ref, zq_ref, ksel_ref, vsel_ref, kwin_ref, vwin_ref, kcr_ref, vcr_ref,
                    zmerge_ref, zgate_ref):
    i = pl.program_id(1)
    x = x_ref[0]
    y = x * lax.rsqrt(jnp.mean(x * x, axis=-1, keepdims=True) + EPS)
    h = y * g_ref[...] * (1.0 + sc_ref[0]) + sh_ref[0]
    z = jnp.dot(h.astype(BF16), w_ref[...], preferred_element_type=F32)
    ts = x.shape[0]
    zpool_ref[0] = z[:, :POOL_WIDTH]
    o = POOL_WIDTH
    zq_ref[0] = z[:, o:o + Q_WIDTH]
    o += Q_WIDTH
    pos = i * ts + lax.broadcasted_iota(jnp.int32, (ts, SEL_FEATS), 0)
    onehot = (pos // SEL_BLOCK == lax.broadcasted_iota(jnp.int32, (ts, SEL_FEATS), 1)).astype(BF16)
    for g in range(N_KV_GROUPS):
        def col(k):
            return z[:, o + k * KV_WIDTH + g * HEAD_DIM:o + k * KV_WIDTH + (g + 1) * HEAD_DIM]
        kcr_ref[0, g] = col(0)
        vcr_ref[0, g] = col(1)
        ks = _rms_rows(col(2), kg_ref[1:2, :]).astype(BF16)
        ksel_ref[0, g] = jnp.concatenate([ks, onehot], axis=1)
        vsel_ref[0, g] = col(3).astype(BF16)
        kwin_ref[0, g] = _rms_rows(col(4), kg_ref[2:3, :]).astype(BF16)
        vwin_ref[0, g] = col(5).astype(BF16)
    o += 6 * KV_WIDTH
    zmerge_ref[0] = z[:, o:o + 2 * D_MODEL]
    o += 2 * D_MODEL
    zgate_ref[0] = jax.nn.sigmoid(z[:, o:o + N_KV_GROUPS * LANES])


def _in_proj(x, g, scale, shift, w_bf16, k_norm_g):
    b_, s_, d_ = x.shape
    n = w_bf16.shape[1]
    row = lambda w: pl.BlockSpec((1, ROW_TILE, w), lambda b, i: (b, i, 0))
    grp = lambda w: pl.BlockSpec((1, N_KV_GROUPS, ROW_TILE, w), lambda b, i: (b, 0, i, 0))
    gshape = lambda w, dt: jax.ShapeDtypeStruct((b_, N_KV_GROUPS, s_, w), dt)
    return pl.pallas_call(
        _in_proj_kernel,
        grid=(b_, s_ // ROW_TILE),
        in_specs=[
            row(d_),
            pl.BlockSpec((1, d_), lambda b, i: (0, 0)),
            pl.BlockSpec((1, 1, d_), lambda b, i: (b, 0, 0)),
            pl.BlockSpec((1, 1, d_), lambda b, i: (b, 0, 0)),
            pl.BlockSpec((d_, n), lambda b, i: (0, 0)),
            pl.BlockSpec((3, HEAD_DIM), lambda b, i: (0, 0)),
        ],
        out_specs=[row(POOL_WIDTH), row(Q_WIDTH), grp(HEAD_DIM + SEL_FEATS), grp(HEAD_DIM), grp(HEAD_DIM),
                   grp(HEAD_DIM), grp(HEAD_DIM), grp(HEAD_DIM), row(2 * d_), row(N_KV_GROUPS * LANES)],
        out_shape=[
            jax.ShapeDtypeStruct((b_, s_, POOL_WIDTH), F32),
            jax.ShapeDtypeStruct((b_, s_, Q_WIDTH), F32),
            gshape(HEAD_DIM + SEL_FEATS, BF16), gshape(HEAD_DIM, BF16), gshape(HEAD_DIM, BF16), gshape(HEAD_DIM, BF16),
            gshape(HEAD_DIM, F32), gshape(HEAD_DIM, F32),
            jax.ShapeDtypeStruct((b_, s_, 2 * d_), F32),
            jax.ShapeDtypeStruct((b_, s_, N_KV_GROUPS * LANES), F32),
        ],
        compiler_params=pltpu.CompilerParams(
            dimension_semantics=("parallel", "parallel"), vmem_limit_bytes=VMEM_LIMIT),
    )(x, g.reshape(1, d_), scale, shift, w_bf16, k_norm_g)


def _compress_kernel(t_ref, pe_ref, w1_ref, w2_ref, g_ref, o_ref, *, normalize):
    half = CMP_STRIDE * HEAD_DIM
    t = t_ref[0, 0].astype(BF16)
    w1 = w1_ref[...]
    a = jnp.dot(t, w1[:half], preferred_element_type=F32)
    b = jnp.dot(t, w1[half:], preferred_element_type=F32)
    pe = jnp.dot(pe_ref[...], w1, preferred_element_type=F32)
    n = a.shape[0]
    b_next = pltpu.roll(b, n - 1, axis=0)
    hid = jax.nn.gelu(a + b_next + pe)
    out = jnp.dot(hid.astype(BF16), w2_ref[...], preferred_element_type=F32)
    if normalize:
        out = _rms_rows(out, g_ref[...])
    o_ref[0, 0] = out.astype(BF16)


def _compress(t_raw, pe, w1, w2, g, normalize):
    b_, g_, s_, dh = t_raw.shape
    n_str = s_ // CMP_STRIDE
    t = t_raw.reshape(b_, g_, n_str, CMP_STRIDE * dh)
    return pl.pallas_call(
        partial(_compress_kernel, normalize=normalize),
        grid=(b_, g_),
        in_specs=[
            pl.BlockSpec((1, 1, n_str, CMP_STRIDE * dh), lambda b, g: (b, g, 0, 0)),
            pl.BlockSpec((1, CMP_BLOCK * dh), lambda b, g: (0, 0)),
            pl.BlockSpec((CMP_BLOCK * dh, CMP_HIDDEN), lambda b, g: (0, 0)),
            pl.BlockSpec((CMP_HIDDEN, dh), lambda b, g: (0, 0)),
            pl.BlockSpec((1, dh), lambda b, g: (0, 0)),
        ],
        out_specs=pl.BlockSpec((1, 1, n_str, dh), lambda b, g: (b, g, 0, 0)),
        out_shape=jax.ShapeDtypeStruct((b_, g_, n_str, dh), BF16),
        compiler_params=pltpu.CompilerParams(
            dimension_semantics=("parallel", "parallel"), vmem_limit_bytes=VMEM_LIMIT),
    )(t, pe.reshape(1, CMP_BLOCK * dh).astype(BF16), w1.astype(BF16), w2.astype(BF16), g.reshape(1, dh))


def _t5_bucket_np(rel):
    n = np.maximum(rel, 0)
    max_exact = N_BUCKETS // 2
    nf = np.maximum(n, 1).astype(np.float32)
    large = max_exact + (np.log(nf / max_exact) / math.log(MAX_DISTANCE / max_exact) * (N_BUCKETS - max_exact)).astype(np.int32)
    large = np.minimum(large, N_BUCKETS - 1)
    return np.where(n < max_exact, n, large)


def _bias_tiles(rel_bias, n_cmp_pad):
    far = rel_bias[N_BUCKETS - 1]
    ii = np.arange(TQ)[:, None]

    def lookup(dist):
        near = (dist >= 0) & (dist < MAX_DISTANCE)
        bucket = _t5_bucket_np(np.where(near, dist, MAX_DISTANCE))
        b = rel_bias[bucket] - far
        b = jnp.transpose(b, (2, 0, 1))
        return b.reshape(N_KV_GROUPS, HEADS_PER_GROUP * dist.shape[0], dist.shape[1])

    jj = np.arange(TQ)[None, :]
    bt = jnp.stack([lookup(ii - jj), lookup(TQ + ii - jj)], axis=1)
    cc = np.arange(n_cmp_pad)[None, :]
    dist_c = ii - CMP_STRIDE * (cc - CMP_BAND_LO) - (CMP_BLOCK - 1)
    dist_c = np.where(cc <= CMP_BAND_LO + TQ // CMP_STRIDE, dist_c, -1)
    bc = lookup(dist_c)
    return bt, bc


def _sel_mapping_t(n_cmp_pad, n_cmp):
    m = np.zeros((SEL_FEATS, n_cmp_pad), np.float32)
    pos = np.arange(n_cmp)[:, None] * CMP_STRIDE + np.arange(CMP_BLOCK)[None, :]
    np.add.at(m, ((pos // SEL_BLOCK).ravel(), np.repeat(np.arange(n_cmp), CMP_BLOCK)), 1.0 / CMP_BLOCK)
    return m


def _flash_step(q, k, v, m_sc, l_sc, acc_sc, bias=None, mask=None):
    s = lax.dot_general(q, k, _NT, preferred_element_type=F32)
    if bias is not None:
        s = s + bias
    if mask is not None:
        s = jnp.where(mask, s, NEG_INF)
    m_prev = m_sc[...]
    m_new = jnp.maximum(m_prev, jnp.max(s, axis=-1, keepdims=True))
    alpha = jnp.exp(m_prev - m_new)
    p = jnp.exp(s - m_new)
    l_sc[...] = alpha * l_sc[...] + jnp.sum(p, axis=-1, keepdims=True)
    acc_sc[...] = alpha * acc_sc[...] + jnp.dot(p.astype(BF16), v, preferred_element_type=F32)
    m_sc[...] = m_new


def _attn_kernel(zq_ref, gate_ref, qg_ref, kc_ref, vc_ref, ksel_ref, vsel_ref, kwin_ref, vwin_ref,
                 bt_ref, bc_ref, smap_ref, o_ref, m_sc, l_sc, acc_sc):
    i = pl.program_id(2)
    rows = HEADS_PER_GROUP * TQ
    ncp = kc_ref.shape[2]
    scale = HEAD_DIM ** -0.5

    zq = zq_ref[0]
    qs = jnp.concatenate([zq[:, j * HEAD_DIM:(j + 1) * HEAD_DIM] for j in range(HEADS_PER_GROUP)], axis=0)
    qn = _rms_rows(qs, qg_ref[...]) * scale
    qb = qn.astype(BF16)

    lc = lax.dot_general(qb, kc_ref[0, 0], _NT, preferred_element_type=F32)
    lc = lc + pltpu.roll(bc_ref[0], (i * (TQ // CMP_STRIDE) + ncp - CMP_BAND_LO) % ncp, axis=1)
    t_c = i * TQ + lax.broadcasted_iota(jnp.int32, (rows, ncp), 0) % TQ
    n_c = lax.broadcasted_iota(jnp.int32, (rows, ncp), 1)
    valid_c = n_c * CMP_STRIDE + (CMP_BLOCK - 1) <= t_c
    lm = jnp.where(valid_c, lc, NEG_INF)
    e = jnp.where(valid_c, jnp.exp(lm - jnp.max(lm, axis=-1, keepdims=True)), 0.0)
    den = jnp.sum(e, axis=-1, keepdims=True)
    p_c = e / jnp.where(den > 0.0, den, 1.0)
    o_cmp = jnp.dot(p_c.astype(BF16), vc_ref[0, 0], preferred_element_type=F32)

    p_sum = p_c[0:TQ] + p_c[TQ:2 * TQ] + p_c[2 * TQ:3 * TQ] + p_c[3 * TQ:4 * TQ]
    p_hi = p_sum.astype(BF16)
    r1 = p_sum - p_hi.astype(F32)
    p_mid = r1.astype(BF16)
    p_lo = (r1 - p_mid.astype(F32)).astype(BF16)
    smap = smap_ref[...]
    imp = (lax.dot_general(smap, p_hi, _NT, preferred_element_type=F32)
           + lax.dot_general(smap, p_mid, _NT, preferred_element_type=F32)
           + lax.dot_general(smap, p_lo, _NT, preferred_element_type=F32))
    blk = lax.broadcasted_iota(jnp.int32, (SEL_FEATS, TQ), 0)
    t_s = i * TQ + lax.broadcasted_iota(jnp.int32, (SEL_FEATS, TQ), 1)
    cur = t_s // SEL_BLOCK
    forced = (blk == 0) | (blk == cur) | (blk == cur - 1)
    visible = blk * SEL_BLOCK <= t_s
    score = jnp.where(visible, imp + jnp.where(forced, FORCE_SCORE, 0.0), -1.0)
    rank = jnp.zeros((SEL_FEATS, TQ), jnp.int32)
    for sp in range(SEL_FEATS):
        row = score[sp:sp + 1, :]
        beats = (row > score) | ((row == score) & (blk > sp))
        rank = rank + beats.astype(jnp.int32)
    pen_t = jnp.where(rank < SEL_TOPK, 0.0, NEG_INF)
    pen = jnp.transpose(pen_t).astype(BF16)
    q_aug = jnp.concatenate([qb, jnp.concatenate([pen] * HEADS_PER_GROUP, axis=0)], axis=1)

    ii = lax.broadcasted_iota(jnp.int32, (rows, TQ), 0) % TQ
    jj = lax.broadcasted_iota(jnp.int32, (rows, TQ), 1)
    causal = jj <= ii

    def reset():
        m_sc[...] = jnp.full((rows, 1), NEG_INF, F32)
        l_sc[...] = jnp.zeros((rows, 1), F32)
        acc_sc[...] = jnp.zeros((rows, HEAD_DIM), F32)

    def tile(ref, kt):
        return ref[0, 0, pl.ds(pl.multiple_of(kt * TQ, TQ), TQ), :]

    reset()

    def sel_body(kt, carry):
        _flash_step(q_aug, tile(ksel_ref, kt), tile(vsel_ref, kt), m_sc, l_sc, acc_sc)
        return carry

    lax.fori_loop(0, i - 1, sel_body, 0)

    @pl.when(i >= 1)
    def _():
        _flash_step(q_aug, tile(ksel_ref, i - 1), tile(vsel_ref, i - 1), m_sc, l_sc, acc_sc, bias=bt_ref[0, 1])

    _flash_step(q_aug, tile(ksel_ref, i), tile(vsel_ref, i), m_sc, l_sc, acc_sc, bias=bt_ref[0, 0], mask=causal)
    o_sel = acc_sc[...] / l_sc[...]

    reset()
    n_win = WINDOW // TQ

    @pl.when(i >= n_win)
    def _():
        _flash_step(qb, tile(kwin_ref, i - n_win), tile(vwin_ref, i - n_win), m_sc, l_sc, acc_sc, mask=jj > ii)

    for back in range(n_win - 1, 1, -1):
        @pl.when(i >= back)
        def _(back=back):
            _flash_step(qb, tile(kwin_ref, i - back), tile(vwin_ref, i - back), m_sc, l_sc, acc_sc)

    @pl.when(i >= 1)
    def _():
        _flash_step(qb, tile(kwin_ref, i - 1), tile(vwin_ref, i - 1), m_sc, l_sc, acc_sc, bias=bt_ref[0, 1])

    _flash_step(qb, tile(kwin_ref, i), tile(vwin_ref, i), m_sc, l_sc, acc_sc, bias=bt_ref[0, 0], mask=causal)
    o_win = acc_sc[...] / l_sc[...]

    gate = gate_ref[0]
    outs = []
    for j in range(HEADS_PER_GROUP):
        sl = slice(j * TQ, (j + 1) * TQ)
        outs.append(gate[:, 3 * j:3 * j + 1] * o_cmp[sl] + gate[:, 3 * j + 1:3 * j + 2] * o_sel[sl]
                    + gate[:, 3 * j + 2:3 * j + 3] * o_win[sl])
    o_ref[0] = jnp.concatenate(outs, axis=1)


def _attention(z_q, z_gate, q_norm_g, kc, vc, ksel, vsel, kwin, vwin, rel_bias):
    b_, s_, _ = z_q.shape
    ncp = kc.shape[2]
    n_cmp = ncp - CMP_BLOCK // CMP_STRIDE + 1
    assert s_ // SEL_BLOCK <= SEL_FEATS and s_ % TQ == 0
    bt, bc = _bias_tiles(rel_bias, ncp)
    smap_t = jnp.asarray(_sel_mapping_t(ncp, n_cmp), BF16)
    rows = HEADS_PER_GROUP * TQ
    gw = HEADS_PER_GROUP * HEAD_DIM
    kv = lambda w: pl.BlockSpec((1, 1, s_, w), lambda b, g, i: (b, g, 0, 0))
    cm = pl.BlockSpec((1, 1, ncp, HEAD_DIM), lambda b, g, i: (b, g, 0, 0))
    return pl.pallas_call(
        _attn_kernel,
        grid=(b_, N_KV_GROUPS, s_ // TQ),
        in_specs=[
            pl.BlockSpec((1, TQ, gw), lambda b, g, i: (b, i, g)),
            pl.BlockSpec((1, TQ, LANES), lambda b, g, i: (b, i, g)),
            pl.BlockSpec((1, HEAD_DIM), lambda b, g, i: (0, 0)),
            cm, cm, kv(HEAD_DIM + SEL_FEATS), kv(HEAD_DIM), kv(HEAD_DIM), kv(HEAD_DIM),
            pl.BlockSpec((1, 2, rows, TQ), lambda b, g, i: (g, 0, 0, 0)),
            pl.BlockSpec((1, rows, ncp), lambda b, g, i: (g, 0, 0)),
            pl.BlockSpec((SEL_FEATS, ncp), lambda b, g, i: (0, 0)),
        ],
        out_specs=pl.BlockSpec((1, TQ, gw), lambda b, g, i: (b, i, g)),
        out_shape=jax.ShapeDtypeStruct((b_, s_, N_KV_GROUPS * gw), F32),
        scratch_shapes=[pltpu.VMEM((rows, 1), F32), pltpu.VMEM((rows, 1), F32), pltpu.VMEM((rows, HEAD_DIM), F32)],
        compiler_params=pltpu.CompilerParams(
            dimension_semantics=("parallel", "parallel", "arbitrary"), vmem_limit_bytes=VMEM_LIMIT),
    )(z_q, z_gate, q_norm_g.reshape(1, HEAD_DIM), kc, vc, ksel, vsel, kwin, vwin, bt, bc, smap_t)


def _merge_kernel(x_ref, yp_ref, ya_ref, zm_ref, g1_ref, wbp_ref, wba_ref, wo_ref, o_ref):
    d_ = x_ref.shape[-1]
    bp = jnp.dot(yp_ref[0].astype(BF16), wbp_ref[...], preferred_element_type=F32)
    ba = jnp.dot(ya_ref[0].astype(BF16), wba_ref[...], preferred_element_type=F32)
    zm = zm_ref[0]
    mixed = jax.nn.sigmoid(zm[:, :d_]) * bp + jax.nn.sigmoid(zm[:, d_:]) * ba
    proj = jnp.dot(mixed.astype(BF16), wo_ref[...], preferred_element_type=F32)
    o_ref[0] = x_ref[0] + g1_ref[0] * proj


def _merge(x, y_pool, y_attn, z_merge, gate1, wbp, wba, wo):
    b_, s_, d_ = x.shape
    row = lambda w: pl.BlockSpec((1, ROW_TILE, w), lambda b, i: (b, i, 0))
    full = lambda a: pl.BlockSpec(a.shape, lambda b, i: (0, 0))
    return pl.pallas_call(
        _merge_kernel,
        grid=(b_, s_ // ROW_TILE),
        in_specs=[row(d_), row(POOL_WIDTH), row(Q_WIDTH), row(2 * d_),
                  pl.BlockSpec((1, 1, d_), lambda b, i: (b, 0, 0)),
                  full(wbp), full(wba), full(wo)],
        out_specs=row(d_),
        out_shape=jax.ShapeDtypeStruct((b_, s_, d_), F32),
        compiler_params=pltpu.CompilerParams(
            dimension_semantics=("parallel", "parallel"), vmem_limit_bytes=VMEM_LIMIT),
    )(x, y_pool, y_attn, z_merge, gate1, wbp, wba, wo)


_CAND_ROWS = PEER_TOPK + 7 * 8 + 8


def _topk_rows(x, k):
    n = x.shape[0]
    rid = lax.broadcasted_iota(jnp.int32, x.shape, 0)
    vals, idxs = [], []
    for _ in range(k):
        m = jnp.max(x, axis=0, keepdims=True)
        idx = jnp.min(jnp.where(x == m, rid, n), axis=0, keepdims=True)
        vals.append(m)
        idxs.append(idx)
        x = jnp.where(rid == idx, -jnp.inf, x)
    return jnp.concatenate(vals, axis=0), jnp.concatenate(idxs, axis=0)


def _pair_grid(r0, r1, combine):
    parts = [combine(r0[0:1], r1)]
    parts += [combine(r0[a:a + 1], r1[0:8]) for a in range(1, 8)]
    parts.append(combine(r0[8:16], r1[0:1]))
    return jnp.concatenate(parts, axis=0)


def _peer_route_kernel(x_ref, g_ref, sc_ref, sh_ref, w_ref, keys_ref, h_ref, eid_ref, gw_ref):
    x = x_ref[0]
    y = x * lax.rsqrt(jnp.mean(x * x, axis=-1, keepdims=True) + EPS)
    h = y * g_ref[...] * (1.0 + sc_ref[0]) + sh_ref[0]
    h_ref[0] = h
    qv = jnp.dot(h.astype(BF16), w_ref[...], preferred_element_type=F32).astype(BF16)
    half = PEER_QDIM // 2
    eids, gws = [], []
    for hd in range(PEER_HEADS):
        tops = []
        for p in range(2):
            c = hd * 2 + p
            sc = lax.dot_general(keys_ref[c], qv[:, c * half:(c + 1) * half], _NT,
                                 preferred_element_type=F32)
            tops.append(_topk_rows(sc, PEER_TOPK))
        (s0, i0), (s1, i1) = tops
        cand = _pair_grid(s0, s1, lambda a, b: a + b)
        cid = _pair_grid(i0, i1, lambda a, b: a * PEER_KEYS + b)
        rid = lax.broadcasted_iota(jnp.int32, cand.shape, 0)
        top_s, top_e = [], []
        for _ in range(PEER_TOPK):
            m = jnp.max(cand, axis=0, keepdims=True)
            idx = jnp.min(jnp.where(cand == m, rid, _CAND_ROWS), axis=0, keepdims=True)
            hit = rid == idx
            top_s.append(m)
            top_e.append(jnp.sum(jnp.where(hit, cid, 0), axis=0, keepdims=True))
            cand = jnp.where(hit, -jnp.inf, cand)
        top_s = jnp.concatenate(top_s, axis=0)
        e = jnp.exp(top_s - top_s[0:1])
        gws.append(e / jnp.sum(e, axis=0, keepdims=True))
        eids.append(jnp.concatenate(top_e, axis=0))
    eid_ref[0] = jnp.transpose(jnp.concatenate(eids, axis=0))
    gw_ref[0] = jnp.transpose(jnp.concatenate(gws, axis=0))


def _peer_route(x, g, scale, shift, w_bf16, keys_bf16):
    b_, s_, d_ = x.shape
    n = w_bf16.shape[1]
    nk = PEER_HEADS * PEER_TOPK
    row = lambda w: pl.BlockSpec((1, PEER_TILE, w), lambda b, i: (b, i, 0))
    return pl.pallas_call(
        _peer_route_kernel,
        grid=(b_, s_ // PEER_TILE),
        in_specs=[
            row(d_),
            pl.BlockSpec((1, d_), lambda b, i: (0, 0)),
            pl.BlockSpec((1, 1, d_), lambda b, i: (b, 0, 0)),
            pl.BlockSpec((1, 1, d_), lambda b, i: (b, 0, 0)),
            pl.BlockSpec((d_, n), lambda b, i: (0, 0)),
            pl.BlockSpec(keys_bf16.shape, lambda b, i: (0, 0, 0)),
        ],
        out_specs=[row(d_), row(nk), row(nk)],
        out_shape=[
            jax.ShapeDtypeStruct((b_, s_, d_), F32),
            jax.ShapeDtypeStruct((b_, s_, nk), jnp.int32),
            jax.ShapeDtypeStruct((b_, s_, nk), F32),
        ],
        compiler_params=pltpu.CompilerParams(
            dimension_semantics=("parallel", "parallel"), vmem_limit_bytes=VMEM_LIMIT),
    )(x, g.reshape(1, d_), scale, shift, w_bf16, keys_bf16)


_HI_MASK = -65536


def _pack_table(t):
    e_, d_ = t.shape
    bits = lax.bitcast_convert_type(t.astype(jnp.bfloat16), jnp.uint16).astype(jnp.uint32)
    words = (bits[:, d_ // 2:] << 16) | bits[:, :d_ // 2]
    return lax.bitcast_convert_type(words, jnp.int32).reshape(e_, d_ // 2 // LANES, LANES)


def _unpack(words):
    lo = lax.bitcast_convert_type(lax.shift_left(words, 16), F32)
    hi = lax.bitcast_convert_type(words & _HI_MASK, F32)
    return lo, hi


def _load_resident(j, tab_hbm, tab_vmem, eid_ref, idx_smem, sem):
    @pl.when(j == 0)
    def _():
        cp = pltpu.make_async_copy(tab_hbm, tab_vmem, sem)
        cp.start()
        cp.wait()

    cp = pltpu.make_async_copy(eid_ref, idx_smem, sem)
    cp.start()
    cp.wait()


def _fold_pairs(q, sub):
    m2, m1 = (sub & 2) == 0, (sub & 1) == 0
    u = [jnp.where(m2, q[i], q[i + 2]) + jnp.where(m2, pltpu.roll(q[i], 6, axis=0), pltpu.roll(q[i + 2], 2, axis=0))
         for i in range(2)]
    return jnp.where(m1, u[0], u[1]) + jnp.where(m1, pltpu.roll(u[0], 7, axis=0), pltpu.roll(u[1], 1, axis=0))


def _peer_u_kernel(eid_ref, gw_ref, h_ref, tab_hbm, w_ref, idx_smem, tab_vmem, acc_ref, sem):
    j = pl.program_id(0)
    nk = PEER_HEADS * PEER_TOPK
    half = D_MODEL // 2 // LANES
    _load_resident(j, tab_hbm, tab_vmem, eid_ref, idx_smem, sem)
    sub = lax.broadcasted_iota(jnp.int32, (2 * half, LANES), 0)
    lane = lax.broadcasted_iota(jnp.int32, (nk, PEER_EXPERT_TILE), 1)
    acc_ref[...] = jnp.zeros((nk, PEER_EXPERT_TILE), F32)

    def token(t):
        x = h_ref[t]
        x_lo = jnp.concatenate([x[0:half], x[0:half]], axis=0)
        x_hi = jnp.concatenate([x[half:], x[half:]], axis=0)
        folded = []
        for gi in range(nk // 8):
            q = []
            for i in range(4):
                k = gi * 8 + i
                words = jnp.concatenate([tab_vmem[idx_smem[t, k]], tab_vmem[idx_smem[t, k + 4]]], axis=0)
                lo, hi = _unpack(words)
                q.append(lo * x_lo + hi * x_hi)
            folded.append(_fold_pairs(q, sub))
        return jnp.sum(jnp.concatenate(folded, axis=0), axis=-1, keepdims=True)

    def body(tb, carry):
        acc = acc_ref[...]
        for u in range(PEER_TOKEN_UNROLL):
            t = tb * PEER_TOKEN_UNROLL + u
            acc = jnp.where(lane == t, token(t), acc)
        acc_ref[...] = acc
        return carry

    lax.fori_loop(0, PEER_EXPERT_TILE // PEER_TOKEN_UNROLL, body, 0)
    w_ref[...] = jax.nn.gelu(jnp.transpose(acc_ref[...])) * gw_ref[...]


def _peer_v_kernel(eid_ref, w_ref, x_ref, g2_ref, tab_hbm, o_ref, idx_smem, wb_ref, tab_vmem, sem):
    j = pl.program_id(0)
    nk = PEER_HEADS * PEER_TOPK
    half = D_MODEL // 2 // LANES
    _load_resident(j, tab_hbm, tab_vmem, eid_ref, idx_smem, sem)
    n_acc = 4

    def token(t, slot):
        wb_ref[slot] = jnp.transpose(jnp.broadcast_to(w_ref[pl.ds(t, 1), :], (nk, nk)))
        lo_acc = [jnp.zeros((half, LANES), F32) for _ in range(n_acc)]
        hi_acc = [jnp.zeros((half, LANES), F32) for _ in range(n_acc)]
        for k in range(nk):
            lo, hi = _unpack(tab_vmem[idx_smem[t, k]])
            wk = wb_ref[slot, k:k + 1, :]
            lo_acc[k % n_acc] = lo_acc[k % n_acc] + wk * lo
            hi_acc[k % n_acc] = hi_acc[k % n_acc] + wk * hi
        ff = jnp.concatenate([sum(lo_acc[1:], lo_acc[0]), sum(hi_acc[1:], hi_acc[0])], axis=0)
        o_ref[t] = x_ref[t] + g2_ref[0] * ff

    def body(tb, carry):
        for u in range(PEER_TOKEN_UNROLL):
            token(tb * PEER_TOKEN_UNROLL + u, u)
        return carry

    lax.fori_loop(0, PEER_EXPERT_TILE // PEER_TOKEN_UNROLL, body, 0)


def _peer_experts(eid, gw, h2, x1, gate2, u_tab, v_tab):
    b_, s_, d_ = x1.shape
    n_tok = b_ * s_
    nk = PEER_HEADS * PEER_TOPK
    sub_rows = d_ // LANES
    tt = PEER_EXPERT_TILE
    nsteps = n_tok // tt
    steps_per_batch = s_ // tt
    tok = pl.BlockSpec((tt, sub_rows, LANES), lambda j: (j, 0, 0))
    sel = pl.BlockSpec((tt, nk), lambda j: (j, 0))
    params = pltpu.CompilerParams(dimension_semantics=("arbitrary",), vmem_limit_bytes=PEER_VMEM_LIMIT)
    eid2 = eid.reshape(n_tok, nk)
    w = pl.pallas_call(
        _peer_u_kernel,
        grid=(nsteps,),
        in_specs=[sel, sel, tok, pl.BlockSpec(memory_space=pl.ANY)],
        out_specs=sel,
        out_shape=jax.ShapeDtypeStruct((n_tok, nk), F32),
        scratch_shapes=[pltpu.SMEM((tt, nk), jnp.int32), pltpu.VMEM(u_tab.shape, jnp.int32),
                        pltpu.VMEM((nk, tt), F32), pltpu.SemaphoreType.DMA],
        compiler_params=params,
    )(eid2, gw.reshape(n_tok, nk), h2.reshape(n_tok, sub_rows, LANES), u_tab)
    out = pl.pallas_call(
        _peer_v_kernel,
        grid=(nsteps,),
        in_specs=[sel, sel, tok, pl.BlockSpec((1, sub_rows, LANES), lambda j: (j // steps_per_batch, 0, 0)),
                  pl.BlockSpec(memory_space=pl.ANY)],
        out_specs=tok,
        out_shape=jax.ShapeDtypeStruct((n_tok, sub_rows, LANES), F32),
        scratch_shapes=[pltpu.SMEM((tt, nk), jnp.int32), pltpu.VMEM((PEER_TOKEN_UNROLL, nk, LANES), F32),
                        pltpu.VMEM(v_tab.shape, jnp.int32), pltpu.SemaphoreType.DMA],
        compiler_params=params,
    )(eid2, w, x1.reshape(n_tok, sub_rows, LANES), gate2.reshape(b_, sub_rows, LANES), v_tab)
    return out.reshape(b_, s_, d_)


def _pool_mixer(u, w_pool, pool_scale):
    b_, s_, c_ = u.shape
    csp = jnp.concatenate([jnp.zeros((b_, 1, c_), jnp.float32), jnp.cumsum(u, axis=1)], axis=1)
    t = jnp.arange(s_)
    outs = []
    for gi, w in enumerate(POOL_WINDOWS):
        sl = slice(gi * POOL_GROUP_DIM, (gi + 1) * POOL_GROUP_DIM)
        cg = csp[:, :, sl]
        upper = cg[:, 1:]
        lower = jnp.pad(cg[:, :s_ + 1 - w], ((0, 0), (w - 1, 0), (0, 0)))
        cnt = jnp.minimum(t + 1, w).astype(jnp.float32)[:, None]
        outs.append((upper - lower) / cnt - u[:, :, sl])
    pooled = jnp.stack(outs, axis=2)
    y = jnp.einsum('bsgc,gcd->bsgd', pooled, w_pool).reshape(b_, s_, POOL_WIDTH)
    return y * pool_scale


def _permute_w_in(w):
    d_ = w.shape[0]
    kv_end = POOL_WIDTH + Q_WIDTH + 6 * KV_WIDTH
    per_group = HEADS_PER_GROUP * 3
    parts = [w[:, :kv_end], w[:, kv_end + GATE_COLS:]]
    for g in range(N_KV_GROUPS):
        parts.append(w[:, kv_end + g * per_group:kv_end + (g + 1) * per_group])
        parts.append(jnp.zeros((d_, LANES - per_group), w.dtype))
    return jnp.concatenate(parts, axis=1).astype(BF16)


def kernel(x, c, rel_bias, ada_w, ada_b, norm1_g, norm2_g, w_in, pool_w, pool_scale, cmp_pe_k, cmp_w1_k, cmp_w2_k, cmp_pe_v, cmp_w1_v, cmp_w2_v, q_norm_g, k_norm_g, w_branch_pool, w_branch_attn, w_out, peer_w_q, peer_sub_keys, peer_u, peer_v):
    b_, s_, d_ = x.shape
    l = 0
    ada = (jax.nn.silu(c) @ ada_w[l] + ada_b[l]).reshape(b_, 6, 1, d_)
    shift1, scale1, gate1 = ada[:, 0], ada[:, 1], ada[:, 2]
    shift2, scale2, gate2 = ada[:, 3], ada[:, 4], ada[:, 5]

    (z_pool, z_q, ksel, vsel, kwin, vwin, kc_raw, vc_raw, z_merge, z_gate) = _in_proj(
        x, norm1_g[l], scale1, shift1, _permute_w_in(w_in[l]), k_norm_g[l])

    y_pool = _pool_mixer(z_pool, pool_w[l], pool_scale[l])
    kc = _compress(kc_raw, cmp_pe_k[l], cmp_w1_k[l], cmp_w2_k[l], k_norm_g[l, 0], True)
    vc = _compress(vc_raw, cmp_pe_v[l], cmp_w1_v[l], cmp_w2_v[l], k_norm_g[l, 0], False)
    y_attn = _attention(z_q, z_gate, q_norm_g[l], kc, vc, ksel, vsel, kwin, vwin, rel_bias)

    x1 = _merge(x, y_pool, y_attn, z_merge, gate1,
                w_branch_pool[l].astype(BF16), w_branch_attn[l].astype(BF16), w_out[l].astype(BF16))

    keys = peer_sub_keys[l].reshape(PEER_HEADS * 2, PEER_KEYS, PEER_QDIM // 2).astype(BF16)
    h2, eid, gw = _peer_route(x1, norm2_g[l], scale2, shift2, peer_w_q[l].astype(BF16), keys)
    return _peer_experts(eid, gw, h2, x1, gate2, _pack_table(peer_u[l]), _pack_table(peer_v[l]))
```

```python
import math
from functools import partial

import jax
import jax.numpy as jnp
import numpy as np
from jax import lax
from jax.experimental import pallas as pl
from jax.experimental.pallas import tpu as pltpu

D_MODEL = 1024
POOL_WIDTH = 512
POOL_WINDOWS = (2, 4, 8, 16)
POOL_GROUP_DIM = 128
N_HEADS = 8
N_KV_GROUPS = 2
HEADS_PER_GROUP = 4
HEAD_DIM = 64
Q_WIDTH = 512
KV_WIDTH = 128
CMP_BLOCK = 32
CMP_STRIDE = 16
CMP_HIDDEN = 128
SEL_BLOCK = 64
SEL_TOPK = 16
WINDOW = 512
FORCE_SCORE = 1000.0
N_BUCKETS = 32
MAX_DISTANCE = 128
PEER_HEADS = 8
PEER_KEYS = 128
PEER_TOPK = 16
PEER_QDIM = 256
PEER_CHUNK = 128
GATE_COLS = N_HEADS * 3
EPS = 1e-6
NEG_INF = -1e30

LANES = 128
VMEM_LIMIT = 48 * 1024 * 1024
ROW_TILE = 512
TQ = 128
SEL_FEATS = 64
SEL_CHUNK = 512
KV_PAD = 512
CMP_BAND_LO = 10
PEER_TILE = 256
PEER_EXPERT_TILE = 128
PEER_TOKEN_UNROLL = 4
PEER_VMEM_LIMIT = 56 * 1024 * 1024

BF16 = jnp.bfloat16
F32 = jnp.float32
_NT = (((1,), (1,)), ((), ()))


def _rms_rows(x, g):
    return x * lax.rsqrt(jnp.mean(x * x, axis=-1, keepdims=True) + EPS) * g


def _in_proj_kernel(x_ref, g_ref, sc_ref, sh_ref, w_ref, kg_ref,
                    zpool_ref, zq_ref, ksel_ref, vsel_ref, kwin_ref, vwin_ref, kcr_ref, vcr_ref,
                    zmerge_ref, zgate_ref):
    i = pl.program_id(1)
    x = x_ref[0]
    y = x * lax.rsqrt(jnp.mean(x * x, axis=-1, keepdims=True) + EPS)
    h = y * g_ref[...] * (1.0 + sc_ref[0]) + sh_ref[0]
    z = jnp.dot(h.astype(BF16), w_ref[...], preferred_element_type=F32)
    ts = x.shape[0]
    zpool_ref[0] = z[:, :POOL_WIDTH]
    o = POOL_WIDTH
    zq_ref[0] = z[:, o:o + Q_WIDTH]
    o += Q_WIDTH
    pos = i * ts + lax.broadcasted_iota(jnp.int32, (ts, SEL_FEATS), 0)
    onehot = (pos // SEL_BLOCK == lax.broadcasted_iota(jnp.int32, (ts, SEL_FEATS), 1)).astype(BF16)
    for g in range(N_KV_GROUPS):
        def col(k):
            return z[:, o + k * KV_WIDTH + g * HEAD_DIM:o + k * KV_WIDTH + (g + 1) * HEAD_DIM]
        kcr_ref[0, g] = col(0)
        vcr_ref[0, g] = col(1)
        ks = _rms_rows(col(2), kg_ref[1:2, :]).astype(BF16)
        ksel_ref[0, g] = jnp.concatenate([ks, onehot], axis=1)
        vsel_ref[0, g] = col(3).astype(BF16)
        kwin_ref[0, g] = _rms_rows(col(4), kg_ref[2:3, :]).astype(BF16)
        vwin_ref[0, g] = col(5).astype(BF16)
    o += 6 * KV_WIDTH
    zmerge_ref[0] = z[:, o:o + 2 * D_MODEL]
    o += 2 * D_MODEL
    zgate_ref[0] = jax.nn.sigmoid(z[:, o:o + N_KV_GROUPS * LANES])


def _in_proj(x, g, scale, shift, w_bf16, k_norm_g):
    b_, s_, d_ = x.shape
    n = w_bf16.shape[1]
    row = lambda w: pl.BlockSpec((1, ROW_TILE, w), lambda b, i: (b, i, 0))
    grp = lambda w: pl.BlockSpec((1, N_KV_GROUPS, ROW_TILE, w), lambda b, i: (b, 0, i, 0))
    gshape = lambda w, dt: jax.ShapeDtypeStruct((b_, N_KV_GROUPS, s_, w), dt)
    return pl.pallas_call(
        _in_proj_kernel,
        grid=(b_, s_ // ROW_TILE),
        in_specs=[
            row(d_),
            pl.BlockSpec((1, d_), lambda b, i: (0, 0)),
            pl.BlockSpec((1, 1, d_), lambda b, i: (b, 0, 0)),
            pl.BlockSpec((1, 1, d_), lambda b, i: (b, 0, 0)),
            pl.BlockSpec((d_, n), lambda b, i: (0, 0)),
            pl.BlockSpec((3, HEAD_DIM), lambda b, i: (0, 0)),
        ],
        out_specs=[row(POOL_WIDTH), row(Q_WIDTH), grp(HEAD_DIM + SEL_FEATS), grp(HEAD_DIM), grp(HEAD_DIM),
                   grp(HEAD_DIM), grp(HEAD_DIM), grp(HEAD_DIM), row(2 * d_), row(N_KV_GROUPS * LANES)],
        out_shape=[
            jax.ShapeDtypeStruct((b_, s_, POOL_WIDTH), F32),
            jax.ShapeDtypeStruct((b_, s_, Q_WIDTH), F32),
            gshape(HEAD_DIM + SEL_FEATS, BF16), gshape(HEAD_DIM, BF16), gshape(HEAD_DIM, BF16), gshape(HEAD_DIM, BF16),
            gshape(HEAD_DIM, F32), gshape(HEAD_DIM, F32),
            jax.ShapeDtypeStruct((b_, s_, 2 * d_), F32),
            jax.ShapeDtypeStruct((b_, s_, N_KV_GROUPS * LANES), F32),
        ],
        compiler_params=pltpu.CompilerParams(
            dimension_semantics=("parallel", "parallel"), vmem_limit_bytes=VMEM_LIMIT),
    )(x, g.reshape(1, d_), scale, shift, w_bf16, k_norm_g)


def _compress_kernel(t_ref, pe_ref, w1_ref, w2_ref, g_ref, o_ref, *, normalize):
    half = CMP_STRIDE * HEAD_DIM
    t = t_ref[0, 0].astype(BF16)
    w1 = w1_ref[...]
    a = jnp.dot(t, w1[:half], preferred_element_type=F32)
    b = jnp.dot(t, w1[half:], preferred_element_type=F32)
    pe = jnp.dot(pe_ref[...], w1, preferred_element_type=F32)
    n = a.shape[0]
    b_next = pltpu.roll(b, n - 1, axis=0)
    hid = jax.nn.gelu(a + b_next + pe)
    out = jnp.dot(hid.astype(BF16), w2_ref[...], preferred_element_type=F32)
    if normalize:
        out = _rms_rows(out, g_ref[...])
    o_ref[0, 0] = out.astype(BF16)


def _compress(t_raw, pe, w1, w2, g, normalize):
    b_, g_, s_, dh = t_raw.shape
    n_str = s_ // CMP_STRIDE
    t = t_raw.reshape(b_, g_, n_str, CMP_STRIDE * dh)
    return pl.pallas_call(
        partial(_compress_kernel, normalize=normalize),
        grid=(b_, g_),
        in_specs=[
            pl.BlockSpec((1, 1, n_str, CMP_STRIDE * dh), lambda b, g: (b, g, 0, 0)),
            pl.BlockSpec((1, CMP_BLOCK * dh), lambda b, g: (0, 0)),
            pl.BlockSpec((CMP_BLOCK * dh, CMP_HIDDEN), lambda b, g: (0, 0)),
            pl.BlockSpec((CMP_HIDDEN, dh), lambda b, g: (0, 0)),
            pl.BlockSpec((1, dh), lambda b, g: (0, 0)),
        ],
        out_specs=pl.BlockSpec((1, 1, n_str, dh), lambda b, g: (b, g, 0, 0)),
        out_shape=jax.ShapeDtypeStruct((b_, g_, n_str, dh), BF16),
        compiler_params=pltpu.CompilerParams(
            dimension_semantics=("parallel", "parallel"), vmem_limit_bytes=VMEM_LIMIT),
    )(t, pe.reshape(1, CMP_BLOCK * dh).astype(BF16), w1.astype(BF16), w2.astype(BF16), g.reshape(1, dh))


def _t5_bucket_np(rel):
    n = np.maximum(rel, 0)
    max_exact = N_BUCKETS // 2
    nf = np.maximum(n, 1).astype(np.float32)
    large = max_exact + (np.log(nf / max_exact) / math.log(MAX_DISTANCE / max_exact) * (N_BUCKETS - max_exact)).astype(np.int32)
    large = np.minimum(large, N_BUCKETS - 1)
    return np.where(n < max_exact, n, large)


def _bias_tiles(rel_bias, n_cmp_pad):
    far = rel_bias[N_BUCKETS - 1]
    ii = np.arange(TQ)[:, None]

    def lookup(dist):
        near = (dist >= 0) & (dist < MAX_DISTANCE)
        bucket = _t5_bucket_np(np.where(near, dist, MAX_DISTANCE))
        b = rel_bias[bucket] - far
        b = jnp.transpose(b, (2, 0, 1))
        return b.reshape(N_KV_GROUPS, HEADS_PER_GROUP * dist.shape[0], dist.shape[1])

    jj = np.arange(TQ)[None, :]
    bt = jnp.stack([lookup(ii - jj), lookup(TQ + ii - jj)], axis=1)
    cc = np.arange(n_cmp_pad)[None, :]
    dist_c = ii - CMP_STRIDE * (cc - CMP_BAND_LO) - (CMP_BLOCK - 1)
    dist_c = np.where(cc <= CMP_BAND_LO + TQ // CMP_STRIDE, dist_c, -1)
    bc = lookup(dist_c)
    rows = HEADS_PER_GROUP * TQ
    i_r = np.tile(np.arange(TQ), HEADS_PER_GROUP)[:, None]
    first = jnp.asarray(np.where(jj > i_r, 0.0, NEG_INF), F32)
    last = jnp.where(jnp.asarray(jj <= i_r), bt[:, 0], NEG_INF)
    zeros = jnp.zeros((N_KV_GROUPS, rows, WINDOW - 2 * TQ), F32)
    wa = jnp.concatenate([jnp.broadcast_to(first, (N_KV_GROUPS, rows, TQ)), zeros, bt[:, 1], last], axis=-1)
    return wa, bc


def _sel_mapping_t(n_cmp_pad, n_cmp):
    m = np.zeros((SEL_FEATS, n_cmp_pad), np.float32)
    pos = np.arange(n_cmp)[:, None] * CMP_STRIDE + np.arange(CMP_BLOCK)[None, :]
    np.add.at(m, ((pos // SEL_BLOCK).ravel(), np.repeat(np.arange(n_cmp), CMP_BLOCK)), 1.0 / CMP_BLOCK)
    return m


def _logits(q, k, add=None, lo_cols=None):
    s = lax.dot_general(q, k, _NT, preferred_element_type=F32)
    if add is not None:
        s = s + add
    if lo_cols is not None:
        s = jnp.where(lax.broadcasted_iota(jnp.int32, s.shape, 1) < lo_cols, NEG_INF, s)
    return s


def _flash_step(s, v, m_sc, l_sc, acc_sc):
    m_prev = m_sc[...]
    m_new = jnp.maximum(m_prev, jnp.max(s, axis=-1, keepdims=True))
    alpha = jnp.exp(m_prev - m_new)
    p = jnp.exp(s - m_new)
    l_sc[...] = alpha * l_sc[...] + jnp.sum(p, axis=-1, keepdims=True)
    acc_sc[...] = alpha * acc_sc[...] + jnp.dot(p.astype(BF16), v, preferred_element_type=F32)
    m_sc[...] = m_new


def _attn_kernel(zq_ref, gate_ref, qg_ref, kc_ref, vc_ref, ksel_ref, vsel_ref, kwin_ref, vwin_ref,
                 wa_ref, bc_ref, smap_ref, o_ref, m_sc, l_sc, acc_sc):
    i = pl.program_id(2)
    rows = HEADS_PER_GROUP * TQ
    ncp = kc_ref.shape[2]
    scale = HEAD_DIM ** -0.5

    zq = zq_ref[0]
    qs = jnp.concatenate([zq[:, j * HEAD_DIM:(j + 1) * HEAD_DIM] for j in range(HEADS_PER_GROUP)], axis=0)
    qn = _rms_rows(qs, qg_ref[...]) * scale
    qb = qn.astype(BF16)

    lc = lax.dot_general(qb, kc_ref[0, 0], _NT, preferred_element_type=F32)
    lc = lc + pltpu.roll(bc_ref[0], (i * (TQ // CMP_STRIDE) + ncp - CMP_BAND_LO) % ncp, axis=1)
    t_c = i * TQ + lax.broadcasted_iota(jnp.int32, (rows, ncp), 0) % TQ
    n_c = lax.broadcasted_iota(jnp.int32, (rows, ncp), 1)
    valid_c = n_c * CMP_STRIDE + (CMP_BLOCK - 1) <= t_c
    lm = jnp.where(valid_c, lc, NEG_INF)
    e = jnp.where(valid_c, jnp.exp(lm - jnp.max(lm, axis=-1, keepdims=True)), 0.0)
    den = jnp.sum(e, axis=-1, keepdims=True)
    p_c = e / jnp.where(den > 0.0, den, 1.0)
    o_cmp = jnp.dot(p_c.astype(BF16), vc_ref[0, 0], preferred_element_type=F32)

    p_sum = p_c[0:TQ] + p_c[TQ:2 * TQ] + p_c[2 * TQ:3 * TQ] + p_c[3 * TQ:4 * TQ]
    p_hi = p_sum.astype(BF16)
    r1 = p_sum - p_hi.astype(F32)
    p_mid = r1.astype(BF16)
    p_lo = (r1 - p_mid.astype(F32)).astype(BF16)
    smap = smap_ref[...]
    imp = (lax.dot_general(smap, p_hi, _NT, preferred_element_type=F32)
           + lax.dot_general(smap, p_mid, _NT, preferred_element_type=F32)
           + lax.dot_general(smap, p_lo, _NT, preferred_element_type=F32))
    blk = lax.broadcasted_iota(jnp.int32, (SEL_FEATS, TQ), 0)
    t_s = i * TQ + lax.broadcasted_iota(jnp.int32, (SEL_FEATS, TQ), 1)
    cur = t_s // SEL_BLOCK
    forced = (blk == 0) | (blk == cur) | (blk == cur - 1)
    visible = blk * SEL_BLOCK <= t_s
    score = jnp.where(visible, imp + jnp.where(forced, FORCE_SCORE, 0.0), -1.0)
    rank = jnp.zeros((SEL_FEATS, TQ), jnp.int32)
    for sp in range(SEL_FEATS):
        row = score[sp:sp + 1, :]
        beats = (row > score) | ((row == score) & (blk > sp))
        rank = rank + beats.astype(jnp.int32)
    pen_t = jnp.where(rank < SEL_TOPK, 0.0, NEG_INF)
    pen = jnp.transpose(pen_t).astype(BF16)
    q_aug = jnp.concatenate([qb, jnp.concatenate([pen] * HEADS_PER_GROUP, axis=0)], axis=1)

    def keys(ref, pos, n):
        return ref[0, 0, pl.ds(pl.multiple_of(pos + KV_PAD, TQ), n), :]

    m_sc[...] = jnp.full((rows, 1), NEG_INF, F32)
    l_sc[...] = jnp.zeros((rows, 1), F32)
    acc_sc[...] = jnp.zeros((rows, HEAD_DIM), F32)
    per_chunk = SEL_CHUNK // TQ
    n_plain = jnp.maximum(i - 1, 0)
    rem = n_plain % per_chunk

    @pl.when(rem > 0)
    def _():
        pos = (rem - per_chunk) * TQ
        s = _logits(q_aug, keys(ksel_ref, pos, SEL_CHUNK), lo_cols=-pos)
        _flash_step(s, keys(vsel_ref, pos, SEL_CHUNK), m_sc, l_sc, acc_sc)

    def sel_body(c, carry):
        pos = (rem + c * per_chunk) * TQ
        _flash_step(_logits(q_aug, keys(ksel_ref, pos, SEL_CHUNK)), keys(vsel_ref, pos, SEL_CHUNK), m_sc, l_sc, acc_sc)
        return carry

    lax.fori_loop(0, n_plain // per_chunk, sel_body, 0)
    pos = (i - 1) * TQ
    s = _logits(q_aug, keys(ksel_ref, pos, 2 * TQ), add=wa_ref[0, :, WINDOW - TQ:], lo_cols=-pos)
    _flash_step(s, keys(vsel_ref, pos, 2 * TQ), m_sc, l_sc, acc_sc)
    o_sel = acc_sc[...] / l_sc[...]

    pos = i * TQ - WINDOW
    s = _logits(qb, keys(kwin_ref, pos, WINDOW + TQ), add=wa_ref[0], lo_cols=-pos)
    p = jnp.exp(s - jnp.max(s, axis=-1, keepdims=True))
    o_win = (jnp.dot(p.astype(BF16), keys(vwin_ref, pos, WINDOW + TQ), preferred_element_type=F32)
             / jnp.sum(p, axis=-1, keepdims=True))

    gate = gate_ref[0]
    outs = []
    for j in range(HEADS_PER_GROUP):
        sl = slice(j * TQ, (j + 1) * TQ)
        outs.append(gate[:, 3 * j:3 * j + 1] * o_cmp[sl] + gate[:, 3 * j + 1:3 * j + 2] * o_sel[sl]
                    + gate[:, 3 * j + 2:3 * j + 3] * o_win[sl])
    o_ref[0] = jnp.concatenate(outs, axis=1)


def _attention(z_q, z_gate, q_norm_g, kc, vc, ksel, vsel, kwin, vwin, rel_bias):
    b_, s_, _ = z_q.shape
    ncp = kc.shape[2]
    n_cmp = ncp - CMP_BLOCK // CMP_STRIDE + 1
    assert s_ // SEL_BLOCK <= SEL_FEATS and s_ % TQ == 0
    wa, bc = _bias_tiles(rel_bias, ncp)
    smap_t = jnp.asarray(_sel_mapping_t(ncp, n_cmp), BF16)
    rows = HEADS_PER_GROUP * TQ
    gw = HEADS_PER_GROUP * HEAD_DIM
    pad = lambda a: jnp.pad(a, ((0, 0), (0, 0), (KV_PAD, 0), (0, 0)))
    ksel, vsel, kwin, vwin = pad(ksel), pad(vsel), pad(kwin), pad(vwin)
    kv = lambda w: pl.BlockSpec((1, 1, s_ + KV_PAD, w), lambda b, g, i: (b, g, 0, 0))
    cm = pl.BlockSpec((1, 1, ncp, HEAD_DIM), lambda b, g, i: (b, g, 0, 0))
    return pl.pallas_call(
        _attn_kernel,
        grid=(b_, N_KV_GROUPS, s_ // TQ),
        in_specs=[
            pl.BlockSpec((1, TQ, gw), lambda b, g, i: (b, i, g)),
            pl.BlockSpec((1, TQ, LANES), lambda b, g, i: (b, i, g)),
            pl.BlockSpec((1, HEAD_DIM), lambda b, g, i: (0, 0)),
            cm, cm, kv(HEAD_DIM + SEL_FEATS), kv(HEAD_DIM), kv(HEAD_DIM), kv(HEAD_DIM),
            pl.BlockSpec((1, rows, WINDOW + TQ), lambda b, g, i: (g, 0, 0)),
            pl.BlockSpec((1, rows, ncp), lambda b, g, i: (g, 0, 0)),
            pl.BlockSpec((SEL_FEATS, ncp), lambda b, g, i: (0, 0)),
        ],
        out_specs=pl.BlockSpec((1, TQ, gw), lambda b, g, i: (b, i, g)),
        out_shape=jax.ShapeDtypeStruct((b_, s_, N_KV_GROUPS * gw), F32),
        scratch_shapes=[pltpu.VMEM((rows, 1), F32), pltpu.VMEM((rows, 1), F32), pltpu.VMEM((rows, HEAD_DIM), F32)],
        compiler_params=pltpu.CompilerParams(
            dimension_semantics=("parallel", "parallel", "arbitrary"), vmem_limit_bytes=VMEM_LIMIT),
    )(z_q, z_gate, q_norm_g.reshape(1, HEAD_DIM), kc, vc, ksel, vsel, kwin, vwin, wa, bc, smap_t)


def _merge_kernel(x_ref, yp_ref, ya_ref, zm_ref, g1_ref, wbp_ref, wba_ref, wo_ref, o_ref):
    d_ = x_ref.shape[-1]
    bp = jnp.dot(yp_ref[0].astype(BF16), wbp_ref[...], preferred_element_type=F32)
    ba = jnp.dot(ya_ref[0].astype(BF16), wba_ref[...], preferred_element_type=F32)
    zm = zm_ref[0]
    mixed = jax.nn.sigmoid(zm[:, :d_]) * bp + jax.nn.sigmoid(zm[:, d_:]) * ba
    proj = jnp.dot(mixed.astype(BF16), wo_ref[...], preferred_element_type=F32)
    o_ref[0] = x_ref[0] + g1_ref[0] * proj


def _merge(x, y_pool, y_attn, z_merge, gate1, wbp, wba, wo):
    b_, s_, d_ = x.shape
    row = lambda w: pl.BlockSpec((1, ROW_TILE, w), lambda b, i: (b, i, 0))
    full = lambda a: pl.BlockSpec(a.shape, lambda b, i: (0, 0))
    return pl.pallas_call(
        _merge_kernel,
        grid=(b_, s_ // ROW_TILE),
        in_specs=[row(d_), row(POOL_WIDTH), row(Q_WIDTH), row(2 * d_),
                  pl.BlockSpec((1, 1, d_), lambda b, i: (b, 0, 0)),
                  full(wbp), full(wba), full(wo)],
        out_specs=row(d_),
        out_shape=jax.ShapeDtypeStruct((b_, s_, d_), F32),
        compiler_params=pltpu.CompilerParams(
            dimension_semantics=("parallel", "parallel"), vmem_limit_bytes=VMEM_LIMIT),
    )(x, y_pool, y_attn, z_merge, gate1, wbp, wba, wo)


_CAND_ROWS = PEER_TOPK + 7 * 8 + 8


def _topk_rows(x, k):
    n = x.shape[0]
    rid = lax.broadcasted_iota(jnp.int32, x.shape, 0)
    vals, idxs = [], []
    for _ in range(k):
        m = jnp.max(x, axis=0, keepdims=True)
        idx = jnp.min(jnp.where(x == m, rid, n), axis=0, keepdims=True)
        vals.append(m)
        idxs.append(idx)
        x = jnp.where(rid == idx, -jnp.inf, x)
    return jnp.concatenate(vals, axis=0), jnp.concatenate(idxs, axis=0)


def _pair_grid(r0, r1, combine):
    parts = [combine(r0[0:1], r1)]
    parts += [combine(r0[a:a + 1], r1[0:8]) for a in range(1, 8)]
    parts.append(combine(r0[8:16], r1[0:1]))
    return jnp.concatenate(parts, axis=0)


def _peer_route_kernel(x_ref, g_ref, sc_ref, sh_ref, w_ref, keys_ref, h_ref, eid_ref, gw_ref):
    x = x_ref[0]
    y = x * lax.rsqrt(jnp.mean(x * x, axis=-1, keepdims=True) + EPS)
    h = y * g_ref[...] * (1.0 + sc_ref[0]) + sh_ref[0]
    h_ref[0] = h
    qv = jnp.dot(h.astype(BF16), w_ref[...], preferred_element_type=F32).astype(BF16)
    half = PEER_QDIM // 2
    eids, gws = [], []
    for hd in range(PEER_HEADS):
        tops = []
        for p in range(2):
            c = hd * 2 + p
            sc = lax.dot_general(keys_ref[c], qv[:, c * half:(c + 1) * half], _NT,
                                 preferred_element_type=F32)
            tops.append(_topk_rows(sc, PEER_TOPK))
        (s0, i0), (s1, i1) = tops
        cand = _pair_grid(s0, s1, lambda a, b: a + b)
        cid = _pair_grid(i0, i1, lambda a, b: a * PEER_KEYS + b)
        rid = lax.broadcasted_iota(jnp.int32, cand.shape, 0)
        top_s, top_e = [], []
        for _ in range(PEER_TOPK):
            m = jnp.max(cand, axis=0, keepdims=True)
            idx = jnp.min(jnp.where(cand == m, rid, _CAND_ROWS), axis=0, keepdims=True)
            hit = rid == idx
            top_s.append(m)
            top_e.append(jnp.sum(jnp.where(hit, cid, 0), axis=0, keepdims=True))
            cand = jnp.where(hit, -jnp.inf, cand)
        top_s = jnp.concatenate(top_s, axis=0)
        e = jnp.exp(top_s - top_s[0:1])
        gws.append(e / jnp.sum(e, axis=0, keepdims=True))
        eids.append(jnp.concatenate(top_e, axis=0))
    eid_ref[0] = jnp.transpose(jnp.concatenate(eids, axis=0))
    gw_ref[0] = jnp.transpose(jnp.concatenate(gws, axis=0))


def _peer_route(x, g, scale, shift, w_bf16, keys_bf16):
    b_, s_, d_ = x.shape
    n = w_bf16.shape[1]
    nk = PEER_HEADS * PEER_TOPK
    row = lambda w: pl.BlockSpec((1, PEER_TILE, w), lambda b, i: (b, i, 0))
    return pl.pallas_call(
        _peer_route_kernel,
        grid=(b_, s_ // PEER_TILE),
        in_specs=[
            row(d_),
            pl.BlockSpec((1, d_), lambda b, i: (0, 0)),
            pl.BlockSpec((1, 1, d_), lambda b, i: (b, 0, 0)),
            pl.BlockSpec((1, 1, d_), lambda b, i: (b, 0, 0)),
            pl.BlockSpec((d_, n), lambda b, i: (0, 0)),
            pl.BlockSpec(keys_bf16.shape, lambda b, i: (0, 0, 0)),
        ],
        out_specs=[row(d_), row(nk), row(nk)],
        out_shape=[
            jax.ShapeDtypeStruct((b_, s_, d_), F32),
            jax.ShapeDtypeStruct((b_, s_, nk), jnp.int32),
            jax.ShapeDtypeStruct((b_, s_, nk), F32),
        ],
        compiler_params=pltpu.CompilerParams(
            dimension_semantics=("parallel", "parallel"), vmem_limit_bytes=VMEM_LIMIT),
    )(x, g.reshape(1, d_), scale, shift, w_bf16, keys_bf16)


_HI_MASK = -65536


def _pack_table(t):
    e_, d_ = t.shape
    bits = lax.bitcast_convert_type(t.astype(jnp.bfloat16), jnp.uint16).astype(jnp.uint32)
    words = (bits[:, d_ // 2:] << 16) | bits[:, :d_ // 2]
    return lax.bitcast_convert_type(words, jnp.int32).reshape(e_, d_ // 2 // LANES, LANES)


def _unpack(words):
    lo = lax.bitcast_convert_type(lax.shift_left(words, 16), F32)
    hi = lax.bitcast_convert_type(words & _HI_MASK, F32)
    return lo, hi


def _load_resident(j, tab_hbm, tab_vmem, eid_ref, idx_smem, sem):
    @pl.when(j == 0)
    def _():
        cp = pltpu.make_async_copy(tab_hbm, tab_vmem, sem)
        cp.start()
        cp.wait()

    cp = pltpu.make_async_copy(eid_ref, idx_smem, sem)
    cp.start()
    cp.wait()


def _fold_pairs(q, sub):
    m2, m1 = (sub & 2) == 0, (sub & 1) == 0
    u = [jnp.where(m2, q[i], q[i + 2]) + jnp.where(m2, pltpu.roll(q[i], 6, axis=0), pltpu.roll(q[i + 2], 2, axis=0))
         for i in range(2)]
    return jnp.where(m1, u[0], u[1]) + jnp.where(m1, pltpu.roll(u[0], 7, axis=0), pltpu.roll(u[1], 1, axis=0))


def _peer_u_kernel(eid_ref, gw_ref, h_ref, tab_hbm, w_ref, idx_smem, tab_vmem, acc_ref, sem):
    j = pl.program_id(0)
    nk = PEER_HEADS * PEER_TOPK
    half = D_MODEL // 2 // LANES
    _load_resident(j, tab_hbm, tab_vmem, eid_ref, idx_smem, sem)
    sub = lax.broadcasted_iota(jnp.int32, (2 * half, LANES), 0)
    lane = lax.broadcasted_iota(jnp.int32, (nk, PEER_EXPERT_TILE), 1)
    acc_ref[...] = jnp.zeros((nk, PEER_EXPERT_TILE), F32)

    def token(t):
        x = h_ref[t]
        x_lo = jnp.concatenate([x[0:half], x[0:half]], axis=0)
        x_hi = jnp.concatenate([x[half:], x[half:]], axis=0)
        folded = []
        for gi in range(nk // 8):
            q = []
            for i in range(4):
                k = gi * 8 + i
                words = jnp.concatenate([tab_vmem[idx_smem[t, k]], tab_vmem[idx_smem[t, k + 4]]], axis=0)
                lo, hi = _unpack(words)
                q.append(lo * x_lo + hi * x_hi)
            folded.append(_fold_pairs(q, sub))
        return jnp.sum(jnp.concatenate(folded, axis=0), axis=-1, keepdims=True)

    def body(tb, carry):
        acc = acc_ref[...]
        for u in range(PEER_TOKEN_UNROLL):
            t = tb * PEER_TOKEN_UNROLL + u
            acc = jnp.where(lane == t, token(t), acc)
        acc_ref[...] = acc
        return carry

    lax.fori_loop(0, PEER_EXPERT_TILE // PEER_TOKEN_UNROLL, body, 0)
    w_ref[...] = jax.nn.gelu(jnp.transpose(acc_ref[...])) * gw_ref[...]


def _peer_v_kernel(eid_ref, w_ref, x_ref, g2_ref, tab_hbm, o_ref, idx_smem, wb_ref, tab_vmem, sem):
    j = pl.program_id(0)
    nk = PEER_HEADS * PEER_TOPK
    half = D_MODEL // 2 // LANES
    _load_resident(j, tab_hbm, tab_vmem, eid_ref, idx_smem, sem)
    n_acc = 4

    def token(t, slot):
        wb_ref[slot] = jnp.transpose(jnp.broadcast_to(w_ref[pl.ds(t, 1), :], (nk, nk)))
        lo_acc = [jnp.zeros((half, LANES), F32) for _ in range(n_acc)]
        hi_acc = [jnp.zeros((half, LANES), F32) for _ in range(n_acc)]
        for k in range(nk):
            lo, hi = _unpack(tab_vmem[idx_smem[t, k]])
            wk = wb_ref[slot, k:k + 1, :]
            lo_acc[k % n_acc] = lo_acc[k % n_acc] + wk * lo
            hi_acc[k % n_acc] = hi_acc[k % n_acc] + wk * hi
        ff = jnp.concatenate([sum(lo_acc[1:], lo_acc[0]), sum(hi_acc[1:], hi_acc[0])], axis=0)
        o_ref[t] = x_ref[t] + g2_ref[0] * ff

    def body(tb, carry):
        for u in range(PEER_TOKEN_UNROLL):
            token(tb * PEER_TOKEN_UNROLL + u, u)
        return carry

    lax.fori_loop(0, PEER_EXPERT_TILE // PEER_TOKEN_UNROLL, body, 0)


def _peer_experts(eid, gw, h2, x1, gate2, u_tab, v_tab):
    b_, s_, d_ = x1.shape
    n_tok = b_ * s_
    nk = PEER_HEADS * PEER_TOPK
    sub_rows = d_ // LANES
    tt = PEER_EXPERT_TILE
    nsteps = n_tok // tt
    steps_per_batch = s_ // tt
    tok = pl.BlockSpec((tt, sub_rows, LANES), lambda j: (j, 0, 0))
    sel = pl.BlockSpec((tt, nk), lambda j: (j, 0))
    params = pltpu.CompilerParams(dimension_semantics=("arbitrary",), vmem_limit_bytes=PEER_VMEM_LIMIT)
    eid2 = eid.reshape(n_tok, nk)
    w = pl.pallas_call(
        _peer_u_kernel,
        grid=(nsteps,),
        in_specs=[sel, sel, tok, pl.BlockSpec(memory_space=pl.ANY)],
        out_specs=sel,
        out_shape=jax.ShapeDtypeStruct((n_tok, nk), F32),
        scratch_shapes=[pltpu.SMEM((tt, nk), jnp.int32), pltpu.VMEM(u_tab.shape, jnp.int32),
                        pltpu.VMEM((nk, tt), F32), pltpu.SemaphoreType.DMA],
        compiler_params=params,
    )(eid2, gw.reshape(n_tok, nk), h2.reshape(n_tok, sub_rows, LANES), u_tab)
    out = pl.pallas_call(
        _peer_v_kernel,
        grid=(nsteps,),
        in_specs=[sel, sel, tok, pl.BlockSpec((1, sub_rows, LANES), lambda j: (j // steps_per_batch, 0, 0)),
                  pl.BlockSpec(memory_space=pl.ANY)],
        out_specs=tok,
        out_shape=jax.ShapeDtypeStruct((n_tok, sub_rows, LANES), F32),
        scratch_shapes=[pltpu.SMEM((tt, nk), jnp.int32), pltpu.VMEM((PEER_TOKEN_UNROLL, nk, LANES), F32),
                        pltpu.VMEM(v_tab.shape, jnp.int32), pltpu.SemaphoreType.DMA],
        compiler_params=params,
    )(eid2, w, x1.reshape(n_tok, sub_rows, LANES), gate2.reshape(b_, sub_rows, LANES), v_tab)
    return out.reshape(b_, s_, d_)


def _pool_mixer(u, w_pool, pool_scale):
    b_, s_, c_ = u.shape
    csp = jnp.concatenate([jnp.zeros((b_, 1, c_), jnp.float32), jnp.cumsum(u, axis=1)], axis=1)
    t = jnp.arange(s_)
    outs = []
    for gi, w in enumerate(POOL_WINDOWS):
        sl = slice(gi * POOL_GROUP_DIM, (gi + 1) * POOL_GROUP_DIM)
        cg = csp[:, :, sl]
        upper = cg[:, 1:]
        lower = jnp.pad(cg[:, :s_ + 1 - w], ((0, 0), (w - 1, 0), (0, 0)))
        cnt = jnp.minimum(t + 1, w).astype(jnp.float32)[:, None]
        outs.append((upper - lower) / cnt - u[:, :, sl])
    pooled = jnp.stack(outs, axis=2)
    y = jnp.einsum('bsgc,gcd->bsgd', pooled, w_pool).reshape(b_, s_, POOL_WIDTH)
    return y * pool_scale


def _permute_w_in(w):
    d_ = w.shape[0]
    kv_end = POOL_WIDTH + Q_WIDTH + 6 * KV_WIDTH
    per_group = HEADS_PER_GROUP * 3
    parts = [w[:, :kv_end], w[:, kv_end + GATE_COLS:]]
    for g in range(N_KV_GROUPS):
        parts.append(w[:, kv_end + g * per_group:kv_end + (g + 1) * per_group])
        parts.append(jnp.zeros((d_, LANES - per_group), w.dtype))
    return jnp.concatenate(parts, axis=1).astype(BF16)


def kernel(x, c, rel_bias, ada_w, ada_b, norm1_g, norm2_g, w_in, pool_w, pool_scale, cmp_pe_k, cmp_w1_k, cmp_w2_k, cmp_pe_v, cmp_w1_v, cmp_w2_v, q_norm_g, k_norm_g, w_branch_pool, w_branch_attn, w_out, peer_w_q, peer_sub_keys, peer_u, peer_v):
    b_, s_, d_ = x.shape
    l = 0
    ada = (jax.nn.silu(c) @ ada_w[l] + ada_b[l]).reshape(b_, 6, 1, d_)
    shift1, scale1, gate1 = ada[:, 0], ada[:, 1], ada[:, 2]
    shift2, scale2, gate2 = ada[:, 3], ada[:, 4], ada[:, 5]

    (z_pool, z_q, ksel, vsel, kwin, vwin, kc_raw, vc_raw, z_merge, z_gate) = _in_proj(
        x, norm1_g[l], scale1, shift1, _permute_w_in(w_in[l]), k_norm_g[l])

    y_pool = _pool_mixer(z_pool, pool_w[l], pool_scale[l])
    kc = _compress(kc_raw, cmp_pe_k[l], cmp_w1_k[l], cmp_w2_k[l], k_norm_g[l, 0], True)
    vc = _compress(vc_raw, cmp_pe_v[l], cmp_w1_v[l], cmp_w2_v[l], k_norm_g[l, 0], False)
    y_attn = _attention(z_q, z_gate, q_norm_g[l], kc, vc, ksel, vsel, kwin, vwin, rel_bias)

    x1 = _merge(x, y_pool, y_attn, z_merge, gate1,
                w_branch_pool[l].astype(BF16), w_branch_attn[l].astype(BF16), w_out[l].astype(BF16))

    keys = peer_sub_keys[l].reshape(PEER_HEADS * 2, PEER_KEYS, PEER_QDIM // 2).astype(BF16)
    h2, eid, gw = _peer_route(x1, norm2_g[l], scale2, shift2, peer_w_q[l].astype(BF16), keys)
    return _peer_experts(eid, gw, h2, x1, gate2, _pack_table(peer_u[l]), _pack_table(peer_v[l]))
```

```python
import math
from functools import partial

import jax
import jax.numpy as jnp
import numpy as np
from jax import lax
from jax.experimental import pallas as pl
from jax.experimental.pallas import tpu as pltpu

D_MODEL = 1024
POOL_WIDTH = 512
POOL_WINDOWS = (2, 4, 8, 16)
POOL_GROUP_DIM = 128
N_HEADS = 8
N_KV_GROUPS = 2
HEADS_PER_GROUP = 4
HEAD_DIM = 64
Q_WIDTH = 512
KV_WIDTH = 128
CMP_BLOCK = 32
CMP_STRIDE = 16
CMP_HIDDEN = 128
SEL_BLOCK = 64
SEL_TOPK = 16
WINDOW = 512
FORCE_SCORE = 1000.0
N_BUCKETS = 32
MAX_DISTANCE = 128
PEER_HEADS = 8
PEER_KEYS = 128
PEER_TOPK = 16
PEER_QDIM = 256
PEER_CHUNK = 128
GATE_COLS = N_HEADS * 3
EPS = 1e-6
NEG_INF = -1e30

LANES = 128
VMEM_LIMIT = 48 * 1024 * 1024
ROW_TILE = 512
POOL_HALO = 16
TQ = 128
SEL_FEATS = 64
SEL_CHUNK = 512
KV_PAD = 512
CMP_BAND_LO = 10
PEER_TILE = 256
PEER_EXPERT_TILE = 128
PEER_TOKEN_UNROLL = 4
EXPERT_ROWS = D_MODEL // 2 // LANES
PEER_VMEM_LIMIT = 56 * 1024 * 1024

BF16 = jnp.bfloat16
F32 = jnp.float32
_NT = (((1,), (1,)), ((), ()))


def _rms_rows(x, g):
    return x * lax.rsqrt(jnp.mean(x * x, axis=-1, keepdims=True) + EPS) * g


def _in_proj_kernel(x_ref, g_ref, sc_ref, sh_ref, w_ref, kg_ref,
                    zpool_ref, zq_ref, ksel_ref, vsel_ref, kwin_ref, vwin_ref, kcr_ref, vcr_ref,
                    zmerge_ref, zgate_ref):
    i = pl.program_id(1)
    x = x_ref[0]
    y = x * lax.rsqrt(jnp.mean(x * x, axis=-1, keepdims=True) + EPS)
    h = y * g_ref[...] * (1.0 + sc_ref[0]) + sh_ref[0]
    z = jnp.dot(h.astype(BF16), w_ref[...], preferred_element_type=F32)
    ts = x.shape[0]
    zpool_ref[0] = z[:, :POOL_WIDTH]
    o = POOL_WIDTH
    zq_ref[0] = z[:, o:o + Q_WIDTH]
    o += Q_WIDTH
    pos = i * ts + lax.broadcasted_iota(jnp.int32, (ts, SEL_FEATS), 0)
    onehot = (pos // SEL_BLOCK == lax.broadcasted_iota(jnp.int32, (ts, SEL_FEATS), 1)).astype(BF16)
    for g in range(N_KV_GROUPS):
        def col(k):
            return z[:, o + k * KV_WIDTH + g * HEAD_DIM:o + k * KV_WIDTH + (g + 1) * HEAD_DIM]
        kcr_ref[0, g] = col(0)
        vcr_ref[0, g] = col(1)
        ks = _rms_rows(col(2), kg_ref[1:2, :]).astype(BF16)
        ksel_ref[0, g] = jnp.concatenate([ks, onehot], axis=1)
        vsel_ref[0, g] = col(3).astype(BF16)
        kwin_ref[0, g] = _rms_rows(col(4), kg_ref[2:3, :]).astype(BF16)
        vwin_ref[0, g] = col(5).astype(BF16)
    o += 6 * KV_WIDTH
    zmerge_ref[0] = z[:, o:o + 2 * D_MODEL]
    o += 2 * D_MODEL
    zgate_ref[0] = jax.nn.sigmoid(z[:, o:o + N_KV_GROUPS * LANES])


def _in_proj(x, g, scale, shift, w_bf16, k_norm_g):
    b_, s_, d_ = x.shape
    n = w_bf16.shape[1]
    row = lambda w: pl.BlockSpec((1, ROW_TILE, w), lambda b, i: (b, i, 0))
    grp = lambda w: pl.BlockSpec((1, N_KV_GROUPS, ROW_TILE, w), lambda b, i: (b, 0, i, 0))
    gshape = lambda w, dt: jax.ShapeDtypeStruct((b_, N_KV_GROUPS, s_, w), dt)
    return pl.pallas_call(
        _in_proj_kernel,
        grid=(b_, s_ // ROW_TILE),
        in_specs=[
            row(d_),
            pl.BlockSpec((1, d_), lambda b, i: (0, 0)),
            pl.BlockSpec((1, 1, d_), lambda b, i: (b, 0, 0)),
            pl.BlockSpec((1, 1, d_), lambda b, i: (b, 0, 0)),
            pl.BlockSpec((d_, n), lambda b, i: (0, 0)),
            pl.BlockSpec((3, HEAD_DIM), lambda b, i: (0, 0)),
        ],
        out_specs=[row(POOL_WIDTH), row(Q_WIDTH), grp(HEAD_DIM + SEL_FEATS), grp(HEAD_DIM), grp(HEAD_DIM),
                   grp(HEAD_DIM), grp(HEAD_DIM), grp(HEAD_DIM), row(2 * d_), row(N_KV_GROUPS * LANES)],
        out_shape=[
            jax.ShapeDtypeStruct((b_, s_, POOL_WIDTH), F32),
            jax.ShapeDtypeStruct((b_, s_, Q_WIDTH), F32),
            gshape(HEAD_DIM + SEL_FEATS, BF16), gshape(HEAD_DIM, BF16), gshape(HEAD_DIM, BF16), gshape(HEAD_DIM, BF16),
            gshape(HEAD_DIM, F32), gshape(HEAD_DIM, F32),
            jax.ShapeDtypeStruct((b_, s_, 2 * d_), F32),
            jax.ShapeDtypeStruct((b_, s_, N_KV_GROUPS * LANES), F32),
        ],
        compiler_params=pltpu.CompilerParams(
            dimension_semantics=("parallel", "parallel"), vmem_limit_bytes=VMEM_LIMIT),
    )(x, g.reshape(1, d_), scale, shift, w_bf16, k_norm_g)


def _compress_kernel(t_ref, pe_ref, w1_ref, w2_ref, g_ref, o_ref, *, normalize):
    half = CMP_STRIDE * HEAD_DIM
    t = t_ref[0, 0].astype(BF16)
    w1 = w1_ref[...]
    a = jnp.dot(t, w1[:half], preferred_element_type=F32)
    b = jnp.dot(t, w1[half:], preferred_element_type=F32)
    pe = jnp.dot(pe_ref[...], w1, preferred_element_type=F32)
    n = a.shape[0]
    b_next = pltpu.roll(b, n - 1, axis=0)
    hid = jax.nn.gelu(a + b_next + pe)
    out = jnp.dot(hid.astype(BF16), w2_ref[...], preferred_element_type=F32)
    if normalize:
        out = _rms_rows(out, g_ref[...])
    o_ref[0, 0] = out.astype(BF16)


def _compress(t_raw, pe, w1, w2, g, normalize):
    b_, g_, s_, dh = t_raw.shape
    n_str = s_ // CMP_STRIDE
    t = t_raw.reshape(b_, g_, n_str, CMP_STRIDE * dh)
    return pl.pallas_call(
        partial(_compress_kernel, normalize=normalize),
        grid=(b_, g_),
        in_specs=[
            pl.BlockSpec((1, 1, n_str, CMP_STRIDE * dh), lambda b, g: (b, g, 0, 0)),
            pl.BlockSpec((1, CMP_BLOCK * dh), lambda b, g: (0, 0)),
            pl.BlockSpec((CMP_BLOCK * dh, CMP_HIDDEN), lambda b, g: (0, 0)),
            pl.BlockSpec((CMP_HIDDEN, dh), lambda b, g: (0, 0)),
            pl.BlockSpec((1, dh), lambda b, g: (0, 0)),
        ],
        out_specs=pl.BlockSpec((1, 1, n_str, dh), lambda b, g: (b, g, 0, 0)),
        out_shape=jax.ShapeDtypeStruct((b_, g_, n_str, dh), BF16),
        compiler_params=pltpu.CompilerParams(
            dimension_semantics=("parallel", "parallel"), vmem_limit_bytes=VMEM_LIMIT),
    )(t, pe.reshape(1, CMP_BLOCK * dh).astype(BF16), w1.astype(BF16), w2.astype(BF16), g.reshape(1, dh))


def _t5_bucket_np(rel):
    n = np.maximum(rel, 0)
    max_exact = N_BUCKETS // 2
    nf = np.maximum(n, 1).astype(np.float32)
    large = max_exact + (np.log(nf / max_exact) / math.log(MAX_DISTANCE / max_exact) * (N_BUCKETS - max_exact)).astype(np.int32)
    large = np.minimum(large, N_BUCKETS - 1)
    return np.where(n < max_exact, n, large)


def _bias_tiles(rel_bias, n_cmp_pad):
    far = rel_bias[N_BUCKETS - 1]
    ii = np.arange(TQ)[:, None]

    def lookup(dist):
        near = (dist >= 0) & (dist < MAX_DISTANCE)
        bucket = _t5_bucket_np(np.where(near, dist, MAX_DISTANCE))
        b = rel_bias[bucket] - far
        b = jnp.transpose(b, (2, 0, 1))
        return b.reshape(N_KV_GROUPS, HEADS_PER_GROUP * dist.shape[0], dist.shape[1])

    jj = np.arange(TQ)[None, :]
    bt = jnp.stack([lookup(ii - jj), lookup(TQ + ii - jj)], axis=1)
    cc = np.arange(n_cmp_pad)[None, :]
    dist_c = ii - CMP_STRIDE * (cc - CMP_BAND_LO) - (CMP_BLOCK - 1)
    dist_c = np.where(cc <= CMP_BAND_LO + TQ // CMP_STRIDE, dist_c, -1)
    bc = lookup(dist_c)
    rows = HEADS_PER_GROUP * TQ
    i_r = np.tile(np.arange(TQ), HEADS_PER_GROUP)[:, None]
    first = jnp.asarray(np.where(jj > i_r, 0.0, NEG_INF), F32)
    last = jnp.where(jnp.asarray(jj <= i_r), bt[:, 0], NEG_INF)
    zeros = jnp.zeros((N_KV_GROUPS, rows, WINDOW - 2 * TQ), F32)
    wa = jnp.concatenate([jnp.broadcast_to(first, (N_KV_GROUPS, rows, TQ)), zeros, bt[:, 1], last], axis=-1)
    return wa, bc


def _sel_mapping_t(n_cmp_pad, n_cmp):
    m = np.zeros((SEL_FEATS, n_cmp_pad), np.float32)
    pos = np.arange(n_cmp)[:, None] * CMP_STRIDE + np.arange(CMP_BLOCK)[None, :]
    np.add.at(m, ((pos // SEL_BLOCK).ravel(), np.repeat(np.arange(n_cmp), CMP_BLOCK)), 1.0 / CMP_BLOCK)
    return m


def _logits(q, k, add=None, lo_cols=None):
    s = lax.dot_general(q, k, _NT, preferred_element_type=F32)
    if add is not None:
        s = s + add
    if lo_cols is not None:
        s = jnp.where(lax.broadcasted_iota(jnp.int32, s.shape, 1) < lo_cols, NEG_INF, s)
    return s


def _flash_step(s, v, m_sc, l_sc, acc_sc):
    m_prev = m_sc[...]
    m_new = jnp.maximum(m_prev, jnp.max(s, axis=-1, keepdims=True))
    alpha = jnp.exp(m_prev - m_new)
    p = jnp.exp(s - m_new)
    l_sc[...] = alpha * l_sc[...] + jnp.sum(p, axis=-1, keepdims=True)
    acc_sc[...] = alpha * acc_sc[...] + jnp.dot(p.astype(BF16), v, preferred_element_type=F32)
    m_sc[...] = m_new


def _attn_kernel(zq_ref, gate_ref, qg_ref, kc_ref, vc_ref, ksel_ref, vsel_ref, kwin_ref, vwin_ref,
                 wa_ref, bc_ref, smap_ref, o_ref, m_sc, l_sc, acc_sc):
    i = pl.program_id(2)
    rows = HEADS_PER_GROUP * TQ
    ncp = kc_ref.shape[2]
    scale = HEAD_DIM ** -0.5

    zq = zq_ref[0]
    qs = jnp.concatenate([zq[:, j * HEAD_DIM:(j + 1) * HEAD_DIM] for j in range(HEADS_PER_GROUP)], axis=0)
    qn = _rms_rows(qs, qg_ref[...]) * scale
    qb = qn.astype(BF16)

    lc = lax.dot_general(qb, kc_ref[0, 0], _NT, preferred_element_type=F32)
    lc = lc + pltpu.roll(bc_ref[0], (i * (TQ // CMP_STRIDE) + ncp - CMP_BAND_LO) % ncp, axis=1)
    t_c = i * TQ + lax.broadcasted_iota(jnp.int32, (rows, ncp), 0) % TQ
    n_c = lax.broadcasted_iota(jnp.int32, (rows, ncp), 1)
    valid_c = n_c * CMP_STRIDE + (CMP_BLOCK - 1) <= t_c
    lm = jnp.where(valid_c, lc, NEG_INF)
    e = jnp.where(valid_c, jnp.exp(lm - jnp.max(lm, axis=-1, keepdims=True)), 0.0)
    den = jnp.sum(e, axis=-1, keepdims=True)
    p_c = e / jnp.where(den > 0.0, den, 1.0)
    o_cmp = jnp.dot(p_c.astype(BF16), vc_ref[0, 0], preferred_element_type=F32)

    p_sum = p_c[0:TQ] + p_c[TQ:2 * TQ] + p_c[2 * TQ:3 * TQ] + p_c[3 * TQ:4 * TQ]
    p_hi = p_sum.astype(BF16)
    r1 = p_sum - p_hi.astype(F32)
    p_mid = r1.astype(BF16)
    p_lo = (r1 - p_mid.astype(F32)).astype(BF16)
    smap = smap_ref[...]
    imp = (lax.dot_general(smap, p_hi, _NT, preferred_element_type=F32)
           + lax.dot_general(smap, p_mid, _NT, preferred_element_type=F32)
           + lax.dot_general(smap, p_lo, _NT, preferred_element_type=F32))
    blk = lax.broadcasted_iota(jnp.int32, (SEL_FEATS, TQ), 0)
    t_s = i * TQ + lax.broadcasted_iota(jnp.int32, (SEL_FEATS, TQ), 1)
    cur = t_s // SEL_BLOCK
    forced = (blk == 0) | (blk == cur) | (blk == cur - 1)
    visible = blk * SEL_BLOCK <= t_s
    score = jnp.where(visible, imp + jnp.where(forced, FORCE_SCORE, 0.0), -1.0)
    rank = jnp.zeros((SEL_FEATS, TQ), jnp.int32)
    for sp in range(SEL_FEATS):
        row = score[sp:sp + 1, :]
        beats = (row > score) | ((row == score) & (blk > sp))
        rank = rank + beats.astype(jnp.int32)
    pen_t = jnp.where(rank < SEL_TOPK, 0.0, NEG_INF)
    pen = jnp.transpose(pen_t).astype(BF16)
    q_aug = jnp.concatenate([qb, jnp.concatenate([pen] * HEADS_PER_GROUP, axis=0)], axis=1)

    def keys(ref, pos, n):
        return ref[0, 0, pl.ds(pl.multiple_of(pos + KV_PAD, TQ), n), :]

    m_sc[...] = jnp.full((rows, 1), NEG_INF, F32)
    l_sc[...] = jnp.zeros((rows, 1), F32)
    acc_sc[...] = jnp.zeros((rows, HEAD_DIM), F32)
    per_chunk = SEL_CHUNK // TQ
    n_plain = jnp.maximum(i - 1, 0)
    rem = n_plain % per_chunk

    @pl.when(rem > 0)
    def _():
        pos = (rem - per_chunk) * TQ
        s = _logits(q_aug, keys(ksel_ref, pos, SEL_CHUNK), lo_cols=-pos)
        _flash_step(s, keys(vsel_ref, pos, SEL_CHUNK), m_sc, l_sc, acc_sc)

    def sel_body(c, carry):
        pos = (rem + c * per_chunk) * TQ
        _flash_step(_logits(q_aug, keys(ksel_ref, pos, SEL_CHUNK)), keys(vsel_ref, pos, SEL_CHUNK), m_sc, l_sc, acc_sc)
        return carry

    lax.fori_loop(0, n_plain // per_chunk, sel_body, 0)
    pos = (i - 1) * TQ
    s = _logits(q_aug, keys(ksel_ref, pos, 2 * TQ), add=wa_ref[0, :, WINDOW - TQ:], lo_cols=-pos)
    _flash_step(s, keys(vsel_ref, pos, 2 * TQ), m_sc, l_sc, acc_sc)
    o_sel = acc_sc[...] / l_sc[...]

    pos = i * TQ - WINDOW
    s = _logits(qb, keys(kwin_ref, pos, WINDOW + TQ), add=wa_ref[0], lo_cols=-pos)
    p = jnp.exp(s - jnp.max(s, axis=-1, keepdims=True))
    o_win = (jnp.dot(p.astype(BF16), keys(vwin_ref, pos, WINDOW + TQ), preferred_element_type=F32)
             / jnp.sum(p, axis=-1, keepdims=True))

    gate = gate_ref[0]
    outs = []
    for j in range(HEADS_PER_GROUP):
        sl = slice(j * TQ, (j + 1) * TQ)
        outs.append(gate[:, 3 * j:3 * j + 1] * o_cmp[sl] + gate[:, 3 * j + 1:3 * j + 2] * o_sel[sl]
                    + gate[:, 3 * j + 2:3 * j + 3] * o_win[sl])
    o_ref[0] = jnp.concatenate(outs, axis=1)


def _attention(z_q, z_gate, q_norm_g, kc, vc, ksel, vsel, kwin, vwin, rel_bias):
    b_, s_, _ = z_q.shape
    ncp = kc.shape[2]
    n_cmp = ncp - CMP_BLOCK // CMP_STRIDE + 1
    assert s_ // SEL_BLOCK <= SEL_FEATS and s_ % TQ == 0
    wa, bc = _bias_tiles(rel_bias, ncp)
    smap_t = jnp.asarray(_sel_mapping_t(ncp, n_cmp), BF16)
    rows = HEADS_PER_GROUP * TQ
    gw = HEADS_PER_GROUP * HEAD_DIM
    pad = lambda a: jnp.pad(a, ((0, 0), (0, 0), (KV_PAD, 0), (0, 0)))
    ksel, vsel, kwin, vwin = pad(ksel), pad(vsel), pad(kwin), pad(vwin)
    kv = lambda w: pl.BlockSpec((1, 1, s_ + KV_PAD, w), lambda b, g, i: (b, g, 0, 0))
    cm = pl.BlockSpec((1, 1, ncp, HEAD_DIM), lambda b, g, i: (b, g, 0, 0))
    return pl.pallas_call(
        _attn_kernel,
        grid=(b_, N_KV_GROUPS, s_ // TQ),
        in_specs=[
            pl.BlockSpec((1, TQ, gw), lambda b, g, i: (b, i, g)),
            pl.BlockSpec((1, TQ, LANES), lambda b, g, i: (b, i, g)),
            pl.BlockSpec((1, HEAD_DIM), lambda b, g, i: (0, 0)),
            cm, cm, kv(HEAD_DIM + SEL_FEATS), kv(HEAD_DIM), kv(HEAD_DIM), kv(HEAD_DIM),
            pl.BlockSpec((1, rows, WINDOW + TQ), lambda b, g, i: (g, 0, 0)),
            pl.BlockSpec((1, rows, ncp), lambda b, g, i: (g, 0, 0)),
            pl.BlockSpec((SEL_FEATS, ncp), lambda b, g, i: (0, 0)),
        ],
        out_specs=pl.BlockSpec((1, TQ, gw), lambda b, g, i: (b, i, g)),
        out_shape=jax.ShapeDtypeStruct((b_, s_, N_KV_GROUPS * gw), F32),
        scratch_shapes=[pltpu.VMEM((rows, 1), F32), pltpu.VMEM((rows, 1), F32), pltpu.VMEM((rows, HEAD_DIM), F32)],
        compiler_params=pltpu.CompilerParams(
            dimension_semantics=("parallel", "parallel", "arbitrary"), vmem_limit_bytes=VMEM_LIMIT),
    )(z_q, z_gate, q_norm_g.reshape(1, HEAD_DIM), kc, vc, ksel, vsel, kwin, vwin, wa, bc, smap_t)


def _pool_tile(halo, tile, i, pw_ref, ps_ref):
    ts = tile.shape[0]
    ext = jnp.concatenate([jnp.where(i > 0, halo, 0.0), tile], axis=0)
    t = i * ts + lax.broadcasted_iota(jnp.int32, (ts, 1), 0)
    outs = []
    for gi, w in enumerate(POOL_WINDOWS):
        sl = slice(gi * POOL_GROUP_DIM, (gi + 1) * POOL_GROUP_DIM)
        run = ext[:, sl]
        span = 1
        while span < w:
            run = run + pltpu.roll(run, span, axis=0)
            span *= 2
        cnt = jnp.minimum(t + 1, w).astype(F32)
        pooled = run[POOL_HALO:] / cnt - tile[:, sl]
        outs.append(jnp.dot(pooled.astype(BF16), pw_ref[gi], preferred_element_type=F32))
    return jnp.concatenate(outs, axis=1) * ps_ref[...]


def _merge_kernel(x_ref, zp_ref, zh_ref, ya_ref, zm_ref, g1_ref, pw_ref, ps_ref, wbp_ref, wba_ref, wo_ref, o_ref):
    d_ = x_ref.shape[-1]
    y_pool = _pool_tile(zh_ref[0], zp_ref[0], pl.program_id(1), pw_ref, ps_ref)
    bp = jnp.dot(y_pool.astype(BF16), wbp_ref[...], preferred_element_type=F32)
    ba = jnp.dot(ya_ref[0].astype(BF16), wba_ref[...], preferred_element_type=F32)
    zm = zm_ref[0]
    mixed = jax.nn.sigmoid(zm[:, :d_]) * bp + jax.nn.sigmoid(zm[:, d_:]) * ba
    proj = jnp.dot(mixed.astype(BF16), wo_ref[...], preferred_element_type=F32)
    o_ref[0] = x_ref[0] + g1_ref[0] * proj


def _merge(x, z_pool, y_attn, z_merge, gate1, pool_w, pool_scale, wbp, wba, wo):
    b_, s_, d_ = x.shape
    row = lambda w: pl.BlockSpec((1, ROW_TILE, w), lambda b, i: (b, i, 0))
    full = lambda a: pl.BlockSpec(a.shape, lambda b, i: (0,) * a.ndim)
    halo_blocks = ROW_TILE // POOL_HALO
    halo = pl.BlockSpec((1, POOL_HALO, POOL_WIDTH), lambda b, i: (b, jnp.maximum(i * halo_blocks - 1, 0), 0))
    pool_scale = pool_scale.reshape(1, POOL_WIDTH)
    return pl.pallas_call(
        _merge_kernel,
        grid=(b_, s_ // ROW_TILE),
        in_specs=[row(d_), row(POOL_WIDTH), halo, row(Q_WIDTH), row(2 * d_),
                  pl.BlockSpec((1, 1, d_), lambda b, i: (b, 0, 0)),
                  full(pool_w), full(pool_scale), full(wbp), full(wba), full(wo)],
        out_specs=row(d_),
        out_shape=jax.ShapeDtypeStruct((b_, s_, d_), F32),
        compiler_params=pltpu.CompilerParams(
            dimension_semantics=("parallel", "parallel"), vmem_limit_bytes=VMEM_LIMIT),
    )(x, z_pool, z_pool, y_attn, z_merge, gate1, pool_w, pool_scale, wbp, wba, wo)


def _ada_kernel(c_ref, w_ref, b_ref, o_ref):
    c = c_ref[...]
    o_ref[...] = jnp.dot(jax.nn.silu(c).astype(BF16), w_ref[...].astype(BF16),
                         preferred_element_type=F32) + b_ref[...]


def _ada(c, w, b):
    b_, d_ = c.shape
    n = w.shape[1]
    return pl.pallas_call(
        _ada_kernel,
        grid=(n // d_,),
        in_specs=[pl.BlockSpec((b_, d_), lambda j: (0, 0)),
                  pl.BlockSpec((d_, d_), lambda j: (0, j)),
                  pl.BlockSpec((1, d_), lambda j: (0, j))],
        out_specs=pl.BlockSpec((b_, d_), lambda j: (0, j)),
        out_shape=jax.ShapeDtypeStruct((b_, n), F32),
        compiler_params=pltpu.CompilerParams(dimension_semantics=("parallel",), vmem_limit_bytes=VMEM_LIMIT),
    )(c, w, b.reshape(1, n))


_CAND_ROWS = PEER_TOPK + 7 * 8 + 8


def _topk_rows(x, k):
    n = x.shape[0]
    rid = lax.broadcasted_iota(jnp.int32, x.shape, 0)
    vals, idxs = [], []
    for _ in range(k):
        m = jnp.max(x, axis=0, keepdims=True)
        idx = jnp.min(jnp.where(x == m, rid, n), axis=0, keepdims=True)
        vals.append(m)
        idxs.append(idx)
        x = jnp.where(rid == idx, -jnp.inf, x)
    return jnp.concatenate(vals, axis=0), jnp.concatenate(idxs, axis=0)


def _pair_grid(r0, r1, combine):
    parts = [combine(r0[0:1], r1)]
    parts += [combine(r0[a:a + 1], r1[0:8]) for a in range(1, 8)]
    parts.append(combine(r0[8:16], r1[0:1]))
    return jnp.concatenate(parts, axis=0)


def _peer_route_kernel(x_ref, g_ref, sc_ref, sh_ref, w_ref, keys_ref, h_ref, eid_ref, gw_ref):
    x = x_ref[0]
    y = x * lax.rsqrt(jnp.mean(x * x, axis=-1, keepdims=True) + EPS)
    h = y * g_ref[...] * (1.0 + sc_ref[0]) + sh_ref[0]
    h_ref[0] = h
    qv = jnp.dot(h.astype(BF16), w_ref[...], preferred_element_type=F32).astype(BF16)
    half = PEER_QDIM // 2
    eids, gws = [], []
    for hd in range(PEER_HEADS):
        tops = []
        for p in range(2):
            c = hd * 2 + p
            sc = lax.dot_general(keys_ref[c], qv[:, c * half:(c + 1) * half], _NT,
                                 preferred_element_type=F32)
            tops.append(_topk_rows(sc, PEER_TOPK))
        (s0, i0), (s1, i1) = tops
        cand = _pair_grid(s0, s1, lambda a, b: a + b)
        cid = _pair_grid(i0, i1, lambda a, b: a * (PEER_KEYS * EXPERT_ROWS) + b * EXPERT_ROWS)
        rid = lax.broadcasted_iota(jnp.int32, cand.shape, 0)
        top_s, top_e = [], []
        for _ in range(PEER_TOPK):
            m = jnp.max(cand, axis=0, keepdims=True)
            idx = jnp.min(jnp.where(cand == m, rid, _CAND_ROWS), axis=0, keepdims=True)
            hit = rid == idx
            top_s.append(m)
            top_e.append(jnp.sum(jnp.where(hit, cid, 0), axis=0, keepdims=True))
            cand = jnp.where(hit, -jnp.inf, cand)
        top_s = jnp.concatenate(top_s, axis=0)
        e = jnp.exp(top_s - top_s[0:1])
        gws.append(e / jnp.sum(e, axis=0, keepdims=True))
        eids.append(jnp.concatenate(top_e, axis=0))
    eid_ref[0] = jnp.transpose(jnp.concatenate(eids, axis=0))
    gw_ref[0] = jnp.transpose(jnp.concatenate(gws, axis=0))


def _peer_route(x, g, scale, shift, w_bf16, keys_bf16):
    b_, s_, d_ = x.shape
    n = w_bf16.shape[1]
    nk = PEER_HEADS * PEER_TOPK
    row = lambda w: pl.BlockSpec((1, PEER_TILE, w), lambda b, i: (b, i, 0))
    return pl.pallas_call(
        _peer_route_kernel,
        grid=(b_, s_ // PEER_TILE),
        in_specs=[
            row(d_),
            pl.BlockSpec((1, d_), lambda b, i: (0, 0)),
            pl.BlockSpec((1, 1, d_), lambda b, i: (b, 0, 0)),
            pl.BlockSpec((1, 1, d_), lambda b, i: (b, 0, 0)),
            pl.BlockSpec((d_, n), lambda b, i: (0, 0)),
            pl.BlockSpec(keys_bf16.shape, lambda b, i: (0, 0, 0)),
        ],
        out_specs=[row(d_), row(nk), row(nk)],
        out_shape=[
            jax.ShapeDtypeStruct((b_, s_, d_), F32),
            jax.ShapeDtypeStruct((b_, s_, nk), jnp.int32),
            jax.ShapeDtypeStruct((b_, s_, nk), F32),
        ],
        compiler_params=pltpu.CompilerParams(
            dimension_semantics=("parallel", "parallel"), vmem_limit_bytes=VMEM_LIMIT),
    )(x, g.reshape(1, d_), scale, shift, w_bf16, keys_bf16)


_HI_MASK = -65536


def _pack_table(t):
    e_, d_ = t.shape
    bits = lax.bitcast_convert_type(t.astype(jnp.bfloat16), jnp.uint16).astype(jnp.uint32)
    words = (bits[:, d_ // 2:] << 16) | bits[:, :d_ // 2]
    return lax.bitcast_convert_type(words, jnp.int32).reshape(e_ * EXPERT_ROWS, LANES)


def _expert_words(tab_vmem, row):
    return tab_vmem[pl.ds(pl.multiple_of(row, EXPERT_ROWS), EXPERT_ROWS), :]


def _unpack(words):
    lo = lax.bitcast_convert_type(lax.shift_left(words, 16), F32)
    hi = lax.bitcast_convert_type(words & _HI_MASK, F32)
    return lo, hi


def _load_resident(j, tab_hbm, tab_vmem, eid_ref, idx_smem, sem):
    @pl.when(j == 0)
    def _():
        cp = pltpu.make_async_copy(tab_hbm, tab_vmem, sem)
        cp.start()
        cp.wait()

    cp = pltpu.make_async_copy(eid_ref, idx_smem, sem)
    cp.start()
    cp.wait()


def _fold_pairs(q, sub):
    m2, m1 = (sub & 2) == 0, (sub & 1) == 0
    u = [jnp.where(m2, q[i], q[i + 2]) + jnp.where(m2, pltpu.roll(q[i], 6, axis=0), pltpu.roll(q[i + 2], 2, axis=0))
         for i in range(2)]
    return jnp.where(m1, u[0], u[1]) + jnp.where(m1, pltpu.roll(u[0], 7, axis=0), pltpu.roll(u[1], 1, axis=0))


def _peer_u_kernel(eid_ref, gw_ref, h_ref, tab_hbm, w_ref, idx_smem, tab_vmem, acc_ref, sem):
    j = pl.program_id(0)
    nk = PEER_HEADS * PEER_TOPK
    half = D_MODEL // 2 // LANES
    _load_resident(j, tab_hbm, tab_vmem, eid_ref, idx_smem, sem)
    sub = lax.broadcasted_iota(jnp.int32, (2 * half, LANES), 0)
    lane = lax.broadcasted_iota(jnp.int32, (nk, PEER_EXPERT_TILE), 1)
    acc_ref[...] = jnp.zeros((nk, PEER_EXPERT_TILE), F32)

    def token(t):
        x = h_ref[t]
        x_lo = jnp.concatenate([x[0:half], x[0:half]], axis=0)
        x_hi = jnp.concatenate([x[half:], x[half:]], axis=0)
        folded = []
        for gi in range(nk // 8):
            q = []
            for i in range(4):
                k = gi * 8 + i
                words = jnp.concatenate([_expert_words(tab_vmem, idx_smem[t, k]),
                                         _expert_words(tab_vmem, idx_smem[t, k + 4])], axis=0)
                lo, hi = _unpack(words)
                q.append(lo * x_lo + hi * x_hi)
            folded.append(_fold_pairs(q, sub))
        return jnp.sum(jnp.concatenate(folded, axis=0), axis=-1, keepdims=True)

    def body(tb, carry):
        acc = acc_ref[...]
        for u in range(PEER_TOKEN_UNROLL):
            t = tb * PEER_TOKEN_UNROLL + u
            acc = jnp.where(lane == t, token(t), acc)
        acc_ref[...] = acc
        return carry

    lax.fori_loop(0, PEER_EXPERT_TILE // PEER_TOKEN_UNROLL, body, 0)
    w_ref[...] = jax.nn.gelu(jnp.transpose(acc_ref[...])) * gw_ref[...]


def _peer_v_kernel(eid_ref, w_ref, x_ref, g2_ref, tab_hbm, o_ref, idx_smem, wb_ref, tab_vmem, sem):
    j = pl.program_id(0)
    nk = PEER_HEADS * PEER_TOPK
    half = D_MODEL // 2 // LANES
    _load_resident(j, tab_hbm, tab_vmem, eid_ref, idx_smem, sem)
    n_acc = 4

    def token(t, slot):
        wb_ref[slot] = jnp.transpose(jnp.broadcast_to(w_ref[pl.ds(t, 1), :], (nk, nk)))
        lo_acc = [jnp.zeros((half, LANES), F32) for _ in range(n_acc)]
        hi_acc = [jnp.zeros((half, LANES), F32) for _ in range(n_acc)]
        for k in range(nk):
            lo, hi = _unpack(_expert_words(tab_vmem, idx_smem[t, k]))
            wk = wb_ref[slot, k:k + 1, :]
            lo_acc[k % n_acc] = lo_acc[k % n_acc] + wk * lo
            hi_acc[k % n_acc] = hi_acc[k % n_acc] + wk * hi
        ff = jnp.concatenate([sum(lo_acc[1:], lo_acc[0]), sum(hi_acc[1:], hi_acc[0])], axis=0)
        o_ref[t] = x_ref[t] + g2_ref[0] * ff

    def body(tb, carry):
        for u in range(PEER_TOKEN_UNROLL):
            token(tb * PEER_TOKEN_UNROLL + u, u)
        return carry

    lax.fori_loop(0, PEER_EXPERT_TILE // PEER_TOKEN_UNROLL, body, 0)


def _peer_experts(eid, gw, h2, x1, gate2, u_tab, v_tab):
    b_, s_, d_ = x1.shape
    n_tok = b_ * s_
    nk = PEER_HEADS * PEER_TOPK
    sub_rows = d_ // LANES
    tt = PEER_EXPERT_TILE
    nsteps = n_tok // tt
    steps_per_batch = s_ // tt
    tok = pl.BlockSpec((tt, sub_rows, LANES), lambda j: (j, 0, 0))
    sel = pl.BlockSpec((tt, nk), lambda j: (j, 0))
    params = pltpu.CompilerParams(dimension_semantics=("arbitrary",), vmem_limit_bytes=PEER_VMEM_LIMIT)
    eid2 = eid.reshape(n_tok, nk)
    w = pl.pallas_call(
        _peer_u_kernel,
        grid=(nsteps,),
        in_specs=[sel, sel, tok, pl.BlockSpec(memory_space=pl.ANY)],
        out_specs=sel,
        out_shape=jax.ShapeDtypeStruct((n_tok, nk), F32),
        scratch_shapes=[pltpu.SMEM((tt, nk), jnp.int32), pltpu.VMEM(u_tab.shape, jnp.int32),
                        pltpu.VMEM((nk, tt), F32), pltpu.SemaphoreType.DMA],
        compiler_params=params,
    )(eid2, gw.reshape(n_tok, nk), h2.reshape(n_tok, sub_rows, LANES), u_tab)
    out = pl.pallas_call(
        _peer_v_kernel,
        grid=(nsteps,),
        in_specs=[sel, sel, tok, pl.BlockSpec((1, sub_rows, LANES), lambda j: (j // steps_per_batch, 0, 0)),
                  pl.BlockSpec(memory_space=pl.ANY)],
        out_specs=tok,
        out_shape=jax.ShapeDtypeStruct((n_tok, sub_rows, LANES), F32),
        scratch_shapes=[pltpu.SMEM((tt, nk), jnp.int32), pltpu.VMEM((PEER_TOKEN_UNROLL, nk, LANES), F32),
                        pltpu.VMEM(v_tab.shape, jnp.int32), pltpu.SemaphoreType.DMA],
        compiler_params=params,
    )(eid2, w, x1.reshape(n_tok, sub_rows, LANES), gate2.reshape(b_, sub_rows, LANES), v_tab)
    return out.reshape(b_, s_, d_)


def _permute_w_in(w):
    d_ = w.shape[0]
    kv_end = POOL_WIDTH + Q_WIDTH + 6 * KV_WIDTH
    per_group = HEADS_PER_GROUP * 3
    parts = [w[:, :kv_end], w[:, kv_end + GATE_COLS:]]
    for g in range(N_KV_GROUPS):
        parts.append(w[:, kv_end + g * per_group:kv_end + (g + 1) * per_group])
        parts.append(jnp.zeros((d_, LANES - per_group), w.dtype))
    return jnp.concatenate(parts, axis=1).astype(BF16)


def kernel(x, c, rel_bias, ada_w, ada_b, norm1_g, norm2_g, w_in, pool_w, pool_scale, cmp_pe_k, cmp_w1_k, cmp_w2_k, cmp_pe_v, cmp_w1_v, cmp_w2_v, q_norm_g, k_norm_g, w_branch_pool, w_branch_attn, w_out, peer_w_q, peer_sub_keys, peer_u, peer_v):
    b_, s_, d_ = x.shape
    l = 0
    ada = _ada(c, ada_w[l], ada_b[l]).reshape(b_, 6, 1, d_)
    shift1, scale1, gate1 = ada[:, 0], ada[:, 1], ada[:, 2]
    shift2, scale2, gate2 = ada[:, 3], ada[:, 4], ada[:, 5]

    (z_pool, z_q, ksel, vsel, kwin, vwin, kc_raw, vc_raw, z_merge, z_gate) = _in_proj(
        x, norm1_g[l], scale1, shift1, _permute_w_in(w_in[l]), k_norm_g[l])

    kc = _compress(kc_raw, cmp_pe_k[l], cmp_w1_k[l], cmp_w2_k[l], k_norm_g[l, 0], True)
    vc = _compress(vc_raw, cmp_pe_v[l], cmp_w1_v[l], cmp_w2_v[l], k_norm_g[l, 0], False)
    y_attn = _attention(z_q, z_gate, q_norm_g[l], kc, vc, ksel, vsel, kwin, vwin, rel_bias)

    x1 = _merge(x, z_pool, y_attn, z_merge, gate1, pool_w[l].astype(BF16), pool_scale[l],
                w_branch_pool[l].astype(BF16), w_branch_attn[l].astype(BF16), w_out[l].astype(BF16))

    keys = peer_sub_keys[l].reshape(PEER_HEADS * 2, PEER_KEYS, PEER_QDIM // 2).astype(BF16)
    h2, eid, gw = _peer_route(x1, norm2_g[l], scale2, shift2, peer_w_q[l].astype(BF16), keys)
    return _peer_experts(eid, gw, h2, x1, gate2, _pack_table(peer_u[l]), _pack_table(peer_v[l]))
```

```python
import math
from functools import partial

import jax
import jax.numpy as jnp
import numpy as np
from jax import lax
from jax.experimental import pallas as pl
from jax.experimental.pallas import tpu as pltpu

D_MODEL = 1024
POOL_WIDTH = 512
POOL_WINDOWS = (2, 4, 8, 16)
POOL_GROUP_DIM = 128
N_HEADS = 8
N_KV_GROUPS = 2
HEADS_PER_GROUP = 4
HEAD_DIM = 64
Q_WIDTH = 512
KV_WIDTH = 128
CMP_BLOCK = 32
CMP_STRIDE = 16
CMP_HIDDEN = 128
SEL_BLOCK = 64
SEL_TOPK = 16
WINDOW = 512
FORCE_SCORE = 1000.0
N_BUCKETS = 32
MAX_DISTANCE = 128
PEER_HEADS = 8
PEER_KEYS = 128
PEER_TOPK = 16
PEER_QDIM = 256
PEER_CHUNK = 128
GATE_COLS = N_HEADS * 3
EPS = 1e-6
NEG_INF = -1e30

LANES = 128
VMEM_LIMIT = 48 * 1024 * 1024
ROW_TILE = 512
POOL_HALO = 16
TQ = 128
SEL_FEATS = 64
SEL_CHUNK = 512
KV_PAD = 512
CMP_BAND_LO = 10
PEER_TILE = 256
PEER_EXPERT_TILE = 128
PEER_TOKEN_UNROLL = 4
EXPERT_ROWS = D_MODEL // 2 // LANES
PEER_VMEM_LIMIT = 56 * 1024 * 1024

BF16 = jnp.bfloat16
F32 = jnp.float32
_NT = (((1,), (1,)), ((), ()))


def _rms_rows(x, g):
    return x * lax.rsqrt(jnp.mean(x * x, axis=-1, keepdims=True) + EPS) * g


def _in_proj_kernel(x_ref, g_ref, sc_ref, sh_ref, w_ref, kg_ref,
                    zpool_ref, zq_ref, ksel_ref, vsel_ref, kwin_ref, vwin_ref, kcr_ref, vcr_ref,
                    zmerge_ref, zgate_ref):
    i = pl.program_id(1)
    x = x_ref[0]
    y = x * lax.rsqrt(jnp.mean(x * x, axis=-1, keepdims=True) + EPS)
    h = y * g_ref[...] * (1.0 + sc_ref[0]) + sh_ref[0]
    z = jnp.dot(h.astype(BF16), w_ref[...], preferred_element_type=F32)
    ts = x.shape[0]
    zpool_ref[0] = z[:, :POOL_WIDTH]
    o = POOL_WIDTH
    zq_ref[0] = z[:, o:o + Q_WIDTH]
    o += Q_WIDTH
    pos = i * ts + lax.broadcasted_iota(jnp.int32, (ts, SEL_FEATS), 0)
    onehot = (pos // SEL_BLOCK == lax.broadcasted_iota(jnp.int32, (ts, SEL_FEATS), 1)).astype(BF16)
    for g in range(N_KV_GROUPS):
        def col(k):
            return z[:, o + k * KV_WIDTH + g * HEAD_DIM:o + k * KV_WIDTH + (g + 1) * HEAD_DIM]
        kcr_ref[0, g] = col(0)
        vcr_ref[0, g] = col(1)
        ks = _rms_rows(col(2), kg_ref[1:2, :]).astype(BF16)
        ksel_ref[0, g] = jnp.concatenate([ks, onehot], axis=1)
        vsel_ref[0, g] = col(3).astype(BF16)
        kwin_ref[0, g] = _rms_rows(col(4), kg_ref[2:3, :]).astype(BF16)
        vwin_ref[0, g] = col(5).astype(BF16)
    o += 6 * KV_WIDTH
    zmerge_ref[0] = z[:, o:o + 2 * D_MODEL]
    o += 2 * D_MODEL
    zgate_ref[0] = jax.nn.sigmoid(z[:, o:o + N_KV_GROUPS * LANES])


def _in_proj(x, g, scale, shift, w_bf16, k_norm_g):
    b_, s_, d_ = x.shape
    n = w_bf16.shape[1]
    row = lambda w: pl.BlockSpec((1, ROW_TILE, w), lambda b, i: (b, i, 0))
    grp = lambda w: pl.BlockSpec((1, N_KV_GROUPS, ROW_TILE, w), lambda b, i: (b, 0, i, 0))
    gshape = lambda w, dt: jax.ShapeDtypeStruct((b_, N_KV_GROUPS, s_, w), dt)
    return pl.pallas_call(
        _in_proj_kernel,
        grid=(b_, s_ // ROW_TILE),
        in_specs=[
            row(d_),
            pl.BlockSpec((1, d_), lambda b, i: (0, 0)),
            pl.BlockSpec((1, 1, d_), lambda b, i: (b, 0, 0)),
            pl.BlockSpec((1, 1, d_), lambda b, i: (b, 0, 0)),
            pl.BlockSpec((d_, n), lambda b, i: (0, 0)),
            pl.BlockSpec((3, HEAD_DIM), lambda b, i: (0, 0)),
        ],
        out_specs=[row(POOL_WIDTH), row(Q_WIDTH), grp(HEAD_DIM + SEL_FEATS), grp(HEAD_DIM), grp(HEAD_DIM),
                   grp(HEAD_DIM), grp(HEAD_DIM), grp(HEAD_DIM), row(2 * d_), row(N_KV_GROUPS * LANES)],
        out_shape=[
            jax.ShapeDtypeStruct((b_, s_, POOL_WIDTH), F32),
            jax.ShapeDtypeStruct((b_, s_, Q_WIDTH), F32),
            gshape(HEAD_DIM + SEL_FEATS, BF16), gshape(HEAD_DIM, BF16), gshape(HEAD_DIM, BF16), gshape(HEAD_DIM, BF16),
            gshape(HEAD_DIM, F32), gshape(HEAD_DIM, F32),
            jax.ShapeDtypeStruct((b_, s_, 2 * d_), F32),
            jax.ShapeDtypeStruct((b_, s_, N_KV_GROUPS * LANES), F32),
        ],
        compiler_params=pltpu.CompilerParams(
            dimension_semantics=("parallel", "parallel"), vmem_limit_bytes=VMEM_LIMIT),
    )(x, g.reshape(1, d_), scale, shift, w_bf16, k_norm_g)


def _compress_kernel(t_ref, pe_ref, w1_ref, w2_ref, g_ref, o_ref, *, normalize):
    half = CMP_STRIDE * HEAD_DIM
    t = t_ref[0, 0].astype(BF16)
    w1 = w1_ref[...]
    a = jnp.dot(t, w1[:half], preferred_element_type=F32)
    b = jnp.dot(t, w1[half:], preferred_element_type=F32)
    pe = jnp.dot(pe_ref[...], w1, preferred_element_type=F32)
    n = a.shape[0]
    b_next = pltpu.roll(b, n - 1, axis=0)
    hid = jax.nn.gelu(a + b_next + pe)
    out = jnp.dot(hid.astype(BF16), w2_ref[...], preferred_element_type=F32)
    if normalize:
        out = _rms_rows(out, g_ref[...])
    o_ref[0, 0] = out.astype(BF16)


def _compress(t_raw, pe, w1, w2, g, normalize):
    b_, g_, s_, dh = t_raw.shape
    n_str = s_ // CMP_STRIDE
    t = t_raw.reshape(b_, g_, n_str, CMP_STRIDE * dh)
    return pl.pallas_call(
        partial(_compress_kernel, normalize=normalize),
        grid=(b_, g_),
        in_specs=[
            pl.BlockSpec((1, 1, n_str, CMP_STRIDE * dh), lambda b, g: (b, g, 0, 0)),
            pl.BlockSpec((1, CMP_BLOCK * dh), lambda b, g: (0, 0)),
            pl.BlockSpec((CMP_BLOCK * dh, CMP_HIDDEN), lambda b, g: (0, 0)),
            pl.BlockSpec((CMP_HIDDEN, dh), lambda b, g: (0, 0)),
            pl.BlockSpec((1, dh), lambda b, g: (0, 0)),
        ],
        out_specs=pl.BlockSpec((1, 1, n_str, dh), lambda b, g: (b, g, 0, 0)),
        out_shape=jax.ShapeDtypeStruct((b_, g_, n_str, dh), BF16),
        compiler_params=pltpu.CompilerParams(
            dimension_semantics=("parallel", "parallel"), vmem_limit_bytes=VMEM_LIMIT),
    )(t, pe.reshape(1, CMP_BLOCK * dh).astype(BF16), w1.astype(BF16), w2.astype(BF16), g.reshape(1, dh))


def _t5_bucket_np(rel):
    n = np.maximum(rel, 0)
    max_exact = N_BUCKETS // 2
    nf = np.maximum(n, 1).astype(np.float32)
    large = max_exact + (np.log(nf / max_exact) / math.log(MAX_DISTANCE / max_exact) * (N_BUCKETS - max_exact)).astype(np.int32)
    large = np.minimum(large, N_BUCKETS - 1)
    return np.where(n < max_exact, n, large)


def _bias_tiles(rel_bias, n_cmp_pad):
    far = rel_bias[N_BUCKETS - 1]
    ii = np.arange(TQ)[:, None]

    def lookup(dist):
        near = (dist >= 0) & (dist < MAX_DISTANCE)
        bucket = _t5_bucket_np(np.where(near, dist, MAX_DISTANCE))
        b = rel_bias[bucket] - far
        b = jnp.transpose(b, (2, 0, 1))
        return b.reshape(N_KV_GROUPS, HEADS_PER_GROUP * dist.shape[0], dist.shape[1])

    jj = np.arange(TQ)[None, :]
    bt = jnp.stack([lookup(ii - jj), lookup(TQ + ii - jj)], axis=1)
    cc = np.arange(n_cmp_pad)[None, :]
    dist_c = ii - CMP_STRIDE * (cc - CMP_BAND_LO) - (CMP_BLOCK - 1)
    dist_c = np.where(cc <= CMP_BAND_LO + TQ // CMP_STRIDE, dist_c, -1)
    bc = lookup(dist_c)
    rows = HEADS_PER_GROUP * TQ
    i_r = np.tile(np.arange(TQ), HEADS_PER_GROUP)[:, None]
    first = jnp.asarray(np.where(jj > i_r, 0.0, NEG_INF), F32)
    last = jnp.where(jnp.asarray(jj <= i_r), bt[:, 0], NEG_INF)
    zeros = jnp.zeros((N_KV_GROUPS, rows, WINDOW - 2 * TQ), F32)
    wa = jnp.concatenate([jnp.broadcast_to(first, (N_KV_GROUPS, rows, TQ)), zeros, bt[:, 1], last], axis=-1)
    return wa, bc


def _sel_mapping_t(n_cmp_pad, n_cmp):
    m = np.zeros((SEL_FEATS, n_cmp_pad), np.float32)
    pos = np.arange(n_cmp)[:, None] * CMP_STRIDE + np.arange(CMP_BLOCK)[None, :]
    np.add.at(m, ((pos // SEL_BLOCK).ravel(), np.repeat(np.arange(n_cmp), CMP_BLOCK)), 1.0 / CMP_BLOCK)
    return m


def _logits(q, k, add=None, lo_cols=None):
    s = lax.dot_general(q, k, _NT, preferred_element_type=F32)
    if add is not None:
        s = s + add
    if lo_cols is not None:
        s = jnp.where(lax.broadcasted_iota(jnp.int32, s.shape, 1) < lo_cols, NEG_INF, s)
    return s


def _flash_step(s, v, m_sc, l_sc, acc_sc):
    m_prev = m_sc[...]
    m_new = jnp.maximum(m_prev, jnp.max(s, axis=-1, keepdims=True))
    alpha = jnp.exp(m_prev - m_new)
    p = jnp.exp(s - m_new)
    l_sc[...] = alpha * l_sc[...] + jnp.sum(p, axis=-1, keepdims=True)
    acc_sc[...] = alpha * acc_sc[...] + jnp.dot(p.astype(BF16), v, preferred_element_type=F32)
    m_sc[...] = m_new


def _attn_kernel(zq_ref, gate_ref, qg_ref, kc_ref, vc_ref, ksel_ref, vsel_ref, kwin_ref, vwin_ref,
                 wa_ref, bc_ref, smap_ref, o_ref, m_sc, l_sc, acc_sc):
    i = pl.program_id(2)
    rows = HEADS_PER_GROUP * TQ
    ncp = kc_ref.shape[2]
    scale = HEAD_DIM ** -0.5

    zq = zq_ref[0]
    qs = jnp.concatenate([zq[:, j * HEAD_DIM:(j + 1) * HEAD_DIM] for j in range(HEADS_PER_GROUP)], axis=0)
    qn = _rms_rows(qs, qg_ref[...]) * scale
    qb = qn.astype(BF16)

    lc = lax.dot_general(qb, kc_ref[0, 0], _NT, preferred_element_type=F32)
    lc = lc + pltpu.roll(bc_ref[0], (i * (TQ // CMP_STRIDE) + ncp - CMP_BAND_LO) % ncp, axis=1)
    t_c = i * TQ + lax.broadcasted_iota(jnp.int32, (rows, ncp), 0) % TQ
    n_c = lax.broadcasted_iota(jnp.int32, (rows, ncp), 1)
    valid_c = n_c * CMP_STRIDE + (CMP_BLOCK - 1) <= t_c
    lm = jnp.where(valid_c, lc, NEG_INF)
    e = jnp.where(valid_c, jnp.exp(lm - jnp.max(lm, axis=-1, keepdims=True)), 0.0)
    den = jnp.sum(e, axis=-1, keepdims=True)
    p_c = e / jnp.where(den > 0.0, den, 1.0)
    o_cmp = jnp.dot(p_c.astype(BF16), vc_ref[0, 0], preferred_element_type=F32)

    p_sum = p_c[0:TQ] + p_c[TQ:2 * TQ] + p_c[2 * TQ:3 * TQ] + p_c[3 * TQ:4 * TQ]
    p_hi = p_sum.astype(BF16)
    r1 = p_sum - p_hi.astype(F32)
    p_mid = r1.astype(BF16)
    p_lo = (r1 - p_mid.astype(F32)).astype(BF16)
    smap = smap_ref[...]
    imp = (lax.dot_general(smap, p_hi, _NT, preferred_element_type=F32)
           + lax.dot_general(smap, p_mid, _NT, preferred_element_type=F32)
           + lax.dot_general(smap, p_lo, _NT, preferred_element_type=F32))
    blk = lax.broadcasted_iota(jnp.int32, (SEL_FEATS, TQ), 0)
    t_s = i * TQ + lax.broadcasted_iota(jnp.int32, (SEL_FEATS, TQ), 1)
    cur = t_s // SEL_BLOCK
    forced = (blk == 0) | (blk == cur) | (blk == cur - 1)
    visible = blk * SEL_BLOCK <= t_s
    score = jnp.where(visible, imp + jnp.where(forced, FORCE_SCORE, 0.0), -1.0)
    rank = jnp.zeros((SEL_FEATS, TQ), jnp.int32)
    for sp in range(SEL_FEATS):
        row = score[sp:sp + 1, :]
        beats = (row > score) | ((row == score) & (blk > sp))
        rank = rank + beats.astype(jnp.int32)
    pen_t = jnp.where(rank < SEL_TOPK, 0.0, NEG_INF)
    pen = jnp.transpose(pen_t).astype(BF16)
    q_aug = jnp.concatenate([qb, jnp.concatenate([pen] * HEADS_PER_GROUP, axis=0)], axis=1)

    def keys(ref, pos, n):
        return ref[0, 0, pl.ds(pl.multiple_of(pos + KV_PAD, TQ), n), :]

    m_sc[...] = jnp.full((rows, 1), NEG_INF, F32)
    l_sc[...] = jnp.zeros((rows, 1), F32)
    acc_sc[...] = jnp.zeros((rows, HEAD_DIM), F32)
    per_chunk = SEL_CHUNK // TQ
    n_plain = jnp.maximum(i - 1, 0)
    rem = n_plain % per_chunk

    @pl.when(rem > 0)
    def _():
        pos = (rem - per_chunk) * TQ
        s = _logits(q_aug, keys(ksel_ref, pos, SEL_CHUNK), lo_cols=-pos)
        _flash_step(s, keys(vsel_ref, pos, SEL_CHUNK), m_sc, l_sc, acc_sc)

    def sel_body(c, carry):
        pos = (rem + c * per_chunk) * TQ
        _flash_step(_logits(q_aug, keys(ksel_ref, pos, SEL_CHUNK)), keys(vsel_ref, pos, SEL_CHUNK), m_sc, l_sc, acc_sc)
        return carry

    lax.fori_loop(0, n_plain // per_chunk, sel_body, 0)
    pos = (i - 1) * TQ
    s = _logits(q_aug, keys(ksel_ref, pos, 2 * TQ), add=wa_ref[0, :, WINDOW - TQ:], lo_cols=-pos)
    _flash_step(s, keys(vsel_ref, pos, 2 * TQ), m_sc, l_sc, acc_sc)
    o_sel = acc_sc[...] / l_sc[...]

    pos = i * TQ - WINDOW
    s = _logits(qb, keys(kwin_ref, pos, WINDOW + TQ), add=wa_ref[0], lo_cols=-pos)
    p = jnp.exp(s - jnp.max(s, axis=-1, keepdims=True))
    o_win = (jnp.dot(p.astype(BF16), keys(vwin_ref, pos, WINDOW + TQ), preferred_element_type=F32)
             / jnp.sum(p, axis=-1, keepdims=True))

    gate = gate_ref[0]
    outs = []
    for j in range(HEADS_PER_GROUP):
        sl = slice(j * TQ, (j + 1) * TQ)
        outs.append(gate[:, 3 * j:3 * j + 1] * o_cmp[sl] + gate[:, 3 * j + 1:3 * j + 2] * o_sel[sl]
                    + gate[:, 3 * j + 2:3 * j + 3] * o_win[sl])
    o_ref[0] = jnp.concatenate(outs, axis=1)


def _attention(z_q, z_gate, q_norm_g, kc, vc, ksel, vsel, kwin, vwin, rel_bias):
    b_, s_, _ = z_q.shape
    ncp = kc.shape[2]
    n_cmp = ncp - CMP_BLOCK // CMP_STRIDE + 1
    assert s_ // SEL_BLOCK <= SEL_FEATS and s_ % TQ == 0
    wa, bc = _bias_tiles(rel_bias, ncp)
    smap_t = jnp.asarray(_sel_mapping_t(ncp, n_cmp), BF16)
    rows = HEADS_PER_GROUP * TQ
    gw = HEADS_PER_GROUP * HEAD_DIM
    pad = lambda a: jnp.pad(a, ((0, 0), (0, 0), (KV_PAD, 0), (0, 0)))
    ksel, vsel, kwin, vwin = pad(ksel), pad(vsel), pad(kwin), pad(vwin)
    kv = lambda w: pl.BlockSpec((1, 1, s_ + KV_PAD, w), lambda b, g, i: (b, g, 0, 0))
    cm = pl.BlockSpec((1, 1, ncp, HEAD_DIM), lambda b, g, i: (b, g, 0, 0))
    return pl.pallas_call(
        _attn_kernel,
        grid=(b_, N_KV_GROUPS, s_ // TQ),
        in_specs=[
            pl.BlockSpec((1, TQ, gw), lambda b, g, i: (b, i, g)),
            pl.BlockSpec((1, TQ, LANES), lambda b, g, i: (b, i, g)),
            pl.BlockSpec((1, HEAD_DIM), lambda b, g, i: (0, 0)),
            cm, cm, kv(HEAD_DIM + SEL_FEATS), kv(HEAD_DIM), kv(HEAD_DIM), kv(HEAD_DIM),
            pl.BlockSpec((1, rows, WINDOW + TQ), lambda b, g, i: (g, 0, 0)),
            pl.BlockSpec((1, rows, ncp), lambda b, g, i: (g, 0, 0)),
            pl.BlockSpec((SEL_FEATS, ncp), lambda b, g, i: (0, 0)),
        ],
        out_specs=pl.BlockSpec((1, TQ, gw), lambda b, g, i: (b, i, g)),
        out_shape=jax.ShapeDtypeStruct((b_, s_, N_KV_GROUPS * gw), F32),
        scratch_shapes=[pltpu.VMEM((rows, 1), F32), pltpu.VMEM((rows, 1), F32), pltpu.VMEM((rows, HEAD_DIM), F32)],
        compiler_params=pltpu.CompilerParams(
            dimension_semantics=("parallel", "parallel", "arbitrary"), vmem_limit_bytes=VMEM_LIMIT),
    )(z_q, z_gate, q_norm_g.reshape(1, HEAD_DIM), kc, vc, ksel, vsel, kwin, vwin, wa, bc, smap_t)


def _pool_tile(halo, tile, i, pw_ref, ps_ref):
    ts = tile.shape[0]
    ext = jnp.concatenate([jnp.where(i > 0, halo, 0.0), tile], axis=0)
    t = i * ts + lax.broadcasted_iota(jnp.int32, (ts, 1), 0)
    outs = []
    for gi, w in enumerate(POOL_WINDOWS):
        sl = slice(gi * POOL_GROUP_DIM, (gi + 1) * POOL_GROUP_DIM)
        run = ext[:, sl]
        span = 1
        while span < w:
            run = run + pltpu.roll(run, span, axis=0)
            span *= 2
        cnt = jnp.minimum(t + 1, w).astype(F32)
        pooled = run[POOL_HALO:] / cnt - tile[:, sl]
        outs.append(jnp.dot(pooled.astype(BF16), pw_ref[gi], preferred_element_type=F32))
    return jnp.concatenate(outs, axis=1) * ps_ref[...]


def _merge_kernel(x_ref, zp_ref, zh_ref, ya_ref, zm_ref, g1_ref, pw_ref, ps_ref, wbp_ref, wba_ref, wo_ref, o_ref):
    d_ = x_ref.shape[-1]
    y_pool = _pool_tile(zh_ref[0], zp_ref[0], pl.program_id(1), pw_ref, ps_ref)
    bp = jnp.dot(y_pool.astype(BF16), wbp_ref[...], preferred_element_type=F32)
    ba = jnp.dot(ya_ref[0].astype(BF16), wba_ref[...], preferred_element_type=F32)
    zm = zm_ref[0]
    mixed = jax.nn.sigmoid(zm[:, :d_]) * bp + jax.nn.sigmoid(zm[:, d_:]) * ba
    proj = jnp.dot(mixed.astype(BF16), wo_ref[...], preferred_element_type=F32)
    o_ref[0] = x_ref[0] + g1_ref[0] * proj


def _merge(x, z_pool, y_attn, z_merge, gate1, pool_w, pool_scale, wbp, wba, wo):
    b_, s_, d_ = x.shape
    row = lambda w: pl.BlockSpec((1, ROW_TILE, w), lambda b, i: (b, i, 0))
    full = lambda a: pl.BlockSpec(a.shape, lambda b, i: (0,) * a.ndim)
    halo_blocks = ROW_TILE // POOL_HALO
    halo = pl.BlockSpec((1, POOL_HALO, POOL_WIDTH), lambda b, i: (b, jnp.maximum(i * halo_blocks - 1, 0), 0))
    pool_scale = pool_scale.reshape(1, POOL_WIDTH)
    return pl.pallas_call(
        _merge_kernel,
        grid=(b_, s_ // ROW_TILE),
        in_specs=[row(d_), row(POOL_WIDTH), halo, row(Q_WIDTH), row(2 * d_),
                  pl.BlockSpec((1, 1, d_), lambda b, i: (b, 0, 0)),
                  full(pool_w), full(pool_scale), full(wbp), full(wba), full(wo)],
        out_specs=row(d_),
        out_shape=jax.ShapeDtypeStruct((b_, s_, d_), F32),
        compiler_params=pltpu.CompilerParams(
            dimension_semantics=("parallel", "parallel"), vmem_limit_bytes=VMEM_LIMIT),
    )(x, z_pool, z_pool, y_attn, z_merge, gate1, pool_w, pool_scale, wbp, wba, wo)


def _ada_kernel(c_ref, w_ref, b_ref, o_ref):
    c = c_ref[...]
    o_ref[...] = jnp.dot(jax.nn.silu(c).astype(BF16), w_ref[...].astype(BF16),
                         preferred_element_type=F32) + b_ref[...]


def _ada(c, w, b):
    b_, d_ = c.shape
    n = w.shape[1]
    return pl.pallas_call(
        _ada_kernel,
        grid=(n // d_,),
        in_specs=[pl.BlockSpec((b_, d_), lambda j: (0, 0)),
                  pl.BlockSpec((d_, d_), lambda j: (0, j)),
                  pl.BlockSpec((1, d_), lambda j: (0, j))],
        out_specs=pl.BlockSpec((b_, d_), lambda j: (0, j)),
        out_shape=jax.ShapeDtypeStruct((b_, n), F32),
        compiler_params=pltpu.CompilerParams(dimension_semantics=("parallel",), vmem_limit_bytes=VMEM_LIMIT),
    )(c, w, b.reshape(1, n))


_CAND_ROWS = PEER_TOPK + 7 * 8 + 8


def _topk_rows(x, k):
    n = x.shape[0]
    rid = lax.broadcasted_iota(jnp.int32, x.shape, 0)
    vals, idxs = [], []
    for _ in range(k):
        m = jnp.max(x, axis=0, keepdims=True)
        idx = jnp.min(jnp.where(x == m, rid, n), axis=0, keepdims=True)
        vals.append(m)
        idxs.append(idx)
        x = jnp.where(rid == idx, -jnp.inf, x)
    return jnp.concatenate(vals, axis=0), jnp.concatenate(idxs, axis=0)


def _pair_grid(r0, r1, combine):
    parts = [combine(r0[0:1], r1)]
    parts += [combine(r0[a:a + 1], r1[0:8]) for a in range(1, 8)]
    parts.append(combine(r0[8:16], r1[0:1]))
    return jnp.concatenate(parts, axis=0)


def _peer_route_kernel(x_ref, g_ref, sc_ref, sh_ref, w_ref, keys_ref, h_ref, eid_ref, gw_ref):
    x = x_ref[0]
    y = x * lax.rsqrt(jnp.mean(x * x, axis=-1, keepdims=True) + EPS)
    h = y * g_ref[...] * (1.0 + sc_ref[0]) + sh_ref[0]
    h_ref[0] = h
    qv = jnp.dot(h.astype(BF16), w_ref[...], preferred_element_type=F32).astype(BF16)
    half = PEER_QDIM // 2
    eids, gws = [], []
    for hd in range(PEER_HEADS):
        tops = []
        for p in range(2):
            c = hd * 2 + p
            sc = lax.dot_general(keys_ref[c], qv[:, c * half:(c + 1) * half], _NT,
                                 preferred_element_type=F32)
            tops.append(_topk_rows(sc, PEER_TOPK))
        (s0, i0), (s1, i1) = tops
        cand = _pair_grid(s0, s1, lambda a, b: a + b)
        cid = _pair_grid(i0, i1, lambda a, b: a * (PEER_KEYS * EXPERT_ROWS) + (b * EXPERT_ROWS + EXPERT_ROWS))
        rid = lax.broadcasted_iota(jnp.int32, cand.shape, 0)
        top_s, top_e = [], []
        for _ in range(PEER_TOPK):
            m = jnp.max(cand, axis=0, keepdims=True)
            idx = jnp.min(jnp.where(cand == m, rid, _CAND_ROWS), axis=0, keepdims=True)
            hit = rid == idx
            top_s.append(m)
            top_e.append(jnp.sum(jnp.where(hit, cid, 0), axis=0, keepdims=True))
            cand = jnp.where(hit, -jnp.inf, cand)
        top_s = jnp.concatenate(top_s, axis=0)
        e = jnp.exp(top_s - top_s[0:1])
        gws.append(e / jnp.sum(e, axis=0, keepdims=True))
        eids.append(jnp.concatenate(top_e, axis=0))
    eids = jnp.concatenate(eids, axis=0)
    for c in range(PEER_TILE // PEER_EXPERT_TILE):
        eid_ref[0, c] = eids[:, c * PEER_EXPERT_TILE:(c + 1) * PEER_EXPERT_TILE]
    gw_ref[0] = jnp.transpose(jnp.concatenate(gws, axis=0))


def _peer_route(x, g, scale, shift, w_bf16, keys_bf16):
    b_, s_, d_ = x.shape
    n = w_bf16.shape[1]
    nk = PEER_HEADS * PEER_TOPK
    per_tile = PEER_TILE // PEER_EXPERT_TILE
    row = lambda w: pl.BlockSpec((1, PEER_TILE, w), lambda b, i: (b, i, 0))
    return pl.pallas_call(
        _peer_route_kernel,
        grid=(b_, s_ // PEER_TILE),
        in_specs=[
            row(d_),
            pl.BlockSpec((1, d_), lambda b, i: (0, 0)),
            pl.BlockSpec((1, 1, d_), lambda b, i: (b, 0, 0)),
            pl.BlockSpec((1, 1, d_), lambda b, i: (b, 0, 0)),
            pl.BlockSpec((d_, n), lambda b, i: (0, 0)),
            pl.BlockSpec(keys_bf16.shape, lambda b, i: (0, 0, 0)),
        ],
        out_specs=[row(d_),
                   pl.BlockSpec((1, per_tile, nk, PEER_EXPERT_TILE), lambda b, i: (b, i, 0, 0)),
                   row(nk)],
        out_shape=[
            jax.ShapeDtypeStruct((b_, s_, d_), F32),
            jax.ShapeDtypeStruct((b_, s_ // PEER_EXPERT_TILE, nk, PEER_EXPERT_TILE), jnp.int32),
            jax.ShapeDtypeStruct((b_, s_, nk), F32),
        ],
        compiler_params=pltpu.CompilerParams(
            dimension_semantics=("parallel", "parallel"), vmem_limit_bytes=VMEM_LIMIT),
    )(x, g.reshape(1, d_), scale, shift, w_bf16, keys_bf16)


_HI_MASK = -65536


def _pack_table(t):
    e_, d_ = t.shape
    bits = lax.bitcast_convert_type(t.astype(jnp.bfloat16), jnp.uint16).astype(jnp.uint32)
    words = (bits[:, d_ // 2:] << 16) | bits[:, :d_ // 2]
    rows = lax.bitcast_convert_type(words, jnp.int32).reshape(e_ * EXPERT_ROWS, LANES)
    return jnp.pad(rows, ((EXPERT_ROWS, EXPERT_ROWS), (0, 0)))


def _expert_words(tab_vmem, row):
    return tab_vmem[pl.ds(pl.multiple_of(row, EXPERT_ROWS), EXPERT_ROWS), :]


def _expert_pair_words(tab_vmem, row_a, row_b, sub):
    top = tab_vmem[pl.ds(pl.multiple_of(row_a, EXPERT_ROWS), 2 * EXPERT_ROWS), :]
    bot = tab_vmem[pl.ds(pl.multiple_of(row_b - EXPERT_ROWS, EXPERT_ROWS), 2 * EXPERT_ROWS), :]
    return jnp.where(sub < EXPERT_ROWS, top, bot)


def _unpack(words):
    lo = lax.bitcast_convert_type(lax.shift_left(words, 16), F32)
    hi = lax.bitcast_convert_type(words & _HI_MASK, F32)
    return lo, hi


def _load_resident(j, tab_hbm, tab_vmem, eid_ref, idx_smem, sem):
    @pl.when(j == 0)
    def _():
        cp = pltpu.make_async_copy(tab_hbm, tab_vmem, sem)
        cp.start()
        cp.wait()

    copies = [pltpu.make_async_copy(eid_ref.at[0, k], idx_smem[k], sem) for k in range(len(idx_smem))]
    for cp in copies:
        cp.start()
    for cp in copies:
        cp.wait()


def _fold_pairs(q, sub):
    m2, m1 = (sub & 2) == 0, (sub & 1) == 0
    u = [jnp.where(m2, q[i], q[i + 2]) + jnp.where(m2, pltpu.roll(q[i], 6, axis=0), pltpu.roll(q[i + 2], 2, axis=0))
         for i in range(2)]
    return jnp.where(m1, u[0], u[1]) + jnp.where(m1, pltpu.roll(u[0], 7, axis=0), pltpu.roll(u[1], 1, axis=0))


def _peer_u_kernel(eid_ref, gw_ref, h_ref, tab_hbm, w_ref, *scratch):
    j = pl.program_id(0)
    nk = PEER_HEADS * PEER_TOPK
    idx_smem, (tab_vmem, acc_ref, sem) = scratch[:nk], scratch[nk:]
    half = D_MODEL // 2 // LANES
    _load_resident(j, tab_hbm, tab_vmem, eid_ref, idx_smem, sem)
    sub = lax.broadcasted_iota(jnp.int32, (2 * half, LANES), 0)
    lane = lax.broadcasted_iota(jnp.int32, (nk, PEER_EXPERT_TILE), 1)
    acc_ref[...] = jnp.zeros((nk, PEER_EXPERT_TILE), F32)

    def token(t):
        x = h_ref[t]
        x_lo = jnp.concatenate([x[0:half], x[0:half]], axis=0)
        x_hi = jnp.concatenate([x[half:], x[half:]], axis=0)
        folded = []
        for gi in range(nk // 8):
            q = []
            for i in range(4):
                k = gi * 8 + i
                words = _expert_pair_words(tab_vmem, idx_smem[k][t], idx_smem[k + 4][t], sub)
                lo, hi = _unpack(words)
                q.append(lo * x_lo + hi * x_hi)
            folded.append(_fold_pairs(q, sub))
        return jnp.sum(jnp.concatenate(folded, axis=0), axis=-1, keepdims=True)

    def body(tb, carry):
        acc = acc_ref[...]
        for u in range(PEER_TOKEN_UNROLL):
            t = tb * PEER_TOKEN_UNROLL + u
            acc = jnp.where(lane == t, token(t), acc)
        acc_ref[...] = acc
        return carry

    lax.fori_loop(0, PEER_EXPERT_TILE // PEER_TOKEN_UNROLL, body, 0)
    w_ref[...] = jax.nn.gelu(jnp.transpose(acc_ref[...])) * gw_ref[...]


def _peer_v_kernel(eid_ref, w_ref, x_ref, g2_ref, tab_hbm, o_ref, *scratch):
    j = pl.program_id(0)
    nk = PEER_HEADS * PEER_TOPK
    idx_smem, (wb_ref, tab_vmem, sem) = scratch[:nk], scratch[nk:]
    half = D_MODEL // 2 // LANES
    _load_resident(j, tab_hbm, tab_vmem, eid_ref, idx_smem, sem)
    n_acc = 4
    sub = lax.broadcasted_iota(jnp.int32, (2 * half, LANES), 0)

    def token(t, slot):
        wb_ref[slot] = jnp.transpose(jnp.broadcast_to(w_ref[pl.ds(t, 1), :], (nk, nk)))
        lo_acc = [jnp.zeros((2 * half, LANES), F32) for _ in range(n_acc)]
        hi_acc = [jnp.zeros((2 * half, LANES), F32) for _ in range(n_acc)]
        for p in range(nk // 2):
            k = 2 * p
            lo, hi = _unpack(_expert_pair_words(tab_vmem, idx_smem[k][t], idx_smem[k + 1][t], sub))
            wk = jnp.where(sub < half, wb_ref[slot, k:k + 1, :], wb_ref[slot, k + 1:k + 2, :])
            lo_acc[p % n_acc] = lo_acc[p % n_acc] + wk * lo
            hi_acc[p % n_acc] = hi_acc[p % n_acc] + wk * hi
        lo_sum, hi_sum = sum(lo_acc[1:], lo_acc[0]), sum(hi_acc[1:], hi_acc[0])
        ff = jnp.concatenate([lo_sum[:half] + lo_sum[half:], hi_sum[:half] + hi_sum[half:]], axis=0)
        o_ref[t] = x_ref[t] + g2_ref[0] * ff

    def body(tb, carry):
        for u in range(PEER_TOKEN_UNROLL):
            token(tb * PEER_TOKEN_UNROLL + u, u)
        return carry

    lax.fori_loop(0, PEER_EXPERT_TILE // PEER_TOKEN_UNROLL, body, 0)


def _peer_experts(eid, gw, h2, x1, gate2, u_tab, v_tab):
    b_, s_, d_ = x1.shape
    n_tok = b_ * s_
    nk = PEER_HEADS * PEER_TOPK
    sub_rows = d_ // LANES
    tt = PEER_EXPERT_TILE
    nsteps = n_tok // tt
    steps_per_batch = s_ // tt
    tok = pl.BlockSpec((tt, sub_rows, LANES), lambda j: (j, 0, 0))
    sel = pl.BlockSpec((tt, nk), lambda j: (j, 0))
    slots = pl.BlockSpec((1, nk, tt), lambda j: (j, 0, 0))
    idx_scratch = [pltpu.SMEM((tt,), jnp.int32) for _ in range(nk)]
    params = pltpu.CompilerParams(dimension_semantics=("arbitrary",), vmem_limit_bytes=PEER_VMEM_LIMIT)
    eid3 = eid.reshape(nsteps, nk, tt)
    w = pl.pallas_call(
        _peer_u_kernel,
        grid=(nsteps,),
        in_specs=[slots, sel, tok, pl.BlockSpec(memory_space=pl.ANY)],
        out_specs=sel,
        out_shape=jax.ShapeDtypeStruct((n_tok, nk), F32),
        scratch_shapes=idx_scratch + [pltpu.VMEM(u_tab.shape, jnp.int32), pltpu.VMEM((nk, tt), F32),
                                      pltpu.SemaphoreType.DMA],
        compiler_params=params,
    )(eid3, gw.reshape(n_tok, nk), h2.reshape(n_tok, sub_rows, LANES), u_tab)
    out = pl.pallas_call(
        _peer_v_kernel,
        grid=(nsteps,),
        in_specs=[slots, sel, tok, pl.BlockSpec((1, sub_rows, LANES), lambda j: (j // steps_per_batch, 0, 0)),
                  pl.BlockSpec(memory_space=pl.ANY)],
        out_specs=tok,
        out_shape=jax.ShapeDtypeStruct((n_tok, sub_rows, LANES), F32),
        scratch_shapes=idx_scratch + [pltpu.VMEM((PEER_TOKEN_UNROLL, nk, LANES), F32),
                                      pltpu.VMEM(v_tab.shape, jnp.int32), pltpu.SemaphoreType.DMA],
        compiler_params=params,
    )(eid3, w, x1.reshape(n_tok, sub_rows, LANES), gate2.reshape(b_, sub_rows, LANES), v_tab)
    return out.reshape(b_, s_, d_)


def _permute_w_in(w):
    d_ = w.shape[0]
    kv_end = POOL_WIDTH + Q_WIDTH + 6 * KV_WIDTH
    per_group = HEADS_PER_GROUP * 3
    parts = [w[:, :kv_end], w[:, kv_end + GATE_COLS:]]
    for g in range(N_KV_GROUPS):
        parts.append(w[:, kv_end + g * per_group:kv_end + (g + 1) * per_group])
        parts.append(jnp.zeros((d_, LANES - per_group), w.dtype))
    return jnp.concatenate(parts, axis=1).astype(BF16)


def kernel(x, c, rel_bias, ada_w, ada_b, norm1_g, norm2_g, w_in, pool_w, pool_scale, cmp_pe_k, cmp_w1_k, cmp_w2_k, cmp_pe_v, cmp_w1_v, cmp_w2_v, q_norm_g, k_norm_g, w_branch_pool, w_branch_attn, w_out, peer_w_q, peer_sub_keys, peer_u, peer_v):
    b_, s_, d_ = x.shape
    l = 0
    ada = _ada(c, ada_w[l], ada_b[l]).reshape(b_, 6, 1, d_)
    shift1, scale1, gate1 = ada[:, 0], ada[:, 1], ada[:, 2]
    shift2, scale2, gate2 = ada[:, 3], ada[:, 4], ada[:, 5]

    (z_pool, z_q, ksel, vsel, kwin, vwin, kc_raw, vc_raw, z_merge, z_gate) = _in_proj(
        x, norm1_g[l], scale1, shift1, _permute_w_in(w_in[l]), k_norm_g[l])

    kc = _compress(kc_raw, cmp_pe_k[l], cmp_w1_k[l], cmp_w2_k[l], k_norm_g[l, 0], True)
    vc = _compress(vc_raw, cmp_pe_v[l], cmp_w1_v[l], cmp_w2_v[l], k_norm_g[l, 0], False)
    y_attn = _attention(z_q, z_gate, q_norm_g[l], kc, vc, ksel, vsel, kwin, vwin, rel_bias)

    x1 = _merge(x, z_pool, y_attn, z_merge, gate1, pool_w[l].astype(BF16), pool_scale[l],
                w_branch_pool[l].astype(BF16), w_branch_attn[l].astype(BF16), w_out[l].astype(BF16))

    keys = peer_sub_keys[l].reshape(PEER_HEADS * 2, PEER_KEYS, PEER_QDIM // 2).astype(BF16)
    h2, eid, gw = _peer_route(x1, norm2_g[l], scale2, shift2, peer_w_q[l].astype(BF16), keys)
    return _peer_experts(eid, gw, h2, x1, gate2, _pack_table(peer_u[l]), _pack_table(peer_v[l]))
```

```python
import math
from functools import partial

import jax
import jax.numpy as jnp
import numpy as np
from jax import lax
from jax.experimental import pallas as pl
from jax.experimental.pallas import tpu as pltpu

D_MODEL = 1024
POOL_WIDTH = 512
POOL_WINDOWS = (2, 4, 8, 16)
POOL_GROUP_DIM = 128
N_HEADS = 8
N_KV_GROUPS = 2
HEADS_PER_GROUP = 4
HEAD_DIM = 64
Q_WIDTH = 512
KV_WIDTH = 128
CMP_BLOCK = 32
CMP_STRIDE = 16
CMP_HIDDEN = 128
SEL_BLOCK = 64
SEL_TOPK = 16
WINDOW = 512
FORCE_SCORE = 1000.0
N_BUCKETS = 32
MAX_DISTANCE = 128
PEER_HEADS = 8
PEER_KEYS = 128
PEER_TOPK = 16
PEER_QDIM = 256
PEER_CHUNK = 128
GATE_COLS = N_HEADS * 3
EPS = 1e-6
NEG_INF = -1e30

LANES = 128
VMEM_LIMIT = 48 * 1024 * 1024
ROW_TILE = 512
POOL_HALO = 16
TQ = 128
SEL_FEATS = 64
SEL_CHUNK = 512
V_WIDTH = 128
KV_PAD = 512
CMP_BAND_LO = 10
PEER_TILE = 256
PEER_EXPERT_TILE = 256
PEER_TOKEN_UNROLL = 4
EXPERT_ROWS = D_MODEL // 2 // LANES
PEER_VMEM_LIMIT = 56 * 1024 * 1024

BF16 = jnp.bfloat16
F32 = jnp.float32
_NT = (((1,), (1,)), ((), ()))


def _rms_rows(x, g):
    return x * lax.rsqrt(jnp.mean(x * x, axis=-1, keepdims=True) + EPS) * g


def _in_proj_kernel(x_ref, g_ref, sc_ref, sh_ref, w_ref, kg_ref,
                    zpool_ref, zq_ref, ksel_ref, vsel_ref, kwin_ref, vwin_ref, kcr_ref, vcr_ref,
                    zmerge_ref, zgate_ref):
    i = pl.program_id(1)
    x = x_ref[0]
    y = x * lax.rsqrt(jnp.mean(x * x, axis=-1, keepdims=True) + EPS)
    h = y * g_ref[...] * (1.0 + sc_ref[0]) + sh_ref[0]
    z = jnp.dot(h.astype(BF16), w_ref[...], preferred_element_type=F32)
    ts = x.shape[0]
    zpool_ref[0] = z[:, :POOL_WIDTH]
    o = POOL_WIDTH
    zq_ref[0] = z[:, o:o + Q_WIDTH]
    o += Q_WIDTH
    pos = i * ts + lax.broadcasted_iota(jnp.int32, (ts, SEL_FEATS), 0)
    onehot = (pos // SEL_BLOCK == lax.broadcasted_iota(jnp.int32, (ts, SEL_FEATS), 1)).astype(BF16)
    one_col = (lax.broadcasted_iota(jnp.int32, (ts, V_WIDTH - HEAD_DIM), 1) == 0).astype(BF16)
    for g in range(N_KV_GROUPS):
        def col(k):
            return z[:, o + k * KV_WIDTH + g * HEAD_DIM:o + k * KV_WIDTH + (g + 1) * HEAD_DIM]
        kcr_ref[0, g] = col(0)
        vcr_ref[0, g] = col(1)
        ks = _rms_rows(col(2), kg_ref[1:2, :]).astype(BF16)
        ksel_ref[0, g] = jnp.concatenate([ks, onehot], axis=1)
        vsel_ref[0, g] = jnp.concatenate([col(3).astype(BF16), one_col], axis=1)
        kwin_ref[0, g] = _rms_rows(col(4), kg_ref[2:3, :]).astype(BF16)
        vwin_ref[0, g] = jnp.concatenate([col(5).astype(BF16), one_col], axis=1)
    o += 6 * KV_WIDTH
    zmerge_ref[0] = z[:, o:o + 2 * D_MODEL]
    o += 2 * D_MODEL
    zgate_ref[0] = jax.nn.sigmoid(z[:, o:o + N_KV_GROUPS * LANES])


def _in_proj(x, g, scale, shift, w_bf16, k_norm_g):
    b_, s_, d_ = x.shape
    n = w_bf16.shape[1]
    row = lambda w: pl.BlockSpec((1, ROW_TILE, w), lambda b, i: (b, i, 0))
    grp = lambda w: pl.BlockSpec((1, N_KV_GROUPS, ROW_TILE, w), lambda b, i: (b, 0, i, 0))
    gshape = lambda w, dt: jax.ShapeDtypeStruct((b_, N_KV_GROUPS, s_, w), dt)
    return pl.pallas_call(
        _in_proj_kernel,
        grid=(b_, s_ // ROW_TILE),
        in_specs=[
            row(d_),
            pl.BlockSpec((1, d_), lambda b, i: (0, 0)),
            pl.BlockSpec((1, 1, d_), lambda b, i: (b, 0, 0)),
            pl.BlockSpec((1, 1, d_), lambda b, i: (b, 0, 0)),
            pl.BlockSpec((d_, n), lambda b, i: (0, 0)),
            pl.BlockSpec((3, HEAD_DIM), lambda b, i: (0, 0)),
        ],
        out_specs=[row(POOL_WIDTH), row(Q_WIDTH), grp(HEAD_DIM + SEL_FEATS), grp(V_WIDTH), grp(HEAD_DIM),
                   grp(V_WIDTH), grp(HEAD_DIM), grp(HEAD_DIM), row(2 * d_), row(N_KV_GROUPS * LANES)],
        out_shape=[
            jax.ShapeDtypeStruct((b_, s_, POOL_WIDTH), F32),
            jax.ShapeDtypeStruct((b_, s_, Q_WIDTH), F32),
            gshape(HEAD_DIM + SEL_FEATS, BF16), gshape(V_WIDTH, BF16), gshape(HEAD_DIM, BF16), gshape(V_WIDTH, BF16),
            gshape(HEAD_DIM, F32), gshape(HEAD_DIM, F32),
            jax.ShapeDtypeStruct((b_, s_, 2 * d_), F32),
            jax.ShapeDtypeStruct((b_, s_, N_KV_GROUPS * LANES), F32),
        ],
        compiler_params=pltpu.CompilerParams(
            dimension_semantics=("parallel", "parallel"), vmem_limit_bytes=VMEM_LIMIT),
    )(x, g.reshape(1, d_), scale, shift, w_bf16, k_norm_g)


def _compress_kernel(t_ref, pe_ref, w1_ref, w2_ref, g_ref, o_ref, *, normalize):
    half = CMP_STRIDE * HEAD_DIM
    t = t_ref[0, 0].astype(BF16)
    w1 = w1_ref[...]
    a = jnp.dot(t, w1[:half], preferred_element_type=F32)
    b = jnp.dot(t, w1[half:], preferred_element_type=F32)
    pe = jnp.dot(pe_ref[...], w1, preferred_element_type=F32)
    n = a.shape[0]
    b_next = pltpu.roll(b, n - 1, axis=0)
    hid = jax.nn.gelu(a + b_next + pe)
    out = jnp.dot(hid.astype(BF16), w2_ref[...], preferred_element_type=F32)
    if normalize:
        out = _rms_rows(out, g_ref[...])
    o_ref[0, 0] = out.astype(BF16)


def _compress(t_raw, pe, w1, w2, g, normalize):
    b_, g_, s_, dh = t_raw.shape
    n_str = s_ // CMP_STRIDE
    t = t_raw.reshape(b_, g_, n_str, CMP_STRIDE * dh)
    return pl.pallas_call(
        partial(_compress_kernel, normalize=normalize),
        grid=(b_, g_),
        in_specs=[
            pl.BlockSpec((1, 1, n_str, CMP_STRIDE * dh), lambda b, g: (b, g, 0, 0)),
            pl.BlockSpec((1, CMP_BLOCK * dh), lambda b, g: (0, 0)),
            pl.BlockSpec((CMP_BLOCK * dh, CMP_HIDDEN), lambda b, g: (0, 0)),
            pl.BlockSpec((CMP_HIDDEN, dh), lambda b, g: (0, 0)),
            pl.BlockSpec((1, dh), lambda b, g: (0, 0)),
        ],
        out_specs=pl.BlockSpec((1, 1, n_str, dh), lambda b, g: (b, g, 0, 0)),
        out_shape=jax.ShapeDtypeStruct((b_, g_, n_str, dh), BF16),
        compiler_params=pltpu.CompilerParams(
            dimension_semantics=("parallel", "parallel"), vmem_limit_bytes=VMEM_LIMIT),
    )(t, pe.reshape(1, CMP_BLOCK * dh).astype(BF16), w1.astype(BF16), w2.astype(BF16), g.reshape(1, dh))


def _t5_bucket_np(rel):
    n = np.maximum(rel, 0)
    max_exact = N_BUCKETS // 2
    nf = np.maximum(n, 1).astype(np.float32)
    large = max_exact + (np.log(nf / max_exact) / math.log(MAX_DISTANCE / max_exact) * (N_BUCKETS - max_exact)).astype(np.int32)
    large = np.minimum(large, N_BUCKETS - 1)
    return np.where(n < max_exact, n, large)


def _bias_tiles(rel_bias, n_cmp_pad):
    far = rel_bias[N_BUCKETS - 1]
    ii = np.arange(TQ)[:, None]

    def lookup(dist):
        near = (dist >= 0) & (dist < MAX_DISTANCE)
        bucket = _t5_bucket_np(np.where(near, dist, MAX_DISTANCE))
        b = rel_bias[bucket] - far
        b = jnp.transpose(b, (2, 0, 1))
        return b.reshape(N_KV_GROUPS, HEADS_PER_GROUP * dist.shape[0], dist.shape[1])

    jj = np.arange(TQ)[None, :]
    bt = jnp.stack([lookup(ii - jj), lookup(TQ + ii - jj)], axis=1)
    cc = np.arange(n_cmp_pad)[None, :]
    dist_c = ii - CMP_STRIDE * (cc - CMP_BAND_LO) - (CMP_BLOCK - 1)
    dist_c = np.where(cc <= CMP_BAND_LO + TQ // CMP_STRIDE, dist_c, -1)
    bc = lookup(dist_c)
    rows = HEADS_PER_GROUP * TQ
    i_r = np.tile(np.arange(TQ), HEADS_PER_GROUP)[:, None]
    first = jnp.asarray(np.where(jj > i_r, 0.0, NEG_INF), F32)
    last = jnp.where(jnp.asarray(jj <= i_r), bt[:, 0], NEG_INF)
    zeros = jnp.zeros((N_KV_GROUPS, rows, WINDOW - 2 * TQ), F32)
    wa = jnp.concatenate([jnp.broadcast_to(first, (N_KV_GROUPS, rows, TQ)), zeros, bt[:, 1], last], axis=-1)
    return wa, bc


def _sel_mapping_t(n_cmp_pad, n_cmp):
    m = np.zeros((SEL_FEATS, n_cmp_pad), np.float32)
    pos = np.arange(n_cmp)[:, None] * CMP_STRIDE + np.arange(CMP_BLOCK)[None, :]
    np.add.at(m, ((pos // SEL_BLOCK).ravel(), np.repeat(np.arange(n_cmp), CMP_BLOCK)), 1.0 / CMP_BLOCK)
    return m


def _logits(q, k, add=None, lo_cols=None):
    s = lax.dot_general(q, k, _NT, preferred_element_type=F32)
    if add is not None:
        s = s + add
    if lo_cols is not None:
        s = jnp.where(lax.broadcasted_iota(jnp.int32, s.shape, 1) < lo_cols, NEG_INF, s)
    return s


def _flash_step(s, v, m_sc, acc_sc):
    m_prev = m_sc[...]
    m_new = jnp.maximum(m_prev, jnp.max(s, axis=-1, keepdims=True))
    alpha = jnp.exp(m_prev - m_new)
    p = jnp.exp(s - jnp.concatenate([m_new] * (s.shape[1] // LANES), axis=1))
    acc_sc[...] = alpha * acc_sc[...] + jnp.dot(p.astype(BF16), v, preferred_element_type=F32)
    m_sc[...] = m_new


def _attn_kernel(zq_ref, gate_ref, qg_ref, kc_ref, vc_ref, ksel_ref, vsel_ref, kwin_ref, vwin_ref,
                 wa_ref, bc_ref, smap_ref, o_ref, m_sc, acc_sc):
    i = pl.program_id(2)
    rows = HEADS_PER_GROUP * TQ
    ncp = kc_ref.shape[2]
    scale = HEAD_DIM ** -0.5

    zq = zq_ref[0]
    qs = jnp.concatenate([zq[:, j * HEAD_DIM:(j + 1) * HEAD_DIM] for j in range(HEADS_PER_GROUP)], axis=0)
    qn = _rms_rows(qs, qg_ref[...]) * scale
    qb = qn.astype(BF16)

    lc = lax.dot_general(qb, kc_ref[0, 0], _NT, preferred_element_type=F32)
    lc = lc + pltpu.roll(bc_ref[0], (i * (TQ // CMP_STRIDE) + ncp - CMP_BAND_LO) % ncp, axis=1)
    t_c = i * TQ + lax.broadcasted_iota(jnp.int32, (rows, ncp), 0) % TQ
    n_c = lax.broadcasted_iota(jnp.int32, (rows, ncp), 1)
    valid_c = n_c * CMP_STRIDE + (CMP_BLOCK - 1) <= t_c
    lm = jnp.where(valid_c, lc, NEG_INF)
    e = jnp.where(valid_c, jnp.exp(lm - jnp.max(lm, axis=-1, keepdims=True)), 0.0)
    den = jnp.sum(e, axis=-1, keepdims=True)
    p_c = e / jnp.where(den > 0.0, den, 1.0)
    o_cmp = jnp.dot(p_c.astype(BF16), vc_ref[0, 0], preferred_element_type=F32)

    p_sum = p_c[0:TQ] + p_c[TQ:2 * TQ] + p_c[2 * TQ:3 * TQ] + p_c[3 * TQ:4 * TQ]
    p_hi = p_sum.astype(BF16)
    r1 = p_sum - p_hi.astype(F32)
    p_mid = r1.astype(BF16)
    p_lo = (r1 - p_mid.astype(F32)).astype(BF16)
    smap = smap_ref[...]
    imp = (lax.dot_general(smap, p_hi, _NT, preferred_element_type=F32)
           + lax.dot_general(smap, p_mid, _NT, preferred_element_type=F32)
           + lax.dot_general(smap, p_lo, _NT, preferred_element_type=F32))
    blk = lax.broadcasted_iota(jnp.int32, (SEL_FEATS, TQ), 0)
    t_s = i * TQ + lax.broadcasted_iota(jnp.int32, (SEL_FEATS, TQ), 1)
    cur = t_s // SEL_BLOCK
    forced = (blk == 0) | (blk == cur) | (blk == cur - 1)
    visible = blk * SEL_BLOCK <= t_s
    score = jnp.where(visible, imp + jnp.where(forced, FORCE_SCORE, 0.0), -1.0)
    rank = jnp.zeros((SEL_FEATS, TQ), jnp.int32)
    for sp in range(SEL_FEATS):
        row = score[sp:sp + 1, :]
        beats = (row > score) | ((row == score) & (blk > sp))
        rank = rank + beats.astype(jnp.int32)
    pen_t = jnp.where(rank < SEL_TOPK, 0.0, NEG_INF)
    pen = jnp.transpose(pen_t).astype(BF16)
    q_aug = jnp.concatenate([qb, jnp.concatenate([pen] * HEADS_PER_GROUP, axis=0)], axis=1)

    def keys(ref, pos, n):
        return ref[0, 0, pl.ds(pl.multiple_of(pos + KV_PAD, TQ), n), :]

    m_sc[...] = jnp.full((rows, LANES), NEG_INF, F32)
    acc_sc[...] = jnp.zeros((rows, V_WIDTH), F32)
    per_chunk = SEL_CHUNK // TQ
    n_plain = jnp.maximum(i - 1, 0)
    rem = n_plain % per_chunk

    @pl.when(rem > 0)
    def _():
        pos = (rem - per_chunk) * TQ
        s = _logits(q_aug, keys(ksel_ref, pos, SEL_CHUNK), lo_cols=-pos)
        _flash_step(s, keys(vsel_ref, pos, SEL_CHUNK), m_sc, acc_sc)

    def sel_body(c, carry):
        pos = (rem + c * per_chunk) * TQ
        _flash_step(_logits(q_aug, keys(ksel_ref, pos, SEL_CHUNK)), keys(vsel_ref, pos, SEL_CHUNK), m_sc, acc_sc)
        return carry

    lax.fori_loop(0, n_plain // per_chunk, sel_body, 0)
    pos = (i - 1) * TQ
    s = _logits(q_aug, keys(ksel_ref, pos, 2 * TQ), add=wa_ref[0, :, WINDOW - TQ:], lo_cols=-pos)
    _flash_step(s, keys(vsel_ref, pos, 2 * TQ), m_sc, acc_sc)
    acc = acc_sc[...]
    o_sel = acc[:, :HEAD_DIM] / acc[:, HEAD_DIM:HEAD_DIM + 1]

    pos = i * TQ - WINDOW
    s = _logits(qb, keys(kwin_ref, pos, WINDOW + TQ), add=wa_ref[0], lo_cols=-pos)
    p = jnp.exp(s - jnp.max(s, axis=-1, keepdims=True))
    acc = jnp.dot(p.astype(BF16), keys(vwin_ref, pos, WINDOW + TQ), preferred_element_type=F32)
    o_win = acc[:, :HEAD_DIM] / acc[:, HEAD_DIM:HEAD_DIM + 1]

    gate = gate_ref[0]
    outs = []
    for j in range(HEADS_PER_GROUP):
        sl = slice(j * TQ, (j + 1) * TQ)
        outs.append(gate[:, 3 * j:3 * j + 1] * o_cmp[sl] + gate[:, 3 * j + 1:3 * j + 2] * o_sel[sl]
                    + gate[:, 3 * j + 2:3 * j + 3] * o_win[sl])
    o_ref[0] = jnp.concatenate(outs, axis=1)


def _attention(z_q, z_gate, q_norm_g, kc, vc, ksel, vsel, kwin, vwin, rel_bias):
    b_, s_, _ = z_q.shape
    ncp = kc.shape[2]
    n_cmp = ncp - CMP_BLOCK // CMP_STRIDE + 1
    assert s_ // SEL_BLOCK <= SEL_FEATS and s_ % TQ == 0
    wa, bc = _bias_tiles(rel_bias, ncp)
    smap_t = jnp.asarray(_sel_mapping_t(ncp, n_cmp), BF16)
    rows = HEADS_PER_GROUP * TQ
    gw = HEADS_PER_GROUP * HEAD_DIM
    pad = lambda a: jnp.pad(a, ((0, 0), (0, 0), (KV_PAD, 0), (0, 0)))
    ksel, vsel, kwin, vwin = pad(ksel), pad(vsel), pad(kwin), pad(vwin)
    kv = lambda w: pl.BlockSpec((1, 1, s_ + KV_PAD, w), lambda b, g, i: (b, g, 0, 0))
    cm = pl.BlockSpec((1, 1, ncp, HEAD_DIM), lambda b, g, i: (b, g, 0, 0))
    return pl.pallas_call(
        _attn_kernel,
        grid=(b_, N_KV_GROUPS, s_ // TQ),
        in_specs=[
            pl.BlockSpec((1, TQ, gw), lambda b, g, i: (b, i, g)),
            pl.BlockSpec((1, TQ, LANES), lambda b, g, i: (b, i, g)),
            pl.BlockSpec((1, HEAD_DIM), lambda b, g, i: (0, 0)),
            cm, cm, kv(HEAD_DIM + SEL_FEATS), kv(V_WIDTH), kv(HEAD_DIM), kv(V_WIDTH),
            pl.BlockSpec((1, rows, WINDOW + TQ), lambda b, g, i: (g, 0, 0)),
            pl.BlockSpec((1, rows, ncp), lambda b, g, i: (g, 0, 0)),
            pl.BlockSpec((SEL_FEATS, ncp), lambda b, g, i: (0, 0)),
        ],
        out_specs=pl.BlockSpec((1, TQ, gw), lambda b, g, i: (b, i, g)),
        out_shape=jax.ShapeDtypeStruct((b_, s_, N_KV_GROUPS * gw), F32),
        scratch_shapes=[pltpu.VMEM((rows, LANES), F32), pltpu.VMEM((rows, V_WIDTH), F32)],
        compiler_params=pltpu.CompilerParams(
            dimension_semantics=("parallel", "parallel", "arbitrary"), vmem_limit_bytes=VMEM_LIMIT),
    )(z_q, z_gate, q_norm_g.reshape(1, HEAD_DIM), kc, vc, ksel, vsel, kwin, vwin, wa, bc, smap_t)


def _pool_tile(halo, tile, i, pw_ref, ps_ref):
    ts = tile.shape[0]
    ext = jnp.concatenate([jnp.where(i > 0, halo, 0.0), tile], axis=0)
    t = i * ts + lax.broadcasted_iota(jnp.int32, (ts, 1), 0)
    outs = []
    for gi, w in enumerate(POOL_WINDOWS):
        sl = slice(gi * POOL_GROUP_DIM, (gi + 1) * POOL_GROUP_DIM)
        run = ext[:, sl]
        span = 1
        while span < w:
            run = run + pltpu.roll(run, span, axis=0)
            span *= 2
        cnt = jnp.minimum(t + 1, w).astype(F32)
        pooled = run[POOL_HALO:] / cnt - tile[:, sl]
        outs.append(jnp.dot(pooled.astype(BF16), pw_ref[gi], preferred_element_type=F32))
    return jnp.concatenate(outs, axis=1) * ps_ref[...]


def _merge_kernel(x_ref, zp_ref, zh_ref, ya_ref, zm_ref, g1_ref, pw_ref, ps_ref, wbp_ref, wba_ref, wo_ref, o_ref):
    d_ = x_ref.shape[-1]
    y_pool = _pool_tile(zh_ref[0], zp_ref[0], pl.program_id(1), pw_ref, ps_ref)
    bp = jnp.dot(y_pool.astype(BF16), wbp_ref[...], preferred_element_type=F32)
    ba = jnp.dot(ya_ref[0].astype(BF16), wba_ref[...], preferred_element_type=F32)
    zm = zm_ref[0]
    mixed = jax.nn.sigmoid(zm[:, :d_]) * bp + jax.nn.sigmoid(zm[:, d_:]) * ba
    proj = jnp.dot(mixed.astype(BF16), wo_ref[...], preferred_element_type=F32)
    o_ref[0] = x_ref[0] + g1_ref[0] * proj


def _merge(x, z_pool, y_attn, z_merge, gate1, pool_w, pool_scale, wbp, wba, wo):
    b_, s_, d_ = x.shape
    row = lambda w: pl.BlockSpec((1, ROW_TILE, w), lambda b, i: (b, i, 0))
    full = lambda a: pl.BlockSpec(a.shape, lambda b, i: (0,) * a.ndim)
    halo_blocks = ROW_TILE // POOL_HALO
    halo = pl.BlockSpec((1, POOL_HALO, POOL_WIDTH), lambda b, i: (b, jnp.maximum(i * halo_blocks - 1, 0), 0))
    pool_scale = pool_scale.reshape(1, POOL_WIDTH)
    return pl.pallas_call(
        _merge_kernel,
        grid=(b_, s_ // ROW_TILE),
        in_specs=[row(d_), row(POOL_WIDTH), halo, row(Q_WIDTH), row(2 * d_),
                  pl.BlockSpec((1, 1, d_), lambda b, i: (b, 0, 0)),
                  full(pool_w), full(pool_scale), full(wbp), full(wba), full(wo)],
        out_specs=row(d_),
        out_shape=jax.ShapeDtypeStruct((b_, s_, d_), F32),
        compiler_params=pltpu.CompilerParams(
            dimension_semantics=("parallel", "parallel"), vmem_limit_bytes=VMEM_LIMIT),
    )(x, z_pool, z_pool, y_attn, z_merge, gate1, pool_w, pool_scale, wbp, wba, wo)


def _ada_kernel(c_ref, w_ref, b_ref, o_ref):
    c = c_ref[...]
    o_ref[...] = jnp.dot(jax.nn.silu(c).astype(BF16), w_ref[...].astype(BF16),
                         preferred_element_type=F32) + b_ref[...]


def _ada(c, w, b):
    b_, d_ = c.shape
    n = w.shape[1]
    return pl.pallas_call(
        _ada_kernel,
        grid=(n // d_,),
        in_specs=[pl.BlockSpec((b_, d_), lambda j: (0, 0)),
                  pl.BlockSpec((d_, d_), lambda j: (0, j)),
                  pl.BlockSpec((1, d_), lambda j: (0, j))],
        out_specs=pl.BlockSpec((b_, d_), lambda j: (0, j)),
        out_shape=jax.ShapeDtypeStruct((b_, n), F32),
        compiler_params=pltpu.CompilerParams(dimension_semantics=("parallel",), vmem_limit_bytes=VMEM_LIMIT),
    )(c, w, b.reshape(1, n))


_CAND_ROWS = PEER_TOPK + 7 * 8 + 8


def _topk_rows(x, k):
    n = x.shape[0]
    rid = lax.broadcasted_iota(jnp.int32, x.shape, 0)
    vals, idxs = [], []
    for _ in range(k):
        m = jnp.max(x, axis=0, keepdims=True)
        idx = jnp.min(jnp.where(x == m, rid, n), axis=0, keepdims=True)
        vals.append(m)
        idxs.append(idx)
        x = jnp.where(rid == idx, -jnp.inf, x)
    return jnp.concatenate(vals, axis=0), jnp.concatenate(idxs, axis=0)


def _pair_grid(r0, r1, combine):
    parts = [combine(r0[0:1], r1)]
    parts += [combine(r0[a:a + 1], r1[0:8]) for a in range(1, 8)]
    parts.append(combine(r0[8:16], r1[0:1]))
    return jnp.concatenate(parts, axis=0)


def _peer_route_kernel(x_ref, g_ref, sc_ref, sh_ref, w_ref, keys_ref, h_ref, eid_ref, gw_ref):
    x = x_ref[0]
    y = x * lax.rsqrt(jnp.mean(x * x, axis=-1, keepdims=True) + EPS)
    h = y * g_ref[...] * (1.0 + sc_ref[0]) + sh_ref[0]
    h_ref[0] = h
    qv = jnp.dot(h.astype(BF16), w_ref[...], preferred_element_type=F32).astype(BF16)
    half = PEER_QDIM // 2
    eids, gws = [], []
    for hd in range(PEER_HEADS):
        tops = []
        for p in range(2):
            c = hd * 2 + p
            sc = lax.dot_general(keys_ref[c], qv[:, c * half:(c + 1) * half], _NT,
                                 preferred_element_type=F32)
            tops.append(_topk_rows(sc, PEER_TOPK))
        (s0, i0), (s1, i1) = tops
        cand = _pair_grid(s0, s1, lambda a, b: a + b)
        cid = _pair_grid(i0, i1, lambda a, b: a * (PEER_KEYS * EXPERT_ROWS) + (b * EXPERT_ROWS + EXPERT_ROWS))
        rid = lax.broadcasted_iota(jnp.int32, cand.shape, 0)
        top_s, top_e = [], []
        for _ in range(PEER_TOPK):
            m = jnp.max(cand, axis=0, keepdims=True)
            idx = jnp.min(jnp.where(cand == m, rid, _CAND_ROWS), axis=0, keepdims=True)
            hit = rid == idx
            top_s.append(m)
            top_e.append(jnp.sum(jnp.where(hit, cid, 0), axis=0, keepdims=True))
            cand = jnp.where(hit, -jnp.inf, cand)
        top_s = jnp.concatenate(top_s, axis=0)
        e = jnp.exp(top_s - top_s[0:1])
        gws.append(e / jnp.sum(e, axis=0, keepdims=True))
        eids.append(jnp.concatenate(top_e, axis=0))
    eids = jnp.concatenate(eids, axis=0)
    for c in range(PEER_TILE // PEER_EXPERT_TILE):
        eid_ref[0, c] = eids[:, c * PEER_EXPERT_TILE:(c + 1) * PEER_EXPERT_TILE]
    gw_ref[0] = jnp.transpose(jnp.concatenate(gws, axis=0))


def _peer_route(x, g, scale, shift, w_bf16, keys_bf16):
    b_, s_, d_ = x.shape
    n = w_bf16.shape[1]
    nk = PEER_HEADS * PEER_TOPK
    per_tile = PEER_TILE // PEER_EXPERT_TILE
    row = lambda w: pl.BlockSpec((1, PEER_TILE, w), lambda b, i: (b, i, 0))
    return pl.pallas_call(
        _peer_route_kernel,
        grid=(b_, s_ // PEER_TILE),
        in_specs=[
            row(d_),
            pl.BlockSpec((1, d_), lambda b, i: (0, 0)),
            pl.BlockSpec((1, 1, d_), lambda b, i: (b, 0, 0)),
            pl.BlockSpec((1, 1, d_), lambda b, i: (b, 0, 0)),
            pl.BlockSpec((d_, n), lambda b, i: (0, 0)),
            pl.BlockSpec(keys_bf16.shape, lambda b, i: (0, 0, 0)),
        ],
        out_specs=[row(d_),
                   pl.BlockSpec((1, per_tile, nk, PEER_EXPERT_TILE), lambda b, i: (b, i, 0, 0)),
                   row(nk)],
        out_shape=[
            jax.ShapeDtypeStruct((b_, s_, d_), F32),
            jax.ShapeDtypeStruct((b_, s_ // PEER_EXPERT_TILE, nk, PEER_EXPERT_TILE), jnp.int32),
            jax.ShapeDtypeStruct((b_, s_, nk), F32),
        ],
        compiler_params=pltpu.CompilerParams(
            dimension_semantics=("parallel", "parallel"), vmem_limit_bytes=VMEM_LIMIT),
    )(x, g.reshape(1, d_), scale, shift, w_bf16, keys_bf16)


_HI_MASK = -65536


def _pack_table(t):
    e_, d_ = t.shape
    bits = lax.bitcast_convert_type(t.astype(jnp.bfloat16), jnp.uint16).astype(jnp.uint32)
    words = (bits[:, d_ // 2:] << 16) | bits[:, :d_ // 2]
    rows = lax.bitcast_convert_type(words, jnp.int32).reshape(e_ * EXPERT_ROWS, LANES)
    return jnp.pad(rows, ((EXPERT_ROWS, EXPERT_ROWS), (0, 0)))


def _expert_words(tab_vmem, row):
    return tab_vmem[pl.ds(pl.multiple_of(row, EXPERT_ROWS), EXPERT_ROWS), :]


def _expert_pair_words(tab_vmem, row_a, row_b, sub):
    top = tab_vmem[pl.ds(pl.multiple_of(row_a, EXPERT_ROWS), 2 * EXPERT_ROWS), :]
    bot = tab_vmem[pl.ds(pl.multiple_of(row_b - EXPERT_ROWS, EXPERT_ROWS), 2 * EXPERT_ROWS), :]
    return jnp.where(sub < EXPERT_ROWS, top, bot)


def _unpack(words):
    lo = lax.bitcast_convert_type(lax.shift_left(words, 16), F32)
    hi = lax.bitcast_convert_type(words & _HI_MASK, F32)
    return lo, hi


def _load_resident(j, tab_hbm, tab_vmem, eid_ref, idx_smem, sem):
    @pl.when(j == 0)
    def _():
        cp = pltpu.make_async_copy(tab_hbm, tab_vmem, sem)
        cp.start()
        cp.wait()

    copies = [pltpu.make_async_copy(eid_ref.at[0, k], idx_smem[k], sem) for k in range(len(idx_smem))]
    for cp in copies:
        cp.start()
    for cp in copies:
        cp.wait()


def _fold_pairs(q, sub):
    m2, m1 = (sub & 2) == 0, (sub & 1) == 0
    u = [jnp.where(m2, q[i], q[i + 2]) + jnp.where(m2, pltpu.roll(q[i], 6, axis=0), pltpu.roll(q[i + 2], 2, axis=0))
         for i in range(2)]
    return jnp.where(m1, u[0], u[1]) + jnp.where(m1, pltpu.roll(u[0], 7, axis=0), pltpu.roll(u[1], 1, axis=0))


def _peer_u_kernel(eid_ref, gw_ref, h_ref, tab_hbm, w_ref, *scratch):
    j = pl.program_id(0)
    nk = PEER_HEADS * PEER_TOPK
    idx_smem, (tab_vmem, acc_ref, sem) = scratch[:nk], scratch[nk:]
    half = D_MODEL // 2 // LANES
    _load_resident(j, tab_hbm, tab_vmem, eid_ref, idx_smem, sem)
    sub = lax.broadcasted_iota(jnp.int32, (2 * half, LANES), 0)
    lane = lax.broadcasted_iota(jnp.int32, (nk, PEER_EXPERT_TILE), 1)
    acc_ref[...] = jnp.zeros((nk, PEER_EXPERT_TILE), F32)

    def token(t):
        x = h_ref[t]
        x_lo = jnp.concatenate([x[0:half], x[0:half]], axis=0)
        x_hi = jnp.concatenate([x[half:], x[half:]], axis=0)
        folded = []
        for gi in range(nk // 8):
            q = []
            for i in range(4):
                k = gi * 8 + i
                words = _expert_pair_words(tab_vmem, idx_smem[k][t], idx_smem[k + 4][t], sub)
                lo, hi = _unpack(words)
                q.append(lo * x_lo + hi * x_hi)
            folded.append(_fold_pairs(q, sub))
        return jnp.sum(jnp.concatenate(folded, axis=0), axis=-1, keepdims=True)

    def body(tb, carry):
        acc = acc_ref[...]
        for u in range(PEER_TOKEN_UNROLL):
            t = tb * PEER_TOKEN_UNROLL + u
            acc = jnp.where(lane == t, token(t), acc)
        acc_ref[...] = acc
        return carry

    lax.fori_loop(0, PEER_EXPERT_TILE // PEER_TOKEN_UNROLL, body, 0)
    w_ref[...] = jax.nn.gelu(jnp.transpose(acc_ref[...])) * gw_ref[...]


def _peer_v_kernel(eid_ref, w_ref, x_ref, g2_ref, tab_hbm, o_ref, *scratch):
    j = pl.program_id(0)
    nk = PEER_HEADS * PEER_TOPK
    idx_smem, (wb_ref, tab_vmem, sem) = scratch[:nk], scratch[nk:]
    half = D_MODEL // 2 // LANES
    _load_resident(j, tab_hbm, tab_vmem, eid_ref, idx_smem, sem)
    n_acc = 4
    sub = lax.broadcasted_iota(jnp.int32, (2 * half, LANES), 0)

    def token(t, slot):
        wb_ref[slot] = jnp.transpose(jnp.broadcast_to(w_ref[pl.ds(t, 1), :], (nk, nk)))
        lo_acc = [jnp.zeros((2 * half, LANES), F32) for _ in range(n_acc)]
        hi_acc = [jnp.zeros((2 * half, LANES), F32) for _ in range(n_acc)]
        for p in range(nk // 2):
            k = 2 * p
            lo, hi = _unpack(_expert_pair_words(tab_vmem, idx_smem[k][t], idx_smem[k + 1][t], sub))
            wk = jnp.where(sub < half, wb_ref[slot, k:k + 1, :], wb_ref[slot, k + 1:k + 2, :])
            lo_acc[p % n_acc] = lo_acc[p % n_acc] + wk * lo
            hi_acc[p % n_acc] = hi_acc[p % n_acc] + wk * hi
        lo_sum, hi_sum = sum(lo_acc[1:], lo_acc[0]), sum(hi_acc[1:], hi_acc[0])
        ff = jnp.concatenate([lo_sum[:half] + lo_sum[half:], hi_sum[:half] + hi_sum[half:]], axis=0)
        o_ref[t] = x_ref[t] + g2_ref[0] * ff

    def body(tb, carry):
        for u in range(PEER_TOKEN_UNROLL):
            token(tb * PEER_TOKEN_UNROLL + u, u)
        return carry

    lax.fori_loop(0, PEER_EXPERT_TILE // PEER_TOKEN_UNROLL, body, 0)


def _peer_experts(eid, gw, h2, x1, gate2, u_tab, v_tab):
    b_, s_, d_ = x1.shape
    n_tok = b_ * s_
    nk = PEER_HEADS * PEER_TOPK
    sub_rows = d_ // LANES
    tt = PEER_EXPERT_TILE
    nsteps = n_tok // tt
    steps_per_batch = s_ // tt
    tok = pl.BlockSpec((tt, sub_rows, LANES), lambda j: (j, 0, 0))
    sel = pl.BlockSpec((tt, nk), lambda j: (j, 0))
    slots = pl.BlockSpec((1, nk, tt), lambda j: (j, 0, 0))
    idx_scratch = [pltpu.SMEM((tt,), jnp.int32) for _ in range(nk)]
    params = pltpu.CompilerParams(dimension_semantics=("arbitrary",), vmem_limit_bytes=PEER_VMEM_LIMIT)
    eid3 = eid.reshape(nsteps, nk, tt)
    w = pl.pallas_call(
        _peer_u_kernel,
        grid=(nsteps,),
        in_specs=[slots, sel, tok, pl.BlockSpec(memory_space=pl.ANY)],
        out_specs=sel,
        out_shape=jax.ShapeDtypeStruct((n_tok, nk), F32),
        scratch_shapes=idx_scratch + [pltpu.VMEM(u_tab.shape, jnp.int32), pltpu.VMEM((nk, tt), F32),
                                      pltpu.SemaphoreType.DMA],
        compiler_params=params,
    )(eid3, gw.reshape(n_tok, nk), h2.reshape(n_tok, sub_rows, LANES), u_tab)
    out = pl.pallas_call(
        _peer_v_kernel,
        grid=(nsteps,),
        in_specs=[slots, sel, tok, pl.BlockSpec((1, sub_rows, LANES), lambda j: (j // steps_per_batch, 0, 0)),
                  pl.BlockSpec(memory_space=pl.ANY)],
        out_specs=tok,
        out_shape=jax.ShapeDtypeStruct((n_tok, sub_rows, LANES), F32),
        scratch_shapes=idx_scratch + [pltpu.VMEM((PEER_TOKEN_UNROLL, nk, LANES), F32),
                                      pltpu.VMEM(v_tab.shape, jnp.int32), pltpu.SemaphoreType.DMA],
        compiler_params=params,
    )(eid3, w, x1.reshape(n_tok, sub_rows, LANES), gate2.reshape(b_, sub_rows, LANES), v_tab)
    return out.reshape(b_, s_, d_)


def _permute_w_in(w):
    d_ = w.shape[0]
    kv_end = POOL_WIDTH + Q_WIDTH + 6 * KV_WIDTH
    per_group = HEADS_PER_GROUP * 3
    parts = [w[:, :kv_end], w[:, kv_end + GATE_COLS:]]
    for g in range(N_KV_GROUPS):
        parts.append(w[:, kv_end + g * per_group:kv_end + (g + 1) * per_group])
        parts.append(jnp.zeros((d_, LANES - per_group), w.dtype))
    return jnp.concatenate(parts, axis=1).astype(BF16)


def kernel(x, c, rel_bias, ada_w, ada_b, norm1_g, norm2_g, w_in, pool_w, pool_scale, cmp_pe_k, cmp_w1_k, cmp_w2_k, cmp_pe_v, cmp_w1_v, cmp_w2_v, q_norm_g, k_norm_g, w_branch_pool, w_branch_attn, w_out, peer_w_q, peer_sub_keys, peer_u, peer_v):
    b_, s_, d_ = x.shape
    l = 0
    ada = _ada(c, ada_w[l], ada_b[l]).reshape(b_, 6, 1, d_)
    shift1, scale1, gate1 = ada[:, 0], ada[:, 1], ada[:, 2]
    shift2, scale2, gate2 = ada[:, 3], ada[:, 4], ada[:, 5]

    (z_pool, z_q, ksel, vsel, kwin, vwin, kc_raw, vc_raw, z_merge, z_gate) = _in_proj(
        x, norm1_g[l], scale1, shift1, _permute_w_in(w_in[l]), k_norm_g[l])

    kc = _compress(kc_raw, cmp_pe_k[l], cmp_w1_k[l], cmp_w2_k[l], k_norm_g[l, 0], True)
    vc = _compress(vc_raw, cmp_pe_v[l], cmp_w1_v[l], cmp_w2_v[l], k_norm_g[l, 0], False)
    y_attn = _attention(z_q, z_gate, q_norm_g[l], kc, vc, ksel, vsel, kwin, vwin, rel_bias)

    x1 = _merge(x, z_pool, y_attn, z_merge, gate1, pool_w[l].astype(BF16), pool_scale[l],
                w_branch_pool[l].astype(BF16), w_branch_attn[l].astype(BF16), w_out[l].astype(BF16))

    keys = peer_sub_keys[l].reshape(PEER_HEADS * 2, PEER_KEYS, PEER_QDIM // 2).astype(BF16)
    h2, eid, gw = _peer_route(x1, norm2_g[l], scale2, shift2, peer_w_q[l].astype(BF16), keys)
    return _peer_experts(eid, gw, h2, x1, gate2, _pack_table(peer_u[l]), _pack_table(peer_v[l]))
```

```python
import math
from functools import partial

import jax
import jax.numpy as jnp
import numpy as np
from jax import lax
from jax.experimental import pallas as pl
from jax.experimental.pallas import tpu as pltpu

D_MODEL = 1024
POOL_WIDTH = 512
POOL_WINDOWS = (2, 4, 8, 16)
POOL_GROUP_DIM = 128
N_HEADS = 8
N_KV_GROUPS = 2
HEADS_PER_GROUP = 4
HEAD_DIM = 64
Q_WIDTH = 512
KV_WIDTH = 128
CMP_BLOCK = 32
CMP_STRIDE = 16
CMP_HIDDEN = 128
SEL_BLOCK = 64
SEL_TOPK = 16
WINDOW = 512
FORCE_SCORE = 1000.0
N_BUCKETS = 32
MAX_DISTANCE = 128
PEER_HEADS = 8
PEER_KEYS = 128
PEER_TOPK = 16
PEER_QDIM = 256
PEER_CHUNK = 128
GATE_COLS = N_HEADS * 3
EPS = 1e-6
NEG_INF = -1e30

LANES = 128
VMEM_LIMIT = 48 * 1024 * 1024
ROW_TILE = 512
POOL_HALO = 16
TQ = 128
SEL_FEATS = 64
SEL_CHUNK = 512
V_WIDTH = 128
KV_PAD = 512
CMP_BAND_LO = 10
PEER_TILE = 256
PEER_EXPERT_TILE = 256
PEER_TOKEN_UNROLL = 4
EXPERT_ROWS = D_MODEL // 2 // LANES
PEER_VMEM_LIMIT = 56 * 1024 * 1024

BF16 = jnp.bfloat16
F32 = jnp.float32
_NT = (((1,), (1,)), ((), ()))


def _rms_rows(x, g):
    return x * lax.rsqrt(jnp.mean(x * x, axis=-1, keepdims=True) + EPS) * g


def _in_proj_kernel(x_ref, g_ref, sc_ref, sh_ref, w_ref, kg_ref,
                    zpool_ref, zq_ref, ksel_ref, vsel_ref, kwin_ref, vwin_ref, kcr_ref, vcr_ref,
                    zmerge_ref, zgate_ref):
    i = pl.program_id(1)
    x = x_ref[0]
    y = x * lax.rsqrt(jnp.mean(x * x, axis=-1, keepdims=True) + EPS)
    h = y * g_ref[...] * (1.0 + sc_ref[0]) + sh_ref[0]
    z = jnp.dot(h.astype(BF16), w_ref[...], preferred_element_type=F32)
    ts = x.shape[0]
    zpool_ref[0] = z[:, :POOL_WIDTH]
    o = POOL_WIDTH
    zq_ref[0] = z[:, o:o + Q_WIDTH]
    o += Q_WIDTH
    pos = i * ts + lax.broadcasted_iota(jnp.int32, (ts, SEL_FEATS), 0)
    onehot = (pos // SEL_BLOCK == lax.broadcasted_iota(jnp.int32, (ts, SEL_FEATS), 1)).astype(BF16)
    one_col = (lax.broadcasted_iota(jnp.int32, (ts, V_WIDTH - HEAD_DIM), 1) == 0).astype(BF16)
    for g in range(N_KV_GROUPS):
        def col(k):
            return z[:, o + k * KV_WIDTH + g * HEAD_DIM:o + k * KV_WIDTH + (g + 1) * HEAD_DIM]
        kcr_ref[0, g] = col(0)
        vcr_ref[0, g] = col(1)
        ks = _rms_rows(col(2), kg_ref[1:2, :]).astype(BF16)
        ksel_ref[0, g] = jnp.concatenate([ks, onehot], axis=1)
        vsel_ref[0, g] = jnp.concatenate([col(3).astype(BF16), one_col], axis=1)
        kwin_ref[0, g] = _rms_rows(col(4), kg_ref[2:3, :]).astype(BF16)
        vwin_ref[0, g] = jnp.concatenate([col(5).astype(BF16), one_col], axis=1)
    o += 6 * KV_WIDTH
    zmerge_ref[0] = z[:, o:o + 2 * D_MODEL]
    o += 2 * D_MODEL
    zgate_ref[0] = jax.nn.sigmoid(z[:, o:o + N_KV_GROUPS * LANES])


def _in_proj(x, g, scale, shift, w_bf16, k_norm_g):
    b_, s_, d_ = x.shape
    n = w_bf16.shape[1]
    row = lambda w: pl.BlockSpec((1, ROW_TILE, w), lambda b, i: (b, i, 0))
    grp = lambda w: pl.BlockSpec((1, N_KV_GROUPS, ROW_TILE, w), lambda b, i: (b, 0, i, 0))
    gshape = lambda w, dt: jax.ShapeDtypeStruct((b_, N_KV_GROUPS, s_, w), dt)
    return pl.pallas_call(
        _in_proj_kernel,
        grid=(b_, s_ // ROW_TILE),
        in_specs=[
            row(d_),
            pl.BlockSpec((1, d_), lambda b, i: (0, 0)),
            pl.BlockSpec((1, 1, d_), lambda b, i: (b, 0, 0)),
            pl.BlockSpec((1, 1, d_), lambda b, i: (b, 0, 0)),
            pl.BlockSpec((d_, n), lambda b, i: (0, 0)),
            pl.BlockSpec((3, HEAD_DIM), lambda b, i: (0, 0)),
        ],
        out_specs=[row(POOL_WIDTH), row(Q_WIDTH), grp(HEAD_DIM + SEL_FEATS), grp(V_WIDTH), grp(HEAD_DIM),
                   grp(V_WIDTH), grp(HEAD_DIM), grp(HEAD_DIM), row(2 * d_), row(N_KV_GROUPS * LANES)],
        out_shape=[
            jax.ShapeDtypeStruct((b_, s_, POOL_WIDTH), F32),
            jax.ShapeDtypeStruct((b_, s_, Q_WIDTH), F32),
            gshape(HEAD_DIM + SEL_FEATS, BF16), gshape(V_WIDTH, BF16), gshape(HEAD_DIM, BF16), gshape(V_WIDTH, BF16),
            gshape(HEAD_DIM, F32), gshape(HEAD_DIM, F32),
            jax.ShapeDtypeStruct((b_, s_, 2 * d_), F32),
            jax.ShapeDtypeStruct((b_, s_, N_KV_GROUPS * LANES), F32),
        ],
        compiler_params=pltpu.CompilerParams(
            dimension_semantics=("parallel", "parallel"), vmem_limit_bytes=VMEM_LIMIT),
    )(x, g.reshape(1, d_), scale, shift, w_bf16, k_norm_g)


def _compress_kernel(t_ref, pe_ref, w1_ref, w2_ref, g_ref, o_ref, *, normalize):
    half = CMP_STRIDE * HEAD_DIM
    t = t_ref[0, 0].astype(BF16)
    w1 = w1_ref[...]
    a = jnp.dot(t, w1[:half], preferred_element_type=F32)
    b = jnp.dot(t, w1[half:], preferred_element_type=F32)
    pe = jnp.dot(pe_ref[...], w1, preferred_element_type=F32)
    n = a.shape[0]
    b_next = pltpu.roll(b, n - 1, axis=0)
    hid = jax.nn.gelu(a + b_next + pe)
    out = jnp.dot(hid.astype(BF16), w2_ref[...], preferred_element_type=F32)
    if normalize:
        out = _rms_rows(out, g_ref[...])
    o_ref[0, 0] = out.astype(BF16)


def _compress(t_raw, pe, w1, w2, g, normalize):
    b_, g_, s_, dh = t_raw.shape
    n_str = s_ // CMP_STRIDE
    t = t_raw.reshape(b_, g_, n_str, CMP_STRIDE * dh)
    return pl.pallas_call(
        partial(_compress_kernel, normalize=normalize),
        grid=(b_, g_),
        in_specs=[
            pl.BlockSpec((1, 1, n_str, CMP_STRIDE * dh), lambda b, g: (b, g, 0, 0)),
            pl.BlockSpec((1, CMP_BLOCK * dh), lambda b, g: (0, 0)),
            pl.BlockSpec((CMP_BLOCK * dh, CMP_HIDDEN), lambda b, g: (0, 0)),
            pl.BlockSpec((CMP_HIDDEN, dh), lambda b, g: (0, 0)),
            pl.BlockSpec((1, dh), lambda b, g: (0, 0)),
        ],
        out_specs=pl.BlockSpec((1, 1, n_str, dh), lambda b, g: (b, g, 0, 0)),
        out_shape=jax.ShapeDtypeStruct((b_, g_, n_str, dh), BF16),
        compiler_params=pltpu.CompilerParams(
            dimension_semantics=("parallel", "parallel"), vmem_limit_bytes=VMEM_LIMIT),
    )(t, pe.reshape(1, CMP_BLOCK * dh).astype(BF16), w1.astype(BF16), w2.astype(BF16), g.reshape(1, dh))


def _t5_bucket_np(rel):
    n = np.maximum(rel, 0)
    max_exact = N_BUCKETS // 2
    nf = np.maximum(n, 1).astype(np.float32)
    large = max_exact + (np.log(nf / max_exact) / math.log(MAX_DISTANCE / max_exact) * (N_BUCKETS - max_exact)).astype(np.int32)
    large = np.minimum(large, N_BUCKETS - 1)
    return np.where(n < max_exact, n, large)


def _bias_tiles(rel_bias, n_cmp_pad):
    far = rel_bias[N_BUCKETS - 1]
    ii = np.arange(TQ)[:, None]

    def lookup(dist):
        near = (dist >= 0) & (dist < MAX_DISTANCE)
        bucket = _t5_bucket_np(np.where(near, dist, MAX_DISTANCE))
        onehot = (jnp.asarray(bucket.astype(np.int8))[..., None] == jnp.arange(N_BUCKETS, dtype=jnp.int8)).astype(F32)
        b = jnp.dot(onehot, rel_bias - far, precision=lax.Precision.HIGHEST)
        b = jnp.transpose(b, (2, 0, 1))
        return b.reshape(N_KV_GROUPS, HEADS_PER_GROUP * dist.shape[0], dist.shape[1])

    jj = np.arange(TQ)[None, :]
    bt = jnp.stack([lookup(ii - jj), lookup(TQ + ii - jj)], axis=1)
    cc = np.arange(n_cmp_pad)[None, :]
    dist_c = ii - CMP_STRIDE * (cc - CMP_BAND_LO) - (CMP_BLOCK - 1)
    dist_c = np.where(cc <= CMP_BAND_LO + TQ // CMP_STRIDE, dist_c, -1)
    bc = lookup(dist_c)
    rows = HEADS_PER_GROUP * TQ
    i_r = np.tile(np.arange(TQ), HEADS_PER_GROUP)[:, None]
    first = jnp.asarray(np.where(jj > i_r, 0.0, NEG_INF), F32)
    last = jnp.where(jnp.asarray(jj <= i_r), bt[:, 0], NEG_INF)
    zeros = jnp.zeros((N_KV_GROUPS, rows, WINDOW - 2 * TQ), F32)
    wa = jnp.concatenate([jnp.broadcast_to(first, (N_KV_GROUPS, rows, TQ)), zeros, bt[:, 1], last], axis=-1)
    return wa, bc


def _sel_mapping_t(n_cmp_pad, n_cmp):
    m = np.zeros((SEL_FEATS, n_cmp_pad), np.float32)
    pos = np.arange(n_cmp)[:, None] * CMP_STRIDE + np.arange(CMP_BLOCK)[None, :]
    np.add.at(m, ((pos // SEL_BLOCK).ravel(), np.repeat(np.arange(n_cmp), CMP_BLOCK)), 1.0 / CMP_BLOCK)
    return m


def _logits(q, k, add=None, lo_cols=None):
    s = lax.dot_general(q, k, _NT, preferred_element_type=F32)
    if add is not None:
        s = s + add
    if lo_cols is not None:
        s = jnp.where(lax.broadcasted_iota(jnp.int32, s.shape, 1) < lo_cols, NEG_INF, s)
    return s


def _flash_step(s, v, m_sc, acc_sc):
    m_prev = m_sc[...]
    m_new = jnp.maximum(m_prev, jnp.max(s, axis=-1, keepdims=True))
    alpha = jnp.exp(m_prev - m_new)
    p = jnp.exp(s - jnp.concatenate([m_new] * (s.shape[1] // LANES), axis=1))
    acc_sc[...] = alpha * acc_sc[...] + jnp.dot(p.astype(BF16), v, preferred_element_type=F32)
    m_sc[...] = m_new


def _attn_kernel(zq_ref, gate_ref, qg_ref, kc_ref, vc_ref, ksel_ref, vsel_ref, kwin_ref, vwin_ref,
                 wa_ref, bc_ref, smap_ref, o_ref, m_sc, acc_sc):
    i = pl.program_id(2)
    rows = HEADS_PER_GROUP * TQ
    ncp = kc_ref.shape[2]
    scale = HEAD_DIM ** -0.5

    zq = zq_ref[0]
    qs = jnp.concatenate([zq[:, j * HEAD_DIM:(j + 1) * HEAD_DIM] for j in range(HEADS_PER_GROUP)], axis=0)
    qn = _rms_rows(qs, qg_ref[...]) * scale
    qb = qn.astype(BF16)

    lc = lax.dot_general(qb, kc_ref[0, 0], _NT, preferred_element_type=F32)
    lc = lc + pltpu.roll(bc_ref[0], (i * (TQ // CMP_STRIDE) + ncp - CMP_BAND_LO) % ncp, axis=1)
    t_c = i * TQ + lax.broadcasted_iota(jnp.int32, (rows, ncp), 0) % TQ
    n_c = lax.broadcasted_iota(jnp.int32, (rows, ncp), 1)
    valid_c = n_c * CMP_STRIDE + (CMP_BLOCK - 1) <= t_c
    lm = jnp.where(valid_c, lc, NEG_INF)
    e = jnp.where(valid_c, jnp.exp(lm - jnp.max(lm, axis=-1, keepdims=True)), 0.0)
    den = jnp.sum(e, axis=-1, keepdims=True)
    p_c = e / jnp.where(den > 0.0, den, 1.0)
    o_cmp = jnp.dot(p_c.astype(BF16), vc_ref[0, 0], preferred_element_type=F32)

    p_sum = p_c[0:TQ] + p_c[TQ:2 * TQ] + p_c[2 * TQ:3 * TQ] + p_c[3 * TQ:4 * TQ]
    p_hi = p_sum.astype(BF16)
    r1 = p_sum - p_hi.astype(F32)
    p_mid = r1.astype(BF16)
    p_lo = (r1 - p_mid.astype(F32)).astype(BF16)
    smap = smap_ref[...]
    imp = (lax.dot_general(smap, p_hi, _NT, preferred_element_type=F32)
           + lax.dot_general(smap, p_mid, _NT, preferred_element_type=F32)
           + lax.dot_general(smap, p_lo, _NT, preferred_element_type=F32))
    blk = lax.broadcasted_iota(jnp.int32, (SEL_FEATS, TQ), 0)
    t_s = i * TQ + lax.broadcasted_iota(jnp.int32, (SEL_FEATS, TQ), 1)
    cur = t_s // SEL_BLOCK
    forced = (blk == 0) | (blk == cur) | (blk == cur - 1)
    visible = blk * SEL_BLOCK <= t_s
    score = jnp.where(visible, imp + jnp.where(forced, FORCE_SCORE, 0.0), -1.0)
    rank = jnp.zeros((SEL_FEATS, TQ), jnp.int32)
    for sp in range(SEL_FEATS):
        row = score[sp:sp + 1, :]
        beats = (row > score) | ((row == score) & (blk > sp))
        rank = rank + beats.astype(jnp.int32)
    pen_t = jnp.where(rank < SEL_TOPK, 0.0, NEG_INF)
    pen = jnp.transpose(pen_t).astype(BF16)
    q_aug = jnp.concatenate([qb, jnp.concatenate([pen] * HEADS_PER_GROUP, axis=0)], axis=1)

    def keys(ref, pos, n):
        return ref[0, 0, pl.ds(pl.multiple_of(pos + KV_PAD, TQ), n), :]

    m_sc[...] = jnp.full((rows, LANES), NEG_INF, F32)
    acc_sc[...] = jnp.zeros((rows, V_WIDTH), F32)
    per_chunk = SEL_CHUNK // TQ
    n_plain = jnp.maximum(i - 1, 0)
    rem = n_plain % per_chunk

    @pl.when(rem > 0)
    def _():
        pos = (rem - per_chunk) * TQ
        s = _logits(q_aug, keys(ksel_ref, pos, SEL_CHUNK), lo_cols=-pos)
        _flash_step(s, keys(vsel_ref, pos, SEL_CHUNK), m_sc, acc_sc)

    def sel_body(c, carry):
        pos = (rem + c * per_chunk) * TQ
        _flash_step(_logits(q_aug, keys(ksel_ref, pos, SEL_CHUNK)), keys(vsel_ref, pos, SEL_CHUNK), m_sc, acc_sc)
        return carry

    lax.fori_loop(0, n_plain // per_chunk, sel_body, 0)
    pos = (i - 1) * TQ
    s = _logits(q_aug, keys(ksel_ref, pos, 2 * TQ), add=wa_ref[0, :, WINDOW - TQ:], lo_cols=-pos)
    _flash_step(s, keys(vsel_ref, pos, 2 * TQ), m_sc, acc_sc)
    acc = acc_sc[...]
    o_sel = acc[:, :HEAD_DIM] / acc[:, HEAD_DIM:HEAD_DIM + 1]

    pos = i * TQ - WINDOW
    s = _logits(qb, keys(kwin_ref, pos, WINDOW + TQ), add=wa_ref[0], lo_cols=-pos)
    p = jnp.exp(s - jnp.max(s, axis=-1, keepdims=True))
    acc = jnp.dot(p.astype(BF16), keys(vwin_ref, pos, WINDOW + TQ), preferred_element_type=F32)
    o_win = acc[:, :HEAD_DIM] / acc[:, HEAD_DIM:HEAD_DIM + 1]

    gate = gate_ref[0]
    outs = []
    for j in range(HEADS_PER_GROUP):
        sl = slice(j * TQ, (j + 1) * TQ)
        outs.append(gate[:, 3 * j:3 * j + 1] * o_cmp[sl] + gate[:, 3 * j + 1:3 * j + 2] * o_sel[sl]
                    + gate[:, 3 * j + 2:3 * j + 3] * o_win[sl])
    o_ref[0] = jnp.concatenate(outs, axis=1)


def _attention(z_q, z_gate, q_norm_g, kc, vc, ksel, vsel, kwin, vwin, rel_bias):
    b_, s_, _ = z_q.shape
    ncp = kc.shape[2]
    n_cmp = ncp - CMP_BLOCK // CMP_STRIDE + 1
    assert s_ // SEL_BLOCK <= SEL_FEATS and s_ % TQ == 0
    wa, bc = _bias_tiles(rel_bias, ncp)
    smap_t = jnp.asarray(_sel_mapping_t(ncp, n_cmp), BF16)
    rows = HEADS_PER_GROUP * TQ
    gw = HEADS_PER_GROUP * HEAD_DIM
    pad = lambda a: jnp.pad(a, ((0, 0), (0, 0), (KV_PAD, 0), (0, 0)))
    ksel, vsel, kwin, vwin = pad(ksel), pad(vsel), pad(kwin), pad(vwin)
    kv = lambda w: pl.BlockSpec((1, 1, s_ + KV_PAD, w), lambda b, g, i: (b, g, 0, 0))
    cm = pl.BlockSpec((1, 1, ncp, HEAD_DIM), lambda b, g, i: (b, g, 0, 0))
    return pl.pallas_call(
        _attn_kernel,
        grid=(b_, N_KV_GROUPS, s_ // TQ),
        in_specs=[
            pl.BlockSpec((1, TQ, gw), lambda b, g, i: (b, i, g)),
            pl.BlockSpec((1, TQ, LANES), lambda b, g, i: (b, i, g)),
            pl.BlockSpec((1, HEAD_DIM), lambda b, g, i: (0, 0)),
            cm, cm, kv(HEAD_DIM + SEL_FEATS), kv(V_WIDTH), kv(HEAD_DIM), kv(V_WIDTH),
            pl.BlockSpec((1, rows, WINDOW + TQ), lambda b, g, i: (g, 0, 0)),
            pl.BlockSpec((1, rows, ncp), lambda b, g, i: (g, 0, 0)),
            pl.BlockSpec((SEL_FEATS, ncp), lambda b, g, i: (0, 0)),
        ],
        out_specs=pl.BlockSpec((1, TQ, gw), lambda b, g, i: (b, i, g)),
        out_shape=jax.ShapeDtypeStruct((b_, s_, N_KV_GROUPS * gw), F32),
        scratch_shapes=[pltpu.VMEM((rows, LANES), F32), pltpu.VMEM((rows, V_WIDTH), F32)],
        compiler_params=pltpu.CompilerParams(
            dimension_semantics=("parallel", "parallel", "arbitrary"), vmem_limit_bytes=VMEM_LIMIT),
    )(z_q, z_gate, q_norm_g.reshape(1, HEAD_DIM), kc, vc, ksel, vsel, kwin, vwin, wa, bc, smap_t)


def _pool_tile(halo, tile, i, pw_ref, ps_ref):
    ts = tile.shape[0]
    ext = jnp.concatenate([jnp.where(i > 0, halo, 0.0), tile], axis=0)
    t = i * ts + lax.broadcasted_iota(jnp.int32, (ts, 1), 0)
    outs = []
    for gi, w in enumerate(POOL_WINDOWS):
        sl = slice(gi * POOL_GROUP_DIM, (gi + 1) * POOL_GROUP_DIM)
        run = ext[:, sl]
        span = 1
        while span < w:
            run = run + pltpu.roll(run, span, axis=0)
            span *= 2
        cnt = jnp.minimum(t + 1, w).astype(F32)
        pooled = run[POOL_HALO:] / cnt - tile[:, sl]
        outs.append(jnp.dot(pooled.astype(BF16), pw_ref[gi], preferred_element_type=F32))
    return jnp.concatenate(outs, axis=1) * ps_ref[...]


def _merge_kernel(x_ref, zp_ref, zh_ref, ya_ref, zm_ref, g1_ref, pw_ref, ps_ref, wbp_ref, wba_ref, wo_ref, o_ref):
    d_ = x_ref.shape[-1]
    y_pool = _pool_tile(zh_ref[0], zp_ref[0], pl.program_id(1), pw_ref, ps_ref)
    bp = jnp.dot(y_pool.astype(BF16), wbp_ref[...], preferred_element_type=F32)
    ba = jnp.dot(ya_ref[0].astype(BF16), wba_ref[...], preferred_element_type=F32)
    zm = zm_ref[0]
    mixed = jax.nn.sigmoid(zm[:, :d_]) * bp + jax.nn.sigmoid(zm[:, d_:]) * ba
    proj = jnp.dot(mixed.astype(BF16), wo_ref[...], preferred_element_type=F32)
    o_ref[0] = x_ref[0] + g1_ref[0] * proj


def _merge(x, z_pool, y_attn, z_merge, gate1, pool_w, pool_scale, wbp, wba, wo):
    b_, s_, d_ = x.shape
    row = lambda w: pl.BlockSpec((1, ROW_TILE, w), lambda b, i: (b, i, 0))
    full = lambda a: pl.BlockSpec(a.shape, lambda b, i: (0,) * a.ndim)
    halo_blocks = ROW_TILE // POOL_HALO
    halo = pl.BlockSpec((1, POOL_HALO, POOL_WIDTH), lambda b, i: (b, jnp.maximum(i * halo_blocks - 1, 0), 0))
    pool_scale = pool_scale.reshape(1, POOL_WIDTH)
    return pl.pallas_call(
        _merge_kernel,
        grid=(b_, s_ // ROW_TILE),
        in_specs=[row(d_), row(POOL_WIDTH), halo, row(Q_WIDTH), row(2 * d_),
                  pl.BlockSpec((1, 1, d_), lambda b, i: (b, 0, 0)),
                  full(pool_w), full(pool_scale), full(wbp), full(wba), full(wo)],
        out_specs=row(d_),
        out_shape=jax.ShapeDtypeStruct((b_, s_, d_), F32),
        compiler_params=pltpu.CompilerParams(
            dimension_semantics=("parallel", "parallel"), vmem_limit_bytes=VMEM_LIMIT),
    )(x, z_pool, z_pool, y_attn, z_merge, gate1, pool_w, pool_scale, wbp, wba, wo)


def _ada_kernel(c_ref, w_ref, b_ref, o_ref):
    c = c_ref[...]
    o_ref[...] = jnp.dot(jax.nn.silu(c).astype(BF16), w_ref[...].astype(BF16),
                         preferred_element_type=F32) + b_ref[...]


def _ada(c, w, b):
    b_, d_ = c.shape
    n = w.shape[1]
    return pl.pallas_call(
        _ada_kernel,
        grid=(n // d_,),
        in_specs=[pl.BlockSpec((b_, d_), lambda j: (0, 0)),
                  pl.BlockSpec((d_, d_), lambda j: (0, j)),
                  pl.BlockSpec((1, d_), lambda j: (0, j))],
        out_specs=pl.BlockSpec((b_, d_), lambda j: (0, j)),
        out_shape=jax.ShapeDtypeStruct((b_, n), F32),
        compiler_params=pltpu.CompilerParams(dimension_semantics=("parallel",), vmem_limit_bytes=VMEM_LIMIT),
    )(c, w, b.reshape(1, n))


_CAND_ROWS = PEER_TOPK + 7 * 8 + 8


def _topk_rows(x, k):
    n = x.shape[0]
    rid = lax.broadcasted_iota(jnp.int32, x.shape, 0)
    vals, idxs = [], []
    for _ in range(k):
        m = jnp.max(x, axis=0, keepdims=True)
        idx = jnp.min(jnp.where(x == m, rid, n), axis=0, keepdims=True)
        vals.append(m)
        idxs.append(idx)
        x = jnp.where(rid == idx, -jnp.inf, x)
    return jnp.concatenate(vals, axis=0), jnp.concatenate(idxs, axis=0)


def _pair_grid(r0, r1, combine):
    parts = [combine(r0[0:1], r1)]
    parts += [combine(r0[a:a + 1], r1[0:8]) for a in range(1, 8)]
    parts.append(combine(r0[8:16], r1[0:1]))
    return jnp.concatenate(parts, axis=0)


def _peer_route_kernel(x_ref, g_ref, sc_ref, sh_ref, w_ref, keys_ref, h_ref, eid_ref, gw_ref):
    x = x_ref[0]
    y = x * lax.rsqrt(jnp.mean(x * x, axis=-1, keepdims=True) + EPS)
    h = y * g_ref[...] * (1.0 + sc_ref[0]) + sh_ref[0]
    h_ref[0] = h
    qv = jnp.dot(h.astype(BF16), w_ref[...], preferred_element_type=F32).astype(BF16)
    half = PEER_QDIM // 2
    eids, gws = [], []
    for hd in range(PEER_HEADS):
        tops = []
        for p in range(2):
            c = hd * 2 + p
            sc = lax.dot_general(keys_ref[c], qv[:, c * half:(c + 1) * half], _NT,
                                 preferred_element_type=F32)
            tops.append(_topk_rows(sc, PEER_TOPK))
        (s0, i0), (s1, i1) = tops
        cand = _pair_grid(s0, s1, lambda a, b: a + b)
        cid = _pair_grid(i0, i1, lambda a, b: a * (PEER_KEYS * EXPERT_ROWS) + (b * EXPERT_ROWS + EXPERT_ROWS))
        rid = lax.broadcasted_iota(jnp.int32, cand.shape, 0)
        top_s, top_e = [], []
        for _ in range(PEER_TOPK):
            m = jnp.max(cand, axis=0, keepdims=True)
            idx = jnp.min(jnp.where(cand == m, rid, _CAND_ROWS), axis=0, keepdims=True)
            hit = rid == idx
            top_s.append(m)
            top_e.append(jnp.sum(jnp.where(hit, cid, 0), axis=0, keepdims=True))
            cand = jnp.where(hit, -jnp.inf, cand)
        top_s = jnp.concatenate(top_s, axis=0)
        e = jnp.exp(top_s - top_s[0:1])
        gws.append(e / jnp.sum(e, axis=0, keepdims=True))
        eids.append(jnp.concatenate(top_e, axis=0))
    eids = jnp.concatenate(eids, axis=0)
    for c in range(PEER_TILE // PEER_EXPERT_TILE):
        eid_ref[0, c] = eids[:, c * PEER_EXPERT_TILE:(c + 1) * PEER_EXPERT_TILE]
    gw_ref[0] = jnp.transpose(jnp.concatenate(gws, axis=0))


def _peer_route(x, g, scale, shift, w_bf16, keys_bf16):
    b_, s_, d_ = x.shape
    n = w_bf16.shape[1]
    nk = PEER_HEADS * PEER_TOPK
    per_tile = PEER_TILE // PEER_EXPERT_TILE
    row = lambda w: pl.BlockSpec((1, PEER_TILE, w), lambda b, i: (b, i, 0))
    return pl.pallas_call(
        _peer_route_kernel,
        grid=(b_, s_ // PEER_TILE),
        in_specs=[
            row(d_),
            pl.BlockSpec((1, d_), lambda b, i: (0, 0)),
            pl.BlockSpec((1, 1, d_), lambda b, i: (b, 0, 0)),
            pl.BlockSpec((1, 1, d_), lambda b, i: (b, 0, 0)),
            pl.BlockSpec((d_, n), lambda b, i: (0, 0)),
            pl.BlockSpec(keys_bf16.shape, lambda b, i: (0, 0, 0)),
        ],
        out_specs=[row(d_),
                   pl.BlockSpec((1, per_tile, nk, PEER_EXPERT_TILE), lambda b, i: (b, i, 0, 0)),
                   row(nk)],
        out_shape=[
            jax.ShapeDtypeStruct((b_, s_, d_), F32),
            jax.ShapeDtypeStruct((b_, s_ // PEER_EXPERT_TILE, nk, PEER_EXPERT_TILE), jnp.int32),
            jax.ShapeDtypeStruct((b_, s_, nk), F32),
        ],
        compiler_params=pltpu.CompilerParams(
            dimension_semantics=("parallel", "parallel"), vmem_limit_bytes=VMEM_LIMIT),
    )(x, g.reshape(1, d_), scale, shift, w_bf16, keys_bf16)


_HI_MASK = -65536


def _pack_table(t):
    e_, d_ = t.shape
    bits = lax.bitcast_convert_type(t.astype(jnp.bfloat16), jnp.uint16).astype(jnp.uint32)
    words = (bits[:, d_ // 2:] << 16) | bits[:, :d_ // 2]
    rows = lax.bitcast_convert_type(words, jnp.int32).reshape(e_ * EXPERT_ROWS, LANES)
    return jnp.pad(rows, ((EXPERT_ROWS, EXPERT_ROWS), (0, 0)))


def _expert_words(tab_vmem, row):
    return tab_vmem[pl.ds(pl.multiple_of(row, EXPERT_ROWS), EXPERT_ROWS), :]


def _expert_pair_words(tab_vmem, row_a, row_b, sub):
    top = tab_vmem[pl.ds(pl.multiple_of(row_a, EXPERT_ROWS), 2 * EXPERT_ROWS), :]
    bot = tab_vmem[pl.ds(pl.multiple_of(row_b - EXPERT_ROWS, EXPERT_ROWS), 2 * EXPERT_ROWS), :]
    return jnp.where(sub < EXPERT_ROWS, top, bot)


def _unpack(words):
    lo = lax.bitcast_convert_type(lax.shift_left(words, 16), F32)
    hi = lax.bitcast_convert_type(words & _HI_MASK, F32)
    return lo, hi


def _load_resident(j, tab_hbm, tab_vmem, eid_ref, idx_smem, sem):
    @pl.when(j == 0)
    def _():
        cp = pltpu.make_async_copy(tab_hbm, tab_vmem, sem)
        cp.start()
        cp.wait()

    copies = [pltpu.make_async_copy(eid_ref.at[0, k], idx_smem[k], sem) for k in range(len(idx_smem))]
    for cp in copies:
        cp.start()
    for cp in copies:
        cp.wait()


def _fold_pairs(q, sub):
    m2, m1 = (sub & 2) == 0, (sub & 1) == 0
    u = [jnp.where(m2, q[i], q[i + 2]) + jnp.where(m2, pltpu.roll(q[i], 6, axis=0), pltpu.roll(q[i + 2], 2, axis=0))
         for i in range(2)]
    return jnp.where(m1, u[0], u[1]) + jnp.where(m1, pltpu.roll(u[0], 7, axis=0), pltpu.roll(u[1], 1, axis=0))


def _peer_u_kernel(eid_ref, gw_ref, h_ref, tab_hbm, w_ref, *scratch):
    j = pl.program_id(0)
    nk = PEER_HEADS * PEER_TOPK
    idx_smem, (tab_vmem, acc_ref, sem) = scratch[:nk], scratch[nk:]
    half = D_MODEL // 2 // LANES
    _load_resident(j, tab_hbm, tab_vmem, eid_ref, idx_smem, sem)
    sub = lax.broadcasted_iota(jnp.int32, (2 * half, LANES), 0)
    lane = lax.broadcasted_iota(jnp.int32, (nk, PEER_EXPERT_TILE), 1)
    acc_ref[...] = jnp.zeros((nk, PEER_EXPERT_TILE), F32)

    def token(t):
        row = h_ref[pl.ds(t, 1), :]

        def seg(s):
            return jnp.broadcast_to(row[:, s * LANES:(s + 1) * LANES], (2 * half, LANES))

        def rows_of(first):
            out = seg(first + half - 1)
            for s in range(half - 2, -1, -1):
                out = jnp.where(sub % half == s, seg(first + s), out)
            return out

        x_lo, x_hi = rows_of(0), rows_of(half)
        folded = []
        for gi in range(nk // 8):
            q = []
            for i in range(4):
                k = gi * 8 + i
                words = _expert_pair_words(tab_vmem, idx_smem[k][t], idx_smem[k + 4][t], sub)
                lo, hi = _unpack(words)
                q.append(lo * x_lo + hi * x_hi)
            folded.append(_fold_pairs(q, sub))
        return jnp.sum(jnp.concatenate(folded, axis=0), axis=-1, keepdims=True)

    def body(tb, carry):
        acc = acc_ref[...]
        for u in range(PEER_TOKEN_UNROLL):
            t = tb * PEER_TOKEN_UNROLL + u
            acc = jnp.where(lane == t, token(t), acc)
        acc_ref[...] = acc
        return carry

    lax.fori_loop(0, PEER_EXPERT_TILE // PEER_TOKEN_UNROLL, body, 0)
    w_ref[...] = jax.nn.gelu(jnp.transpose(acc_ref[...])) * gw_ref[...]


def _peer_v_kernel(eid_ref, w_ref, x_ref, g2_ref, tab_hbm, o_ref, *scratch):
    j = pl.program_id(0)
    nk = PEER_HEADS * PEER_TOPK
    idx_smem, (wb_ref, tab_vmem, sem) = scratch[:nk], scratch[nk:]
    half = D_MODEL // 2 // LANES
    _load_resident(j, tab_hbm, tab_vmem, eid_ref, idx_smem, sem)
    n_acc = 4
    sub = lax.broadcasted_iota(jnp.int32, (2 * half, LANES), 0)

    def token(t, slot):
        wb_ref[slot] = jnp.transpose(jnp.broadcast_to(w_ref[pl.ds(t, 1), :], (nk, nk)))
        lo_acc = [jnp.zeros((2 * half, LANES), F32) for _ in range(n_acc)]
        hi_acc = [jnp.zeros((2 * half, LANES), F32) for _ in range(n_acc)]
        for p in range(nk // 2):
            k = 2 * p
            lo, hi = _unpack(_expert_pair_words(tab_vmem, idx_smem[k][t], idx_smem[k + 1][t], sub))
            wk = jnp.where(sub < half, wb_ref[slot, k:k + 1, :], wb_ref[slot, k + 1:k + 2, :])
            lo_acc[p % n_acc] = lo_acc[p % n_acc] + wk * lo
            hi_acc[p % n_acc] = hi_acc[p % n_acc] + wk * hi
        lo_sum, hi_sum = sum(lo_acc[1:], lo_acc[0]), sum(hi_acc[1:], hi_acc[0])
        ff = jnp.concatenate([lo_sum[:half] + lo_sum[half:], hi_sum[:half] + hi_sum[half:]], axis=0)
        ff_row = jnp.concatenate([ff[s:s + 1, :] for s in range(2 * half)], axis=1)
        o_ref[pl.ds(t, 1), :] = x_ref[pl.ds(t, 1), :] + g2_ref[0] * ff_row

    def body(tb, carry):
        for u in range(PEER_TOKEN_UNROLL):
            token(tb * PEER_TOKEN_UNROLL + u, u)
        return carry

    lax.fori_loop(0, PEER_EXPERT_TILE // PEER_TOKEN_UNROLL, body, 0)


def _peer_experts(eid, gw, h2, x1, gate2, u_tab, v_tab):
    b_, s_, d_ = x1.shape
    n_tok = b_ * s_
    nk = PEER_HEADS * PEER_TOPK
    tt = PEER_EXPERT_TILE
    nsteps = n_tok // tt
    steps_per_batch = s_ // tt
    tok = pl.BlockSpec((tt, d_), lambda j: (j, 0))
    sel = pl.BlockSpec((tt, nk), lambda j: (j, 0))
    slots = pl.BlockSpec((1, nk, tt), lambda j: (j, 0, 0))
    idx_scratch = [pltpu.SMEM((tt,), jnp.int32) for _ in range(nk)]
    params = pltpu.CompilerParams(dimension_semantics=("arbitrary",), vmem_limit_bytes=PEER_VMEM_LIMIT)
    eid3 = eid.reshape(nsteps, nk, tt)
    w = pl.pallas_call(
        _peer_u_kernel,
        grid=(nsteps,),
        in_specs=[slots, sel, tok, pl.BlockSpec(memory_space=pl.ANY)],
        out_specs=sel,
        out_shape=jax.ShapeDtypeStruct((n_tok, nk), F32),
        scratch_shapes=idx_scratch + [pltpu.VMEM(u_tab.shape, jnp.int32), pltpu.VMEM((nk, tt), F32),
                                      pltpu.SemaphoreType.DMA],
        compiler_params=params,
    )(eid3, gw.reshape(n_tok, nk), h2.reshape(n_tok, d_), u_tab)
    out = pl.pallas_call(
        _peer_v_kernel,
        grid=(nsteps,),
        in_specs=[slots, sel, tok, pl.BlockSpec((1, 1, d_), lambda j: (j // steps_per_batch, 0, 0)),
                  pl.BlockSpec(memory_space=pl.ANY)],
        out_specs=tok,
        out_shape=jax.ShapeDtypeStruct((n_tok, d_), F32),
        scratch_shapes=idx_scratch + [pltpu.VMEM((PEER_TOKEN_UNROLL, nk, LANES), F32),
                                      pltpu.VMEM(v_tab.shape, jnp.int32), pltpu.SemaphoreType.DMA],
        compiler_params=params,
    )(eid3, w, x1.reshape(n_tok, d_), gate2, v_tab)
    return out.reshape(b_, s_, d_)


def _permute_w_in(w):
    d_ = w.shape[0]
    kv_end = POOL_WIDTH + Q_WIDTH + 6 * KV_WIDTH
    per_group = HEADS_PER_GROUP * 3
    parts = [w[:, :kv_end], w[:, kv_end + GATE_COLS:]]
    for g in range(N_KV_GROUPS):
        parts.append(w[:, kv_end + g * per_group:kv_end + (g + 1) * per_group])
        parts.append(jnp.zeros((d_, LANES - per_group), w.dtype))
    return jnp.concatenate(parts, axis=1).astype(BF16)


def kernel(x, c, rel_bias, ada_w, ada_b, norm1_g, norm2_g, w_in, pool_w, pool_scale, cmp_pe_k, cmp_w1_k, cmp_w2_k, cmp_pe_v, cmp_w1_v, cmp_w2_v, q_norm_g, k_norm_g, w_branch_pool, w_branch_attn, w_out, peer_w_q, peer_sub_keys, peer_u, peer_v):
    b_, s_, d_ = x.shape
    l = 0
    ada = _ada(c, ada_w[l], ada_b[l]).reshape(b_, 6, 1, d_)
    shift1, scale1, gate1 = ada[:, 0], ada[:, 1], ada[:, 2]
    shift2, scale2, gate2 = ada[:, 3], ada[:, 4], ada[:, 5]

    (z_pool, z_q, ksel, vsel, kwin, vwin, kc_raw, vc_raw, z_merge, z_gate) = _in_proj(
        x, norm1_g[l], scale1, shift1, _permute_w_in(w_in[l]), k_norm_g[l])

    kc = _compress(kc_raw, cmp_pe_k[l], cmp_w1_k[l], cmp_w2_k[l], k_norm_g[l, 0], True)
    vc = _compress(vc_raw, cmp_pe_v[l], cmp_w1_v[l], cmp_w2_v[l], k_norm_g[l, 0], False)
    y_attn = _attention(z_q, z_gate, q_norm_g[l], kc, vc, ksel, vsel, kwin, vwin, rel_bias)

    x1 = _merge(x, z_pool, y_attn, z_merge, gate1, pool_w[l].astype(BF16), pool_scale[l],
                w_branch_pool[l].astype(BF16), w_branch_attn[l].astype(BF16), w_out[l].astype(BF16))

    keys = peer_sub_keys[l].reshape(PEER_HEADS * 2, PEER_KEYS, PEER_QDIM // 2).astype(BF16)
    h2, eid, gw = _peer_route(x1, norm2_g[l], scale2, shift2, peer_w_q[l].astype(BF16), keys)
    return _peer_experts(eid, gw, h2, x1, gate2, _pack_table(peer_u[l]), _pack_table(peer_v[l]))
```

```python
import math
from functools import partial

import jax
import jax.numpy as jnp
import numpy as np
from jax import lax
from jax.experimental import pallas as pl
from jax.experimental.pallas import tpu as pltpu

D_MODEL = 1024
POOL_WIDTH = 512
POOL_WINDOWS = (2, 4, 8, 16)
POOL_GROUP_DIM = 128
N_HEADS = 8
N_KV_GROUPS = 2
HEADS_PER_GROUP = 4
HEAD_DIM = 64
Q_WIDTH = 512
KV_WIDTH = 128
CMP_BLOCK = 32
CMP_STRIDE = 16
CMP_HIDDEN = 128
SEL_BLOCK = 64
SEL_TOPK = 16
WINDOW = 512
FORCE_SCORE = 1000.0
N_BUCKETS = 32
MAX_DISTANCE = 128
PEER_HEADS = 8
PEER_KEYS = 128
PEER_TOPK = 16
PEER_QDIM = 256
PEER_CHUNK = 128
GATE_COLS = N_HEADS * 3
EPS = 1e-6
NEG_INF = -1e30

LANES = 128
VMEM_LIMIT = 48 * 1024 * 1024
ROW_TILE = 512
POOL_HALO = 16
TQ = 128
SEL_FEATS = 64
SEL_CHUNK = 512
V_WIDTH = 128
KV_PAD = 512
CMP_BAND_LO = 10
PEER_TILE = 256
PEER_EXPERT_TILE = 256
PEER_TOKEN_UNROLL = 8
PEER_V_UNROLL = 8
EXPERT_ROWS = D_MODEL // 2 // LANES
PEER_VMEM_LIMIT = 56 * 1024 * 1024

BF16 = jnp.bfloat16
F32 = jnp.float32
_NT = (((1,), (1,)), ((), ()))


def _rms_rows(x, g):
    return x * lax.rsqrt(jnp.mean(x * x, axis=-1, keepdims=True) + EPS) * g


def _in_proj_kernel(x_ref, g_ref, sc_ref, sh_ref, w_ref, kg_ref,
                    zpool_ref, zq_ref, ksel_ref, vsel_ref, kwin_ref, vwin_ref, kcr_ref, vcr_ref,
                    zmerge_ref, zgate_ref):
    i = pl.program_id(1)
    x = x_ref[0]
    y = x * lax.rsqrt(jnp.mean(x * x, axis=-1, keepdims=True) + EPS)
    h = y * g_ref[...] * (1.0 + sc_ref[0]) + sh_ref[0]
    z = jnp.dot(h.astype(BF16), w_ref[...], preferred_element_type=F32)
    ts = x.shape[0]
    zpool_ref[0] = z[:, :POOL_WIDTH]
    o = POOL_WIDTH
    zq_ref[0] = z[:, o:o + Q_WIDTH]
    o += Q_WIDTH
    pos = i * ts + lax.broadcasted_iota(jnp.int32, (ts, SEL_FEATS), 0)
    onehot = (pos // SEL_BLOCK == lax.broadcasted_iota(jnp.int32, (ts, SEL_FEATS), 1)).astype(BF16)
    one_col = (lax.broadcasted_iota(jnp.int32, (ts, V_WIDTH - HEAD_DIM), 1) == 0).astype(BF16)
    for g in range(N_KV_GROUPS):
        def col(k):
            return z[:, o + k * KV_WIDTH + g * HEAD_DIM:o + k * KV_WIDTH + (g + 1) * HEAD_DIM]
        kcr_ref[0, g] = col(0)
        vcr_ref[0, g] = col(1)
        ks = _rms_rows(col(2), kg_ref[1:2, :]).astype(BF16)
        ksel_ref[0, g] = jnp.concatenate([ks, onehot], axis=1)
        vsel_ref[0, g] = jnp.concatenate([col(3).astype(BF16), one_col], axis=1)
        kwin_ref[0, g] = _rms_rows(col(4), kg_ref[2:3, :]).astype(BF16)
        vwin_ref[0, g] = jnp.concatenate([col(5).astype(BF16), one_col], axis=1)
    o += 6 * KV_WIDTH
    zmerge_ref[0] = z[:, o:o + 2 * D_MODEL]
    o += 2 * D_MODEL
    zgate_ref[0] = jax.nn.sigmoid(z[:, o:o + N_KV_GROUPS * LANES])


def _in_proj(x, g, scale, shift, w_bf16, k_norm_g):
    b_, s_, d_ = x.shape
    n = w_bf16.shape[1]
    row = lambda w: pl.BlockSpec((1, ROW_TILE, w), lambda b, i: (b, i, 0))
    grp = lambda w: pl.BlockSpec((1, N_KV_GROUPS, ROW_TILE, w), lambda b, i: (b, 0, i, 0))
    gshape = lambda w, dt: jax.ShapeDtypeStruct((b_, N_KV_GROUPS, s_, w), dt)
    return pl.pallas_call(
        _in_proj_kernel,
        grid=(b_, s_ // ROW_TILE),
        in_specs=[
            row(d_),
            pl.BlockSpec((1, d_), lambda b, i: (0, 0)),
            pl.BlockSpec((1, 1, d_), lambda b, i: (b, 0, 0)),
            pl.BlockSpec((1, 1, d_), lambda b, i: (b, 0, 0)),
            pl.BlockSpec((d_, n), lambda b, i: (0, 0)),
            pl.BlockSpec((3, HEAD_DIM), lambda b, i: (0, 0)),
        ],
        out_specs=[row(POOL_WIDTH), row(Q_WIDTH), grp(HEAD_DIM + SEL_FEATS), grp(V_WIDTH), grp(HEAD_DIM),
                   grp(V_WIDTH), grp(HEAD_DIM), grp(HEAD_DIM), row(2 * d_), row(N_KV_GROUPS * LANES)],
        out_shape=[
            jax.ShapeDtypeStruct((b_, s_, POOL_WIDTH), F32),
            jax.ShapeDtypeStruct((b_, s_, Q_WIDTH), F32),
            gshape(HEAD_DIM + SEL_FEATS, BF16), gshape(V_WIDTH, BF16), gshape(HEAD_DIM, BF16), gshape(V_WIDTH, BF16),
            gshape(HEAD_DIM, F32), gshape(HEAD_DIM, F32),
            jax.ShapeDtypeStruct((b_, s_, 2 * d_), F32),
            jax.ShapeDtypeStruct((b_, s_, N_KV_GROUPS * LANES), F32),
        ],
        compiler_params=pltpu.CompilerParams(
            dimension_semantics=("parallel", "parallel"), vmem_limit_bytes=VMEM_LIMIT),
    )(x, g.reshape(1, d_), scale, shift, w_bf16, k_norm_g)


def _compress_kernel(t_ref, pe_ref, w1_ref, w2_ref, g_ref, o_ref, *, normalize):
    half = CMP_STRIDE * HEAD_DIM
    t = t_ref[0, 0].astype(BF16)
    w1 = w1_ref[...]
    a = jnp.dot(t, w1[:half], preferred_element_type=F32)
    b = jnp.dot(t, w1[half:], preferred_element_type=F32)
    pe = jnp.dot(pe_ref[...], w1, preferred_element_type=F32)
    n = a.shape[0]
    b_next = pltpu.roll(b, n - 1, axis=0)
    hid = jax.nn.gelu(a + b_next + pe)
    out = jnp.dot(hid.astype(BF16), w2_ref[...], preferred_element_type=F32)
    if normalize:
        out = _rms_rows(out, g_ref[...])
    o_ref[0, 0] = out.astype(BF16)


def _compress(t_raw, pe, w1, w2, g, normalize):
    b_, g_, s_, dh = t_raw.shape
    n_str = s_ // CMP_STRIDE
    t = t_raw.reshape(b_, g_, n_str, CMP_STRIDE * dh)
    return pl.pallas_call(
        partial(_compress_kernel, normalize=normalize),
        grid=(b_, g_),
        in_specs=[
            pl.BlockSpec((1, 1, n_str, CMP_STRIDE * dh), lambda b, g: (b, g, 0, 0)),
            pl.BlockSpec((1, CMP_BLOCK * dh), lambda b, g: (0, 0)),
            pl.BlockSpec((CMP_BLOCK * dh, CMP_HIDDEN), lambda b, g: (0, 0)),
            pl.BlockSpec((CMP_HIDDEN, dh), lambda b, g: (0, 0)),
            pl.BlockSpec((1, dh), lambda b, g: (0, 0)),
        ],
        out_specs=pl.BlockSpec((1, 1, n_str, dh), lambda b, g: (b, g, 0, 0)),
        out_shape=jax.ShapeDtypeStruct((b_, g_, n_str, dh), BF16),
        compiler_params=pltpu.CompilerParams(
            dimension_semantics=("parallel", "parallel"), vmem_limit_bytes=VMEM_LIMIT),
    )(t, pe.reshape(1, CMP_BLOCK * dh).astype(BF16), w1.astype(BF16), w2.astype(BF16), g.reshape(1, dh))


def _t5_bucket_np(rel):
    n = np.maximum(rel, 0)
    max_exact = N_BUCKETS // 2
    nf = np.maximum(n, 1).astype(np.float32)
    large = max_exact + (np.log(nf / max_exact) / math.log(MAX_DISTANCE / max_exact) * (N_BUCKETS - max_exact)).astype(np.int32)
    large = np.minimum(large, N_BUCKETS - 1)
    return np.where(n < max_exact, n, large)


def _bias_tiles(rel_bias, n_cmp_pad):
    far = rel_bias[N_BUCKETS - 1]
    ii = np.arange(TQ)[:, None]

    def lookup(dist):
        near = (dist >= 0) & (dist < MAX_DISTANCE)
        bucket = _t5_bucket_np(np.where(near, dist, MAX_DISTANCE))
        onehot = (jnp.asarray(bucket.astype(np.int8))[..., None] == jnp.arange(N_BUCKETS, dtype=jnp.int8)).astype(F32)
        b = jnp.dot(onehot, rel_bias - far, precision=lax.Precision.HIGHEST)
        b = jnp.transpose(b, (2, 0, 1))
        return b.reshape(N_KV_GROUPS, HEADS_PER_GROUP * dist.shape[0], dist.shape[1])

    jj = np.arange(TQ)[None, :]
    bt = jnp.stack([lookup(ii - jj), lookup(TQ + ii - jj)], axis=1)
    cc = np.arange(n_cmp_pad)[None, :]
    dist_c = ii - CMP_STRIDE * (cc - CMP_BAND_LO) - (CMP_BLOCK - 1)
    dist_c = np.where(cc <= CMP_BAND_LO + TQ // CMP_STRIDE, dist_c, -1)
    bc = lookup(dist_c)
    rows = HEADS_PER_GROUP * TQ
    i_r = np.tile(np.arange(TQ), HEADS_PER_GROUP)[:, None]
    first = jnp.asarray(np.where(jj > i_r, 0.0, NEG_INF), F32)
    last = jnp.where(jnp.asarray(jj <= i_r), bt[:, 0], NEG_INF)
    zeros = jnp.zeros((N_KV_GROUPS, rows, WINDOW - 2 * TQ), F32)
    wa = jnp.concatenate([jnp.broadcast_to(first, (N_KV_GROUPS, rows, TQ)), zeros, bt[:, 1], last], axis=-1)
    return wa, bc


def _sel_mapping_t(n_cmp_pad, n_cmp):
    m = np.zeros((SEL_FEATS, n_cmp_pad), np.float32)
    pos = np.arange(n_cmp)[:, None] * CMP_STRIDE + np.arange(CMP_BLOCK)[None, :]
    np.add.at(m, ((pos // SEL_BLOCK).ravel(), np.repeat(np.arange(n_cmp), CMP_BLOCK)), 1.0 / CMP_BLOCK)
    return m


def _logits(q, k, add=None, lo_cols=None):
    s = lax.dot_general(q, k, _NT, preferred_element_type=F32)
    if add is not None:
        s = s + add
    if lo_cols is not None:
        s = jnp.where(lax.broadcasted_iota(jnp.int32, s.shape, 1) < lo_cols, NEG_INF, s)
    return s


def _flash_step(s, v, m_sc, acc_sc):
    m_prev = m_sc[...]
    m_new = jnp.maximum(m_prev, jnp.max(s, axis=-1, keepdims=True))
    alpha = jnp.exp(m_prev - m_new)
    p = jnp.exp(s - jnp.concatenate([m_new] * (s.shape[1] // LANES), axis=1))
    acc_sc[...] = alpha * acc_sc[...] + jnp.dot(p.astype(BF16), v, preferred_element_type=F32)
    m_sc[...] = m_new


def _attn_kernel(zq_ref, gate_ref, qg_ref, kc_ref, vc_ref, ksel_ref, vsel_ref, kwin_ref, vwin_ref,
                 wa_ref, bc_ref, smap_ref, o_ref, m_sc, acc_sc):
    i = pl.program_id(2)
    rows = HEADS_PER_GROUP * TQ
    ncp = kc_ref.shape[2]
    scale = HEAD_DIM ** -0.5

    zq = zq_ref[0]
    qs = jnp.concatenate([zq[:, j * HEAD_DIM:(j + 1) * HEAD_DIM] for j in range(HEADS_PER_GROUP)], axis=0)
    qn = _rms_rows(qs, qg_ref[...]) * scale
    qb = qn.astype(BF16)

    lc = lax.dot_general(qb, kc_ref[0, 0], _NT, preferred_element_type=F32)
    lc = lc + pltpu.roll(bc_ref[0], (i * (TQ // CMP_STRIDE) + ncp - CMP_BAND_LO) % ncp, axis=1)
    t_c = i * TQ + lax.broadcasted_iota(jnp.int32, (rows, ncp), 0) % TQ
    n_c = lax.broadcasted_iota(jnp.int32, (rows, ncp), 1)
    valid_c = n_c * CMP_STRIDE + (CMP_BLOCK - 1) <= t_c
    lm = jnp.where(valid_c, lc, NEG_INF)
    e = jnp.where(valid_c, jnp.exp(lm - jnp.max(lm, axis=-1, keepdims=True)), 0.0)
    den = jnp.sum(e, axis=-1, keepdims=True)
    p_c = e / jnp.where(den > 0.0, den, 1.0)
    o_cmp = jnp.dot(p_c.astype(BF16), vc_ref[0, 0], preferred_element_type=F32)

    p_sum = p_c[0:TQ] + p_c[TQ:2 * TQ] + p_c[2 * TQ:3 * TQ] + p_c[3 * TQ:4 * TQ]
    p_hi = p_sum.astype(BF16)
    r1 = p_sum - p_hi.astype(F32)
    p_mid = r1.astype(BF16)
    p_lo = (r1 - p_mid.astype(F32)).astype(BF16)
    smap = smap_ref[...]
    imp = (lax.dot_general(smap, p_hi, _NT, preferred_element_type=F32)
           + lax.dot_general(smap, p_mid, _NT, preferred_element_type=F32)
           + lax.dot_general(smap, p_lo, _NT, preferred_element_type=F32))
    blk = lax.broadcasted_iota(jnp.int32, (SEL_FEATS, TQ), 0)
    t_s = i * TQ + lax.broadcasted_iota(jnp.int32, (SEL_FEATS, TQ), 1)
    cur = t_s // SEL_BLOCK
    forced = (blk == 0) | (blk == cur) | (blk == cur - 1)
    visible = blk * SEL_BLOCK <= t_s
    score = jnp.where(visible, imp + jnp.where(forced, FORCE_SCORE, 0.0), -1.0)
    rank = jnp.zeros((SEL_FEATS, TQ), jnp.int32)
    for sp in range(SEL_FEATS):
        row = score[sp:sp + 1, :]
        beats = (row > score) | ((row == score) & (blk > sp))
        rank = rank + beats.astype(jnp.int32)
    pen_t = jnp.where(rank < SEL_TOPK, 0.0, NEG_INF)
    pen = jnp.transpose(pen_t).astype(BF16)
    q_aug = jnp.concatenate([qb, jnp.concatenate([pen] * HEADS_PER_GROUP, axis=0)], axis=1)

    def keys(ref, pos, n):
        return ref[0, 0, pl.ds(pl.multiple_of(pos + KV_PAD, TQ), n), :]

    m_sc[...] = jnp.full((rows, LANES), NEG_INF, F32)
    acc_sc[...] = jnp.zeros((rows, V_WIDTH), F32)
    per_chunk = SEL_CHUNK // TQ
    n_plain = jnp.maximum(i - 1, 0)
    rem = n_plain % per_chunk

    @pl.when(rem > 0)
    def _():
        pos = (rem - per_chunk) * TQ
        s = _logits(q_aug, keys(ksel_ref, pos, SEL_CHUNK), lo_cols=-pos)
        _flash_step(s, keys(vsel_ref, pos, SEL_CHUNK), m_sc, acc_sc)

    def sel_body(c, carry):
        pos = (rem + c * per_chunk) * TQ
        _flash_step(_logits(q_aug, keys(ksel_ref, pos, SEL_CHUNK)), keys(vsel_ref, pos, SEL_CHUNK), m_sc, acc_sc)
        return carry

    lax.fori_loop(0, n_plain // per_chunk, sel_body, 0)
    pos = (i - 1) * TQ
    s = _logits(q_aug, keys(ksel_ref, pos, 2 * TQ), add=wa_ref[0, :, WINDOW - TQ:], lo_cols=-pos)
    _flash_step(s, keys(vsel_ref, pos, 2 * TQ), m_sc, acc_sc)
    acc = acc_sc[...]
    o_sel = acc[:, :HEAD_DIM] / acc[:, HEAD_DIM:HEAD_DIM + 1]

    pos = i * TQ - WINDOW
    s = _logits(qb, keys(kwin_ref, pos, WINDOW + TQ), add=wa_ref[0], lo_cols=-pos)
    p = jnp.exp(s - jnp.max(s, axis=-1, keepdims=True))
    acc = jnp.dot(p.astype(BF16), keys(vwin_ref, pos, WINDOW + TQ), preferred_element_type=F32)
    o_win = acc[:, :HEAD_DIM] / acc[:, HEAD_DIM:HEAD_DIM + 1]

    gate = gate_ref[0]
    outs = []
    for j in range(HEADS_PER_GROUP):
        sl = slice(j * TQ, (j + 1) * TQ)
        outs.append(gate[:, 3 * j:3 * j + 1] * o_cmp[sl] + gate[:, 3 * j + 1:3 * j + 2] * o_sel[sl]
                    + gate[:, 3 * j + 2:3 * j + 3] * o_win[sl])
    o_ref[0] = jnp.concatenate(outs, axis=1)


def _attention(z_q, z_gate, q_norm_g, kc, vc, ksel, vsel, kwin, vwin, rel_bias):
    b_, s_, _ = z_q.shape
    ncp = kc.shape[2]
    n_cmp = ncp - CMP_BLOCK // CMP_STRIDE + 1
    assert s_ // SEL_BLOCK <= SEL_FEATS and s_ % TQ == 0
    wa, bc = _bias_tiles(rel_bias, ncp)
    smap_t = jnp.asarray(_sel_mapping_t(ncp, n_cmp), BF16)
    rows = HEADS_PER_GROUP * TQ
    gw = HEADS_PER_GROUP * HEAD_DIM
    pad = lambda a: jnp.pad(a, ((0, 0), (0, 0), (KV_PAD, 0), (0, 0)))
    ksel, vsel, kwin, vwin = pad(ksel), pad(vsel), pad(kwin), pad(vwin)
    kv = lambda w: pl.BlockSpec((1, 1, s_ + KV_PAD, w), lambda b, g, i: (b, g, 0, 0))
    cm = pl.BlockSpec((1, 1, ncp, HEAD_DIM), lambda b, g, i: (b, g, 0, 0))
    return pl.pallas_call(
        _attn_kernel,
        grid=(b_, N_KV_GROUPS, s_ // TQ),
        in_specs=[
            pl.BlockSpec((1, TQ, gw), lambda b, g, i: (b, i, g)),
            pl.BlockSpec((1, TQ, LANES), lambda b, g, i: (b, i, g)),
            pl.BlockSpec((1, HEAD_DIM), lambda b, g, i: (0, 0)),
            cm, cm, kv(HEAD_DIM + SEL_FEATS), kv(V_WIDTH), kv(HEAD_DIM), kv(V_WIDTH),
            pl.BlockSpec((1, rows, WINDOW + TQ), lambda b, g, i: (g, 0, 0)),
            pl.BlockSpec((1, rows, ncp), lambda b, g, i: (g, 0, 0)),
            pl.BlockSpec((SEL_FEATS, ncp), lambda b, g, i: (0, 0)),
        ],
        out_specs=pl.BlockSpec((1, TQ, gw), lambda b, g, i: (b, i, g)),
        out_shape=jax.ShapeDtypeStruct((b_, s_, N_KV_GROUPS * gw), F32),
        scratch_shapes=[pltpu.VMEM((rows, LANES), F32), pltpu.VMEM((rows, V_WIDTH), F32)],
        compiler_params=pltpu.CompilerParams(
            dimension_semantics=("parallel", "parallel", "arbitrary"), vmem_limit_bytes=VMEM_LIMIT),
    )(z_q, z_gate, q_norm_g.reshape(1, HEAD_DIM), kc, vc, ksel, vsel, kwin, vwin, wa, bc, smap_t)


def _pool_tile(halo, tile, i, pw_ref, ps_ref):
    ts = tile.shape[0]
    ext = jnp.concatenate([jnp.where(i > 0, halo, 0.0), tile], axis=0)
    t = i * ts + lax.broadcasted_iota(jnp.int32, (ts, 1), 0)
    outs = []
    for gi, w in enumerate(POOL_WINDOWS):
        sl = slice(gi * POOL_GROUP_DIM, (gi + 1) * POOL_GROUP_DIM)
        run = ext[:, sl]
        span = 1
        while span < w:
            run = run + pltpu.roll(run, span, axis=0)
            span *= 2
        cnt = jnp.minimum(t + 1, w).astype(F32)
        pooled = run[POOL_HALO:] / cnt - tile[:, sl]
        outs.append(jnp.dot(pooled.astype(BF16), pw_ref[gi], preferred_element_type=F32))
    return jnp.concatenate(outs, axis=1) * ps_ref[...]


def _merge_kernel(x_ref, zp_ref, zh_ref, ya_ref, zm_ref, g1_ref, pw_ref, ps_ref, wbp_ref, wba_ref, wo_ref, o_ref):
    d_ = x_ref.shape[-1]
    y_pool = _pool_tile(zh_ref[0], zp_ref[0], pl.program_id(1), pw_ref, ps_ref)
    bp = jnp.dot(y_pool.astype(BF16), wbp_ref[...], preferred_element_type=F32)
    ba = jnp.dot(ya_ref[0].astype(BF16), wba_ref[...], preferred_element_type=F32)
    zm = zm_ref[0]
    mixed = jax.nn.sigmoid(zm[:, :d_]) * bp + jax.nn.sigmoid(zm[:, d_:]) * ba
    proj = jnp.dot(mixed.astype(BF16), wo_ref[...], preferred_element_type=F32)
    o_ref[0] = x_ref[0] + g1_ref[0] * proj


def _merge(x, z_pool, y_attn, z_merge, gate1, pool_w, pool_scale, wbp, wba, wo):
    b_, s_, d_ = x.shape
    row = lambda w: pl.BlockSpec((1, ROW_TILE, w), lambda b, i: (b, i, 0))
    full = lambda a: pl.BlockSpec(a.shape, lambda b, i: (0,) * a.ndim)
    halo_blocks = ROW_TILE // POOL_HALO
    halo = pl.BlockSpec((1, POOL_HALO, POOL_WIDTH), lambda b, i: (b, jnp.maximum(i * halo_blocks - 1, 0), 0))
    pool_scale = pool_scale.reshape(1, POOL_WIDTH)
    return pl.pallas_call(
        _merge_kernel,
        grid=(b_, s_ // ROW_TILE),
        in_specs=[row(d_), row(POOL_WIDTH), halo, row(Q_WIDTH), row(2 * d_),
                  pl.BlockSpec((1, 1, d_), lambda b, i: (b, 0, 0)),
                  full(pool_w), full(pool_scale), full(wbp), full(wba), full(wo)],
        out_specs=row(d_),
        out_shape=jax.ShapeDtypeStruct((b_, s_, d_), F32),
        compiler_params=pltpu.CompilerParams(
            dimension_semantics=("parallel", "parallel"), vmem_limit_bytes=VMEM_LIMIT),
    )(x, z_pool, z_pool, y_attn, z_merge, gate1, pool_w, pool_scale, wbp, wba, wo)


def _ada_kernel(c_ref, w_ref, b_ref, o_ref):
    c = c_ref[...]
    o_ref[...] = jnp.dot(jax.nn.silu(c).astype(BF16), w_ref[...].astype(BF16),
                         preferred_element_type=F32) + b_ref[...]


def _ada(c, w, b):
    b_, d_ = c.shape
    n = w.shape[1]
    return pl.pallas_call(
        _ada_kernel,
        grid=(n // d_,),
        in_specs=[pl.BlockSpec((b_, d_), lambda j: (0, 0)),
                  pl.BlockSpec((d_, d_), lambda j: (0, j)),
                  pl.BlockSpec((1, d_), lambda j: (0, j))],
        out_specs=pl.BlockSpec((b_, d_), lambda j: (0, j)),
        out_shape=jax.ShapeDtypeStruct((b_, n), F32),
        compiler_params=pltpu.CompilerParams(dimension_semantics=("parallel",), vmem_limit_bytes=VMEM_LIMIT),
    )(c, w, b.reshape(1, n))


_CAND_ROWS = PEER_TOPK + 7 * 8 + 8


def _topk_rows(x, k):
    n = x.shape[0]
    rid = lax.broadcasted_iota(jnp.int32, x.shape, 0)
    vals, idxs = [], []
    for _ in range(k):
        m = jnp.max(x, axis=0, keepdims=True)
        idx = jnp.min(jnp.where(x == m, rid, n), axis=0, keepdims=True)
        vals.append(m)
        idxs.append(idx)
        x = jnp.where(rid == idx, -jnp.inf, x)
    return jnp.concatenate(vals, axis=0), jnp.concatenate(idxs, axis=0)


def _pair_grid(r0, r1, combine):
    parts = [combine(r0[0:1], r1)]
    parts += [combine(r0[a:a + 1], r1[0:8]) for a in range(1, 8)]
    parts.append(combine(r0[8:16], r1[0:1]))
    return jnp.concatenate(parts, axis=0)


def _peer_route_kernel(x_ref, g_ref, sc_ref, sh_ref, w_ref, keys_ref, h_ref, eid_ref, gw_ref):
    x = x_ref[0]
    y = x * lax.rsqrt(jnp.mean(x * x, axis=-1, keepdims=True) + EPS)
    h = y * g_ref[...] * (1.0 + sc_ref[0]) + sh_ref[0]
    h_ref[0] = h
    qv = jnp.dot(h.astype(BF16), w_ref[...], preferred_element_type=F32).astype(BF16)
    half = PEER_QDIM // 2
    eids, gws = [], []
    for hd in range(PEER_HEADS):
        tops = []
        for p in range(2):
            c = hd * 2 + p
            sc = lax.dot_general(keys_ref[c], qv[:, c * half:(c + 1) * half], _NT,
                                 preferred_element_type=F32)
            tops.append(_topk_rows(sc, PEER_TOPK))
        (s0, i0), (s1, i1) = tops
        cand = _pair_grid(s0, s1, lambda a, b: a + b)
        cid = _pair_grid(i0, i1, lambda a, b: a * (PEER_KEYS * EXPERT_ROWS) + (b * EXPERT_ROWS + EXPERT_ROWS))
        rid = lax.broadcasted_iota(jnp.int32, cand.shape, 0)
        top_s, top_e = [], []
        for _ in range(PEER_TOPK):
            m = jnp.max(cand, axis=0, keepdims=True)
            idx = jnp.min(jnp.where(cand == m, rid, _CAND_ROWS), axis=0, keepdims=True)
            hit = rid == idx
            top_s.append(m)
            top_e.append(jnp.sum(jnp.where(hit, cid, 0), axis=0, keepdims=True))
            cand = jnp.where(hit, -jnp.inf, cand)
        top_s = jnp.concatenate(top_s, axis=0)
        e = jnp.exp(top_s - top_s[0:1])
        gws.append(e / jnp.sum(e, axis=0, keepdims=True))
        eids.append(jnp.concatenate(top_e, axis=0))
    eids = jnp.concatenate(eids, axis=0)
    for c in range(PEER_TILE // PEER_EXPERT_TILE):
        eid_ref[0, c] = eids[:, c * PEER_EXPERT_TILE:(c + 1) * PEER_EXPERT_TILE]
    gw_ref[0] = jnp.transpose(jnp.concatenate(gws, axis=0))


def _peer_route(x, g, scale, shift, w_bf16, keys_bf16):
    b_, s_, d_ = x.shape
    n = w_bf16.shape[1]
    nk = PEER_HEADS * PEER_TOPK
    per_tile = PEER_TILE // PEER_EXPERT_TILE
    row = lambda w: pl.BlockSpec((1, PEER_TILE, w), lambda b, i: (b, i, 0))
    return pl.pallas_call(
        _peer_route_kernel,
        grid=(b_, s_ // PEER_TILE),
        in_specs=[
            row(d_),
            pl.BlockSpec((1, d_), lambda b, i: (0, 0)),
            pl.BlockSpec((1, 1, d_), lambda b, i: (b, 0, 0)),
            pl.BlockSpec((1, 1, d_), lambda b, i: (b, 0, 0)),
            pl.BlockSpec((d_, n), lambda b, i: (0, 0)),
            pl.BlockSpec(keys_bf16.shape, lambda b, i: (0, 0, 0)),
        ],
        out_specs=[row(d_),
                   pl.BlockSpec((1, per_tile, nk, PEER_EXPERT_TILE), lambda b, i: (b, i, 0, 0)),
                   row(nk)],
        out_shape=[
            jax.ShapeDtypeStruct((b_, s_, d_), F32),
            jax.ShapeDtypeStruct((b_, s_ // PEER_EXPERT_TILE, nk, PEER_EXPERT_TILE), jnp.int32),
            jax.ShapeDtypeStruct((b_, s_, nk), F32),
        ],
        compiler_params=pltpu.CompilerParams(
            dimension_semantics=("parallel", "parallel"), vmem_limit_bytes=VMEM_LIMIT),
    )(x, g.reshape(1, d_), scale, shift, w_bf16, keys_bf16)


_HI_MASK = -65536


def _pack_table(t):
    e_, d_ = t.shape
    bits = lax.bitcast_convert_type(t.astype(jnp.bfloat16), jnp.uint16).astype(jnp.uint32)
    words = (bits[:, d_ // 2:] << 16) | bits[:, :d_ // 2]
    rows = lax.bitcast_convert_type(words, jnp.int32).reshape(e_ * EXPERT_ROWS, LANES)
    return jnp.pad(rows, ((EXPERT_ROWS, EXPERT_ROWS), (0, 0)))


def _expert_words(tab_vmem, row):
    return tab_vmem[pl.ds(pl.multiple_of(row, EXPERT_ROWS), EXPERT_ROWS), :]


def _expert_pair_words(tab_vmem, row_a, row_b, sub):
    top = tab_vmem[pl.ds(pl.multiple_of(row_a, EXPERT_ROWS), 2 * EXPERT_ROWS), :]
    bot = tab_vmem[pl.ds(pl.multiple_of(row_b - EXPERT_ROWS, EXPERT_ROWS), 2 * EXPERT_ROWS), :]
    return jnp.where(sub < EXPERT_ROWS, top, bot)


def _unpack(words):
    lo = lax.bitcast_convert_type(lax.shift_left(words, 16), F32)
    hi = lax.bitcast_convert_type(words & _HI_MASK, F32)
    return lo, hi


def _load_resident(j, tab_hbm, tab_vmem, eid_ref, idx_smem, sem):
    @pl.when(j == 0)
    def _():
        cp = pltpu.make_async_copy(tab_hbm, tab_vmem, sem)
        cp.start()
        cp.wait()

    copies = [pltpu.make_async_copy(eid_ref.at[0, k], idx_smem[k], sem) for k in range(len(idx_smem))]
    for cp in copies:
        cp.start()
    for cp in copies:
        cp.wait()


def _fold_pairs(q, sub):
    m2, m1 = (sub & 2) == 0, (sub & 1) == 0
    u = [jnp.where(m2, q[i], q[i + 2]) + jnp.where(m2, pltpu.roll(q[i], 6, axis=0), pltpu.roll(q[i + 2], 2, axis=0))
         for i in range(2)]
    return jnp.where(m1, u[0], u[1]) + jnp.where(m1, pltpu.roll(u[0], 7, axis=0), pltpu.roll(u[1], 1, axis=0))


def _peer_u_kernel(eid_ref, gw_ref, h_ref, tab_hbm, w_ref, *scratch):
    j = pl.program_id(0)
    nk = PEER_HEADS * PEER_TOPK
    idx_smem, (tab_vmem, acc_ref, sem) = scratch[:nk], scratch[nk:]
    half = D_MODEL // 2 // LANES
    _load_resident(j, tab_hbm, tab_vmem, eid_ref, idx_smem, sem)
    sub = lax.broadcasted_iota(jnp.int32, (2 * half, LANES), 0)
    lane = lax.broadcasted_iota(jnp.int32, (nk, PEER_EXPERT_TILE), 1)
    acc_ref[...] = jnp.zeros((nk, PEER_EXPERT_TILE), F32)

    def token(t):
        row = h_ref[pl.ds(t, 1), :]

        def seg(s):
            return jnp.broadcast_to(row[:, s * LANES:(s + 1) * LANES], (2 * half, LANES))

        def rows_of(first):
            out = seg(first + half - 1)
            for s in range(half - 2, -1, -1):
                out = jnp.where(sub % half == s, seg(first + s), out)
            return out

        x_lo, x_hi = rows_of(0), rows_of(half)
        folded = []
        for gi in range(nk // 8):
            q = []
            for i in range(4):
                k = gi * 8 + i
                words = _expert_pair_words(tab_vmem, idx_smem[k][t], idx_smem[k + 4][t], sub)
                lo, hi = _unpack(words)
                q.append(lo * x_lo + hi * x_hi)
            folded.append(_fold_pairs(q, sub))
        return jnp.sum(jnp.concatenate(folded, axis=0), axis=-1, keepdims=True)

    def body(tb, carry):
        acc = acc_ref[...]
        for u in range(PEER_TOKEN_UNROLL):
            t = tb * PEER_TOKEN_UNROLL + u
            acc = jnp.where(lane == t, token(t), acc)
        acc_ref[...] = acc
        return carry

    lax.fori_loop(0, PEER_EXPERT_TILE // PEER_TOKEN_UNROLL, body, 0)
    w_ref[...] = jax.nn.gelu(jnp.transpose(acc_ref[...])) * gw_ref[...]


def _peer_v_kernel(eid_ref, w_ref, x_ref, g2_ref, tab_hbm, o_ref, *scratch):
    j = pl.program_id(0)
    nk = PEER_HEADS * PEER_TOPK
    idx_smem, (wb_ref, tab_vmem, sem) = scratch[:nk], scratch[nk:]
    half = D_MODEL // 2 // LANES
    _load_resident(j, tab_hbm, tab_vmem, eid_ref, idx_smem, sem)
    n_acc = 4
    sub = lax.broadcasted_iota(jnp.int32, (2 * half, LANES), 0)

    def token(t, slot):
        wb_ref[slot] = jnp.transpose(jnp.broadcast_to(w_ref[pl.ds(t, 1), :], (nk, nk)))
        lo_acc = [jnp.zeros((2 * half, LANES), F32) for _ in range(n_acc)]
        hi_acc = [jnp.zeros((2 * half, LANES), F32) for _ in range(n_acc)]
        for p in range(nk // 2):
            k = 2 * p
            lo, hi = _unpack(_expert_pair_words(tab_vmem, idx_smem[k][t], idx_smem[k + 1][t], sub))
            wk = jnp.where(sub < half, wb_ref[slot, k:k + 1, :], wb_ref[slot, k + 1:k + 2, :])
            lo_acc[p % n_acc] = lo_acc[p % n_acc] + wk * lo
            hi_acc[p % n_acc] = hi_acc[p % n_acc] + wk * hi
        lo_sum, hi_sum = sum(lo_acc[1:], lo_acc[0]), sum(hi_acc[1:], hi_acc[0])
        ff = jnp.concatenate([lo_sum[:half] + lo_sum[half:], hi_sum[:half] + hi_sum[half:]], axis=0)
        ff_row = jnp.concatenate([ff[s:s + 1, :] for s in range(2 * half)], axis=1)
        o_ref[pl.ds(t, 1), :] = x_ref[pl.ds(t, 1), :] + g2_ref[0] * ff_row

    def body(tb, carry):
        for u in range(PEER_V_UNROLL):
            token(tb * PEER_V_UNROLL + u, u)
        return carry

    lax.fori_loop(0, PEER_EXPERT_TILE // PEER_V_UNROLL, body, 0)


def _peer_experts(eid, gw, h2, x1, gate2, u_tab, v_tab):
    b_, s_, d_ = x1.shape
    n_tok = b_ * s_
    nk = PEER_HEADS * PEER_TOPK
    tt = PEER_EXPERT_TILE
    nsteps = n_tok // tt
    steps_per_batch = s_ // tt
    tok = pl.BlockSpec((tt, d_), lambda j: (j, 0))
    sel = pl.BlockSpec((tt, nk), lambda j: (j, 0))
    slots = pl.BlockSpec((1, nk, tt), lambda j: (j, 0, 0))
    idx_scratch = [pltpu.SMEM((tt,), jnp.int32) for _ in range(nk)]
    params = pltpu.CompilerParams(dimension_semantics=("arbitrary",), vmem_limit_bytes=PEER_VMEM_LIMIT)
    eid3 = eid.reshape(nsteps, nk, tt)
    w = pl.pallas_call(
        _peer_u_kernel,
        grid=(nsteps,),
        in_specs=[slots, sel, tok, pl.BlockSpec(memory_space=pl.ANY)],
        out_specs=sel,
        out_shape=jax.ShapeDtypeStruct((n_tok, nk), F32),
        scratch_shapes=idx_scratch + [pltpu.VMEM(u_tab.shape, jnp.int32), pltpu.VMEM((nk, tt), F32),
                                      pltpu.SemaphoreType.DMA],
        compiler_params=params,
    )(eid3, gw.reshape(n_tok, nk), h2.reshape(n_tok, d_), u_tab)
    out = pl.pallas_call(
        _peer_v_kernel,
        grid=(nsteps,),
        in_specs=[slots, sel, tok, pl.BlockSpec((1, 1, d_), lambda j: (j // steps_per_batch, 0, 0)),
                  pl.BlockSpec(memory_space=pl.ANY)],
        out_specs=tok,
        out_shape=jax.ShapeDtypeStruct((n_tok, d_), F32),
        scratch_shapes=idx_scratch + [pltpu.VMEM((PEER_V_UNROLL, nk, LANES), F32),
                                      pltpu.VMEM(v_tab.shape, jnp.int32), pltpu.SemaphoreType.DMA],
        compiler_params=params,
    )(eid3, w, x1.reshape(n_tok, d_), gate2, v_tab)
    return out.reshape(b_, s_, d_)


def _permute_w_in(w):
    d_ = w.shape[0]
    kv_end = POOL_WIDTH + Q_WIDTH + 6 * KV_WIDTH
    per_group = HEADS_PER_GROUP * 3
    parts = [w[:, :kv_end], w[:, kv_end + GATE_COLS:]]
    for g in range(N_KV_GROUPS):
        parts.append(w[:, kv_end + g * per_group:kv_end + (g + 1) * per_group])
        parts.append(jnp.zeros((d_, LANES - per_group), w.dtype))
    return jnp.concatenate(parts, axis=1).astype(BF16)


def kernel(x, c, rel_bias, ada_w, ada_b, norm1_g, norm2_g, w_in, pool_w, pool_scale, cmp_pe_k, cmp_w1_k, cmp_w2_k, cmp_pe_v, cmp_w1_v, cmp_w2_v, q_norm_g, k_norm_g, w_branch_pool, w_branch_attn, w_out, peer_w_q, peer_sub_keys, peer_u, peer_v):
    b_, s_, d_ = x.shape
    l = 0
    ada = _ada(c, ada_w[l], ada_b[l]).reshape(b_, 6, 1, d_)
    shift1, scale1, gate1 = ada[:, 0], ada[:, 1], ada[:, 2]
    shift2, scale2, gate2 = ada[:, 3], ada[:, 4], ada[:, 5]

    (z_pool, z_q, ksel, vsel, kwin, vwin, kc_raw, vc_raw, z_merge, z_gate) = _in_proj(
        x, norm1_g[l], scale1, shift1, _permute_w_in(w_in[l]), k_norm_g[l])

    kc = _compress(kc_raw, cmp_pe_k[l], cmp_w1_k[l], cmp_w2_k[l], k_norm_g[l, 0], True)
    vc = _compress(vc_raw, cmp_pe_v[l], cmp_w1_v[l], cmp_w2_v[l], k_norm_g[l, 0], False)
    y_attn = _attention(z_q, z_gate, q_norm_g[l], kc, vc, ksel, vsel, kwin, vwin, rel_bias)

    x1 = _merge(x, z_pool, y_attn, z_merge, gate1, pool_w[l].astype(BF16), pool_scale[l],
                w_branch_pool[l].astype(BF16), w_branch_attn[l].astype(BF16), w_out[l].astype(BF16))

    keys = peer_sub_keys[l].reshape(PEER_HEADS * 2, PEER_KEYS, PEER_QDIM // 2).astype(BF16)
    h2, eid, gw = _peer_route(x1, norm2_g[l], scale2, shift2, peer_w_q[l].astype(BF16), keys)
    return _peer_experts(eid, gw, h2, x1, gate2, _pack_table(peer_u[l]), _pack_table(peer_v[l]))
```

```python
import math
from functools import partial

import jax
import jax.numpy as jnp
import numpy as np
from jax import lax
from jax.experimental import pallas as pl
from jax.experimental.pallas import tpu as pltpu

D_MODEL = 1024
POOL_WIDTH = 512
POOL_WINDOWS = (2, 4, 8, 16)
POOL_GROUP_DIM = 128
N_HEADS = 8
N_KV_GROUPS = 2
HEADS_PER_GROUP = 4
HEAD_DIM = 64
Q_WIDTH = 512
KV_WIDTH = 128
CMP_BLOCK = 32
CMP_STRIDE = 16
CMP_HIDDEN = 128
SEL_BLOCK = 64
SEL_TOPK = 16
WINDOW = 512
FORCE_SCORE = 1000.0
N_BUCKETS = 32
MAX_DISTANCE = 128
PEER_HEADS = 8
PEER_KEYS = 128
PEER_TOPK = 16
PEER_QDIM = 256
PEER_CHUNK = 128
GATE_COLS = N_HEADS * 3
EPS = 1e-6
NEG_INF = -1e30

LANES = 128
VMEM_LIMIT = 48 * 1024 * 1024
ROW_TILE = 512
POOL_HALO = 16
TQ = 128
SEL_FEATS = 64
SEL_CHUNK = 512
V_WIDTH = 128
KV_PAD = 512
CMP_BAND_LO = 10
PEER_TILE = 256
PEER_EXPERT_TILE = 256
PEER_TOKEN_UNROLL = 8
PEER_V_UNROLL = 8
EXPERT_ROWS = D_MODEL // 2 // LANES
PEER_VMEM_LIMIT = 56 * 1024 * 1024

BF16 = jnp.bfloat16
F32 = jnp.float32
_NT = (((1,), (1,)), ((), ()))


def _rms_rows(x, g):
    return x * lax.rsqrt(jnp.mean(x * x, axis=-1, keepdims=True) + EPS) * g


def _in_proj_kernel(x_ref, g_ref, sc_ref, sh_ref, w_ref, kg_ref,
                    zpool_ref, zq_ref, ksel_ref, vsel_ref, kwin_ref, vwin_ref, kcr_ref, vcr_ref,
                    zmerge_ref, zgate_ref):
    j = pl.program_id(1)

    @pl.when(j == 0)
    def _():
        for ref in (ksel_ref, vsel_ref, kwin_ref, vwin_ref):
            ref[...] = jnp.zeros(ref.shape, ref.dtype)

    @pl.when(j > 0)
    def _():
        _in_proj_tile(j - 1, x_ref, g_ref, sc_ref, sh_ref, w_ref, kg_ref, zpool_ref, zq_ref, ksel_ref, vsel_ref,
                      kwin_ref, vwin_ref, kcr_ref, vcr_ref, zmerge_ref, zgate_ref)


def _in_proj_tile(i, x_ref, g_ref, sc_ref, sh_ref, w_ref, kg_ref,
                  zpool_ref, zq_ref, ksel_ref, vsel_ref, kwin_ref, vwin_ref, kcr_ref, vcr_ref,
                  zmerge_ref, zgate_ref):
    x = x_ref[0]
    y = x * lax.rsqrt(jnp.mean(x * x, axis=-1, keepdims=True) + EPS)
    h = y * g_ref[...] * (1.0 + sc_ref[0]) + sh_ref[0]
    z = jnp.dot(h.astype(BF16), w_ref[...], preferred_element_type=F32)
    ts = x.shape[0]
    zpool_ref[0] = z[:, :POOL_WIDTH]
    o = POOL_WIDTH
    zq_ref[0] = z[:, o:o + Q_WIDTH]
    o += Q_WIDTH
    pos = i * ts + lax.broadcasted_iota(jnp.int32, (ts, SEL_FEATS), 0)
    onehot = (pos // SEL_BLOCK == lax.broadcasted_iota(jnp.int32, (ts, SEL_FEATS), 1)).astype(BF16)
    one_col = (lax.broadcasted_iota(jnp.int32, (ts, V_WIDTH - HEAD_DIM), 1) == 0).astype(BF16)
    for g in range(N_KV_GROUPS):
        def col(k):
            return z[:, o + k * KV_WIDTH + g * HEAD_DIM:o + k * KV_WIDTH + (g + 1) * HEAD_DIM]
        kcr_ref[0, g] = col(0)
        vcr_ref[0, g] = col(1)
        ks = _rms_rows(col(2), kg_ref[1:2, :]).astype(BF16)
        ksel_ref[0, g] = jnp.concatenate([ks, onehot], axis=1)
        vsel_ref[0, g] = jnp.concatenate([col(3).astype(BF16), one_col], axis=1)
        kwin_ref[0, g] = _rms_rows(col(4), kg_ref[2:3, :]).astype(BF16)
        vwin_ref[0, g] = jnp.concatenate([col(5).astype(BF16), one_col], axis=1)
    o += 6 * KV_WIDTH
    zmerge_ref[0] = z[:, o:o + 2 * D_MODEL]
    o += 2 * D_MODEL
    zgate_ref[0] = jax.nn.sigmoid(z[:, o:o + N_KV_GROUPS * LANES])


def _in_proj(x, g, scale, shift, w_bf16, k_norm_g):
    b_, s_, d_ = x.shape
    n = w_bf16.shape[1]
    assert KV_PAD == ROW_TILE
    tile = lambda j: jnp.maximum(j - 1, 0)
    row = lambda w: pl.BlockSpec((1, ROW_TILE, w), lambda b, j: (b, tile(j), 0))
    grp = lambda w: pl.BlockSpec((1, N_KV_GROUPS, ROW_TILE, w), lambda b, j: (b, 0, tile(j), 0))
    padded = lambda w: pl.BlockSpec((1, N_KV_GROUPS, ROW_TILE, w), lambda b, j: (b, 0, j, 0))
    gshape = lambda w, dt: jax.ShapeDtypeStruct((b_, N_KV_GROUPS, s_, w), dt)
    pshape = lambda w, dt: jax.ShapeDtypeStruct((b_, N_KV_GROUPS, s_ + KV_PAD, w), dt)
    return pl.pallas_call(
        _in_proj_kernel,
        grid=(b_, s_ // ROW_TILE + 1),
        in_specs=[
            row(d_),
            pl.BlockSpec((1, d_), lambda b, i: (0, 0)),
            pl.BlockSpec((1, 1, d_), lambda b, i: (b, 0, 0)),
            pl.BlockSpec((1, 1, d_), lambda b, i: (b, 0, 0)),
            pl.BlockSpec((d_, n), lambda b, i: (0, 0)),
            pl.BlockSpec((3, HEAD_DIM), lambda b, i: (0, 0)),
        ],
        out_specs=[row(POOL_WIDTH), row(Q_WIDTH), padded(HEAD_DIM + SEL_FEATS), padded(V_WIDTH), padded(HEAD_DIM),
                   padded(V_WIDTH), grp(HEAD_DIM), grp(HEAD_DIM), row(2 * d_), row(N_KV_GROUPS * LANES)],
        out_shape=[
            jax.ShapeDtypeStruct((b_, s_, POOL_WIDTH), F32),
            jax.ShapeDtypeStruct((b_, s_, Q_WIDTH), F32),
            pshape(HEAD_DIM + SEL_FEATS, BF16), pshape(V_WIDTH, BF16), pshape(HEAD_DIM, BF16), pshape(V_WIDTH, BF16),
            gshape(HEAD_DIM, F32), gshape(HEAD_DIM, F32),
            jax.ShapeDtypeStruct((b_, s_, 2 * d_), F32),
            jax.ShapeDtypeStruct((b_, s_, N_KV_GROUPS * LANES), F32),
        ],
        compiler_params=pltpu.CompilerParams(
            dimension_semantics=("parallel", "arbitrary"), vmem_limit_bytes=VMEM_LIMIT),
    )(x, g.reshape(1, d_), scale, shift, w_bf16, k_norm_g)


def _compress_kernel(t_ref, pe_ref, w1_ref, w2_ref, g_ref, o_ref, *, normalize):
    half = CMP_STRIDE * HEAD_DIM
    t = t_ref[0, 0].astype(BF16)
    w1 = w1_ref[...]
    a = jnp.dot(t, w1[:half], preferred_element_type=F32)
    b = jnp.dot(t, w1[half:], preferred_element_type=F32)
    pe = jnp.dot(pe_ref[...], w1, preferred_element_type=F32)
    n = a.shape[0]
    b_next = pltpu.roll(b, n - 1, axis=0)
    hid = jax.nn.gelu(a + b_next + pe)
    out = jnp.dot(hid.astype(BF16), w2_ref[...], preferred_element_type=F32)
    if normalize:
        out = _rms_rows(out, g_ref[...])
    o_ref[0, 0] = out.astype(BF16)


def _compress(t_raw, pe, w1, w2, g, normalize):
    b_, g_, s_, dh = t_raw.shape
    n_str = s_ // CMP_STRIDE
    t = t_raw.reshape(b_, g_, n_str, CMP_STRIDE * dh)
    return pl.pallas_call(
        partial(_compress_kernel, normalize=normalize),
        grid=(b_, g_),
        in_specs=[
            pl.BlockSpec((1, 1, n_str, CMP_STRIDE * dh), lambda b, g: (b, g, 0, 0)),
            pl.BlockSpec((1, CMP_BLOCK * dh), lambda b, g: (0, 0)),
            pl.BlockSpec((CMP_BLOCK * dh, CMP_HIDDEN), lambda b, g: (0, 0)),
            pl.BlockSpec((CMP_HIDDEN, dh), lambda b, g: (0, 0)),
            pl.BlockSpec((1, dh), lambda b, g: (0, 0)),
        ],
        out_specs=pl.BlockSpec((1, 1, n_str, dh), lambda b, g: (b, g, 0, 0)),
        out_shape=jax.ShapeDtypeStruct((b_, g_, n_str, dh), BF16),
        compiler_params=pltpu.CompilerParams(
            dimension_semantics=("parallel", "parallel"), vmem_limit_bytes=VMEM_LIMIT),
    )(t, pe.reshape(1, CMP_BLOCK * dh).astype(BF16), w1.astype(BF16), w2.astype(BF16), g.reshape(1, dh))


def _t5_bucket_np(rel):
    n = np.maximum(rel, 0)
    max_exact = N_BUCKETS // 2
    nf = np.maximum(n, 1).astype(np.float32)
    large = max_exact + (np.log(nf / max_exact) / math.log(MAX_DISTANCE / max_exact) * (N_BUCKETS - max_exact)).astype(np.int32)
    large = np.minimum(large, N_BUCKETS - 1)
    return np.where(n < max_exact, n, large)


def _bias_tiles(rel_bias, n_cmp_pad):
    far = rel_bias[N_BUCKETS - 1]
    ii = np.arange(TQ)[:, None]

    def lookup(dist):
        near = (dist >= 0) & (dist < MAX_DISTANCE)
        bucket = _t5_bucket_np(np.where(near, dist, MAX_DISTANCE))
        onehot = (jnp.asarray(bucket.astype(np.int8))[..., None] == jnp.arange(N_BUCKETS, dtype=jnp.int8)).astype(F32)
        b = jnp.dot(onehot, rel_bias - far, precision=lax.Precision.HIGHEST)
        b = jnp.transpose(b, (2, 0, 1))
        return b.reshape(N_KV_GROUPS, HEADS_PER_GROUP * dist.shape[0], dist.shape[1])

    jj = np.arange(TQ)[None, :]
    bt = jnp.stack([lookup(ii - jj), lookup(TQ + ii - jj)], axis=1)
    cc = np.arange(n_cmp_pad)[None, :]
    dist_c = ii - CMP_STRIDE * (cc - CMP_BAND_LO) - (CMP_BLOCK - 1)
    dist_c = np.where(cc <= CMP_BAND_LO + TQ // CMP_STRIDE, dist_c, -1)
    bc = lookup(dist_c)
    rows = HEADS_PER_GROUP * TQ
    i_r = np.tile(np.arange(TQ), HEADS_PER_GROUP)[:, None]
    first = jnp.asarray(np.where(jj > i_r, 0.0, NEG_INF), F32)
    last = jnp.where(jnp.asarray(jj <= i_r), bt[:, 0], NEG_INF)
    zeros = jnp.zeros((N_KV_GROUPS, rows, WINDOW - 2 * TQ), F32)
    wa = jnp.concatenate([jnp.broadcast_to(first, (N_KV_GROUPS, rows, TQ)), zeros, bt[:, 1], last], axis=-1)
    return wa, bc


def _sel_mapping_t(n_cmp_pad, n_cmp):
    m = np.zeros((SEL_FEATS, n_cmp_pad), np.float32)
    pos = np.arange(n_cmp)[:, None] * CMP_STRIDE + np.arange(CMP_BLOCK)[None, :]
    np.add.at(m, ((pos // SEL_BLOCK).ravel(), np.repeat(np.arange(n_cmp), CMP_BLOCK)), 1.0 / CMP_BLOCK)
    return m


def _logits(q, k, add=None, lo_cols=None):
    s = lax.dot_general(q, k, _NT, preferred_element_type=F32)
    if add is not None:
        s = s + add
    if lo_cols is not None:
        s = jnp.where(lax.broadcasted_iota(jnp.int32, s.shape, 1) < lo_cols, NEG_INF, s)
    return s


def _flash_step(s, v, m_sc, acc_sc):
    m_prev = m_sc[...]
    m_new = jnp.maximum(m_prev, jnp.max(s, axis=-1, keepdims=True))
    alpha = jnp.exp(m_prev - m_new)
    p = jnp.exp(s - jnp.concatenate([m_new] * (s.shape[1] // LANES), axis=1))
    acc_sc[...] = alpha * acc_sc[...] + jnp.dot(p.astype(BF16), v, preferred_element_type=F32)
    m_sc[...] = m_new


def _attn_kernel(zq_ref, gate_ref, qg_ref, kc_ref, vc_ref, ksel_ref, vsel_ref, kwin_ref, vwin_ref,
                 wa_ref, bc_ref, smap_ref, o_ref, m_sc, acc_sc):
    i = pl.program_id(2)
    rows = HEADS_PER_GROUP * TQ
    ncp = kc_ref.shape[2]
    scale = HEAD_DIM ** -0.5

    zq = zq_ref[0]
    qs = jnp.concatenate([zq[:, j * HEAD_DIM:(j + 1) * HEAD_DIM] for j in range(HEADS_PER_GROUP)], axis=0)
    qn = _rms_rows(qs, qg_ref[...]) * scale
    qb = qn.astype(BF16)

    lc = lax.dot_general(qb, kc_ref[0, 0], _NT, preferred_element_type=F32)
    lc = lc + pltpu.roll(bc_ref[0], (i * (TQ // CMP_STRIDE) + ncp - CMP_BAND_LO) % ncp, axis=1)
    t_c = i * TQ + lax.broadcasted_iota(jnp.int32, (rows, ncp), 0) % TQ
    n_c = lax.broadcasted_iota(jnp.int32, (rows, ncp), 1)
    valid_c = n_c * CMP_STRIDE + (CMP_BLOCK - 1) <= t_c
    lm = jnp.where(valid_c, lc, NEG_INF)
    e = jnp.where(valid_c, jnp.exp(lm - jnp.max(lm, axis=-1, keepdims=True)), 0.0)
    den = jnp.sum(e, axis=-1, keepdims=True)
    p_c = e / jnp.where(den > 0.0, den, 1.0)
    o_cmp = jnp.dot(p_c.astype(BF16), vc_ref[0, 0], preferred_element_type=F32)

    p_sum = p_c[0:TQ] + p_c[TQ:2 * TQ] + p_c[2 * TQ:3 * TQ] + p_c[3 * TQ:4 * TQ]
    p_hi = p_sum.astype(BF16)
    r1 = p_sum - p_hi.astype(F32)
    p_mid = r1.astype(BF16)
    p_lo = (r1 - p_mid.astype(F32)).astype(BF16)
    smap = smap_ref[...]
    imp = (lax.dot_general(smap, p_hi, _NT, preferred_element_type=F32)
           + lax.dot_general(smap, p_mid, _NT, preferred_element_type=F32)
           + lax.dot_general(smap, p_lo, _NT, preferred_element_type=F32))
    blk = lax.broadcasted_iota(jnp.int32, (SEL_FEATS, TQ), 0)
    t_s = i * TQ + lax.broadcasted_iota(jnp.int32, (SEL_FEATS, TQ), 1)
    cur = t_s // SEL_BLOCK
    forced = (blk == 0) | (blk == cur) | (blk == cur - 1)
    visible = blk * SEL_BLOCK <= t_s
    score = jnp.where(visible, imp + jnp.where(forced, FORCE_SCORE, 0.0), -1.0)
    rank = jnp.zeros((SEL_FEATS, TQ), jnp.int32)
    for sp in range(SEL_FEATS):
        row = score[sp:sp + 1, :]
        beats = (row > score) | ((row == score) & (blk > sp))
        rank = rank + beats.astype(jnp.int32)
    pen_t = jnp.where(rank < SEL_TOPK, 0.0, NEG_INF)
    pen = jnp.transpose(pen_t).astype(BF16)
    q_aug = jnp.concatenate([qb, jnp.concatenate([pen] * HEADS_PER_GROUP, axis=0)], axis=1)

    def keys(ref, pos, n):
        return ref[0, 0, pl.ds(pl.multiple_of(pos + KV_PAD, TQ), n), :]

    m_sc[...] = jnp.full((rows, LANES), NEG_INF, F32)
    acc_sc[...] = jnp.zeros((rows, V_WIDTH), F32)
    per_chunk = SEL_CHUNK // TQ
    n_plain = jnp.maximum(i - 1, 0)
    rem = n_plain % per_chunk

    @pl.when(rem > 0)
    def _():
        pos = (rem - per_chunk) * TQ
        s = _logits(q_aug, keys(ksel_ref, pos, SEL_CHUNK), lo_cols=-pos)
        _flash_step(s, keys(vsel_ref, pos, SEL_CHUNK), m_sc, acc_sc)

    def sel_body(c, carry):
        pos = (rem + c * per_chunk) * TQ
        _flash_step(_logits(q_aug, keys(ksel_ref, pos, SEL_CHUNK)), keys(vsel_ref, pos, SEL_CHUNK), m_sc, acc_sc)
        return carry

    lax.fori_loop(0, n_plain // per_chunk, sel_body, 0)
    pos = (i - 1) * TQ
    s = _logits(q_aug, keys(ksel_ref, pos, 2 * TQ), add=wa_ref[0, :, WINDOW - TQ:], lo_cols=-pos)
    _flash_step(s, keys(vsel_ref, pos, 2 * TQ), m_sc, acc_sc)
    acc = acc_sc[...]
    o_sel = acc[:, :HEAD_DIM] / acc[:, HEAD_DIM:HEAD_DIM + 1]

    pos = i * TQ - WINDOW
    s = _logits(qb, keys(kwin_ref, pos, WINDOW + TQ), add=wa_ref[0], lo_cols=-pos)
    p = jnp.exp(s - jnp.max(s, axis=-1, keepdims=True))
    acc = jnp.dot(p.astype(BF16), keys(vwin_ref, pos, WINDOW + TQ), preferred_element_type=F32)
    o_win = acc[:, :HEAD_DIM] / acc[:, HEAD_DIM:HEAD_DIM + 1]

    gate = gate_ref[0]
    outs = []
    for j in range(HEADS_PER_GROUP):
        sl = slice(j * TQ, (j + 1) * TQ)
        outs.append(gate[:, 3 * j:3 * j + 1] * o_cmp[sl] + gate[:, 3 * j + 1:3 * j + 2] * o_sel[sl]
                    + gate[:, 3 * j + 2:3 * j + 3] * o_win[sl])
    o_ref[0] = jnp.concatenate(outs, axis=1)


def _attention(z_q, z_gate, q_norm_g, kc, vc, ksel, vsel, kwin, vwin, rel_bias):
    b_, s_, _ = z_q.shape
    ncp = kc.shape[2]
    n_cmp = ncp - CMP_BLOCK // CMP_STRIDE + 1
    assert s_ // SEL_BLOCK <= SEL_FEATS and s_ % TQ == 0
    wa, bc = _bias_tiles(rel_bias, ncp)
    smap_t = jnp.asarray(_sel_mapping_t(ncp, n_cmp), BF16)
    rows = HEADS_PER_GROUP * TQ
    gw = HEADS_PER_GROUP * HEAD_DIM
    kv = lambda w: pl.BlockSpec((1, 1, s_ + KV_PAD, w), lambda b, g, i: (b, g, 0, 0))
    cm = pl.BlockSpec((1, 1, ncp, HEAD_DIM), lambda b, g, i: (b, g, 0, 0))
    return pl.pallas_call(
        _attn_kernel,
        grid=(b_, N_KV_GROUPS, s_ // TQ),
        in_specs=[
            pl.BlockSpec((1, TQ, gw), lambda b, g, i: (b, i, g)),
            pl.BlockSpec((1, TQ, LANES), lambda b, g, i: (b, i, g)),
            pl.BlockSpec((1, HEAD_DIM), lambda b, g, i: (0, 0)),
            cm, cm, kv(HEAD_DIM + SEL_FEATS), kv(V_WIDTH), kv(HEAD_DIM), kv(V_WIDTH),
            pl.BlockSpec((1, rows, WINDOW + TQ), lambda b, g, i: (g, 0, 0)),
            pl.BlockSpec((1, rows, ncp), lambda b, g, i: (g, 0, 0)),
            pl.BlockSpec((SEL_FEATS, ncp), lambda b, g, i: (0, 0)),
        ],
        out_specs=pl.BlockSpec((1, TQ, gw), lambda b, g, i: (b, i, g)),
        out_shape=jax.ShapeDtypeStruct((b_, s_, N_KV_GROUPS * gw), F32),
        scratch_shapes=[pltpu.VMEM((rows, LANES), F32), pltpu.VMEM((rows, V_WIDTH), F32)],
        compiler_params=pltpu.CompilerParams(
            dimension_semantics=("parallel", "parallel", "arbitrary"), vmem_limit_bytes=VMEM_LIMIT),
    )(z_q, z_gate, q_norm_g.reshape(1, HEAD_DIM), kc, vc, ksel, vsel, kwin, vwin, wa, bc, smap_t)


def _pool_tile(halo, tile, i, pw_ref, ps_ref):
    ts = tile.shape[0]
    ext = jnp.concatenate([jnp.where(i > 0, halo, 0.0), tile], axis=0)
    t = i * ts + lax.broadcasted_iota(jnp.int32, (ts, 1), 0)
    outs = []
    for gi, w in enumerate(POOL_WINDOWS):
        sl = slice(gi * POOL_GROUP_DIM, (gi + 1) * POOL_GROUP_DIM)
        run = ext[:, sl]
        span = 1
        while span < w:
            run = run + pltpu.roll(run, span, axis=0)
            span *= 2
        cnt = jnp.minimum(t + 1, w).astype(F32)
        pooled = run[POOL_HALO:] / cnt - tile[:, sl]
        outs.append(jnp.dot(pooled.astype(BF16), pw_ref[gi], preferred_element_type=F32))
    return jnp.concatenate(outs, axis=1) * ps_ref[...]


def _merge_kernel(x_ref, zp_ref, zh_ref, ya_ref, zm_ref, g1_ref, pw_ref, ps_ref, wbp_ref, wba_ref, wo_ref, o_ref):
    d_ = x_ref.shape[-1]
    y_pool = _pool_tile(zh_ref[0], zp_ref[0], pl.program_id(1), pw_ref, ps_ref)
    bp = jnp.dot(y_pool.astype(BF16), wbp_ref[...], preferred_element_type=F32)
    ba = jnp.dot(ya_ref[0].astype(BF16), wba_ref[...], preferred_element_type=F32)
    zm = zm_ref[0]
    mixed = jax.nn.sigmoid(zm[:, :d_]) * bp + jax.nn.sigmoid(zm[:, d_:]) * ba
    proj = jnp.dot(mixed.astype(BF16), wo_ref[...], preferred_element_type=F32)
    o_ref[0] = x_ref[0] + g1_ref[0] * proj


def _merge(x, z_pool, y_attn, z_merge, gate1, pool_w, pool_scale, wbp, wba, wo):
    b_, s_, d_ = x.shape
    row = lambda w: pl.BlockSpec((1, ROW_TILE, w), lambda b, i: (b, i, 0))
    full = lambda a: pl.BlockSpec(a.shape, lambda b, i: (0,) * a.ndim)
    halo_blocks = ROW_TILE // POOL_HALO
    halo = pl.BlockSpec((1, POOL_HALO, POOL_WIDTH), lambda b, i: (b, jnp.maximum(i * halo_blocks - 1, 0), 0))
    pool_scale = pool_scale.reshape(1, POOL_WIDTH)
    return pl.pallas_call(
        _merge_kernel,
        grid=(b_, s_ // ROW_TILE),
        in_specs=[row(d_), row(POOL_WIDTH), halo, row(Q_WIDTH), row(2 * d_),
                  pl.BlockSpec((1, 1, d_), lambda b, i: (b, 0, 0)),
                  full(pool_w), full(pool_scale), full(wbp), full(wba), full(wo)],
        out_specs=row(d_),
        out_shape=jax.ShapeDtypeStruct((b_, s_, d_), F32),
        compiler_params=pltpu.CompilerParams(
            dimension_semantics=("parallel", "parallel"), vmem_limit_bytes=VMEM_LIMIT),
    )(x, z_pool, z_pool, y_attn, z_merge, gate1, pool_w, pool_scale, wbp, wba, wo)


def _ada_kernel(c_ref, w_ref, b_ref, o_ref):
    c = c_ref[...]
    o_ref[...] = jnp.dot(jax.nn.silu(c).astype(BF16), w_ref[...].astype(BF16),
                         preferred_element_type=F32) + b_ref[...]


def _ada(c, w, b):
    b_, d_ = c.shape
    n = w.shape[1]
    return pl.pallas_call(
        _ada_kernel,
        grid=(n // d_,),
        in_specs=[pl.BlockSpec((b_, d_), lambda j: (0, 0)),
                  pl.BlockSpec((d_, d_), lambda j: (0, j)),
                  pl.BlockSpec((1, d_), lambda j: (0, j))],
        out_specs=pl.BlockSpec((b_, d_), lambda j: (0, j)),
        out_shape=jax.ShapeDtypeStruct((b_, n), F32),
        compiler_params=pltpu.CompilerParams(dimension_semantics=("parallel",), vmem_limit_bytes=VMEM_LIMIT),
    )(c, w, b.reshape(1, n))


_CAND_ROWS = PEER_TOPK + 7 * 8 + 8


def _topk_rows(x, k):
    n = x.shape[0]
    rid = lax.broadcasted_iota(jnp.int32, x.shape, 0)
    vals, idxs = [], []
    for _ in range(k):
        m = jnp.max(x, axis=0, keepdims=True)
        idx = jnp.min(jnp.where(x == m, rid, n), axis=0, keepdims=True)
        vals.append(m)
        idxs.append(idx)
        x = jnp.where(rid == idx, -jnp.inf, x)
    return jnp.concatenate(vals, axis=0), jnp.concatenate(idxs, axis=0)


def _pair_grid(r0, r1, combine):
    parts = [combine(r0[0:1], r1)]
    parts += [combine(r0[a:a + 1], r1[0:8]) for a in range(1, 8)]
    parts.append(combine(r0[8:16], r1[0:1]))
    return jnp.concatenate(parts, axis=0)


def _peer_route_kernel(x_ref, g_ref, sc_ref, sh_ref, w_ref, keys_ref, h_ref, eid_ref, gw_ref):
    x = x_ref[0]
    y = x * lax.rsqrt(jnp.mean(x * x, axis=-1, keepdims=True) + EPS)
    h = y * g_ref[...] * (1.0 + sc_ref[0]) + sh_ref[0]
    h_ref[0] = h
    qv = jnp.dot(h.astype(BF16), w_ref[...], preferred_element_type=F32).astype(BF16)
    half = PEER_QDIM // 2
    eids, gws = [], []
    for hd in range(PEER_HEADS):
        tops = []
        for p in range(2):
            c = hd * 2 + p
            sc = lax.dot_general(keys_ref[c], qv[:, c * half:(c + 1) * half], _NT,
                                 preferred_element_type=F32)
            tops.append(_topk_rows(sc, PEER_TOPK))
        (s0, i0), (s1, i1) = tops
        cand = _pair_grid(s0, s1, lambda a, b: a + b)
        cid = _pair_grid(i0, i1, lambda a, b: a * (PEER_KEYS * EXPERT_ROWS) + (b * EXPERT_ROWS + EXPERT_ROWS))
        rid = lax.broadcasted_iota(jnp.int32, cand.shape, 0)
        top_s, top_e = [], []
        for _ in range(PEER_TOPK):
            m = jnp.max(cand, axis=0, keepdims=True)
            idx = jnp.min(jnp.where(cand == m, rid, _CAND_ROWS), axis=0, keepdims=True)
            hit = rid == idx
            top_s.append(m)
            top_e.append(jnp.sum(jnp.where(hit, cid, 0), axis=0, keepdims=True))
            cand = jnp.where(hit, -jnp.inf, cand)
        top_s = jnp.concatenate(top_s, axis=0)
        e = jnp.exp(top_s - top_s[0:1])
        gws.append(e / jnp.sum(e, axis=0, keepdims=True))
        eids.append(jnp.concatenate(top_e, axis=0))
    eids = jnp.concatenate(eids, axis=0)
    for c in range(PEER_TILE // PEER_EXPERT_TILE):
        eid_ref[0, c] = eids[:, c * PEER_EXPERT_TILE:(c + 1) * PEER_EXPERT_TILE]
    gw_ref[0] = jnp.transpose(jnp.concatenate(gws, axis=0))


def _peer_route(x, g, scale, shift, w_bf16, keys_bf16):
    b_, s_, d_ = x.shape
    n = w_bf16.shape[1]
    nk = PEER_HEADS * PEER_TOPK
    per_tile = PEER_TILE // PEER_EXPERT_TILE
    row = lambda w: pl.BlockSpec((1, PEER_TILE, w), lambda b, i: (b, i, 0))
    return pl.pallas_call(
        _peer_route_kernel,
        grid=(b_, s_ // PEER_TILE),
        in_specs=[
            row(d_),
            pl.BlockSpec((1, d_), lambda b, i: (0, 0)),
            pl.BlockSpec((1, 1, d_), lambda b, i: (b, 0, 0)),
            pl.BlockSpec((1, 1, d_), lambda b, i: (b, 0, 0)),
            pl.BlockSpec((d_, n), lambda b, i: (0, 0)),
            pl.BlockSpec(keys_bf16.shape, lambda b, i: (0, 0, 0)),
        ],
        out_specs=[row(d_),
                   pl.BlockSpec((1, per_tile, nk, PEER_EXPERT_TILE), lambda b, i: (b, i, 0, 0)),
                   row(nk)],
        out_shape=[
            jax.ShapeDtypeStruct((b_, s_, d_), F32),
            jax.ShapeDtypeStruct((b_, s_ // PEER_EXPERT_TILE, nk, PEER_EXPERT_TILE), jnp.int32),
            jax.ShapeDtypeStruct((b_, s_, nk), F32),
        ],
        compiler_params=pltpu.CompilerParams(
            dimension_semantics=("parallel", "parallel"), vmem_limit_bytes=VMEM_LIMIT),
    )(x, g.reshape(1, d_), scale, shift, w_bf16, keys_bf16)


_HI_MASK = -65536


def _pack_table(t):
    e_, d_ = t.shape
    bits = lax.bitcast_convert_type(t.astype(jnp.bfloat16), jnp.uint16).astype(jnp.uint32)
    words = (bits[:, d_ // 2:] << 16) | bits[:, :d_ // 2]
    rows = lax.bitcast_convert_type(words, jnp.int32).reshape(e_ * EXPERT_ROWS, LANES)
    return jnp.pad(rows, ((EXPERT_ROWS, EXPERT_ROWS), (0, 0)))


def _expert_words(tab_vmem, row):
    return tab_vmem[pl.ds(pl.multiple_of(row, EXPERT_ROWS), EXPERT_ROWS), :]


def _expert_pair_words(tab_vmem, row_a, row_b, sub):
    top = tab_vmem[pl.ds(pl.multiple_of(row_a, EXPERT_ROWS), 2 * EXPERT_ROWS), :]
    bot = tab_vmem[pl.ds(pl.multiple_of(row_b - EXPERT_ROWS, EXPERT_ROWS), 2 * EXPERT_ROWS), :]
    return jnp.where(sub < EXPERT_ROWS, top, bot)


def _unpack(words):
    lo = lax.bitcast_convert_type(lax.shift_left(words, 16), F32)
    hi = lax.bitcast_convert_type(words & _HI_MASK, F32)
    return lo, hi


def _load_resident(j, tab_hbm, tab_vmem, eid_ref, idx_smem, sem):
    @pl.when(j == 0)
    def _():
        cp = pltpu.make_async_copy(tab_hbm, tab_vmem, sem)
        cp.start()
        cp.wait()

    copies = [pltpu.make_async_copy(eid_ref.at[0, k], idx_smem[k], sem) for k in range(len(idx_smem))]
    for cp in copies:
        cp.start()
    for cp in copies:
        cp.wait()


def _fold_pairs(q, sub):
    m2, m1 = (sub & 2) == 0, (sub & 1) == 0
    u = [jnp.where(m2, q[i], q[i + 2]) + jnp.where(m2, pltpu.roll(q[i], 6, axis=0), pltpu.roll(q[i + 2], 2, axis=0))
         for i in range(2)]
    return jnp.where(m1, u[0], u[1]) + jnp.where(m1, pltpu.roll(u[0], 7, axis=0), pltpu.roll(u[1], 1, axis=0))


def _peer_u_kernel(eid_ref, gw_ref, h_ref, tab_hbm, w_ref, *scratch):
    j = pl.program_id(0)
    nk = PEER_HEADS * PEER_TOPK
    idx_smem, (tab_vmem, acc_ref, sem) = scratch[:nk], scratch[nk:]
    half = D_MODEL // 2 // LANES
    _load_resident(j, tab_hbm, tab_vmem, eid_ref, idx_smem, sem)
    sub = lax.broadcasted_iota(jnp.int32, (2 * half, LANES), 0)
    lane = lax.broadcasted_iota(jnp.int32, (nk, PEER_EXPERT_TILE), 1)
    acc_ref[...] = jnp.zeros((nk, PEER_EXPERT_TILE), F32)

    def token(t):
        row = h_ref[pl.ds(t, 1), :]

        def seg(s):
            return jnp.broadcast_to(row[:, s * LANES:(s + 1) * LANES], (2 * half, LANES))

        def rows_of(first):
            out = seg(first + half - 1)
            for s in range(half - 2, -1, -1):
                out = jnp.where(sub % half == s, seg(first + s), out)
            return out

        x_lo, x_hi = rows_of(0), rows_of(half)
        folded = []
        for gi in range(nk // 8):
            q = []
            for i in range(4):
                k = gi * 8 + i
                words = _expert_pair_words(tab_vmem, idx_smem[k][t], idx_smem[k + 4][t], sub)
                lo, hi = _unpack(words)
                q.append(lo * x_lo + hi * x_hi)
            folded.append(_fold_pairs(q, sub))
        return jnp.sum(jnp.concatenate(folded, axis=0), axis=-1, keepdims=True)

    def body(tb, carry):
        acc = acc_ref[...]
        for u in range(PEER_TOKEN_UNROLL):
            t = tb * PEER_TOKEN_UNROLL + u
            acc = jnp.where(lane == t, token(t), acc)
        acc_ref[...] = acc
        return carry

    lax.fori_loop(0, PEER_EXPERT_TILE // PEER_TOKEN_UNROLL, body, 0)
    w_ref[...] = jax.nn.gelu(jnp.transpose(acc_ref[...])) * gw_ref[...]


def _peer_v_kernel(eid_ref, w_ref, x_ref, g2_ref, tab_hbm, o_ref, *scratch):
    j = pl.program_id(0)
    nk = PEER_HEADS * PEER_TOPK
    idx_smem, (wb_ref, tab_vmem, sem) = scratch[:nk], scratch[nk:]
    half = D_MODEL // 2 // LANES
    _load_resident(j, tab_hbm, tab_vmem, eid_ref, idx_smem, sem)
    n_acc = 4
    sub = lax.broadcasted_iota(jnp.int32, (2 * half, LANES), 0)

    def token(t, slot):
        wb_ref[slot] = jnp.transpose(jnp.broadcast_to(w_ref[pl.ds(t, 1), :], (nk, nk)))
        lo_acc = [jnp.zeros((2 * half, LANES), F32) for _ in range(n_acc)]
        hi_acc = [jnp.zeros((2 * half, LANES), F32) for _ in range(n_acc)]
        for p in range(nk // 2):
            k = 2 * p
            lo, hi = _unpack(_expert_pair_words(tab_vmem, idx_smem[k][t], idx_smem[k + 1][t], sub))
            wk = jnp.where(sub < half, wb_ref[slot, k:k + 1, :], wb_ref[slot, k + 1:k + 2, :])
            lo_acc[p % n_acc] = lo_acc[p % n_acc] + wk * lo
            hi_acc[p % n_acc] = hi_acc[p % n_acc] + wk * hi
        lo_sum, hi_sum = sum(lo_acc[1:], lo_acc[0]), sum(hi_acc[1:], hi_acc[0])
        ff = jnp.concatenate([lo_sum[:half] + lo_sum[half:], hi_sum[:half] + hi_sum[half:]], axis=0)
        ff_row = jnp.concatenate([ff[s:s + 1, :] for s in range(2 * half)], axis=1)
        o_ref[pl.ds(t, 1), :] = x_ref[pl.ds(t, 1), :] + g2_ref[0] * ff_row

    def body(tb, carry):
        for u in range(PEER_V_UNROLL):
            token(tb * PEER_V_UNROLL + u, u)
        return carry

    lax.fori_loop(0, PEER_EXPERT_TILE // PEER_V_UNROLL, body, 0)


def _peer_experts(eid, gw, h2, x1, gate2, u_tab, v_tab):
    b_, s_, d_ = x1.shape
    n_tok = b_ * s_
    nk = PEER_HEADS * PEER_TOPK
    tt = PEER_EXPERT_TILE
    nsteps = n_tok // tt
    steps_per_batch = s_ // tt
    tok = pl.BlockSpec((tt, d_), lambda j: (j, 0))
    sel = pl.BlockSpec((tt, nk), lambda j: (j, 0))
    slots = pl.BlockSpec((1, nk, tt), lambda j: (j, 0, 0))
    idx_scratch = [pltpu.SMEM((tt,), jnp.int32) for _ in range(nk)]
    params = pltpu.CompilerParams(dimension_semantics=("arbitrary",), vmem_limit_bytes=PEER_VMEM_LIMIT)
    eid3 = eid.reshape(nsteps, nk, tt)
    w = pl.pallas_call(
        _peer_u_kernel,
        grid=(nsteps,),
        in_specs=[slots, sel, tok, pl.BlockSpec(memory_space=pl.ANY)],
        out_specs=sel,
        out_shape=jax.ShapeDtypeStruct((n_tok, nk), F32),
        scratch_shapes=idx_scratch + [pltpu.VMEM(u_tab.shape, jnp.int32), pltpu.VMEM((nk, tt), F32),
                                      pltpu.SemaphoreType.DMA],
        compiler_params=params,
    )(eid3, gw.reshape(n_tok, nk), h2.reshape(n_tok, d_), u_tab)
    out = pl.pallas_call(
        _peer_v_kernel,
        grid=(nsteps,),
        in_specs=[slots, sel, tok, pl.BlockSpec((1, 1, d_), lambda j: (j // steps_per_batch, 0, 0)),
                  pl.BlockSpec(memory_space=pl.ANY)],
        out_specs=tok,
        out_shape=jax.ShapeDtypeStruct((n_tok, d_), F32),
        scratch_shapes=idx_scratch + [pltpu.VMEM((PEER_V_UNROLL, nk, LANES), F32),
                                      pltpu.VMEM(v_tab.shape, jnp.int32), pltpu.SemaphoreType.DMA],
        compiler_params=params,
    )(eid3, w, x1.reshape(n_tok, d_), gate2, v_tab)
    return out.reshape(b_, s_, d_)


def _permute_w_in(w):
    d_ = w.shape[0]
    kv_end = POOL_WIDTH + Q_WIDTH + 6 * KV_WIDTH
    per_group = HEADS_PER_GROUP * 3
    parts = [w[:, :kv_end], w[:, kv_end + GATE_COLS:]]
    for g in range(N_KV_GROUPS):
        parts.append(w[:, kv_end + g * per_group:kv_end + (g + 1) * per_group])
        parts.append(jnp.zeros((d_, LANES - per_group), w.dtype))
    return jnp.concatenate(parts, axis=1).astype(BF16)


def kernel(x, c, rel_bias, ada_w, ada_b, norm1_g, norm2_g, w_in, pool_w, pool_scale, cmp_pe_k, cmp_w1_k, cmp_w2_k, cmp_pe_v, cmp_w1_v, cmp_w2_v, q_norm_g, k_norm_g, w_branch_pool, w_branch_attn, w_out, peer_w_q, peer_sub_keys, peer_u, peer_v):
    b_, s_, d_ = x.shape
    l = 0
    ada = _ada(c, ada_w[l], ada_b[l]).reshape(b_, 6, 1, d_)
    shift1, scale1, gate1 = ada[:, 0], ada[:, 1], ada[:, 2]
    shift2, scale2, gate2 = ada[:, 3], ada[:, 4], ada[:, 5]

    (z_pool, z_q, ksel, vsel, kwin, vwin, kc_raw, vc_raw, z_merge, z_gate) = _in_proj(
        x, norm1_g[l], scale1, shift1, _permute_w_in(w_in[l]), k_norm_g[l])

    kc = _compress(kc_raw, cmp_pe_k[l], cmp_w1_k[l], cmp_w2_k[l], k_norm_g[l, 0], True)
    vc = _compress(vc_raw, cmp_pe_v[l], cmp_w1_v[l], cmp_w2_v[l], k_norm_g[l, 0], False)
    y_attn = _attention(z_q, z_gate, q_norm_g[l], kc, vc, ksel, vsel, kwin, vwin, rel_bias)

    x1 = _merge(x, z_pool, y_attn, z_merge, gate1, pool_w[l].astype(BF16), pool_scale[l],
                w_branch_pool[l].astype(BF16), w_branch_attn[l].astype(BF16), w_out[l].astype(BF16))

    keys = peer_sub_keys[l].reshape(PEER_HEADS * 2, PEER_KEYS, PEER_QDIM // 2).astype(BF16)
    h2, eid, gw = _peer_route(x1, norm2_g[l], scale2, shift2, peer_w_q[l].astype(BF16), keys)
    return _peer_experts(eid, gw, h2, x1, gate2, _pack_table(peer_u[l]), _pack_table(peer_v[l]))
```

```python
import math
from functools import partial

import jax
import jax.numpy as jnp
import numpy as np
from jax import lax
from jax.experimental import pallas as pl
from jax.experimental.pallas import tpu as pltpu

D_MODEL = 1024
POOL_WIDTH = 512
POOL_WINDOWS = (2, 4, 8, 16)
POOL_GROUP_DIM = 128
N_HEADS = 8
N_KV_GROUPS = 2
HEADS_PER_GROUP = 4
HEAD_DIM = 64
Q_WIDTH = 512
KV_WIDTH = 128
CMP_BLOCK = 32
CMP_STRIDE = 16
CMP_HIDDEN = 128
SEL_BLOCK = 64
SEL_TOPK = 16
WINDOW = 512
FORCE_SCORE = 1000.0
N_BUCKETS = 32
MAX_DISTANCE = 128
PEER_HEADS = 8
PEER_KEYS = 128
PEER_TOPK = 16
PEER_QDIM = 256
PEER_CHUNK = 128
GATE_COLS = N_HEADS * 3
EPS = 1e-6
NEG_INF = -1e30

LANES = 128
VMEM_LIMIT = 48 * 1024 * 1024
ROW_TILE = 512
POOL_HALO = 16
TQ = 128
SEL_FEATS = 64
SEL_CHUNK = 512
V_WIDTH = 128
KV_PAD = 512
CMP_BAND_LO = 10
PEER_TILE = 256
PEER_EXPERT_TILE = 256
PEER_TOKEN_UNROLL = 8
PEER_V_UNROLL = 8
EXPERT_ROWS = D_MODEL // 2 // LANES
TABLE_PAD = 8
PEER_VMEM_LIMIT = 56 * 1024 * 1024

BF16 = jnp.bfloat16
F32 = jnp.float32
_NT = (((1,), (1,)), ((), ()))


def _rms_rows(x, g):
    return x * lax.rsqrt(jnp.mean(x * x, axis=-1, keepdims=True) + EPS) * g


def _in_proj_kernel(x_ref, g_ref, sc_ref, sh_ref, w_ref, kg_ref,
                    zpool_ref, zq_ref, ksel_ref, vsel_ref, kwin_ref, vwin_ref, kcr_ref, vcr_ref,
                    zmerge_ref, zgate_ref):
    j = pl.program_id(1)

    @pl.when(j == 0)
    def _():
        for ref in (ksel_ref, vsel_ref, kwin_ref, vwin_ref):
            ref[...] = jnp.zeros(ref.shape, ref.dtype)

    @pl.when(j > 0)
    def _():
        _in_proj_tile(j - 1, x_ref, g_ref, sc_ref, sh_ref, w_ref, kg_ref, zpool_ref, zq_ref, ksel_ref, vsel_ref,
                      kwin_ref, vwin_ref, kcr_ref, vcr_ref, zmerge_ref, zgate_ref)


def _in_proj_tile(i, x_ref, g_ref, sc_ref, sh_ref, w_ref, kg_ref,
                  zpool_ref, zq_ref, ksel_ref, vsel_ref, kwin_ref, vwin_ref, kcr_ref, vcr_ref,
                  zmerge_ref, zgate_ref):
    x = x_ref[0]
    y = x * lax.rsqrt(jnp.mean(x * x, axis=-1, keepdims=True) + EPS)
    h = y * g_ref[...] * (1.0 + sc_ref[0]) + sh_ref[0]
    z = jnp.dot(h.astype(BF16), w_ref[...], preferred_element_type=F32)
    ts = x.shape[0]
    zpool_ref[0] = z[:, :POOL_WIDTH]
    o = POOL_WIDTH
    zq_ref[0] = z[:, o:o + Q_WIDTH]
    o += Q_WIDTH
    pos = i * ts + lax.broadcasted_iota(jnp.int32, (ts, SEL_FEATS), 0)
    onehot = (pos // SEL_BLOCK == lax.broadcasted_iota(jnp.int32, (ts, SEL_FEATS), 1)).astype(BF16)
    one_col = (lax.broadcasted_iota(jnp.int32, (ts, V_WIDTH - HEAD_DIM), 1) == 0).astype(BF16)
    for g in range(N_KV_GROUPS):
        def col(k):
            return z[:, o + k * KV_WIDTH + g * HEAD_DIM:o + k * KV_WIDTH + (g + 1) * HEAD_DIM]
        kcr_ref[0, g] = col(0)
        vcr_ref[0, g] = col(1)
        ks = _rms_rows(col(2), kg_ref[1:2, :]).astype(BF16)
        ksel_ref[0, g] = jnp.concatenate([ks, onehot], axis=1)
        vsel_ref[0, g] = jnp.concatenate([col(3).astype(BF16), one_col], axis=1)
        kwin_ref[0, g] = _rms_rows(col(4), kg_ref[2:3, :]).astype(BF16)
        vwin_ref[0, g] = jnp.concatenate([col(5).astype(BF16), one_col], axis=1)
    o += 6 * KV_WIDTH
    zmerge_ref[0] = z[:, o:o + 2 * D_MODEL]
    o += 2 * D_MODEL
    zgate_ref[0] = jax.nn.sigmoid(z[:, o:o + N_KV_GROUPS * LANES])


def _in_proj(x, g, scale, shift, w_bf16, k_norm_g):
    b_, s_, d_ = x.shape
    n = w_bf16.shape[1]
    assert KV_PAD == ROW_TILE
    tile = lambda j: jnp.maximum(j - 1, 0)
    row = lambda w: pl.BlockSpec((1, ROW_TILE, w), lambda b, j: (b, tile(j), 0))
    grp = lambda w: pl.BlockSpec((1, N_KV_GROUPS, ROW_TILE, w), lambda b, j: (b, 0, tile(j), 0))
    padded = lambda w: pl.BlockSpec((1, N_KV_GROUPS, ROW_TILE, w), lambda b, j: (b, 0, j, 0))
    gshape = lambda w, dt: jax.ShapeDtypeStruct((b_, N_KV_GROUPS, s_, w), dt)
    pshape = lambda w, dt: jax.ShapeDtypeStruct((b_, N_KV_GROUPS, s_ + KV_PAD, w), dt)
    return pl.pallas_call(
        _in_proj_kernel,
        grid=(b_, s_ // ROW_TILE + 1),
        in_specs=[
            row(d_),
            pl.BlockSpec((1, d_), lambda b, i: (0, 0)),
            pl.BlockSpec((1, 1, d_), lambda b, i: (b, 0, 0)),
            pl.BlockSpec((1, 1, d_), lambda b, i: (b, 0, 0)),
            pl.BlockSpec((d_, n), lambda b, i: (0, 0)),
            pl.BlockSpec((3, HEAD_DIM), lambda b, i: (0, 0)),
        ],
        out_specs=[row(POOL_WIDTH), row(Q_WIDTH), padded(HEAD_DIM + SEL_FEATS), padded(V_WIDTH), padded(HEAD_DIM),
                   padded(V_WIDTH), grp(HEAD_DIM), grp(HEAD_DIM), row(2 * d_), row(N_KV_GROUPS * LANES)],
        out_shape=[
            jax.ShapeDtypeStruct((b_, s_, POOL_WIDTH), F32),
            jax.ShapeDtypeStruct((b_, s_, Q_WIDTH), F32),
            pshape(HEAD_DIM + SEL_FEATS, BF16), pshape(V_WIDTH, BF16), pshape(HEAD_DIM, BF16), pshape(V_WIDTH, BF16),
            gshape(HEAD_DIM, F32), gshape(HEAD_DIM, F32),
            jax.ShapeDtypeStruct((b_, s_, 2 * d_), F32),
            jax.ShapeDtypeStruct((b_, s_, N_KV_GROUPS * LANES), F32),
        ],
        compiler_params=pltpu.CompilerParams(
            dimension_semantics=("parallel", "arbitrary"), vmem_limit_bytes=VMEM_LIMIT),
    )(x, g.reshape(1, d_), scale, shift, w_bf16, k_norm_g)


def _compress_kernel(t_ref, pe_ref, w1_ref, w2_ref, g_ref, o_ref, *, normalize):
    half = CMP_STRIDE * HEAD_DIM
    n = t_ref.shape[2] // CMP_STRIDE
    w1 = w1_ref[...]
    a = jnp.zeros((n, CMP_HIDDEN), F32)
    b = jnp.zeros((n, CMP_HIDDEN), F32)
    for r in range(CMP_STRIDE):
        t_r = t_ref[0, 0, pl.ds(r, n, stride=CMP_STRIDE), :].astype(BF16)
        a = a + jnp.dot(t_r, w1[r * HEAD_DIM:(r + 1) * HEAD_DIM], preferred_element_type=F32)
        b = b + jnp.dot(t_r, w1[half + r * HEAD_DIM:half + (r + 1) * HEAD_DIM], preferred_element_type=F32)
    pe = jnp.dot(pe_ref[...], w1, preferred_element_type=F32)
    b_next = pltpu.roll(b, n - 1, axis=0)
    hid = jax.nn.gelu(a + b_next + pe)
    out = jnp.dot(hid.astype(BF16), w2_ref[...], preferred_element_type=F32)
    if normalize:
        out = _rms_rows(out, g_ref[...])
    o_ref[0, 0] = out.astype(BF16)


def _compress(t_raw, pe, w1, w2, g, normalize):
    b_, g_, s_, dh = t_raw.shape
    n_str = s_ // CMP_STRIDE
    return pl.pallas_call(
        partial(_compress_kernel, normalize=normalize),
        grid=(b_, g_),
        in_specs=[
            pl.BlockSpec((1, 1, s_, dh), lambda b, g: (b, g, 0, 0)),
            pl.BlockSpec((1, CMP_BLOCK * dh), lambda b, g: (0, 0)),
            pl.BlockSpec((CMP_BLOCK * dh, CMP_HIDDEN), lambda b, g: (0, 0)),
            pl.BlockSpec((CMP_HIDDEN, dh), lambda b, g: (0, 0)),
            pl.BlockSpec((1, dh), lambda b, g: (0, 0)),
        ],
        out_specs=pl.BlockSpec((1, 1, n_str, dh), lambda b, g: (b, g, 0, 0)),
        out_shape=jax.ShapeDtypeStruct((b_, g_, n_str, dh), BF16),
        compiler_params=pltpu.CompilerParams(
            dimension_semantics=("parallel", "parallel"), vmem_limit_bytes=VMEM_LIMIT),
    )(t_raw, pe.reshape(1, CMP_BLOCK * dh).astype(BF16), w1.astype(BF16), w2.astype(BF16), g.reshape(1, dh))


def _t5_bucket_np(rel):
    n = np.maximum(rel, 0)
    max_exact = N_BUCKETS // 2
    nf = np.maximum(n, 1).astype(np.float32)
    large = max_exact + (np.log(nf / max_exact) / math.log(MAX_DISTANCE / max_exact) * (N_BUCKETS - max_exact)).astype(np.int32)
    large = np.minimum(large, N_BUCKETS - 1)
    return np.where(n < max_exact, n, large)


def _bias_tiles(rel_bias, n_cmp_pad):
    far = rel_bias[N_BUCKETS - 1]
    ii = np.arange(TQ)[:, None]

    def lookup(dist):
        near = (dist >= 0) & (dist < MAX_DISTANCE)
        bucket = _t5_bucket_np(np.where(near, dist, MAX_DISTANCE))
        onehot = (jnp.asarray(bucket.astype(np.int8))[..., None] == jnp.arange(N_BUCKETS, dtype=jnp.int8)).astype(F32)
        b = jnp.dot(onehot, rel_bias - far, precision=lax.Precision.HIGHEST)
        b = jnp.transpose(b, (2, 0, 1))
        return b.reshape(N_KV_GROUPS, HEADS_PER_GROUP * dist.shape[0], dist.shape[1])

    jj = np.arange(TQ)[None, :]
    bt = jnp.stack([lookup(ii - jj), lookup(TQ + ii - jj)], axis=1)
    cc = np.arange(n_cmp_pad)[None, :]
    dist_c = ii - CMP_STRIDE * (cc - CMP_BAND_LO) - (CMP_BLOCK - 1)
    dist_c = np.where(cc <= CMP_BAND_LO + TQ // CMP_STRIDE, dist_c, -1)
    bc = lookup(dist_c)
    rows = HEADS_PER_GROUP * TQ
    i_r = np.tile(np.arange(TQ), HEADS_PER_GROUP)[:, None]
    first = jnp.asarray(np.where(jj > i_r, 0.0, NEG_INF), F32)
    last = jnp.where(jnp.asarray(jj <= i_r), bt[:, 0], NEG_INF)
    zeros = jnp.zeros((N_KV_GROUPS, rows, WINDOW - 2 * TQ), F32)
    wa = jnp.concatenate([jnp.broadcast_to(first, (N_KV_GROUPS, rows, TQ)), zeros, bt[:, 1], last], axis=-1)
    return wa, bc


def _sel_mapping_t(n_cmp_pad, n_cmp):
    m = np.zeros((SEL_FEATS, n_cmp_pad), np.float32)
    pos = np.arange(n_cmp)[:, None] * CMP_STRIDE + np.arange(CMP_BLOCK)[None, :]
    np.add.at(m, ((pos // SEL_BLOCK).ravel(), np.repeat(np.arange(n_cmp), CMP_BLOCK)), 1.0 / CMP_BLOCK)
    return m


def _logits(q, k, add=None, lo_cols=None):
    s = lax.dot_general(q, k, _NT, preferred_element_type=F32)
    if add is not None:
        s = s + add
    if lo_cols is not None:
        s = jnp.where(lax.broadcasted_iota(jnp.int32, s.shape, 1) < lo_cols, NEG_INF, s)
    return s


def _flash_step(s, v, m_sc, acc_sc):
    m_prev = m_sc[...]
    m_new = jnp.maximum(m_prev, jnp.max(s, axis=-1, keepdims=True))
    alpha = jnp.exp(m_prev - m_new)
    p = jnp.exp(s - jnp.concatenate([m_new] * (s.shape[1] // LANES), axis=1))
    acc_sc[...] = alpha * acc_sc[...] + jnp.dot(p.astype(BF16), v, preferred_element_type=F32)
    m_sc[...] = m_new


def _attn_kernel(zq_ref, gate_ref, qg_ref, kc_ref, vc_ref, ksel_ref, vsel_ref, kwin_ref, vwin_ref,
                 wa_ref, bc_ref, smap_ref, o_ref, m_sc, acc_sc):
    i = pl.program_id(2)
    rows = HEADS_PER_GROUP * TQ
    ncp = kc_ref.shape[2]
    scale = HEAD_DIM ** -0.5

    zq = zq_ref[0]
    qs = jnp.concatenate([zq[:, j * HEAD_DIM:(j + 1) * HEAD_DIM] for j in range(HEADS_PER_GROUP)], axis=0)
    qn = _rms_rows(qs, qg_ref[...]) * scale
    qb = qn.astype(BF16)

    lc = lax.dot_general(qb, kc_ref[0, 0], _NT, preferred_element_type=F32)
    lc = lc + pltpu.roll(bc_ref[0], (i * (TQ // CMP_STRIDE) + ncp - CMP_BAND_LO) % ncp, axis=1)
    t_c = i * TQ + lax.broadcasted_iota(jnp.int32, (rows, ncp), 0) % TQ
    n_c = lax.broadcasted_iota(jnp.int32, (rows, ncp), 1)
    valid_c = n_c * CMP_STRIDE + (CMP_BLOCK - 1) <= t_c
    lm = jnp.where(valid_c, lc, NEG_INF)
    e = jnp.where(valid_c, jnp.exp(lm - jnp.max(lm, axis=-1, keepdims=True)), 0.0)
    den = jnp.sum(e, axis=-1, keepdims=True)
    p_c = e / jnp.where(den > 0.0, den, 1.0)
    o_cmp = jnp.dot(p_c.astype(BF16), vc_ref[0, 0], preferred_element_type=F32)

    p_sum = p_c[0:TQ] + p_c[TQ:2 * TQ] + p_c[2 * TQ:3 * TQ] + p_c[3 * TQ:4 * TQ]
    p_hi = p_sum.astype(BF16)
    r1 = p_sum - p_hi.astype(F32)
    p_mid = r1.astype(BF16)
    p_lo = (r1 - p_mid.astype(F32)).astype(BF16)
    smap = smap_ref[...]
    imp = (lax.dot_general(smap, p_hi, _NT, preferred_element_type=F32)
           + lax.dot_general(smap, p_mid, _NT, preferred_element_type=F32)
           + lax.dot_general(smap, p_lo, _NT, preferred_element_type=F32))
    blk = lax.broadcasted_iota(jnp.int32, (SEL_FEATS, TQ), 0)
    t_s = i * TQ + lax.broadcasted_iota(jnp.int32, (SEL_FEATS, TQ), 1)
    cur = t_s // SEL_BLOCK
    forced = (blk == 0) | (blk == cur) | (blk == cur - 1)
    visible = blk * SEL_BLOCK <= t_s
    score = jnp.where(visible, imp + jnp.where(forced, FORCE_SCORE, 0.0), -1.0)
    rank = jnp.zeros((SEL_FEATS, TQ), jnp.int32)
    for sp in range(SEL_FEATS):
        row = score[sp:sp + 1, :]
        beats = (row > score) | ((row == score) & (blk > sp))
        rank = rank + beats.astype(jnp.int32)
    pen_t = jnp.where(rank < SEL_TOPK, 0.0, NEG_INF)
    pen = jnp.transpose(pen_t).astype(BF16)
    q_aug = jnp.concatenate([qb, jnp.concatenate([pen] * HEADS_PER_GROUP, axis=0)], axis=1)

    def keys(ref, pos, n):
        return ref[0, 0, pl.ds(pl.multiple_of(pos + KV_PAD, TQ), n), :]

    m_sc[...] = jnp.full((rows, LANES), NEG_INF, F32)
    acc_sc[...] = jnp.zeros((rows, V_WIDTH), F32)
    per_chunk = SEL_CHUNK // TQ
    n_plain = jnp.maximum(i - 1, 0)
    rem = n_plain % per_chunk

    @pl.when(rem > 0)
    def _():
        pos = (rem - per_chunk) * TQ
        s = _logits(q_aug, keys(ksel_ref, pos, SEL_CHUNK), lo_cols=-pos)
        _flash_step(s, keys(vsel_ref, pos, SEL_CHUNK), m_sc, acc_sc)

    def sel_body(c, carry):
        pos = (rem + c * per_chunk) * TQ
        _flash_step(_logits(q_aug, keys(ksel_ref, pos, SEL_CHUNK)), keys(vsel_ref, pos, SEL_CHUNK), m_sc, acc_sc)
        return carry

    lax.fori_loop(0, n_plain // per_chunk, sel_body, 0)
    pos = (i - 1) * TQ
    s = _logits(q_aug, keys(ksel_ref, pos, 2 * TQ), add=wa_ref[0, :, WINDOW - TQ:], lo_cols=-pos)
    _flash_step(s, keys(vsel_ref, pos, 2 * TQ), m_sc, acc_sc)
    acc = acc_sc[...]
    o_sel = acc[:, :HEAD_DIM] / acc[:, HEAD_DIM:HEAD_DIM + 1]

    pos = i * TQ - WINDOW
    s = _logits(qb, keys(kwin_ref, pos, WINDOW + TQ), add=wa_ref[0], lo_cols=-pos)
    p = jnp.exp(s - jnp.max(s, axis=-1, keepdims=True))
    acc = jnp.dot(p.astype(BF16), keys(vwin_ref, pos, WINDOW + TQ), preferred_element_type=F32)
    o_win = acc[:, :HEAD_DIM] / acc[:, HEAD_DIM:HEAD_DIM + 1]

    gate = gate_ref[0]
    outs = []
    for j in range(HEADS_PER_GROUP):
        sl = slice(j * TQ, (j + 1) * TQ)
        outs.append(gate[:, 3 * j:3 * j + 1] * o_cmp[sl] + gate[:, 3 * j + 1:3 * j + 2] * o_sel[sl]
                    + gate[:, 3 * j + 2:3 * j + 3] * o_win[sl])
    o_ref[0] = jnp.concatenate(outs, axis=1)


def _attention(z_q, z_gate, q_norm_g, kc, vc, ksel, vsel, kwin, vwin, rel_bias):
    b_, s_, _ = z_q.shape
    ncp = kc.shape[2]
    n_cmp = ncp - CMP_BLOCK // CMP_STRIDE + 1
    assert s_ // SEL_BLOCK <= SEL_FEATS and s_ % TQ == 0
    wa, bc = _bias_tiles(rel_bias, ncp)
    smap_t = jnp.asarray(_sel_mapping_t(ncp, n_cmp), BF16)
    rows = HEADS_PER_GROUP * TQ
    gw = HEADS_PER_GROUP * HEAD_DIM
    kv = lambda w: pl.BlockSpec((1, 1, s_ + KV_PAD, w), lambda b, g, i: (b, g, 0, 0))
    cm = pl.BlockSpec((1, 1, ncp, HEAD_DIM), lambda b, g, i: (b, g, 0, 0))
    return pl.pallas_call(
        _attn_kernel,
        grid=(b_, N_KV_GROUPS, s_ // TQ),
        in_specs=[
            pl.BlockSpec((1, TQ, gw), lambda b, g, i: (b, i, g)),
            pl.BlockSpec((1, TQ, LANES), lambda b, g, i: (b, i, g)),
            pl.BlockSpec((1, HEAD_DIM), lambda b, g, i: (0, 0)),
            cm, cm, kv(HEAD_DIM + SEL_FEATS), kv(V_WIDTH), kv(HEAD_DIM), kv(V_WIDTH),
            pl.BlockSpec((1, rows, WINDOW + TQ), lambda b, g, i: (g, 0, 0)),
            pl.BlockSpec((1, rows, ncp), lambda b, g, i: (g, 0, 0)),
            pl.BlockSpec((SEL_FEATS, ncp), lambda b, g, i: (0, 0)),
        ],
        out_specs=pl.BlockSpec((1, TQ, gw), lambda b, g, i: (b, i, g)),
        out_shape=jax.ShapeDtypeStruct((b_, s_, N_KV_GROUPS * gw), F32),
        scratch_shapes=[pltpu.VMEM((rows, LANES), F32), pltpu.VMEM((rows, V_WIDTH), F32)],
        compiler_params=pltpu.CompilerParams(
            dimension_semantics=("parallel", "parallel", "arbitrary"), vmem_limit_bytes=VMEM_LIMIT),
    )(z_q, z_gate, q_norm_g.reshape(1, HEAD_DIM), kc, vc, ksel, vsel, kwin, vwin, wa, bc, smap_t)


def _pool_tile(halo, tile, i, pw_ref, ps_ref):
    ts = tile.shape[0]
    ext = jnp.concatenate([jnp.where(i > 0, halo, 0.0), tile], axis=0)
    t = i * ts + lax.broadcasted_iota(jnp.int32, (ts, 1), 0)
    outs = []
    for gi, w in enumerate(POOL_WINDOWS):
        sl = slice(gi * POOL_GROUP_DIM, (gi + 1) * POOL_GROUP_DIM)
        run = ext[:, sl]
        span = 1
        while span < w:
            run = run + pltpu.roll(run, span, axis=0)
            span *= 2
        cnt = jnp.minimum(t + 1, w).astype(F32)
        pooled = run[POOL_HALO:] / cnt - tile[:, sl]
        outs.append(jnp.dot(pooled.astype(BF16), pw_ref[gi], preferred_element_type=F32))
    return jnp.concatenate(outs, axis=1) * ps_ref[...]


def _merge_kernel(x_ref, zp_ref, zh_ref, ya_ref, zm_ref, g1_ref, pw_ref, ps_ref, wbp_ref, wba_ref, wo_ref, o_ref):
    d_ = x_ref.shape[-1]
    y_pool = _pool_tile(zh_ref[0], zp_ref[0], pl.program_id(1), pw_ref, ps_ref)
    bp = jnp.dot(y_pool.astype(BF16), wbp_ref[...], preferred_element_type=F32)
    ba = jnp.dot(ya_ref[0].astype(BF16), wba_ref[...], preferred_element_type=F32)
    zm = zm_ref[0]
    mixed = jax.nn.sigmoid(zm[:, :d_]) * bp + jax.nn.sigmoid(zm[:, d_:]) * ba
    proj = jnp.dot(mixed.astype(BF16), wo_ref[...], preferred_element_type=F32)
    o_ref[0] = x_ref[0] + g1_ref[0] * proj


def _merge(x, z_pool, y_attn, z_merge, gate1, pool_w, pool_scale, wbp, wba, wo):
    b_, s_, d_ = x.shape
    row = lambda w: pl.BlockSpec((1, ROW_TILE, w), lambda b, i: (b, i, 0))
    full = lambda a: pl.BlockSpec(a.shape, lambda b, i: (0,) * a.ndim)
    halo_blocks = ROW_TILE // POOL_HALO
    halo = pl.BlockSpec((1, POOL_HALO, POOL_WIDTH), lambda b, i: (b, jnp.maximum(i * halo_blocks - 1, 0), 0))
    pool_scale = pool_scale.reshape(1, POOL_WIDTH)
    return pl.pallas_call(
        _merge_kernel,
        grid=(b_, s_ // ROW_TILE),
        in_specs=[row(d_), row(POOL_WIDTH), halo, row(Q_WIDTH), row(2 * d_),
                  pl.BlockSpec((1, 1, d_), lambda b, i: (b, 0, 0)),
                  full(pool_w), full(pool_scale), full(wbp), full(wba), full(wo)],
        out_specs=row(d_),
        out_shape=jax.ShapeDtypeStruct((b_, s_, d_), F32),
        compiler_params=pltpu.CompilerParams(
            dimension_semantics=("parallel", "parallel"), vmem_limit_bytes=VMEM_LIMIT),
    )(x, z_pool, z_pool, y_attn, z_merge, gate1, pool_w, pool_scale, wbp, wba, wo)


def _ada_kernel(c_ref, w_ref, b_ref, o_ref):
    c = c_ref[...]
    o_ref[...] = jnp.dot(jax.nn.silu(c).astype(BF16), w_ref[...].astype(BF16),
                         preferred_element_type=F32) + b_ref[...]


def _ada(c, w, b):
    b_, d_ = c.shape
    n = w.shape[1]
    return pl.pallas_call(
        _ada_kernel,
        grid=(n // d_,),
        in_specs=[pl.BlockSpec((b_, d_), lambda j: (0, 0)),
                  pl.BlockSpec((d_, d_), lambda j: (0, j)),
                  pl.BlockSpec((1, d_), lambda j: (0, j))],
        out_specs=pl.BlockSpec((b_, d_), lambda j: (0, j)),
        out_shape=jax.ShapeDtypeStruct((b_, n), F32),
        compiler_params=pltpu.CompilerParams(dimension_semantics=("parallel",), vmem_limit_bytes=VMEM_LIMIT),
    )(c, w, b.reshape(1, n))


_CAND_ROWS = PEER_TOPK + 7 * 8 + 8


def _topk_rows(x, k):
    n = x.shape[0]
    rid = lax.broadcasted_iota(jnp.int32, x.shape, 0)
    vals, idxs = [], []
    for _ in range(k):
        m = jnp.max(x, axis=0, keepdims=True)
        idx = jnp.min(jnp.where(x == m, rid, n), axis=0, keepdims=True)
        vals.append(m)
        idxs.append(idx)
        x = jnp.where(rid == idx, -jnp.inf, x)
    return jnp.concatenate(vals, axis=0), jnp.concatenate(idxs, axis=0)


def _pair_grid(r0, r1, combine):
    parts = [combine(r0[0:1], r1)]
    parts += [combine(r0[a:a + 1], r1[0:8]) for a in range(1, 8)]
    parts.append(combine(r0[8:16], r1[0:1]))
    return jnp.concatenate(parts, axis=0)


def _peer_route_kernel(x_ref, g_ref, sc_ref, sh_ref, w_ref, keys_ref, h_ref, eid_ref, gw_ref):
    x = x_ref[0]
    y = x * lax.rsqrt(jnp.mean(x * x, axis=-1, keepdims=True) + EPS)
    h = y * g_ref[...] * (1.0 + sc_ref[0]) + sh_ref[0]
    h_ref[0] = h
    qv = jnp.dot(h.astype(BF16), w_ref[...], preferred_element_type=F32).astype(BF16)
    half = PEER_QDIM // 2
    eids, gws = [], []
    for hd in range(PEER_HEADS):
        tops = []
        for p in range(2):
            c = hd * 2 + p
            sc = lax.dot_general(keys_ref[c], qv[:, c * half:(c + 1) * half], _NT,
                                 preferred_element_type=F32)
            tops.append(_topk_rows(sc, PEER_TOPK))
        (s0, i0), (s1, i1) = tops
        cand = _pair_grid(s0, s1, lambda a, b: a + b)
        cid = _pair_grid(i0, i1, lambda a, b: a * (PEER_KEYS * EXPERT_ROWS) + (b * EXPERT_ROWS + TABLE_PAD))
        rid = lax.broadcasted_iota(jnp.int32, cand.shape, 0)
        top_s, top_e = [], []
        for _ in range(PEER_TOPK):
            m = jnp.max(cand, axis=0, keepdims=True)
            idx = jnp.min(jnp.where(cand == m, rid, _CAND_ROWS), axis=0, keepdims=True)
            hit = rid == idx
            top_s.append(m)
            top_e.append(jnp.sum(jnp.where(hit, cid, 0), axis=0, keepdims=True))
            cand = jnp.where(hit, -jnp.inf, cand)
        top_s = jnp.concatenate(top_s, axis=0)
        e = jnp.exp(top_s - top_s[0:1])
        gws.append(e / jnp.sum(e, axis=0, keepdims=True))
        eids.append(jnp.concatenate(top_e, axis=0))
    eids = jnp.concatenate(eids, axis=0)
    for c in range(PEER_TILE // PEER_EXPERT_TILE):
        eid_ref[c] = eids[:, c * PEER_EXPERT_TILE:(c + 1) * PEER_EXPERT_TILE]
    gw_ref[...] = jnp.transpose(jnp.concatenate(gws, axis=0))


def _peer_route(x, g, scale, shift, w_bf16, keys_bf16):
    b_, s_, d_ = x.shape
    n = w_bf16.shape[1]
    nk = PEER_HEADS * PEER_TOPK
    per_tile = PEER_TILE // PEER_EXPERT_TILE
    tiles = s_ // PEER_TILE
    row = lambda w: pl.BlockSpec((1, PEER_TILE, w), lambda b, i: (b, i, 0))
    return pl.pallas_call(
        _peer_route_kernel,
        grid=(b_, s_ // PEER_TILE),
        in_specs=[
            row(d_),
            pl.BlockSpec((1, d_), lambda b, i: (0, 0)),
            pl.BlockSpec((1, 1, d_), lambda b, i: (b, 0, 0)),
            pl.BlockSpec((1, 1, d_), lambda b, i: (b, 0, 0)),
            pl.BlockSpec((d_, n), lambda b, i: (0, 0)),
            pl.BlockSpec(keys_bf16.shape, lambda b, i: (0, 0, 0)),
        ],
        out_specs=[row(d_),
                   pl.BlockSpec((per_tile, nk, PEER_EXPERT_TILE), lambda b, i: (b * tiles + i, 0, 0)),
                   pl.BlockSpec((PEER_TILE, nk), lambda b, i: (b * tiles + i, 0))],
        out_shape=[
            jax.ShapeDtypeStruct((b_, s_, d_), F32),
            jax.ShapeDtypeStruct((b_ * s_ // PEER_EXPERT_TILE, nk, PEER_EXPERT_TILE), jnp.int32),
            jax.ShapeDtypeStruct((b_ * s_, nk), F32),
        ],
        compiler_params=pltpu.CompilerParams(
            dimension_semantics=("parallel", "parallel"), vmem_limit_bytes=VMEM_LIMIT),
    )(x, g.reshape(1, d_), scale, shift, w_bf16, keys_bf16)


_HI_MASK = -65536


def _pack_table(t):
    e_, d_ = t.shape
    bits = lax.bitcast_convert_type(t.astype(jnp.bfloat16), jnp.uint16).astype(jnp.uint32)
    words = (bits[:, d_ // 2:] << 16) | bits[:, :d_ // 2]
    return lax.bitcast_convert_type(words, jnp.int32).reshape(e_ * EXPERT_ROWS, LANES)


def _expert_words(tab_vmem, row):
    return tab_vmem[pl.ds(pl.multiple_of(row, EXPERT_ROWS), EXPERT_ROWS), :]


def _expert_pair_words(tab_vmem, row_a, row_b, sub):
    top = tab_vmem[pl.ds(pl.multiple_of(row_a, EXPERT_ROWS), 2 * EXPERT_ROWS), :]
    bot = tab_vmem[pl.ds(pl.multiple_of(row_b - EXPERT_ROWS, EXPERT_ROWS), 2 * EXPERT_ROWS), :]
    return jnp.where(sub < EXPERT_ROWS, top, bot)


def _unpack(words):
    lo = lax.bitcast_convert_type(lax.shift_left(words, 16), F32)
    hi = lax.bitcast_convert_type(words & _HI_MASK, F32)
    return lo, hi


def _load_resident(j, tab_hbm, tab_vmem, eid_ref, idx_smem, sem):
    @pl.when(j == 0)
    def _():
        n_rows = tab_hbm.shape[0]
        zeros = jnp.zeros((TABLE_PAD, LANES), jnp.int32)
        tab_vmem[pl.ds(0, TABLE_PAD), :] = zeros
        tab_vmem[pl.ds(TABLE_PAD + n_rows, TABLE_PAD), :] = zeros
        cp = pltpu.make_async_copy(tab_hbm, tab_vmem.at[pl.ds(TABLE_PAD, n_rows)], sem)
        cp.start()
        cp.wait()

    copies = [pltpu.make_async_copy(eid_ref.at[0, k], idx_smem[k], sem) for k in range(len(idx_smem))]
    for cp in copies:
        cp.start()
    for cp in copies:
        cp.wait()


def _fold_pairs(q, sub):
    m2, m1 = (sub & 2) == 0, (sub & 1) == 0
    u = [jnp.where(m2, q[i], q[i + 2]) + jnp.where(m2, pltpu.roll(q[i], 6, axis=0), pltpu.roll(q[i + 2], 2, axis=0))
         for i in range(2)]
    return jnp.where(m1, u[0], u[1]) + jnp.where(m1, pltpu.roll(u[0], 7, axis=0), pltpu.roll(u[1], 1, axis=0))


def _peer_u_kernel(eid_ref, gw_ref, h_ref, tab_hbm, w_ref, *scratch):
    j = pl.program_id(0)
    nk = PEER_HEADS * PEER_TOPK
    idx_smem, (tab_vmem, acc_ref, sem) = scratch[:nk], scratch[nk:]
    half = D_MODEL // 2 // LANES
    _load_resident(j, tab_hbm, tab_vmem, eid_ref, idx_smem, sem)
    sub = lax.broadcasted_iota(jnp.int32, (2 * half, LANES), 0)
    lane = lax.broadcasted_iota(jnp.int32, (nk, PEER_EXPERT_TILE), 1)
    acc_ref[...] = jnp.zeros((nk, PEER_EXPERT_TILE), F32)

    def token(t):
        row = h_ref[pl.ds(t, 1), :]

        def seg(s):
            return jnp.broadcast_to(row[:, s * LANES:(s + 1) * LANES], (2 * half, LANES))

        def rows_of(first):
            out = seg(first + half - 1)
            for s in range(half - 2, -1, -1):
                out = jnp.where(sub % half == s, seg(first + s), out)
            return out

        x_lo, x_hi = rows_of(0), rows_of(half)
        folded = []
        for gi in range(nk // 8):
            q = []
            for i in range(4):
                k = gi * 8 + i
                words = _expert_pair_words(tab_vmem, idx_smem[k][t], idx_smem[k + 4][t], sub)
                lo, hi = _unpack(words)
                q.append(lo * x_lo + hi * x_hi)
            folded.append(_fold_pairs(q, sub))
        return jnp.sum(jnp.concatenate(folded, axis=0), axis=-1, keepdims=True)

    def body(tb, carry):
        acc = acc_ref[...]
        for u in range(PEER_TOKEN_UNROLL):
            t = tb * PEER_TOKEN_UNROLL + u
            acc = jnp.where(lane == t, token(t), acc)
        acc_ref[...] = acc
        return carry

    lax.fori_loop(0, PEER_EXPERT_TILE // PEER_TOKEN_UNROLL, body, 0)
    w_ref[...] = jax.nn.gelu(jnp.transpose(acc_ref[...])) * gw_ref[...]


def _peer_v_kernel(eid_ref, w_ref, x_ref, g2_ref, tab_hbm, o_ref, *scratch):
    j = pl.program_id(0)
    nk = PEER_HEADS * PEER_TOPK
    idx_smem, (wb_ref, tab_vmem, sem) = scratch[:nk], scratch[nk:]
    half = D_MODEL // 2 // LANES
    _load_resident(j, tab_hbm, tab_vmem, eid_ref, idx_smem, sem)
    n_acc = 4
    sub = lax.broadcasted_iota(jnp.int32, (2 * half, LANES), 0)

    def token(t, slot):
        wb_ref[slot] = jnp.transpose(jnp.broadcast_to(w_ref[pl.ds(t, 1), :], (nk, nk)))
        lo_acc = [jnp.zeros((2 * half, LANES), F32) for _ in range(n_acc)]
        hi_acc = [jnp.zeros((2 * half, LANES), F32) for _ in range(n_acc)]
        for p in range(nk // 2):
            k = 2 * p
            lo, hi = _unpack(_expert_pair_words(tab_vmem, idx_smem[k][t], idx_smem[k + 1][t], sub))
            wk = jnp.where(sub < half, wb_ref[slot, k:k + 1, :], wb_ref[slot, k + 1:k + 2, :])
            lo_acc[p % n_acc] = lo_acc[p % n_acc] + wk * lo
            hi_acc[p % n_acc] = hi_acc[p % n_acc] + wk * hi
        lo_sum, hi_sum = sum(lo_acc[1:], lo_acc[0]), sum(hi_acc[1:], hi_acc[0])
        ff = jnp.concatenate([lo_sum[:half] + lo_sum[half:], hi_sum[:half] + hi_sum[half:]], axis=0)
        ff_row = jnp.concatenate([ff[s:s + 1, :] for s in range(2 * half)], axis=1)
        o_ref[pl.ds(t, 1), :] = x_ref[pl.ds(t, 1), :] + g2_ref[0] * ff_row

    def body(tb, carry):
        for u in range(PEER_V_UNROLL):
            token(tb * PEER_V_UNROLL + u, u)
        return carry

    lax.fori_loop(0, PEER_EXPERT_TILE // PEER_V_UNROLL, body, 0)


def _peer_experts(eid, gw, h2, x1, gate2, u_tab, v_tab):
    b_, s_, d_ = x1.shape
    n_tok = b_ * s_
    nk = PEER_HEADS * PEER_TOPK
    tt = PEER_EXPERT_TILE
    nsteps = n_tok // tt
    steps_per_batch = s_ // tt
    tok = pl.BlockSpec((tt, d_), lambda j: (j, 0))
    sel = pl.BlockSpec((tt, nk), lambda j: (j, 0))
    slots = pl.BlockSpec((1, nk, tt), lambda j: (j, 0, 0))
    idx_scratch = [pltpu.SMEM((tt,), jnp.int32) for _ in range(nk)]
    params = pltpu.CompilerParams(dimension_semantics=("arbitrary",), vmem_limit_bytes=PEER_VMEM_LIMIT)
    eid3 = eid.reshape(nsteps, nk, tt)
    resident = (u_tab.shape[0] + 2 * TABLE_PAD, LANES)
    w = pl.pallas_call(
        _peer_u_kernel,
        grid=(nsteps,),
        in_specs=[slots, sel, tok, pl.BlockSpec(memory_space=pl.ANY)],
        out_specs=sel,
        out_shape=jax.ShapeDtypeStruct((n_tok, nk), F32),
        scratch_shapes=idx_scratch + [pltpu.VMEM(resident, jnp.int32), pltpu.VMEM((nk, tt), F32),
                                      pltpu.SemaphoreType.DMA],
        compiler_params=params,
    )(eid3, gw.reshape(n_tok, nk), h2.reshape(n_tok, d_), u_tab)
    out = pl.pallas_call(
        _peer_v_kernel,
        grid=(nsteps,),
        in_specs=[slots, sel, tok, pl.BlockSpec((1, 1, d_), lambda j: (j // steps_per_batch, 0, 0)),
                  pl.BlockSpec(memory_space=pl.ANY)],
        out_specs=tok,
        out_shape=jax.ShapeDtypeStruct((n_tok, d_), F32),
        scratch_shapes=idx_scratch + [pltpu.VMEM((PEER_V_UNROLL, nk, LANES), F32),
                                      pltpu.VMEM(resident, jnp.int32), pltpu.SemaphoreType.DMA],
        compiler_params=params,
    )(eid3, w, x1.reshape(n_tok, d_), gate2, v_tab)
    return out.reshape(b_, s_, d_)


def _permute_w_in(w):
    d_ = w.shape[0]
    kv_end = POOL_WIDTH + Q_WIDTH + 6 * KV_WIDTH
    per_group = HEADS_PER_GROUP * 3
    parts = [w[:, :kv_end], w[:, kv_end + GATE_COLS:]]
    for g in range(N_KV_GROUPS):
        parts.append(w[:, kv_end + g * per_group:kv_end + (g + 1) * per_group])
        parts.append(jnp.zeros((d_, LANES - per_group), w.dtype))
    return jnp.concatenate(parts, axis=1).astype(BF16)


def kernel(x, c, rel_bias, ada_w, ada_b, norm1_g, norm2_g, w_in, pool_w, pool_scale, cmp_pe_k, cmp_w1_k, cmp_w2_k, cmp_pe_v, cmp_w1_v, cmp_w2_v, q_norm_g, k_norm_g, w_branch_pool, w_branch_attn, w_out, peer_w_q, peer_sub_keys, peer_u, peer_v):
    b_, s_, d_ = x.shape
    l = 0
    ada = _ada(c, ada_w[l], ada_b[l]).reshape(b_, 6, 1, d_)
    shift1, scale1, gate1 = ada[:, 0], ada[:, 1], ada[:, 2]
    shift2, scale2, gate2 = ada[:, 3], ada[:, 4], ada[:, 5]

    (z_pool, z_q, ksel, vsel, kwin, vwin, kc_raw, vc_raw, z_merge, z_gate) = _in_proj(
        x, norm1_g[l], scale1, shift1, _permute_w_in(w_in[l]), k_norm_g[l])

    kc = _compress(kc_raw, cmp_pe_k[l], cmp_w1_k[l], cmp_w2_k[l], k_norm_g[l, 0], True)
    vc = _compress(vc_raw, cmp_pe_v[l], cmp_w1_v[l], cmp_w2_v[l], k_norm_g[l, 0], False)
    y_attn = _attention(z_q, z_gate, q_norm_g[l], kc, vc, ksel, vsel, kwin, vwin, rel_bias)

    x1 = _merge(x, z_pool, y_attn, z_merge, gate1, pool_w[l].astype(BF16), pool_scale[l],
                w_branch_pool[l].astype(BF16), w_branch_attn[l].astype(BF16), w_out[l].astype(BF16))

    keys = peer_sub_keys[l].reshape(PEER_HEADS * 2, PEER_KEYS, PEER_QDIM // 2).astype(BF16)
    h2, eid, gw = _peer_route(x1, norm2_g[l], scale2, shift2, peer_w_q[l].astype(BF16), keys)
    return _peer_experts(eid, gw, h2, x1, gate2, _pack_table(peer_u[l]), _pack_table(peer_v[l]))
```

```python
import math
from functools import partial

import jax
import jax.numpy as jnp
import numpy as np
from jax import lax
from jax.experimental import pallas as pl
from jax.experimental.pallas import tpu as pltpu

D_MODEL = 1024
POOL_WIDTH = 512
POOL_WINDOWS = (2, 4, 8, 16)
POOL_GROUP_DIM = 128
N_HEADS = 8
N_KV_GROUPS = 2
HEADS_PER_GROUP = 4
HEAD_DIM = 64
Q_WIDTH = 512
KV_WIDTH = 128
CMP_BLOCK = 32
CMP_STRIDE = 16
CMP_HIDDEN = 128
SEL_BLOCK = 64
SEL_TOPK = 16
WINDOW = 512
FORCE_SCORE = 1000.0
N_BUCKETS = 32
MAX_DISTANCE = 128
PEER_HEADS = 8
PEER_KEYS = 128
PEER_TOPK = 16
PEER_QDIM = 256
PEER_CHUNK = 128
GATE_COLS = N_HEADS * 3
EPS = 1e-6
NEG_INF = -1e30

LANES = 128
VMEM_LIMIT = 48 * 1024 * 1024
ROW_TILE = 512
POOL_HALO = 16
TQ = 128
SEL_FEATS = 64
SEL_CHUNK = 512
V_WIDTH = 128
KV_PAD = 512
CMP_BAND_LO = 10
PEER_TILE = 256
PEER_EXPERT_TILE = 256
PEER_TOKEN_UNROLL = 8
PEER_V_UNROLL = 8
EXPERT_ROWS = D_MODEL // 2 // LANES
TABLE_PAD = 8
PEER_VMEM_LIMIT = 56 * 1024 * 1024

BF16 = jnp.bfloat16
F32 = jnp.float32
_NT = (((1,), (1,)), ((), ()))


def _rms_rows(x, g):
    return x * lax.rsqrt(jnp.mean(x * x, axis=-1, keepdims=True) + EPS) * g


def _in_proj_kernel(x_ref, g_ref, sc_ref, sh_ref, w_ref, kg_ref,
                    zpool_ref, zq_ref, ksel_ref, vsel_ref, kwin_ref, vwin_ref, kcr_ref, vcr_ref,
                    zmerge_ref, zgate_ref):
    j = pl.program_id(1)

    @pl.when(j == 0)
    def _():
        for ref in (ksel_ref, vsel_ref, kwin_ref, vwin_ref):
            ref[...] = jnp.zeros(ref.shape, ref.dtype)

    @pl.when(j > 0)
    def _():
        _in_proj_tile(j - 1, x_ref, g_ref, sc_ref, sh_ref, w_ref, kg_ref, zpool_ref, zq_ref, ksel_ref, vsel_ref,
                      kwin_ref, vwin_ref, kcr_ref, vcr_ref, zmerge_ref, zgate_ref)


def _in_proj_tile(i, x_ref, g_ref, sc_ref, sh_ref, w_ref, kg_ref,
                  zpool_ref, zq_ref, ksel_ref, vsel_ref, kwin_ref, vwin_ref, kcr_ref, vcr_ref,
                  zmerge_ref, zgate_ref):
    x = x_ref[0]
    y = x * lax.rsqrt(jnp.mean(x * x, axis=-1, keepdims=True) + EPS)
    h = y * g_ref[...] * (1.0 + sc_ref[0]) + sh_ref[0]
    z = jnp.dot(h.astype(BF16), w_ref[...], preferred_element_type=F32)
    ts = x.shape[0]
    zpool_ref[0] = z[:, :POOL_WIDTH]
    o = POOL_WIDTH
    zq_ref[0] = z[:, o:o + Q_WIDTH]
    o += Q_WIDTH
    pos = i * ts + lax.broadcasted_iota(jnp.int32, (ts, SEL_FEATS), 0)
    onehot = (pos // SEL_BLOCK == lax.broadcasted_iota(jnp.int32, (ts, SEL_FEATS), 1)).astype(BF16)
    one_col = (lax.broadcasted_iota(jnp.int32, (ts, V_WIDTH - HEAD_DIM), 1) == 0).astype(BF16)
    for g in range(N_KV_GROUPS):
        def col(k):
            return z[:, o + k * KV_WIDTH + g * HEAD_DIM:o + k * KV_WIDTH + (g + 1) * HEAD_DIM]
        kcr_ref[0, g] = col(0)
        vcr_ref[0, g] = col(1)
        ks = _rms_rows(col(2), kg_ref[1:2, :]).astype(BF16)
        ksel_ref[0, g] = jnp.concatenate([ks, onehot], axis=1)
        vsel_ref[0, g] = jnp.concatenate([col(3).astype(BF16), one_col], axis=1)
        kwin_ref[0, g] = _rms_rows(col(4), kg_ref[2:3, :]).astype(BF16)
        vwin_ref[0, g] = jnp.concatenate([col(5).astype(BF16), one_col], axis=1)
    o += 6 * KV_WIDTH
    zmerge_ref[0] = z[:, o:o + 2 * D_MODEL]
    o += 2 * D_MODEL
    zgate_ref[0] = jax.nn.sigmoid(z[:, o:o + N_KV_GROUPS * LANES])


def _in_proj(x, g, scale, shift, w_bf16, k_norm_g):
    b_, s_, d_ = x.shape
    n = w_bf16.shape[1]
    assert KV_PAD == ROW_TILE
    tile = lambda j: jnp.maximum(j - 1, 0)
    row = lambda w: pl.BlockSpec((1, ROW_TILE, w), lambda b, j: (b, tile(j), 0))
    grp = lambda w: pl.BlockSpec((1, N_KV_GROUPS, ROW_TILE, w), lambda b, j: (b, 0, tile(j), 0))
    padded = lambda w: pl.BlockSpec((1, N_KV_GROUPS, ROW_TILE, w), lambda b, j: (b, 0, j, 0))
    gshape = lambda w, dt: jax.ShapeDtypeStruct((b_, N_KV_GROUPS, s_, w), dt)
    pshape = lambda w, dt: jax.ShapeDtypeStruct((b_, N_KV_GROUPS, s_ + KV_PAD, w), dt)
    return pl.pallas_call(
        _in_proj_kernel,
        grid=(b_, s_ // ROW_TILE + 1),
        in_specs=[
            row(d_),
            pl.BlockSpec((1, d_), lambda b, i: (0, 0)),
            pl.BlockSpec((1, 1, d_), lambda b, i: (b, 0, 0)),
            pl.BlockSpec((1, 1, d_), lambda b, i: (b, 0, 0)),
            pl.BlockSpec((d_, n), lambda b, i: (0, 0)),
            pl.BlockSpec((3, HEAD_DIM), lambda b, i: (0, 0)),
        ],
        out_specs=[row(POOL_WIDTH), row(Q_WIDTH), padded(HEAD_DIM + SEL_FEATS), padded(V_WIDTH), padded(HEAD_DIM),
                   padded(V_WIDTH), grp(HEAD_DIM), grp(HEAD_DIM), row(2 * d_), row(N_KV_GROUPS * LANES)],
        out_shape=[
            jax.ShapeDtypeStruct((b_, s_, POOL_WIDTH), F32),
            jax.ShapeDtypeStruct((b_, s_, Q_WIDTH), F32),
            pshape(HEAD_DIM + SEL_FEATS, BF16), pshape(V_WIDTH, BF16), pshape(HEAD_DIM, BF16), pshape(V_WIDTH, BF16),
            gshape(HEAD_DIM, F32), gshape(HEAD_DIM, F32),
            jax.ShapeDtypeStruct((b_, s_, 2 * d_), F32),
            jax.ShapeDtypeStruct((b_, s_, N_KV_GROUPS * LANES), F32),
        ],
        compiler_params=pltpu.CompilerParams(
            dimension_semantics=("parallel", "arbitrary"), vmem_limit_bytes=VMEM_LIMIT),
    )(x, g.reshape(1, d_), scale, shift, w_bf16, k_norm_g)


def _compress_kernel(t_ref, pe_ref, w1_ref, w2_ref, g_ref, o_ref, *, normalize):
    half = CMP_STRIDE * HEAD_DIM
    n = t_ref.shape[2] // CMP_STRIDE
    w1 = w1_ref[...]
    a = jnp.zeros((n, CMP_HIDDEN), F32)
    b = jnp.zeros((n, CMP_HIDDEN), F32)
    for r in range(CMP_STRIDE):
        t_r = t_ref[0, 0, pl.ds(r, n, stride=CMP_STRIDE), :].astype(BF16)
        a = a + jnp.dot(t_r, w1[r * HEAD_DIM:(r + 1) * HEAD_DIM], preferred_element_type=F32)
        b = b + jnp.dot(t_r, w1[half + r * HEAD_DIM:half + (r + 1) * HEAD_DIM], preferred_element_type=F32)
    pe = jnp.dot(pe_ref[...], w1, preferred_element_type=F32)
    b_next = pltpu.roll(b, n - 1, axis=0)
    hid = jax.nn.gelu(a + b_next + pe)
    out = jnp.dot(hid.astype(BF16), w2_ref[...], preferred_element_type=F32)
    if normalize:
        out = _rms_rows(out, g_ref[...])
    o_ref[0, 0] = out.astype(BF16)


def _compress(t_raw, pe, w1, w2, g, normalize):
    b_, g_, s_, dh = t_raw.shape
    n_str = s_ // CMP_STRIDE
    return pl.pallas_call(
        partial(_compress_kernel, normalize=normalize),
        grid=(b_, g_),
        in_specs=[
            pl.BlockSpec((1, 1, s_, dh), lambda b, g: (b, g, 0, 0)),
            pl.BlockSpec((1, CMP_BLOCK * dh), lambda b, g: (0, 0)),
            pl.BlockSpec((CMP_BLOCK * dh, CMP_HIDDEN), lambda b, g: (0, 0)),
            pl.BlockSpec((CMP_HIDDEN, dh), lambda b, g: (0, 0)),
            pl.BlockSpec((1, dh), lambda b, g: (0, 0)),
        ],
        out_specs=pl.BlockSpec((1, 1, n_str, dh), lambda b, g: (b, g, 0, 0)),
        out_shape=jax.ShapeDtypeStruct((b_, g_, n_str, dh), BF16),
        compiler_params=pltpu.CompilerParams(
            dimension_semantics=("parallel", "parallel"), vmem_limit_bytes=VMEM_LIMIT),
    )(t_raw, pe.reshape(1, CMP_BLOCK * dh).astype(BF16), w1.astype(BF16), w2.astype(BF16), g.reshape(1, dh))


def _t5_bucket_np(rel):
    n = np.maximum(rel, 0)
    max_exact = N_BUCKETS // 2
    nf = np.maximum(n, 1).astype(np.float32)
    large = max_exact + (np.log(nf / max_exact) / math.log(MAX_DISTANCE / max_exact) * (N_BUCKETS - max_exact)).astype(np.int32)
    large = np.minimum(large, N_BUCKETS - 1)
    return np.where(n < max_exact, n, large)


def _bias_tiles(rel_bias, n_cmp_pad):
    far = rel_bias[N_BUCKETS - 1]
    ii = np.arange(TQ)[:, None]

    def lookup(dist):
        near = (dist >= 0) & (dist < MAX_DISTANCE)
        bucket = _t5_bucket_np(np.where(near, dist, MAX_DISTANCE))
        onehot = (jnp.asarray(bucket.astype(np.int8))[..., None] == jnp.arange(N_BUCKETS, dtype=jnp.int8)).astype(F32)
        b = jnp.dot(onehot, rel_bias - far, precision=lax.Precision.HIGHEST)
        b = jnp.transpose(b, (2, 0, 1))
        return b.reshape(N_KV_GROUPS, HEADS_PER_GROUP * dist.shape[0], dist.shape[1])

    jj = np.arange(TQ)[None, :]
    bt = jnp.stack([lookup(ii - jj), lookup(TQ + ii - jj)], axis=1)
    cc = np.arange(n_cmp_pad)[None, :]
    dist_c = ii - CMP_STRIDE * (cc - CMP_BAND_LO) - (CMP_BLOCK - 1)
    dist_c = np.where(cc <= CMP_BAND_LO + TQ // CMP_STRIDE, dist_c, -1)
    bc = lookup(dist_c)
    rows = HEADS_PER_GROUP * TQ
    i_r = np.tile(np.arange(TQ), HEADS_PER_GROUP)[:, None]
    first = jnp.asarray(np.where(jj > i_r, 0.0, NEG_INF), F32)
    last = jnp.where(jnp.asarray(jj <= i_r), bt[:, 0], NEG_INF)
    zeros = jnp.zeros((N_KV_GROUPS, rows, WINDOW - 2 * TQ), F32)
    wa = jnp.concatenate([jnp.broadcast_to(first, (N_KV_GROUPS, rows, TQ)), zeros, bt[:, 1], last], axis=-1)
    return wa, bc


def _sel_mapping_t(n_cmp_pad, n_cmp):
    m = np.zeros((SEL_FEATS, n_cmp_pad), np.float32)
    pos = np.arange(n_cmp)[:, None] * CMP_STRIDE + np.arange(CMP_BLOCK)[None, :]
    np.add.at(m, ((pos // SEL_BLOCK).ravel(), np.repeat(np.arange(n_cmp), CMP_BLOCK)), 1.0 / CMP_BLOCK)
    return m


def _logits(q, k, add=None, lo_cols=None):
    s = lax.dot_general(q, k, _NT, preferred_element_type=F32)
    if add is not None:
        s = s + add
    if lo_cols is not None:
        s = jnp.where(lax.broadcasted_iota(jnp.int32, s.shape, 1) < lo_cols, NEG_INF, s)
    return s


def _flash_step(s, v, m_sc, acc_sc):
    m_prev = m_sc[...]
    m_new = jnp.maximum(m_prev, jnp.max(s, axis=-1, keepdims=True))
    alpha = jnp.exp(m_prev - m_new)
    p = jnp.exp(s - jnp.concatenate([m_new] * (s.shape[1] // LANES), axis=1))
    acc_sc[...] = alpha * acc_sc[...] + jnp.dot(p.astype(BF16), v, preferred_element_type=F32)
    m_sc[...] = m_new


def _attn_kernel(zq_ref, gate_ref, qg_ref, kc_ref, vc_ref, ksel_ref, vsel_ref, kwin_ref, vwin_ref,
                 wa_ref, bc_ref, smap_ref, o_ref, m_sc, acc_sc):
    i = pl.program_id(2)
    rows = HEADS_PER_GROUP * TQ
    ncp = kc_ref.shape[2]
    scale = HEAD_DIM ** -0.5

    zq = zq_ref[0]
    qs = jnp.concatenate([zq[:, j * HEAD_DIM:(j + 1) * HEAD_DIM] for j in range(HEADS_PER_GROUP)], axis=0)
    qn = _rms_rows(qs, qg_ref[...]) * scale
    qb = qn.astype(BF16)

    lc = lax.dot_general(qb, kc_ref[0, 0], _NT, preferred_element_type=F32)
    lc = lc + pltpu.roll(bc_ref[0], (i * (TQ // CMP_STRIDE) + ncp - CMP_BAND_LO) % ncp, axis=1)
    t_c = i * TQ + lax.broadcasted_iota(jnp.int32, (rows, ncp), 0) % TQ
    n_c = lax.broadcasted_iota(jnp.int32, (rows, ncp), 1)
    valid_c = n_c * CMP_STRIDE + (CMP_BLOCK - 1) <= t_c
    lm = jnp.where(valid_c, lc, NEG_INF)
    e = jnp.where(valid_c, jnp.exp(lm - jnp.max(lm, axis=-1, keepdims=True)), 0.0)
    den = jnp.sum(e, axis=-1, keepdims=True)
    p_c = e / jnp.where(den > 0.0, den, 1.0)
    o_cmp = jnp.dot(p_c.astype(BF16), vc_ref[0, 0], preferred_element_type=F32)

    p_sum = p_c[0:TQ] + p_c[TQ:2 * TQ] + p_c[2 * TQ:3 * TQ] + p_c[3 * TQ:4 * TQ]
    p_hi = p_sum.astype(BF16)
    r1 = p_sum - p_hi.astype(F32)
    p_mid = r1.astype(BF16)
    p_lo = (r1 - p_mid.astype(F32)).astype(BF16)
    smap = smap_ref[...]
    imp = (lax.dot_general(smap, p_hi, _NT, preferred_element_type=F32)
           + lax.dot_general(smap, p_mid, _NT, preferred_element_type=F32)
           + lax.dot_general(smap, p_lo, _NT, preferred_element_type=F32))
    blk = lax.broadcasted_iota(jnp.int32, (SEL_FEATS, TQ), 0)
    t_s = i * TQ + lax.broadcasted_iota(jnp.int32, (SEL_FEATS, TQ), 1)
    cur = t_s // SEL_BLOCK
    forced = (blk == 0) | (blk == cur) | (blk == cur - 1)
    visible = blk * SEL_BLOCK <= t_s
    score = jnp.where(visible, imp + jnp.where(forced, FORCE_SCORE, 0.0), -1.0)
    rank = jnp.zeros((SEL_FEATS, TQ), jnp.int32)
    for sp in range(SEL_FEATS):
        row = score[sp:sp + 1, :]
        beats = (row > score) | ((row == score) & (blk > sp))
        rank = rank + beats.astype(jnp.int32)
    pen_t = jnp.where(rank < SEL_TOPK, 0.0, NEG_INF)
    pen = jnp.transpose(pen_t).astype(BF16)
    q_aug = jnp.concatenate([qb, jnp.concatenate([pen] * HEADS_PER_GROUP, axis=0)], axis=1)

    def keys(ref, pos, n):
        return ref[0, 0, pl.ds(pl.multiple_of(pos + KV_PAD, TQ), n), :]

    m_sc[...] = jnp.full((rows, LANES), NEG_INF, F32)
    acc_sc[...] = jnp.zeros((rows, V_WIDTH), F32)
    per_chunk = SEL_CHUNK // TQ
    n_plain = jnp.maximum(i - 1, 0)
    rem = n_plain % per_chunk

    @pl.when(rem > 0)
    def _():
        pos = (rem - per_chunk) * TQ
        s = _logits(q_aug, keys(ksel_ref, pos, SEL_CHUNK), lo_cols=-pos)
        _flash_step(s, keys(vsel_ref, pos, SEL_CHUNK), m_sc, acc_sc)

    def sel_body(c, carry):
        pos = (rem + c * per_chunk) * TQ
        _flash_step(_logits(q_aug, keys(ksel_ref, pos, SEL_CHUNK)), keys(vsel_ref, pos, SEL_CHUNK), m_sc, acc_sc)
        return carry

    lax.fori_loop(0, n_plain // per_chunk, sel_body, 0)
    pos = (i - 1) * TQ
    s = _logits(q_aug, keys(ksel_ref, pos, 2 * TQ), add=wa_ref[0, :, WINDOW - TQ:], lo_cols=-pos)
    _flash_step(s, keys(vsel_ref, pos, 2 * TQ), m_sc, acc_sc)
    acc = acc_sc[...]
    o_sel = acc[:, :HEAD_DIM] / acc[:, HEAD_DIM:HEAD_DIM + 1]

    pos = i * TQ - WINDOW
    s = _logits(qb, keys(kwin_ref, pos, WINDOW + TQ), add=wa_ref[0], lo_cols=-pos)
    p = jnp.exp(s - jnp.max(s, axis=-1, keepdims=True))
    acc = jnp.dot(p.astype(BF16), keys(vwin_ref, pos, WINDOW + TQ), preferred_element_type=F32)
    o_win = acc[:, :HEAD_DIM] / acc[:, HEAD_DIM:HEAD_DIM + 1]

    gate = gate_ref[0]
    outs = []
    for j in range(HEADS_PER_GROUP):
        sl = slice(j * TQ, (j + 1) * TQ)
        outs.append(gate[:, 3 * j:3 * j + 1] * o_cmp[sl] + gate[:, 3 * j + 1:3 * j + 2] * o_sel[sl]
                    + gate[:, 3 * j + 2:3 * j + 3] * o_win[sl])
    o_ref[0] = jnp.concatenate(outs, axis=1)


def _attention(z_q, z_gate, q_norm_g, kc, vc, ksel, vsel, kwin, vwin, rel_bias):
    b_, s_, _ = z_q.shape
    ncp = kc.shape[2]
    n_cmp = ncp - CMP_BLOCK // CMP_STRIDE + 1
    assert s_ // SEL_BLOCK <= SEL_FEATS and s_ % TQ == 0
    wa, bc = _bias_tiles(rel_bias, ncp)
    smap_t = jnp.asarray(_sel_mapping_t(ncp, n_cmp), BF16)
    rows = HEADS_PER_GROUP * TQ
    gw = HEADS_PER_GROUP * HEAD_DIM
    kv = lambda w: pl.BlockSpec((1, 1, s_ + KV_PAD, w), lambda b, g, i: (b, g, 0, 0))
    cm = pl.BlockSpec((1, 1, ncp, HEAD_DIM), lambda b, g, i: (b, g, 0, 0))
    return pl.pallas_call(
        _attn_kernel,
        grid=(b_, N_KV_GROUPS, s_ // TQ),
        in_specs=[
            pl.BlockSpec((1, TQ, gw), lambda b, g, i: (b, i, g)),
            pl.BlockSpec((1, TQ, LANES), lambda b, g, i: (b, i, g)),
            pl.BlockSpec((1, HEAD_DIM), lambda b, g, i: (0, 0)),
            cm, cm, kv(HEAD_DIM + SEL_FEATS), kv(V_WIDTH), kv(HEAD_DIM), kv(V_WIDTH),
            pl.BlockSpec((1, rows, WINDOW + TQ), lambda b, g, i: (g, 0, 0)),
            pl.BlockSpec((1, rows, ncp), lambda b, g, i: (g, 0, 0)),
            pl.BlockSpec((SEL_FEATS, ncp), lambda b, g, i: (0, 0)),
        ],
        out_specs=pl.BlockSpec((1, TQ, gw), lambda b, g, i: (b, i, g)),
        out_shape=jax.ShapeDtypeStruct((b_, s_, N_KV_GROUPS * gw), F32),
        scratch_shapes=[pltpu.VMEM((rows, LANES), F32), pltpu.VMEM((rows, V_WIDTH), F32)],
        compiler_params=pltpu.CompilerParams(
            dimension_semantics=("parallel", "parallel", "arbitrary"), vmem_limit_bytes=VMEM_LIMIT),
    )(z_q, z_gate, q_norm_g.reshape(1, HEAD_DIM), kc, vc, ksel, vsel, kwin, vwin, wa, bc, smap_t)


def _pool_tile(halo, tile, i, pw_ref, ps_ref):
    ts = tile.shape[0]
    ext = jnp.concatenate([jnp.where(i > 0, halo, 0.0), tile], axis=0)
    t = i * ts + lax.broadcasted_iota(jnp.int32, (ts, 1), 0)
    outs = []
    for gi, w in enumerate(POOL_WINDOWS):
        sl = slice(gi * POOL_GROUP_DIM, (gi + 1) * POOL_GROUP_DIM)
        run = ext[:, sl]
        span = 1
        while span < w:
            run = run + pltpu.roll(run, span, axis=0)
            span *= 2
        cnt = jnp.minimum(t + 1, w).astype(F32)
        pooled = run[POOL_HALO:] / cnt - tile[:, sl]
        outs.append(jnp.dot(pooled.astype(BF16), pw_ref[gi], preferred_element_type=F32))
    return jnp.concatenate(outs, axis=1) * ps_ref[...]


def _merge_kernel(x_ref, zp_ref, zh_ref, ya_ref, zm_ref, g1_ref, pw_ref, ps_ref, wbp_ref, wba_ref, wo_ref, o_ref):
    d_ = x_ref.shape[-1]
    y_pool = _pool_tile(zh_ref[0], zp_ref[0], pl.program_id(1), pw_ref, ps_ref)
    bp = jnp.dot(y_pool.astype(BF16), wbp_ref[...], preferred_element_type=F32)
    ba = jnp.dot(ya_ref[0].astype(BF16), wba_ref[...], preferred_element_type=F32)
    zm = zm_ref[0]
    mixed = jax.nn.sigmoid(zm[:, :d_]) * bp + jax.nn.sigmoid(zm[:, d_:]) * ba
    proj = jnp.dot(mixed.astype(BF16), wo_ref[...], preferred_element_type=F32)
    o_ref[0] = x_ref[0] + g1_ref[0] * proj


def _merge(x, z_pool, y_attn, z_merge, gate1, pool_w, pool_scale, wbp, wba, wo):
    b_, s_, d_ = x.shape
    row = lambda w: pl.BlockSpec((1, ROW_TILE, w), lambda b, i: (b, i, 0))
    full = lambda a: pl.BlockSpec(a.shape, lambda b, i: (0,) * a.ndim)
    halo_blocks = ROW_TILE // POOL_HALO
    halo = pl.BlockSpec((1, POOL_HALO, POOL_WIDTH), lambda b, i: (b, jnp.maximum(i * halo_blocks - 1, 0), 0))
    pool_scale = pool_scale.reshape(1, POOL_WIDTH)
    return pl.pallas_call(
        _merge_kernel,
        grid=(b_, s_ // ROW_TILE),
        in_specs=[row(d_), row(POOL_WIDTH), halo, row(Q_WIDTH), row(2 * d_),
                  pl.BlockSpec((1, 1, d_), lambda b, i: (b, 0, 0)),
                  full(pool_w), full(pool_scale), full(wbp), full(wba), full(wo)],
        out_specs=row(d_),
        out_shape=jax.ShapeDtypeStruct((b_, s_, d_), F32),
        compiler_params=pltpu.CompilerParams(
            dimension_semantics=("parallel", "parallel"), vmem_limit_bytes=VMEM_LIMIT),
    )(x, z_pool, z_pool, y_attn, z_merge, gate1, pool_w, pool_scale, wbp, wba, wo)


def _ada_kernel(c_ref, w_ref, b_ref, o_ref):
    c = c_ref[...]
    o_ref[...] = jnp.dot(jax.nn.silu(c).astype(BF16), w_ref[...].astype(BF16),
                         preferred_element_type=F32) + b_ref[...]


def _ada(c, w, b):
    b_, d_ = c.shape
    n = w.shape[1]
    return pl.pallas_call(
        _ada_kernel,
        grid=(n // d_,),
        in_specs=[pl.BlockSpec((b_, d_), lambda j: (0, 0)),
                  pl.BlockSpec((d_, d_), lambda j: (0, j)),
                  pl.BlockSpec((1, d_), lambda j: (0, j))],
        out_specs=pl.BlockSpec((b_, d_), lambda j: (0, j)),
        out_shape=jax.ShapeDtypeStruct((b_, n), F32),
        compiler_params=pltpu.CompilerParams(dimension_semantics=("parallel",), vmem_limit_bytes=VMEM_LIMIT),
    )(c, w, b.reshape(1, n))


_CAND_ROWS = PEER_TOPK + 7 * 8 + 8


def _topk_rows(x, k):
    n = x.shape[0]
    rid = lax.broadcasted_iota(jnp.int32, x.shape, 0)
    vals, idxs = [], []
    for _ in range(k):
        m = jnp.max(x, axis=0, keepdims=True)
        idx = jnp.min(jnp.where(x == m, rid, n), axis=0, keepdims=True)
        vals.append(m)
        idxs.append(idx)
        x = jnp.where(rid == idx, -jnp.inf, x)
    return jnp.concatenate(vals, axis=0), jnp.concatenate(idxs, axis=0)


def _pair_grid(r0, r1, combine):
    parts = [combine(r0[0:1], r1)]
    parts += [combine(r0[a:a + 1], r1[0:8]) for a in range(1, 8)]
    parts.append(combine(r0[8:16], r1[0:1]))
    return jnp.concatenate(parts, axis=0)


def _peer_route_kernel(x_ref, g_ref, sc_ref, sh_ref, w_ref, keys_ref, h_ref, eid_ref, gw_ref):
    x = x_ref[0]
    y = x * lax.rsqrt(jnp.mean(x * x, axis=-1, keepdims=True) + EPS)
    h = y * g_ref[...] * (1.0 + sc_ref[0]) + sh_ref[0]
    h_ref[0] = h
    qv = jnp.dot(h.astype(BF16), w_ref[...], preferred_element_type=F32).astype(BF16)
    half = PEER_QDIM // 2
    eids, gws = [], []
    for hd in range(PEER_HEADS):
        tops = []
        for p in range(2):
            c = hd * 2 + p
            sc = lax.dot_general(keys_ref[c], qv[:, c * half:(c + 1) * half], _NT,
                                 preferred_element_type=F32)
            tops.append(_topk_rows(sc, PEER_TOPK))
        (s0, i0), (s1, i1) = tops
        cand = _pair_grid(s0, s1, lambda a, b: a + b)
        cid = _pair_grid(i0, i1, lambda a, b: a * (PEER_KEYS * EXPERT_ROWS) + (b * EXPERT_ROWS + TABLE_PAD))
        rid = lax.broadcasted_iota(jnp.int32, cand.shape, 0)
        top_s, top_e = [], []
        for _ in range(PEER_TOPK):
            m = jnp.max(cand, axis=0, keepdims=True)
            idx = jnp.min(jnp.where(cand == m, rid, _CAND_ROWS), axis=0, keepdims=True)
            hit = rid == idx
            top_s.append(m)
            top_e.append(jnp.sum(jnp.where(hit, cid, 0), axis=0, keepdims=True))
            cand = jnp.where(hit, -jnp.inf, cand)
        top_s = jnp.concatenate(top_s, axis=0)
        e = jnp.exp(top_s - top_s[0:1])
        gws.append(e / jnp.sum(e, axis=0, keepdims=True))
        eids.append(jnp.concatenate(top_e, axis=0))
    eids = jnp.concatenate(eids, axis=0)
    for c in range(PEER_TILE // PEER_EXPERT_TILE):
        eid_ref[c] = eids[:, c * PEER_EXPERT_TILE:(c + 1) * PEER_EXPERT_TILE]
    gw_ref[...] = jnp.transpose(jnp.concatenate(gws, axis=0))


def _peer_route(x, g, scale, shift, w_bf16, keys_bf16):
    b_, s_, d_ = x.shape
    n = w_bf16.shape[1]
    nk = PEER_HEADS * PEER_TOPK
    per_tile = PEER_TILE // PEER_EXPERT_TILE
    tiles = s_ // PEER_TILE
    row = lambda w: pl.BlockSpec((1, PEER_TILE, w), lambda b, i: (b, i, 0))
    return pl.pallas_call(
        _peer_route_kernel,
        grid=(b_, s_ // PEER_TILE),
        in_specs=[
            row(d_),
            pl.BlockSpec((1, d_), lambda b, i: (0, 0)),
            pl.BlockSpec((1, 1, d_), lambda b, i: (b, 0, 0)),
            pl.BlockSpec((1, 1, d_), lambda b, i: (b, 0, 0)),
            pl.BlockSpec((d_, n), lambda b, i: (0, 0)),
            pl.BlockSpec(keys_bf16.shape, lambda b, i: (0, 0, 0)),
        ],
        out_specs=[row(d_),
                   pl.BlockSpec((per_tile, nk, PEER_EXPERT_TILE), lambda b, i: (b * tiles + i, 0, 0)),
                   pl.BlockSpec((PEER_TILE, nk), lambda b, i: (b * tiles + i, 0))],
        out_shape=[
            jax.ShapeDtypeStruct((b_, s_, d_), F32),
            jax.ShapeDtypeStruct((b_ * s_ // PEER_EXPERT_TILE, nk, PEER_EXPERT_TILE), jnp.int32),
            jax.ShapeDtypeStruct((b_ * s_, nk), F32),
        ],
        compiler_params=pltpu.CompilerParams(
            dimension_semantics=("parallel", "parallel"), vmem_limit_bytes=VMEM_LIMIT),
    )(x, g.reshape(1, d_), scale, shift, w_bf16, keys_bf16)


_HI_MASK = -65536


def _pack_table(t):
    e_, d_ = t.shape
    halves = t.reshape(e_, 2, EXPERT_ROWS, LANES)
    bits = [lax.bitcast_convert_type(halves[:, h].reshape(e_ * EXPERT_ROWS, LANES).astype(jnp.bfloat16),
                                     jnp.uint16).astype(jnp.uint32) for h in range(2)]
    return lax.bitcast_convert_type((bits[1] << 16) | bits[0], jnp.int32)


def _expert_words(tab_vmem, row):
    return tab_vmem[pl.ds(pl.multiple_of(row, EXPERT_ROWS), EXPERT_ROWS), :]


def _expert_pair_words(tab_vmem, row_a, row_b, sub):
    top = tab_vmem[pl.ds(pl.multiple_of(row_a, EXPERT_ROWS), 2 * EXPERT_ROWS), :]
    bot = tab_vmem[pl.ds(pl.multiple_of(row_b - EXPERT_ROWS, EXPERT_ROWS), 2 * EXPERT_ROWS), :]
    return jnp.where(sub < EXPERT_ROWS, top, bot)


def _unpack(words):
    lo = lax.bitcast_convert_type(lax.shift_left(words, 16), F32)
    hi = lax.bitcast_convert_type(words & _HI_MASK, F32)
    return lo, hi


def _load_resident(j, tab_hbm, tab_vmem, eid_ref, idx_smem, sem):
    @pl.when(j == 0)
    def _():
        n_rows = tab_hbm.shape[0]
        zeros = jnp.zeros((TABLE_PAD, LANES), jnp.int32)
        tab_vmem[pl.ds(0, TABLE_PAD), :] = zeros
        tab_vmem[pl.ds(TABLE_PAD + n_rows, TABLE_PAD), :] = zeros
        cp = pltpu.make_async_copy(tab_hbm, tab_vmem.at[pl.ds(TABLE_PAD, n_rows)], sem)
        cp.start()
        cp.wait()

    copies = [pltpu.make_async_copy(eid_ref.at[0, k], idx_smem[k], sem) for k in range(len(idx_smem))]
    for cp in copies:
        cp.start()
    for cp in copies:
        cp.wait()


def _fold_pairs(q, sub):
    m2, m1 = (sub & 2) == 0, (sub & 1) == 0
    u = [jnp.where(m2, q[i], q[i + 2]) + jnp.where(m2, pltpu.roll(q[i], 6, axis=0), pltpu.roll(q[i + 2], 2, axis=0))
         for i in range(2)]
    return jnp.where(m1, u[0], u[1]) + jnp.where(m1, pltpu.roll(u[0], 7, axis=0), pltpu.roll(u[1], 1, axis=0))


def _peer_u_kernel(eid_ref, gw_ref, h_ref, tab_hbm, w_ref, *scratch):
    j = pl.program_id(0)
    nk = PEER_HEADS * PEER_TOPK
    idx_smem, (tab_vmem, acc_ref, sem) = scratch[:nk], scratch[nk:]
    half = D_MODEL // 2 // LANES
    _load_resident(j, tab_hbm, tab_vmem, eid_ref, idx_smem, sem)
    sub = lax.broadcasted_iota(jnp.int32, (2 * half, LANES), 0)
    lane = lax.broadcasted_iota(jnp.int32, (nk, PEER_EXPERT_TILE), 1)
    acc_ref[...] = jnp.zeros((nk, PEER_EXPERT_TILE), F32)

    def token(t):
        row = h_ref[pl.ds(t, 1), :]

        def seg(s):
            return jnp.broadcast_to(row[:, s * LANES:(s + 1) * LANES], (2 * half, LANES))

        def rows_of(first):
            out = seg(first + half - 1)
            for s in range(half - 2, -1, -1):
                out = jnp.where(sub % half == s, seg(first + s), out)
            return out

        x_lo, x_hi = rows_of(0), rows_of(half)
        folded = []
        for gi in range(nk // 8):
            q = []
            for i in range(4):
                k = gi * 8 + i
                words = _expert_pair_words(tab_vmem, idx_smem[k][t], idx_smem[k + 4][t], sub)
                lo, hi = _unpack(words)
                q.append(lo * x_lo + hi * x_hi)
            folded.append(_fold_pairs(q, sub))
        return jnp.sum(jnp.concatenate(folded, axis=0), axis=-1, keepdims=True)

    def body(tb, carry):
        acc = acc_ref[...]
        for u in range(PEER_TOKEN_UNROLL):
            t = tb * PEER_TOKEN_UNROLL + u
            acc = jnp.where(lane == t, token(t), acc)
        acc_ref[...] = acc
        return carry

    lax.fori_loop(0, PEER_EXPERT_TILE // PEER_TOKEN_UNROLL, body, 0)
    w_ref[...] = jax.nn.gelu(jnp.transpose(acc_ref[...])) * gw_ref[...]


def _peer_v_kernel(eid_ref, w_ref, x_ref, g2_ref, tab_hbm, o_ref, *scratch):
    j = pl.program_id(0)
    nk = PEER_HEADS * PEER_TOPK
    idx_smem, (wb_ref, tab_vmem, sem) = scratch[:nk], scratch[nk:]
    half = D_MODEL // 2 // LANES
    _load_resident(j, tab_hbm, tab_vmem, eid_ref, idx_smem, sem)
    n_acc = 4
    sub = lax.broadcasted_iota(jnp.int32, (2 * half, LANES), 0)

    def token(t, slot):
        wb_ref[slot] = jnp.transpose(jnp.broadcast_to(w_ref[pl.ds(t, 1), :], (nk, nk)))
        lo_acc = [jnp.zeros((2 * half, LANES), F32) for _ in range(n_acc)]
        hi_acc = [jnp.zeros((2 * half, LANES), F32) for _ in range(n_acc)]
        for p in range(nk // 2):
            k = 2 * p
            lo, hi = _unpack(_expert_pair_words(tab_vmem, idx_smem[k][t], idx_smem[k + 1][t], sub))
            wk = jnp.where(sub < half, wb_ref[slot, k:k + 1, :], wb_ref[slot, k + 1:k + 2, :])
            lo_acc[p % n_acc] = lo_acc[p % n_acc] + wk * lo
            hi_acc[p % n_acc] = hi_acc[p % n_acc] + wk * hi
        lo_sum, hi_sum = sum(lo_acc[1:], lo_acc[0]), sum(hi_acc[1:], hi_acc[0])
        ff = jnp.concatenate([lo_sum[:half] + lo_sum[half:], hi_sum[:half] + hi_sum[half:]], axis=0)
        ff_row = jnp.concatenate([ff[s:s + 1, :] for s in range(2 * half)], axis=1)
        o_ref[pl.ds(t, 1), :] = x_ref[pl.ds(t, 1), :] + g2_ref[0] * ff_row

    def body(tb, carry):
        for u in range(PEER_V_UNROLL):
            token(tb * PEER_V_UNROLL + u, u)
        return carry

    lax.fori_loop(0, PEER_EXPERT_TILE // PEER_V_UNROLL, body, 0)


def _peer_experts(eid, gw, h2, x1, gate2, u_tab, v_tab):
    b_, s_, d_ = x1.shape
    n_tok = b_ * s_
    nk = PEER_HEADS * PEER_TOPK
    tt = PEER_EXPERT_TILE
    nsteps = n_tok // tt
    steps_per_batch = s_ // tt
    tok = pl.BlockSpec((tt, d_), lambda j: (j, 0))
    sel = pl.BlockSpec((tt, nk), lambda j: (j, 0))
    slots = pl.BlockSpec((1, nk, tt), lambda j: (j, 0, 0))
    idx_scratch = [pltpu.SMEM((tt,), jnp.int32) for _ in range(nk)]
    params = pltpu.CompilerParams(dimension_semantics=("arbitrary",), vmem_limit_bytes=PEER_VMEM_LIMIT)
    eid3 = eid.reshape(nsteps, nk, tt)
    resident = (u_tab.shape[0] + 2 * TABLE_PAD, LANES)
    w = pl.pallas_call(
        _peer_u_kernel,
        grid=(nsteps,),
        in_specs=[slots, sel, tok, pl.BlockSpec(memory_space=pl.ANY)],
        out_specs=sel,
        out_shape=jax.ShapeDtypeStruct((n_tok, nk), F32),
        scratch_shapes=idx_scratch + [pltpu.VMEM(resident, jnp.int32), pltpu.VMEM((nk, tt), F32),
                                      pltpu.SemaphoreType.DMA],
        compiler_params=params,
    )(eid3, gw.reshape(n_tok, nk), h2.reshape(n_tok, d_), u_tab)
    out = pl.pallas_call(
        _peer_v_kernel,
        grid=(nsteps,),
        in_specs=[slots, sel, tok, pl.BlockSpec((1, 1, d_), lambda j: (j // steps_per_batch, 0, 0)),
                  pl.BlockSpec(memory_space=pl.ANY)],
        out_specs=tok,
        out_shape=jax.ShapeDtypeStruct((n_tok, d_), F32),
        scratch_shapes=idx_scratch + [pltpu.VMEM((PEER_V_UNROLL, nk, LANES), F32),
                                      pltpu.VMEM(resident, jnp.int32), pltpu.SemaphoreType.DMA],
        compiler_params=params,
    )(eid3, w, x1.reshape(n_tok, d_), gate2, v_tab)
    return out.reshape(b_, s_, d_)


def _permute_w_in(w):
    d_ = w.shape[0]
    kv_end = POOL_WIDTH + Q_WIDTH + 6 * KV_WIDTH
    per_group = HEADS_PER_GROUP * 3
    parts = [w[:, :kv_end], w[:, kv_end + GATE_COLS:]]
    for g in range(N_KV_GROUPS):
        parts.append(w[:, kv_end + g * per_group:kv_end + (g + 1) * per_group])
        parts.append(jnp.zeros((d_, LANES - per_group), w.dtype))
    return jnp.concatenate(parts, axis=1).astype(BF16)


def kernel(x, c, rel_bias, ada_w, ada_b, norm1_g, norm2_g, w_in, pool_w, pool_scale, cmp_pe_k, cmp_w1_k, cmp_w2_k, cmp_pe_v, cmp_w1_v, cmp_w2_v, q_norm_g, k_norm_g, w_branch_pool, w_branch_attn, w_out, peer_w_q, peer_sub_keys, peer_u, peer_v):
    b_, s_, d_ = x.shape
    l = 0
    ada = _ada(c, ada_w[l], ada_b[l]).reshape(b_, 6, 1, d_)
    shift1, scale1, gate1 = ada[:, 0], ada[:, 1], ada[:, 2]
    shift2, scale2, gate2 = ada[:, 3], ada[:, 4], ada[:, 5]

    (z_pool, z_q, ksel, vsel, kwin, vwin, kc_raw, vc_raw, z_merge, z_gate) = _in_proj(
        x, norm1_g[l], scale1, shift1, _permute_w_in(w_in[l]), k_norm_g[l])

    kc = _compress(kc_raw, cmp_pe_k[l], cmp_w1_k[l], cmp_w2_k[l], k_norm_g[l, 0], True)
    vc = _compress(vc_raw, cmp_pe_v[l], cmp_w1_v[l], cmp_w2_v[l], k_norm_g[l, 0], False)
    y_attn = _attention(z_q, z_gate, q_norm_g[l], kc, vc, ksel, vsel, kwin, vwin, rel_bias)

    x1 = _merge(x, z_pool, y_attn, z_merge, gate1, pool_w[l].astype(BF16), pool_scale[l],
                w_branch_pool[l].astype(BF16), w_branch_attn[l].astype(BF16), w_out[l].astype(BF16))

    keys = peer_sub_keys[l].reshape(PEER_HEADS * 2, PEER_KEYS, PEER_QDIM // 2).astype(BF16)
    h2, eid, gw = _peer_route(x1, norm2_g[l], scale2, shift2, peer_w_q[l].astype(BF16), keys)
    return _peer_experts(eid, gw, h2, x1, gate2, _pack_table(peer_u[l]), _pack_table(peer_v[l]))
```

```python
import math
from functools import partial

import jax
import jax.numpy as jnp
import numpy as np
from jax import lax
from jax.experimental import pallas as pl
from jax.experimental.pallas import tpu as pltpu

D_MODEL = 1024
POOL_WIDTH = 512
POOL_WINDOWS = (2, 4, 8, 16)
POOL_GROUP_DIM = 128
N_HEADS = 8
N_KV_GROUPS = 2
HEADS_PER_GROUP = 4
HEAD_DIM = 64
Q_WIDTH = 512
KV_WIDTH = 128
CMP_BLOCK = 32
CMP_STRIDE = 16
CMP_HIDDEN = 128
SEL_BLOCK = 64
SEL_TOPK = 16
WINDOW = 512
FORCE_SCORE = 1000.0
N_BUCKETS = 32
MAX_DISTANCE = 128
PEER_HEADS = 8
PEER_KEYS = 128
PEER_TOPK = 16
PEER_QDIM = 256
PEER_CHUNK = 128
GATE_COLS = N_HEADS * 3
EPS = 1e-6
NEG_INF = -1e30

LANES = 128
VMEM_LIMIT = 48 * 1024 * 1024
ROW_TILE = 512
POOL_HALO = 16
TQ = 256
SEL_FEATS = 64
SEL_CHUNK = 512
V_WIDTH = 128
KV_PAD = 512
CMP_BAND_LO = 10
PEER_TILE = 256
PEER_EXPERT_TILE = 256
PEER_TOKEN_UNROLL = 8
PEER_V_UNROLL = 8
EXPERT_ROWS = D_MODEL // 2 // LANES
TABLE_PAD = 8
PEER_VMEM_LIMIT = 56 * 1024 * 1024

BF16 = jnp.bfloat16
F32 = jnp.float32
_NT = (((1,), (1,)), ((), ()))


def _rms_rows(x, g):
    return x * lax.rsqrt(jnp.mean(x * x, axis=-1, keepdims=True) + EPS) * g


def _in_proj_kernel(x_ref, g_ref, sc_ref, sh_ref, w_ref, kg_ref,
                    zpool_ref, zq_ref, ksel_ref, vsel_ref, kwin_ref, vwin_ref, kcr_ref, vcr_ref,
                    zmerge_ref, zgate_ref):
    j = pl.program_id(1)

    @pl.when(j == 0)
    def _():
        for ref in (ksel_ref, vsel_ref, kwin_ref, vwin_ref):
            ref[...] = jnp.zeros(ref.shape, ref.dtype)

    @pl.when(j > 0)
    def _():
        _in_proj_tile(j - 1, x_ref, g_ref, sc_ref, sh_ref, w_ref, kg_ref, zpool_ref, zq_ref, ksel_ref, vsel_ref,
                      kwin_ref, vwin_ref, kcr_ref, vcr_ref, zmerge_ref, zgate_ref)


def _in_proj_tile(i, x_ref, g_ref, sc_ref, sh_ref, w_ref, kg_ref,
                  zpool_ref, zq_ref, ksel_ref, vsel_ref, kwin_ref, vwin_ref, kcr_ref, vcr_ref,
                  zmerge_ref, zgate_ref):
    x = x_ref[0]
    y = x * lax.rsqrt(jnp.mean(x * x, axis=-1, keepdims=True) + EPS)
    h = y * g_ref[...] * (1.0 + sc_ref[0]) + sh_ref[0]
    z = jnp.dot(h.astype(BF16), w_ref[...], preferred_element_type=F32)
    ts = x.shape[0]
    zpool_ref[0] = z[:, :POOL_WIDTH]
    o = POOL_WIDTH
    zq_ref[0] = z[:, o:o + Q_WIDTH]
    o += Q_WIDTH
    pos = i * ts + lax.broadcasted_iota(jnp.int32, (ts, SEL_FEATS), 0)
    onehot = (pos // SEL_BLOCK == lax.broadcasted_iota(jnp.int32, (ts, SEL_FEATS), 1)).astype(BF16)
    one_col = (lax.broadcasted_iota(jnp.int32, (ts, V_WIDTH - HEAD_DIM), 1) == 0).astype(BF16)
    for g in range(N_KV_GROUPS):
        def col(k):
            return z[:, o + k * KV_WIDTH + g * HEAD_DIM:o + k * KV_WIDTH + (g + 1) * HEAD_DIM]
        kcr_ref[0, g] = col(0)
        vcr_ref[0, g] = col(1)
        ks = _rms_rows(col(2), kg_ref[1:2, :]).astype(BF16)
        ksel_ref[0, g] = jnp.concatenate([ks, onehot], axis=1)
        vsel_ref[0, g] = jnp.concatenate([col(3).astype(BF16), one_col], axis=1)
        kwin_ref[0, g] = _rms_rows(col(4), kg_ref[2:3, :]).astype(BF16)
        vwin_ref[0, g] = jnp.concatenate([col(5).astype(BF16), one_col], axis=1)
    o += 6 * KV_WIDTH
    zmerge_ref[0] = z[:, o:o + 2 * D_MODEL]
    o += 2 * D_MODEL
    zgate_ref[0] = jax.nn.sigmoid(z[:, o:o + N_KV_GROUPS * LANES])


def _in_proj(x, g, scale, shift, w_bf16, k_norm_g):
    b_, s_, d_ = x.shape
    n = w_bf16.shape[1]
    assert KV_PAD == ROW_TILE
    tile = lambda j: jnp.maximum(j - 1, 0)
    row = lambda w: pl.BlockSpec((1, ROW_TILE, w), lambda b, j: (b, tile(j), 0))
    grp = lambda w: pl.BlockSpec((1, N_KV_GROUPS, ROW_TILE, w), lambda b, j: (b, 0, tile(j), 0))
    padded = lambda w: pl.BlockSpec((1, N_KV_GROUPS, ROW_TILE, w), lambda b, j: (b, 0, j, 0))
    gshape = lambda w, dt: jax.ShapeDtypeStruct((b_, N_KV_GROUPS, s_, w), dt)
    pshape = lambda w, dt: jax.ShapeDtypeStruct((b_, N_KV_GROUPS, s_ + KV_PAD, w), dt)
    return pl.pallas_call(
        _in_proj_kernel,
        grid=(b_, s_ // ROW_TILE + 1),
        in_specs=[
            row(d_),
            pl.BlockSpec((1, d_), lambda b, i: (0, 0)),
            pl.BlockSpec((1, 1, d_), lambda b, i: (b, 0, 0)),
            pl.BlockSpec((1, 1, d_), lambda b, i: (b, 0, 0)),
            pl.BlockSpec((d_, n), lambda b, i: (0, 0)),
            pl.BlockSpec((3, HEAD_DIM), lambda b, i: (0, 0)),
        ],
        out_specs=[row(POOL_WIDTH), row(Q_WIDTH), padded(HEAD_DIM + SEL_FEATS), padded(V_WIDTH), padded(HEAD_DIM),
                   padded(V_WIDTH), grp(HEAD_DIM), grp(HEAD_DIM), row(2 * d_), row(N_KV_GROUPS * LANES)],
        out_shape=[
            jax.ShapeDtypeStruct((b_, s_, POOL_WIDTH), F32),
            jax.ShapeDtypeStruct((b_, s_, Q_WIDTH), F32),
            pshape(HEAD_DIM + SEL_FEATS, BF16), pshape(V_WIDTH, BF16), pshape(HEAD_DIM, BF16), pshape(V_WIDTH, BF16),
            gshape(HEAD_DIM, F32), gshape(HEAD_DIM, F32),
            jax.ShapeDtypeStruct((b_, s_, 2 * d_), F32),
            jax.ShapeDtypeStruct((b_, s_, N_KV_GROUPS * LANES), F32),
        ],
        compiler_params=pltpu.CompilerParams(
            dimension_semantics=("parallel", "arbitrary"), vmem_limit_bytes=VMEM_LIMIT),
    )(x, g.reshape(1, d_), scale, shift, w_bf16, k_norm_g)


def _compress_kernel(t_ref, pe_ref, w1_ref, w2_ref, g_ref, o_ref, *, normalize):
    half = CMP_STRIDE * HEAD_DIM
    n = t_ref.shape[2] // CMP_STRIDE
    w1 = w1_ref[...]
    a = jnp.zeros((n, CMP_HIDDEN), F32)
    b = jnp.zeros((n, CMP_HIDDEN), F32)
    for r in range(CMP_STRIDE):
        t_r = t_ref[0, 0, pl.ds(r, n, stride=CMP_STRIDE), :].astype(BF16)
        a = a + jnp.dot(t_r, w1[r * HEAD_DIM:(r + 1) * HEAD_DIM], preferred_element_type=F32)
        b = b + jnp.dot(t_r, w1[half + r * HEAD_DIM:half + (r + 1) * HEAD_DIM], preferred_element_type=F32)
    pe = jnp.dot(pe_ref[...], w1, preferred_element_type=F32)
    b_next = pltpu.roll(b, n - 1, axis=0)
    hid = jax.nn.gelu(a + b_next + pe)
    out = jnp.dot(hid.astype(BF16), w2_ref[...], preferred_element_type=F32)
    if normalize:
        out = _rms_rows(out, g_ref[...])
    o_ref[0, 0] = out.astype(BF16)


def _compress(t_raw, pe, w1, w2, g, normalize):
    b_, g_, s_, dh = t_raw.shape
    n_str = s_ // CMP_STRIDE
    return pl.pallas_call(
        partial(_compress_kernel, normalize=normalize),
        grid=(b_, g_),
        in_specs=[
            pl.BlockSpec((1, 1, s_, dh), lambda b, g: (b, g, 0, 0)),
            pl.BlockSpec((1, CMP_BLOCK * dh), lambda b, g: (0, 0)),
            pl.BlockSpec((CMP_BLOCK * dh, CMP_HIDDEN), lambda b, g: (0, 0)),
            pl.BlockSpec((CMP_HIDDEN, dh), lambda b, g: (0, 0)),
            pl.BlockSpec((1, dh), lambda b, g: (0, 0)),
        ],
        out_specs=pl.BlockSpec((1, 1, n_str, dh), lambda b, g: (b, g, 0, 0)),
        out_shape=jax.ShapeDtypeStruct((b_, g_, n_str, dh), BF16),
        compiler_params=pltpu.CompilerParams(
            dimension_semantics=("parallel", "parallel"), vmem_limit_bytes=VMEM_LIMIT),
    )(t_raw, pe.reshape(1, CMP_BLOCK * dh).astype(BF16), w1.astype(BF16), w2.astype(BF16), g.reshape(1, dh))


def _t5_bucket_np(rel):
    n = np.maximum(rel, 0)
    max_exact = N_BUCKETS // 2
    nf = np.maximum(n, 1).astype(np.float32)
    large = max_exact + (np.log(nf / max_exact) / math.log(MAX_DISTANCE / max_exact) * (N_BUCKETS - max_exact)).astype(np.int32)
    large = np.minimum(large, N_BUCKETS - 1)
    return np.where(n < max_exact, n, large)


def _bias_tiles(rel_bias, n_cmp_pad):
    far = rel_bias[N_BUCKETS - 1]
    ii = np.arange(TQ)[:, None]

    def lookup(dist):
        near = (dist >= 0) & (dist < MAX_DISTANCE)
        bucket = _t5_bucket_np(np.where(near, dist, MAX_DISTANCE))
        onehot = (jnp.asarray(bucket.astype(np.int8))[..., None] == jnp.arange(N_BUCKETS, dtype=jnp.int8)).astype(F32)
        b = jnp.dot(onehot, rel_bias - far, precision=lax.Precision.HIGHEST)
        b = jnp.transpose(b, (2, 0, 1))
        return b.reshape(N_KV_GROUPS, HEADS_PER_GROUP * dist.shape[0], dist.shape[1])

    jj = np.arange(TQ)[None, :]
    bt = jnp.stack([lookup(ii - jj), lookup(TQ + ii - jj)], axis=1)
    cc = np.arange(n_cmp_pad)[None, :]
    dist_c = ii - CMP_STRIDE * (cc - CMP_BAND_LO) - (CMP_BLOCK - 1)
    dist_c = np.where(cc <= CMP_BAND_LO + TQ // CMP_STRIDE, dist_c, -1)
    bc = lookup(dist_c)
    rows = HEADS_PER_GROUP * TQ
    i_r = np.tile(np.arange(TQ), HEADS_PER_GROUP)[:, None]
    first = jnp.asarray(np.where(jj > i_r, 0.0, NEG_INF), F32)
    last = jnp.where(jnp.asarray(jj <= i_r), bt[:, 0], NEG_INF)
    zeros = jnp.zeros((N_KV_GROUPS, rows, WINDOW - 2 * TQ), F32)
    wa = jnp.concatenate([jnp.broadcast_to(first, (N_KV_GROUPS, rows, TQ)), zeros, bt[:, 1], last], axis=-1)
    return wa, bc


def _sel_mapping_t(n_cmp_pad, n_cmp):
    m = np.zeros((SEL_FEATS, n_cmp_pad), np.float32)
    pos = np.arange(n_cmp)[:, None] * CMP_STRIDE + np.arange(CMP_BLOCK)[None, :]
    np.add.at(m, ((pos // SEL_BLOCK).ravel(), np.repeat(np.arange(n_cmp), CMP_BLOCK)), 1.0 / CMP_BLOCK)
    return m


def _logits(q, k, add=None, lo_cols=None):
    s = lax.dot_general(q, k, _NT, preferred_element_type=F32)
    if add is not None:
        s = s + add
    if lo_cols is not None:
        s = jnp.where(lax.broadcasted_iota(jnp.int32, s.shape, 1) < lo_cols, NEG_INF, s)
    return s


def _flash_step(s, v, m_sc, acc_sc):
    m_prev = m_sc[...]
    m_new = jnp.maximum(m_prev, jnp.max(s, axis=-1, keepdims=True))
    alpha = jnp.exp(m_prev - m_new)
    p = jnp.exp(s - jnp.concatenate([m_new] * (s.shape[1] // LANES), axis=1))
    acc_sc[...] = alpha * acc_sc[...] + jnp.dot(p.astype(BF16), v, preferred_element_type=F32)
    m_sc[...] = m_new


def _attn_kernel(zq_ref, gate_ref, qg_ref, kc_ref, vc_ref, ksel_ref, vsel_ref, kwin_ref, vwin_ref,
                 wa_ref, bc_ref, smap_ref, o_ref, m_sc, acc_sc):
    i = pl.program_id(2)
    rows = HEADS_PER_GROUP * TQ
    ncp = kc_ref.shape[2]
    scale = HEAD_DIM ** -0.5

    zq = zq_ref[0]
    qs = jnp.concatenate([zq[:, j * HEAD_DIM:(j + 1) * HEAD_DIM] for j in range(HEADS_PER_GROUP)], axis=0)
    qn = _rms_rows(qs, qg_ref[...]) * scale
    qb = qn.astype(BF16)

    lc = lax.dot_general(qb, kc_ref[0, 0], _NT, preferred_element_type=F32)
    lc = lc + pltpu.roll(bc_ref[0], (i * (TQ // CMP_STRIDE) + ncp - CMP_BAND_LO) % ncp, axis=1)
    t_c = i * TQ + lax.broadcasted_iota(jnp.int32, (rows, ncp), 0) % TQ
    n_c = lax.broadcasted_iota(jnp.int32, (rows, ncp), 1)
    valid_c = n_c * CMP_STRIDE + (CMP_BLOCK - 1) <= t_c
    lm = jnp.where(valid_c, lc, NEG_INF)
    e = jnp.where(valid_c, jnp.exp(lm - jnp.max(lm, axis=-1, keepdims=True)), 0.0)
    den = jnp.sum(e, axis=-1, keepdims=True)
    p_c = e / jnp.where(den > 0.0, den, 1.0)
    o_cmp = jnp.dot(p_c.astype(BF16), vc_ref[0, 0], preferred_element_type=F32)

    p_sum = p_c[0:TQ] + p_c[TQ:2 * TQ] + p_c[2 * TQ:3 * TQ] + p_c[3 * TQ:4 * TQ]
    p_hi = p_sum.astype(BF16)
    r1 = p_sum - p_hi.astype(F32)
    p_mid = r1.astype(BF16)
    p_lo = (r1 - p_mid.astype(F32)).astype(BF16)
    smap = smap_ref[...]
    imp = (lax.dot_general(smap, p_hi, _NT, preferred_element_type=F32)
           + lax.dot_general(smap, p_mid, _NT, preferred_element_type=F32)
           + lax.dot_general(smap, p_lo, _NT, preferred_element_type=F32))
    blk = lax.broadcasted_iota(jnp.int32, (SEL_FEATS, TQ), 0)
    t_s = i * TQ + lax.broadcasted_iota(jnp.int32, (SEL_FEATS, TQ), 1)
    cur = t_s // SEL_BLOCK
    forced = (blk == 0) | (blk == cur) | (blk == cur - 1)
    visible = blk * SEL_BLOCK <= t_s
    score = jnp.where(visible, imp + jnp.where(forced, FORCE_SCORE, 0.0), -1.0)
    rank = jnp.zeros((SEL_FEATS, TQ), jnp.int32)
    for sp in range(SEL_FEATS):
        row = score[sp:sp + 1, :]
        beats = (row > score) | ((row == score) & (blk > sp))
        rank = rank + beats.astype(jnp.int32)
    pen_t = jnp.where(rank < SEL_TOPK, 0.0, NEG_INF)
    pen = jnp.transpose(pen_t).astype(BF16)
    q_aug = jnp.concatenate([qb, jnp.concatenate([pen] * HEADS_PER_GROUP, axis=0)], axis=1)

    def keys(ref, pos, n):
        return ref[0, 0, pl.ds(pl.multiple_of(pos + KV_PAD, TQ), n), :]

    m_sc[...] = jnp.full((rows, LANES), NEG_INF, F32)
    acc_sc[...] = jnp.zeros((rows, V_WIDTH), F32)
    per_chunk = SEL_CHUNK // TQ
    n_plain = jnp.maximum(i - 1, 0)
    rem = n_plain % per_chunk

    @pl.when(rem > 0)
    def _():
        pos = (rem - per_chunk) * TQ
        s = _logits(q_aug, keys(ksel_ref, pos, SEL_CHUNK), lo_cols=-pos)
        _flash_step(s, keys(vsel_ref, pos, SEL_CHUNK), m_sc, acc_sc)

    def sel_body(c, carry):
        pos = (rem + c * per_chunk) * TQ
        _flash_step(_logits(q_aug, keys(ksel_ref, pos, SEL_CHUNK)), keys(vsel_ref, pos, SEL_CHUNK), m_sc, acc_sc)
        return carry

    lax.fori_loop(0, n_plain // per_chunk, sel_body, 0)
    pos = (i - 1) * TQ
    s = _logits(q_aug, keys(ksel_ref, pos, 2 * TQ), add=wa_ref[0, :, WINDOW - TQ:], lo_cols=-pos)
    _flash_step(s, keys(vsel_ref, pos, 2 * TQ), m_sc, acc_sc)
    acc = acc_sc[...]
    o_sel = acc[:, :HEAD_DIM] / acc[:, HEAD_DIM:HEAD_DIM + 1]

    pos = i * TQ - WINDOW
    s = _logits(qb, keys(kwin_ref, pos, WINDOW + TQ), add=wa_ref[0], lo_cols=-pos)
    p = jnp.exp(s - jnp.max(s, axis=-1, keepdims=True))
    acc = jnp.dot(p.astype(BF16), keys(vwin_ref, pos, WINDOW + TQ), preferred_element_type=F32)
    o_win = acc[:, :HEAD_DIM] / acc[:, HEAD_DIM:HEAD_DIM + 1]

    gate = gate_ref[0]
    outs = []
    for j in range(HEADS_PER_GROUP):
        sl = slice(j * TQ, (j + 1) * TQ)
        outs.append(gate[:, 3 * j:3 * j + 1] * o_cmp[sl] + gate[:, 3 * j + 1:3 * j + 2] * o_sel[sl]
                    + gate[:, 3 * j + 2:3 * j + 3] * o_win[sl])
    o_ref[0] = jnp.concatenate(outs, axis=1)


def _attention(z_q, z_gate, q_norm_g, kc, vc, ksel, vsel, kwin, vwin, rel_bias):
    b_, s_, _ = z_q.shape
    ncp = kc.shape[2]
    n_cmp = ncp - CMP_BLOCK // CMP_STRIDE + 1
    assert s_ // SEL_BLOCK <= SEL_FEATS and s_ % TQ == 0
    wa, bc = _bias_tiles(rel_bias, ncp)
    smap_t = jnp.asarray(_sel_mapping_t(ncp, n_cmp), BF16)
    rows = HEADS_PER_GROUP * TQ
    gw = HEADS_PER_GROUP * HEAD_DIM
    kv = lambda w: pl.BlockSpec((1, 1, s_ + KV_PAD, w), lambda b, g, i: (b, g, 0, 0))
    cm = pl.BlockSpec((1, 1, ncp, HEAD_DIM), lambda b, g, i: (b, g, 0, 0))
    return pl.pallas_call(
        _attn_kernel,
        grid=(b_, N_KV_GROUPS, s_ // TQ),
        in_specs=[
            pl.BlockSpec((1, TQ, gw), lambda b, g, i: (b, i, g)),
            pl.BlockSpec((1, TQ, LANES), lambda b, g, i: (b, i, g)),
            pl.BlockSpec((1, HEAD_DIM), lambda b, g, i: (0, 0)),
            cm, cm, kv(HEAD_DIM + SEL_FEATS), kv(V_WIDTH), kv(HEAD_DIM), kv(V_WIDTH),
            pl.BlockSpec((1, rows, WINDOW + TQ), lambda b, g, i: (g, 0, 0)),
            pl.BlockSpec((1, rows, ncp), lambda b, g, i: (g, 0, 0)),
            pl.BlockSpec((SEL_FEATS, ncp), lambda b, g, i: (0, 0)),
        ],
        out_specs=pl.BlockSpec((1, TQ, gw), lambda b, g, i: (b, i, g)),
        out_shape=jax.ShapeDtypeStruct((b_, s_, N_KV_GROUPS * gw), F32),
        scratch_shapes=[pltpu.VMEM((rows, LANES), F32), pltpu.VMEM((rows, V_WIDTH), F32)],
        compiler_params=pltpu.CompilerParams(
            dimension_semantics=("parallel", "parallel", "arbitrary"), vmem_limit_bytes=VMEM_LIMIT),
    )(z_q, z_gate, q_norm_g.reshape(1, HEAD_DIM), kc, vc, ksel, vsel, kwin, vwin, wa, bc, smap_t)


def _pool_tile(halo, tile, i, pw_ref, ps_ref):
    ts = tile.shape[0]
    ext = jnp.concatenate([jnp.where(i > 0, halo, 0.0), tile], axis=0)
    t = i * ts + lax.broadcasted_iota(jnp.int32, (ts, 1), 0)
    outs = []
    for gi, w in enumerate(POOL_WINDOWS):
        sl = slice(gi * POOL_GROUP_DIM, (gi + 1) * POOL_GROUP_DIM)
        run = ext[:, sl]
        span = 1
        while span < w:
            run = run + pltpu.roll(run, span, axis=0)
            span *= 2
        cnt = jnp.minimum(t + 1, w).astype(F32)
        pooled = run[POOL_HALO:] / cnt - tile[:, sl]
        outs.append(jnp.dot(pooled.astype(BF16), pw_ref[gi], preferred_element_type=F32))
    return jnp.concatenate(outs, axis=1) * ps_ref[...]


def _merge_kernel(x_ref, zp_ref, zh_ref, ya_ref, zm_ref, g1_ref, pw_ref, ps_ref, wbp_ref, wba_ref, wo_ref, o_ref):
    d_ = x_ref.shape[-1]
    y_pool = _pool_tile(zh_ref[0], zp_ref[0], pl.program_id(1), pw_ref, ps_ref)
    bp = jnp.dot(y_pool.astype(BF16), wbp_ref[...], preferred_element_type=F32)
    ba = jnp.dot(ya_ref[0].astype(BF16), wba_ref[...], preferred_element_type=F32)
    zm = zm_ref[0]
    mixed = jax.nn.sigmoid(zm[:, :d_]) * bp + jax.nn.sigmoid(zm[:, d_:]) * ba
    proj = jnp.dot(mixed.astype(BF16), wo_ref[...], preferred_element_type=F32)
    o_ref[0] = x_ref[0] + g1_ref[0] * proj


def _merge(x, z_pool, y_attn, z_merge, gate1, pool_w, pool_scale, wbp, wba, wo):
    b_, s_, d_ = x.shape
    row = lambda w: pl.BlockSpec((1, ROW_TILE, w), lambda b, i: (b, i, 0))
    full = lambda a: pl.BlockSpec(a.shape, lambda b, i: (0,) * a.ndim)
    halo_blocks = ROW_TILE // POOL_HALO
    halo = pl.BlockSpec((1, POOL_HALO, POOL_WIDTH), lambda b, i: (b, jnp.maximum(i * halo_blocks - 1, 0), 0))
    pool_scale = pool_scale.reshape(1, POOL_WIDTH)
    return pl.pallas_call(
        _merge_kernel,
        grid=(b_, s_ // ROW_TILE),
        in_specs=[row(d_), row(POOL_WIDTH), halo, row(Q_WIDTH), row(2 * d_),
                  pl.BlockSpec((1, 1, d_), lambda b, i: (b, 0, 0)),
                  full(pool_w), full(pool_scale), full(wbp), full(wba), full(wo)],
        out_specs=row(d_),
        out_shape=jax.ShapeDtypeStruct((b_, s_, d_), F32),
        compiler_params=pltpu.CompilerParams(
            dimension_semantics=("parallel", "parallel"), vmem_limit_bytes=VMEM_LIMIT),
    )(x, z_pool, z_pool, y_attn, z_merge, gate1, pool_w, pool_scale, wbp, wba, wo)


def _ada_kernel(c_ref, w_ref, b_ref, o_ref):
    c = c_ref[...]
    o_ref[...] = jnp.dot(jax.nn.silu(c).astype(BF16), w_ref[...].astype(BF16),
                         preferred_element_type=F32) + b_ref[...]


def _ada(c, w, b):
    b_, d_ = c.shape
    n = w.shape[1]
    return pl.pallas_call(
        _ada_kernel,
        grid=(n // d_,),
        in_specs=[pl.BlockSpec((b_, d_), lambda j: (0, 0)),
                  pl.BlockSpec((d_, d_), lambda j: (0, j)),
                  pl.BlockSpec((1, d_), lambda j: (0, j))],
        out_specs=pl.BlockSpec((b_, d_), lambda j: (0, j)),
        out_shape=jax.ShapeDtypeStruct((b_, n), F32),
        compiler_params=pltpu.CompilerParams(dimension_semantics=("parallel",), vmem_limit_bytes=VMEM_LIMIT),
    )(c, w, b.reshape(1, n))


_CAND_ROWS = PEER_TOPK + 7 * 8 + 8


def _topk_rows(x, k):
    n = x.shape[0]
    rid = lax.broadcasted_iota(jnp.int32, x.shape, 0)
    vals, idxs = [], []
    for _ in range(k):
        m = jnp.max(x, axis=0, keepdims=True)
        idx = jnp.min(jnp.where(x == m, rid, n), axis=0, keepdims=True)
        vals.append(m)
        idxs.append(idx)
        x = jnp.where(rid == idx, -jnp.inf, x)
    return jnp.concatenate(vals, axis=0), jnp.concatenate(idxs, axis=0)


def _pair_grid(r0, r1, combine):
    parts = [combine(r0[0:1], r1)]
    parts += [combine(r0[a:a + 1], r1[0:8]) for a in range(1, 8)]
    parts.append(combine(r0[8:16], r1[0:1]))
    return jnp.concatenate(parts, axis=0)


def _peer_route_kernel(x_ref, g_ref, sc_ref, sh_ref, w_ref, keys_ref, h_ref, eid_ref, gw_ref):
    x = x_ref[0]
    y = x * lax.rsqrt(jnp.mean(x * x, axis=-1, keepdims=True) + EPS)
    h = y * g_ref[...] * (1.0 + sc_ref[0]) + sh_ref[0]
    h_ref[0] = h
    qv = jnp.dot(h.astype(BF16), w_ref[...], preferred_element_type=F32).astype(BF16)
    half = PEER_QDIM // 2
    eids, gws = [], []
    for hd in range(PEER_HEADS):
        tops = []
        for p in range(2):
            c = hd * 2 + p
            sc = lax.dot_general(keys_ref[c], qv[:, c * half:(c + 1) * half], _NT,
                                 preferred_element_type=F32)
            tops.append(_topk_rows(sc, PEER_TOPK))
        (s0, i0), (s1, i1) = tops
        cand = _pair_grid(s0, s1, lambda a, b: a + b)
        cid = _pair_grid(i0, i1, lambda a, b: a * (PEER_KEYS * EXPERT_ROWS) + (b * EXPERT_ROWS + TABLE_PAD))
        rid = lax.broadcasted_iota(jnp.int32, cand.shape, 0)
        top_s, top_e = [], []
        for _ in range(PEER_TOPK):
            m = jnp.max(cand, axis=0, keepdims=True)
            idx = jnp.min(jnp.where(cand == m, rid, _CAND_ROWS), axis=0, keepdims=True)
            hit = rid == idx
            top_s.append(m)
            top_e.append(jnp.sum(jnp.where(hit, cid, 0), axis=0, keepdims=True))
            cand = jnp.where(hit, -jnp.inf, cand)
        top_s = jnp.concatenate(top_s, axis=0)
        e = jnp.exp(top_s - top_s[0:1])
        gws.append(e / jnp.sum(e, axis=0, keepdims=True))
        eids.append(jnp.concatenate(top_e, axis=0))
    eids = jnp.concatenate(eids, axis=0)
    for c in range(PEER_TILE // PEER_EXPERT_TILE):
        eid_ref[c] = eids[:, c * PEER_EXPERT_TILE:(c + 1) * PEER_EXPERT_TILE]
    gw_ref[...] = jnp.transpose(jnp.concatenate(gws, axis=0))


def _peer_route(x, g, scale, shift, w_bf16, keys_bf16):
    b_, s_, d_ = x.shape
    n = w_bf16.shape[1]
    nk = PEER_HEADS * PEER_TOPK
    per_tile = PEER_TILE // PEER_EXPERT_TILE
    tiles = s_ // PEER_TILE
    row = lambda w: pl.BlockSpec((1, PEER_TILE, w), lambda b, i: (b, i, 0))
    return pl.pallas_call(
        _peer_route_kernel,
        grid=(b_, s_ // PEER_TILE),
        in_specs=[
            row(d_),
            pl.BlockSpec((1, d_), lambda b, i: (0, 0)),
            pl.BlockSpec((1, 1, d_), lambda b, i: (b, 0, 0)),
            pl.BlockSpec((1, 1, d_), lambda b, i: (b, 0, 0)),
            pl.BlockSpec((d_, n), lambda b, i: (0, 0)),
            pl.BlockSpec(keys_bf16.shape, lambda b, i: (0, 0, 0)),
        ],
        out_specs=[row(d_),
                   pl.BlockSpec((per_tile, nk, PEER_EXPERT_TILE), lambda b, i: (b * tiles + i, 0, 0)),
                   pl.BlockSpec((PEER_TILE, nk), lambda b, i: (b * tiles + i, 0))],
        out_shape=[
            jax.ShapeDtypeStruct((b_, s_, d_), F32),
            jax.ShapeDtypeStruct((b_ * s_ // PEER_EXPERT_TILE, nk, PEER_EXPERT_TILE), jnp.int32),
            jax.ShapeDtypeStruct((b_ * s_, nk), F32),
        ],
        compiler_params=pltpu.CompilerParams(
            dimension_semantics=("parallel", "parallel"), vmem_limit_bytes=VMEM_LIMIT),
    )(x, g.reshape(1, d_), scale, shift, w_bf16, keys_bf16)


_HI_MASK = -65536


def _pack_table(t):
    e_, d_ = t.shape
    bits = lax.bitcast_convert_type(t.astype(jnp.bfloat16), jnp.uint16).astype(jnp.uint32)
    words = (bits[:, d_ // 2:] << 16) | bits[:, :d_ // 2]
    return lax.bitcast_convert_type(words, jnp.int32).reshape(e_ * EXPERT_ROWS, LANES)


def _expert_words(tab_vmem, row):
    return tab_vmem[pl.ds(pl.multiple_of(row, EXPERT_ROWS), EXPERT_ROWS), :]


def _expert_pair_words(tab_vmem, row_a, row_b, sub):
    top = tab_vmem[pl.ds(pl.multiple_of(row_a, EXPERT_ROWS), 2 * EXPERT_ROWS), :]
    bot = tab_vmem[pl.ds(pl.multiple_of(row_b - EXPERT_ROWS, EXPERT_ROWS), 2 * EXPERT_ROWS), :]
    return jnp.where(sub < EXPERT_ROWS, top, bot)


def _unpack(words):
    lo = lax.bitcast_convert_type(lax.shift_left(words, 16), F32)
    hi = lax.bitcast_convert_type(words & _HI_MASK, F32)
    return lo, hi


def _load_resident(j, tab_hbm, tab_vmem, eid_ref, idx_smem, sem):
    @pl.when(j == 0)
    def _():
        n_rows = tab_hbm.shape[0]
        zeros = jnp.zeros((TABLE_PAD, LANES), jnp.int32)
        tab_vmem[pl.ds(0, TABLE_PAD), :] = zeros
        tab_vmem[pl.ds(TABLE_PAD + n_rows, TABLE_PAD), :] = zeros
        cp = pltpu.make_async_copy(tab_hbm, tab_vmem.at[pl.ds(TABLE_PAD, n_rows)], sem)
        cp.start()
        cp.wait()

    copies = [pltpu.make_async_copy(eid_ref.at[0, k], idx_smem[k], sem) for k in range(len(idx_smem))]
    for cp in copies:
        cp.start()
    for cp in copies:
        cp.wait()


def _fold_pairs(q, sub):
    m2, m1 = (sub & 2) == 0, (sub & 1) == 0
    u = [jnp.where(m2, q[i], q[i + 2]) + jnp.where(m2, pltpu.roll(q[i], 6, axis=0), pltpu.roll(q[i + 2], 2, axis=0))
         for i in range(2)]
    return jnp.where(m1, u[0], u[1]) + jnp.where(m1, pltpu.roll(u[0], 7, axis=0), pltpu.roll(u[1], 1, axis=0))


def _peer_u_kernel(eid_ref, gw_ref, h_ref, tab_hbm, w_ref, *scratch):
    j = pl.program_id(0)
    nk = PEER_HEADS * PEER_TOPK
    idx_smem, (tab_vmem, acc_ref, sem) = scratch[:nk], scratch[nk:]
    half = D_MODEL // 2 // LANES
    _load_resident(j, tab_hbm, tab_vmem, eid_ref, idx_smem, sem)
    sub = lax.broadcasted_iota(jnp.int32, (2 * half, LANES), 0)
    lane = lax.broadcasted_iota(jnp.int32, (nk, PEER_EXPERT_TILE), 1)
    acc_ref[...] = jnp.zeros((nk, PEER_EXPERT_TILE), F32)

    def token(t):
        row = h_ref[pl.ds(t, 1), :]

        def seg(s):
            return jnp.broadcast_to(row[:, s * LANES:(s + 1) * LANES], (2 * half, LANES))

        def rows_of(first):
            out = seg(first + half - 1)
            for s in range(half - 2, -1, -1):
                out = jnp.where(sub % half == s, seg(first + s), out)
            return out

        x_lo, x_hi = rows_of(0), rows_of(half)
        folded = []
        for gi in range(nk // 8):
            q = []
            for i in range(4):
                k = gi * 8 + i
                words = _expert_pair_words(tab_vmem, idx_smem[k][t], idx_smem[k + 4][t], sub)
                lo, hi = _unpack(words)
                q.append(lo * x_lo + hi * x_hi)
            folded.append(_fold_pairs(q, sub))
        return jnp.sum(jnp.concatenate(folded, axis=0), axis=-1, keepdims=True)

    def body(tb, carry):
        acc = acc_ref[...]
        for u in range(PEER_TOKEN_UNROLL):
            t = tb * PEER_TOKEN_UNROLL + u
            acc = jnp.where(lane == t, token(t), acc)
        acc_ref[...] = acc
        return carry

    lax.fori_loop(0, PEER_EXPERT_TILE // PEER_TOKEN_UNROLL, body, 0)
    w_ref[...] = jax.nn.gelu(jnp.transpose(acc_ref[...])) * gw_ref[...]


def _peer_v_kernel(eid_ref, w_ref, x_ref, g2_ref, tab_hbm, o_ref, *scratch):
    j = pl.program_id(0)
    nk = PEER_HEADS * PEER_TOPK
    idx_smem, (wb_ref, tab_vmem, sem) = scratch[:nk], scratch[nk:]
    half = D_MODEL // 2 // LANES
    _load_resident(j, tab_hbm, tab_vmem, eid_ref, idx_smem, sem)
    n_acc = 4
    sub = lax.broadcasted_iota(jnp.int32, (2 * half, LANES), 0)

    def token(t, slot):
        wb_ref[slot] = jnp.transpose(jnp.broadcast_to(w_ref[pl.ds(t, 1), :], (nk, nk)))
        lo_acc = [jnp.zeros((2 * half, LANES), F32) for _ in range(n_acc)]
        hi_acc = [jnp.zeros((2 * half, LANES), F32) for _ in range(n_acc)]
        for p in range(nk // 2):
            k = 2 * p
            lo, hi = _unpack(_expert_pair_words(tab_vmem, idx_smem[k][t], idx_smem[k + 1][t], sub))
            wk = jnp.where(sub < half, wb_ref[slot, k:k + 1, :], wb_ref[slot, k + 1:k + 2, :])
            lo_acc[p % n_acc] = lo_acc[p % n_acc] + wk * lo
            hi_acc[p % n_acc] = hi_acc[p % n_acc] + wk * hi
        lo_sum, hi_sum = sum(lo_acc[1:], lo_acc[0]), sum(hi_acc[1:], hi_acc[0])
        ff = jnp.concatenate([lo_sum[:half] + lo_sum[half:], hi_sum[:half] + hi_sum[half:]], axis=0)
        ff_row = jnp.concatenate([ff[s:s + 1, :] for s in range(2 * half)], axis=1)
        o_ref[pl.ds(t, 1), :] = x_ref[pl.ds(t, 1), :] + g2_ref[0] * ff_row

    def body(tb, carry):
        for u in range(PEER_V_UNROLL):
            token(tb * PEER_V_UNROLL + u, u)
        return carry

    lax.fori_loop(0, PEER_EXPERT_TILE // PEER_V_UNROLL, body, 0)


def _peer_experts(eid, gw, h2, x1, gate2, u_tab, v_tab):
    b_, s_, d_ = x1.shape
    n_tok = b_ * s_
    nk = PEER_HEADS * PEER_TOPK
    tt = PEER_EXPERT_TILE
    nsteps = n_tok // tt
    steps_per_batch = s_ // tt
    tok = pl.BlockSpec((tt, d_), lambda j: (j, 0))
    sel = pl.BlockSpec((tt, nk), lambda j: (j, 0))
    slots = pl.BlockSpec((1, nk, tt), lambda j: (j, 0, 0))
    idx_scratch = [pltpu.SMEM((tt,), jnp.int32) for _ in range(nk)]
    params = pltpu.CompilerParams(dimension_semantics=("arbitrary",), vmem_limit_bytes=PEER_VMEM_LIMIT)
    eid3 = eid.reshape(nsteps, nk, tt)
    resident = (u_tab.shape[0] + 2 * TABLE_PAD, LANES)
    w = pl.pallas_call(
        _peer_u_kernel,
        grid=(nsteps,),
        in_specs=[slots, sel, tok, pl.BlockSpec(memory_space=pl.ANY)],
        out_specs=sel,
        out_shape=jax.ShapeDtypeStruct((n_tok, nk), F32),
        scratch_shapes=idx_scratch + [pltpu.VMEM(resident, jnp.int32), pltpu.VMEM((nk, tt), F32),
                                      pltpu.SemaphoreType.DMA],
        compiler_params=params,
    )(eid3, gw.reshape(n_tok, nk), h2.reshape(n_tok, d_), u_tab)
    out = pl.pallas_call(
        _peer_v_kernel,
        grid=(nsteps,),
        in_specs=[slots, sel, tok, pl.BlockSpec((1, 1, d_), lambda j: (j // steps_per_batch, 0, 0)),
                  pl.BlockSpec(memory_space=pl.ANY)],
        out_specs=tok,
        out_shape=jax.ShapeDtypeStruct((n_tok, d_), F32),
        scratch_shapes=idx_scratch + [pltpu.VMEM((PEER_V_UNROLL, nk, LANES), F32),
                                      pltpu.VMEM(resident, jnp.int32), pltpu.SemaphoreType.DMA],
        compiler_params=params,
    )(eid3, w, x1.reshape(n_tok, d_), gate2, v_tab)
    return out.reshape(b_, s_, d_)


def _permute_w_in(w):
    d_ = w.shape[0]
    kv_end = POOL_WIDTH + Q_WIDTH + 6 * KV_WIDTH
    per_group = HEADS_PER_GROUP * 3
    parts = [w[:, :kv_end], w[:, kv_end + GATE_COLS:]]
    for g in range(N_KV_GROUPS):
        parts.append(w[:, kv_end + g * per_group:kv_end + (g + 1) * per_group])
        parts.append(jnp.zeros((d_, LANES - per_group), w.dtype))
    return jnp.concatenate(parts, axis=1).astype(BF16)


def kernel(x, c, rel_bias, ada_w, ada_b, norm1_g, norm2_g, w_in, pool_w, pool_scale, cmp_pe_k, cmp_w1_k, cmp_w2_k, cmp_pe_v, cmp_w1_v, cmp_w2_v, q_norm_g, k_norm_g, w_branch_pool, w_branch_attn, w_out, peer_w_q, peer_sub_keys, peer_u, peer_v):
    b_, s_, d_ = x.shape
    l = 0
    ada = _ada(c, ada_w[l], ada_b[l]).reshape(b_, 6, 1, d_)
    shift1, scale1, gate1 = ada[:, 0], ada[:, 1], ada[:, 2]
    shift2, scale2, gate2 = ada[:, 3], ada[:, 4], ada[:, 5]

    (z_pool, z_q, ksel, vsel, kwin, vwin, kc_raw, vc_raw, z_merge, z_gate) = _in_proj(
        x, norm1_g[l], scale1, shift1, _permute_w_in(w_in[l]), k_norm_g[l])

    kc = _compress(kc_raw, cmp_pe_k[l], cmp_w1_k[l], cmp_w2_k[l], k_norm_g[l, 0], True)
    vc = _compress(vc_raw, cmp_pe_v[l], cmp_w1_v[l], cmp_w2_v[l], k_norm_g[l, 0], False)
    y_attn = _attention(z_q, z_gate, q_norm_g[l], kc, vc, ksel, vsel, kwin, vwin, rel_bias)

    x1 = _merge(x, z_pool, y_attn, z_merge, gate1, pool_w[l].astype(BF16), pool_scale[l],
                w_branch_pool[l].astype(BF16), w_branch_attn[l].astype(BF16), w_out[l].astype(BF16))

    keys = peer_sub_keys[l].reshape(PEER_HEADS * 2, PEER_KEYS, PEER_QDIM // 2).astype(BF16)
    h2, eid, gw = _peer_route(x1, norm2_g[l], scale2, shift2, peer_w_q[l].astype(BF16), keys)
    return _peer_experts(eid, gw, h2, x1, gate2, _pack_table(peer_u[l]), _pack_table(peer_v[l]))
```

```python
import math
from functools import partial

import jax
import jax.numpy as jnp
import numpy as np
from jax import lax
from jax.experimental import pallas as pl
from jax.experimental.pallas import tpu as pltpu

D_MODEL = 1024
POOL_WIDTH = 512
POOL_WINDOWS = (2, 4, 8, 16)
POOL_GROUP_DIM = 128
N_HEADS = 8
N_KV_GROUPS = 2
HEADS_PER_GROUP = 4
HEAD_DIM = 64
Q_WIDTH = 512
KV_WIDTH = 128
CMP_BLOCK = 32
CMP_STRIDE = 16
CMP_HIDDEN = 128
SEL_BLOCK = 64
SEL_TOPK = 16
WINDOW = 512
FORCE_SCORE = 1000.0
N_BUCKETS = 32
MAX_DISTANCE = 128
PEER_HEADS = 8
PEER_KEYS = 128
PEER_TOPK = 16
PEER_QDIM = 256
GATE_COLS = N_HEADS * 3
EPS = 1e-6
NEG_INF = -1e30

LANES = 128
VMEM_LIMIT = 48 * 1024 * 1024
ROW_TILE = 512
POOL_HALO = 16
TQ = 256
SEL_FEATS = 64
SEL_CHUNK = 512
V_WIDTH = 128
KV_PAD = 512
CMP_BAND_LO = 10
PEER_TILE = 256
PEER_EXPERT_TILE = 256
PEER_TOKEN_UNROLL = 8
PEER_V_UNROLL = 8
EXPERT_ROWS = D_MODEL // 2 // LANES
TABLE_PAD = 8
PEER_VMEM_LIMIT = 56 * 1024 * 1024

BF16 = jnp.bfloat16
F32 = jnp.float32
_NT = (((1,), (1,)), ((), ()))


def _rms_rows(x, g):
    return x * lax.rsqrt(jnp.mean(x * x, axis=-1, keepdims=True) + EPS) * g


def _in_proj_kernel(x_ref, g_ref, sc_ref, sh_ref, w_ref, kg_ref,
                    zpool_ref, zq_ref, ksel_ref, vsel_ref, kwin_ref, vwin_ref, kcr_ref, vcr_ref,
                    zmerge_ref, zgate_ref):
    j = pl.program_id(1)

    @pl.when(j == 0)
    def _():
        for ref in (ksel_ref, vsel_ref, kwin_ref, vwin_ref):
            ref[...] = jnp.zeros(ref.shape, ref.dtype)

    @pl.when(j > 0)
    def _():
        _in_proj_tile(j - 1, x_ref, g_ref, sc_ref, sh_ref, w_ref, kg_ref, zpool_ref, zq_ref, ksel_ref, vsel_ref,
                      kwin_ref, vwin_ref, kcr_ref, vcr_ref, zmerge_ref, zgate_ref)


def _in_proj_tile(i, x_ref, g_ref, sc_ref, sh_ref, w_ref, kg_ref,
                  zpool_ref, zq_ref, ksel_ref, vsel_ref, kwin_ref, vwin_ref, kcr_ref, vcr_ref,
                  zmerge_ref, zgate_ref):
    x = x_ref[0]
    y = x * lax.rsqrt(jnp.mean(x * x, axis=-1, keepdims=True) + EPS)
    h = y * g_ref[...] * (1.0 + sc_ref[0]) + sh_ref[0]
    z = jnp.dot(h.astype(BF16), w_ref[...], preferred_element_type=F32)
    ts = x.shape[0]
    zpool_ref[0] = z[:, :POOL_WIDTH]
    o = POOL_WIDTH
    zq_ref[0] = z[:, o:o + Q_WIDTH]
    o += Q_WIDTH
    pos = i * ts + lax.broadcasted_iota(jnp.int32, (ts, SEL_FEATS), 0)
    onehot = (pos // SEL_BLOCK == lax.broadcasted_iota(jnp.int32, (ts, SEL_FEATS), 1)).astype(BF16)
    one_col = (lax.broadcasted_iota(jnp.int32, (ts, V_WIDTH - HEAD_DIM), 1) == 0).astype(BF16)
    for g in range(N_KV_GROUPS):
        def col(k):
            return z[:, o + k * KV_WIDTH + g * HEAD_DIM:o + k * KV_WIDTH + (g + 1) * HEAD_DIM]
        kcr_ref[0, g] = col(0)
        vcr_ref[0, g] = col(1)
        ks = _rms_rows(col(2), kg_ref[1:2, :]).astype(BF16)
        ksel_ref[0, g] = jnp.concatenate([ks, onehot], axis=1)
        vsel_ref[0, g] = jnp.concatenate([col(3).astype(BF16), one_col], axis=1)
        kwin_ref[0, g] = _rms_rows(col(4), kg_ref[2:3, :]).astype(BF16)
        vwin_ref[0, g] = jnp.concatenate([col(5).astype(BF16), one_col], axis=1)
    o += 6 * KV_WIDTH
    zmerge_ref[0] = z[:, o:o + 2 * D_MODEL]
    o += 2 * D_MODEL
    zgate_ref[0] = jax.nn.sigmoid(z[:, o:o + N_KV_GROUPS * LANES])


def _in_proj(x, g, scale, shift, w_bf16, k_norm_g):
    b_, s_, d_ = x.shape
    n = w_bf16.shape[1]
    assert KV_PAD == ROW_TILE
    tile = lambda j: jnp.maximum(j - 1, 0)
    row = lambda w: pl.BlockSpec((1, ROW_TILE, w), lambda b, j: (b, tile(j), 0))
    grp = lambda w: pl.BlockSpec((1, N_KV_GROUPS, ROW_TILE, w), lambda b, j: (b, 0, tile(j), 0))
    padded = lambda w: pl.BlockSpec((1, N_KV_GROUPS, ROW_TILE, w), lambda b, j: (b, 0, j, 0))
    gshape = lambda w, dt: jax.ShapeDtypeStruct((b_, N_KV_GROUPS, s_, w), dt)
    pshape = lambda w, dt: jax.ShapeDtypeStruct((b_, N_KV_GROUPS, s_ + KV_PAD, w), dt)
    return pl.pallas_call(
        _in_proj_kernel,
        grid=(b_, s_ // ROW_TILE + 1),
        in_specs=[
            row(d_),
            pl.BlockSpec((1, d_), lambda b, i: (0, 0)),
            pl.BlockSpec((1, 1, d_), lambda b, i: (b, 0, 0)),
            pl.BlockSpec((1, 1, d_), lambda b, i: (b, 0, 0)),
            pl.BlockSpec((d_, n), lambda b, i: (0, 0)),
            pl.BlockSpec((3, HEAD_DIM), lambda b, i: (0, 0)),
        ],
        out_specs=[row(POOL_WIDTH), row(Q_WIDTH), padded(HEAD_DIM + SEL_FEATS), padded(V_WIDTH), padded(HEAD_DIM),
                   padded(V_WIDTH), grp(HEAD_DIM), grp(HEAD_DIM), row(2 * d_), row(N_KV_GROUPS * LANES)],
        out_shape=[
            jax.ShapeDtypeStruct((b_, s_, POOL_WIDTH), F32),
            jax.ShapeDtypeStruct((b_, s_, Q_WIDTH), F32),
            pshape(HEAD_DIM + SEL_FEATS, BF16), pshape(V_WIDTH, BF16), pshape(HEAD_DIM, BF16), pshape(V_WIDTH, BF16),
            gshape(HEAD_DIM, F32), gshape(HEAD_DIM, F32),
            jax.ShapeDtypeStruct((b_, s_, 2 * d_), F32),
            jax.ShapeDtypeStruct((b_, s_, N_KV_GROUPS * LANES), F32),
        ],
        compiler_params=pltpu.CompilerParams(
            dimension_semantics=("parallel", "arbitrary"), vmem_limit_bytes=VMEM_LIMIT),
    )(x, g.reshape(1, d_), scale, shift, w_bf16, k_norm_g)


def _compress_kernel(t_ref, pe_ref, w1_ref, w2_ref, g_ref, o_ref, *, normalize):
    half = CMP_STRIDE * HEAD_DIM
    n = t_ref.shape[2] // CMP_STRIDE
    w1 = w1_ref[...]
    a = jnp.zeros((n, CMP_HIDDEN), F32)
    b = jnp.zeros((n, CMP_HIDDEN), F32)
    for r in range(CMP_STRIDE):
        t_r = t_ref[0, 0, pl.ds(r, n, stride=CMP_STRIDE), :].astype(BF16)
        a = a + jnp.dot(t_r, w1[r * HEAD_DIM:(r + 1) * HEAD_DIM], preferred_element_type=F32)
        b = b + jnp.dot(t_r, w1[half + r * HEAD_DIM:half + (r + 1) * HEAD_DIM], preferred_element_type=F32)
    pe = jnp.dot(pe_ref[...], w1, preferred_element_type=F32)
    b_next = pltpu.roll(b, n - 1, axis=0)
    hid = jax.nn.gelu(a + b_next + pe)
    out = jnp.dot(hid.astype(BF16), w2_ref[...], preferred_element_type=F32)
    if normalize:
        out = _rms_rows(out, g_ref[...])
    o_ref[0, 0] = out.astype(BF16)


def _compress(t_raw, pe, w1, w2, g, normalize):
    b_, g_, s_, dh = t_raw.shape
    n_str = s_ // CMP_STRIDE
    return pl.pallas_call(
        partial(_compress_kernel, normalize=normalize),
        grid=(b_, g_),
        in_specs=[
            pl.BlockSpec((1, 1, s_, dh), lambda b, g: (b, g, 0, 0)),
            pl.BlockSpec((1, CMP_BLOCK * dh), lambda b, g: (0, 0)),
            pl.BlockSpec((CMP_BLOCK * dh, CMP_HIDDEN), lambda b, g: (0, 0)),
            pl.BlockSpec((CMP_HIDDEN, dh), lambda b, g: (0, 0)),
            pl.BlockSpec((1, dh), lambda b, g: (0, 0)),
        ],
        out_specs=pl.BlockSpec((1, 1, n_str, dh), lambda b, g: (b, g, 0, 0)),
        out_shape=jax.ShapeDtypeStruct((b_, g_, n_str, dh), BF16),
        compiler_params=pltpu.CompilerParams(
            dimension_semantics=("parallel", "parallel"), vmem_limit_bytes=VMEM_LIMIT),
    )(t_raw, pe.reshape(1, CMP_BLOCK * dh).astype(BF16), w1.astype(BF16), w2.astype(BF16), g.reshape(1, dh))


def _t5_bucket_np(rel):
    n = np.maximum(rel, 0)
    max_exact = N_BUCKETS // 2
    nf = np.maximum(n, 1).astype(np.float32)
    large = max_exact + (np.log(nf / max_exact) / math.log(MAX_DISTANCE / max_exact) * (N_BUCKETS - max_exact)).astype(np.int32)
    large = np.minimum(large, N_BUCKETS - 1)
    return np.where(n < max_exact, n, large)


def _bias_tiles(rel_bias, n_cmp_pad):
    far = rel_bias[N_BUCKETS - 1]
    ii = np.arange(TQ)[:, None]

    def lookup(dist):
        near = (dist >= 0) & (dist < MAX_DISTANCE)
        bucket = _t5_bucket_np(np.where(near, dist, MAX_DISTANCE))
        onehot = (jnp.asarray(bucket.astype(np.int8))[..., None] == jnp.arange(N_BUCKETS, dtype=jnp.int8)).astype(F32)
        b = jnp.dot(onehot, rel_bias - far, precision=lax.Precision.HIGHEST)
        b = jnp.transpose(b, (2, 0, 1))
        return b.reshape(N_KV_GROUPS, HEADS_PER_GROUP * dist.shape[0], dist.shape[1])

    jj = np.arange(TQ)[None, :]
    bt = jnp.stack([lookup(ii - jj), lookup(TQ + ii - jj)], axis=1)
    cc = np.arange(n_cmp_pad)[None, :]
    dist_c = ii - CMP_STRIDE * (cc - CMP_BAND_LO) - (CMP_BLOCK - 1)
    dist_c = np.where(cc <= CMP_BAND_LO + TQ // CMP_STRIDE, dist_c, -1)
    bc = lookup(dist_c)
    assert WINDOW >= 2 * TQ
    rows = HEADS_PER_GROUP * TQ
    i_r = np.tile(np.arange(TQ), HEADS_PER_GROUP)[:, None]
    first = jnp.asarray(np.where(jj > i_r, 0.0, NEG_INF), F32)
    last = jnp.where(jnp.asarray(jj <= i_r), bt[:, 0], NEG_INF)
    zeros = jnp.zeros((N_KV_GROUPS, rows, WINDOW - 2 * TQ), F32)
    wa = jnp.concatenate([jnp.broadcast_to(first, (N_KV_GROUPS, rows, TQ)), zeros, bt[:, 1], last], axis=-1)
    return wa, bc


def _sel_mapping_t(n_cmp_pad, n_cmp):
    m = np.zeros((SEL_FEATS, n_cmp_pad), np.float32)
    pos = np.arange(n_cmp)[:, None] * CMP_STRIDE + np.arange(CMP_BLOCK)[None, :]
    np.add.at(m, ((pos // SEL_BLOCK).ravel(), np.repeat(np.arange(n_cmp), CMP_BLOCK)), 1.0 / CMP_BLOCK)
    return m


def _logits(q, k, add=None, lo_cols=None):
    s = lax.dot_general(q, k, _NT, preferred_element_type=F32)
    if add is not None:
        s = s + add
    if lo_cols is not None:
        s = jnp.where(lax.broadcasted_iota(jnp.int32, s.shape, 1) < lo_cols, NEG_INF, s)
    return s


def _flash_step(s, v, m_sc, acc_sc):
    m_prev = m_sc[...]
    m_new = jnp.maximum(m_prev, jnp.max(s, axis=-1, keepdims=True))
    alpha = jnp.exp(m_prev - m_new)
    p = jnp.exp(s - jnp.concatenate([m_new] * (s.shape[1] // LANES), axis=1))
    acc_sc[...] = alpha * acc_sc[...] + jnp.dot(p.astype(BF16), v, preferred_element_type=F32)
    m_sc[...] = m_new


def _attn_kernel(zq_ref, gate_ref, qg_ref, kc_ref, vc_ref, ksel_ref, vsel_ref, kwin_ref, vwin_ref,
                 wa_ref, bc_ref, smap_ref, o_ref, m_sc, acc_sc):
    i = pl.program_id(2)
    rows = HEADS_PER_GROUP * TQ
    ncp = kc_ref.shape[2]
    scale = HEAD_DIM ** -0.5

    zq = zq_ref[0]
    qs = jnp.concatenate([zq[:, j * HEAD_DIM:(j + 1) * HEAD_DIM] for j in range(HEADS_PER_GROUP)], axis=0)
    qn = _rms_rows(qs, qg_ref[...]) * scale
    qb = qn.astype(BF16)

    lc = lax.dot_general(qb, kc_ref[0, 0], _NT, preferred_element_type=F32)
    lc = lc + pltpu.roll(bc_ref[0], (i * (TQ // CMP_STRIDE) + ncp - CMP_BAND_LO) % ncp, axis=1)
    t_c = i * TQ + lax.broadcasted_iota(jnp.int32, (rows, ncp), 0) % TQ
    n_c = lax.broadcasted_iota(jnp.int32, (rows, ncp), 1)
    valid_c = n_c * CMP_STRIDE + (CMP_BLOCK - 1) <= t_c
    lm = jnp.where(valid_c, lc, NEG_INF)
    e = jnp.where(valid_c, jnp.exp(lm - jnp.max(lm, axis=-1, keepdims=True)), 0.0)
    den = jnp.sum(e, axis=-1, keepdims=True)
    p_c = e / jnp.where(den > 0.0, den, 1.0)
    o_cmp = jnp.dot(p_c.astype(BF16), vc_ref[0, 0], preferred_element_type=F32)

    p_sum = p_c[0:TQ] + p_c[TQ:2 * TQ] + p_c[2 * TQ:3 * TQ] + p_c[3 * TQ:4 * TQ]
    p_hi = p_sum.astype(BF16)
    r1 = p_sum - p_hi.astype(F32)
    p_mid = r1.astype(BF16)
    p_lo = (r1 - p_mid.astype(F32)).astype(BF16)
    smap = smap_ref[...]
    imp = (lax.dot_general(smap, p_hi, _NT, preferred_element_type=F32)
           + lax.dot_general(smap, p_mid, _NT, preferred_element_type=F32)
           + lax.dot_general(smap, p_lo, _NT, preferred_element_type=F32))
    blk = lax.broadcasted_iota(jnp.int32, (SEL_FEATS, TQ), 0)
    t_s = i * TQ + lax.broadcasted_iota(jnp.int32, (SEL_FEATS, TQ), 1)
    cur = t_s // SEL_BLOCK
    forced = (blk == 0) | (blk == cur) | (blk == cur - 1)
    visible = blk * SEL_BLOCK <= t_s
    score = jnp.where(visible, imp + jnp.where(forced, FORCE_SCORE, 0.0), -1.0)
    rank = jnp.zeros((SEL_FEATS, TQ), jnp.int32)
    for sp in range(SEL_FEATS):
        row = score[sp:sp + 1, :]
        beats = (row > score) | ((row == score) & (blk > sp))
        rank = rank + beats.astype(jnp.int32)
    pen_t = jnp.where(rank < SEL_TOPK, 0.0, NEG_INF)
    pen = jnp.transpose(pen_t).astype(BF16)
    q_aug = jnp.concatenate([qb, jnp.concatenate([pen] * HEADS_PER_GROUP, axis=0)], axis=1)

    def keys(ref, pos, n):
        return ref[0, 0, pl.ds(pl.multiple_of(pos + KV_PAD, TQ), n), :]

    m_sc[...] = jnp.full((rows, LANES), NEG_INF, F32)
    acc_sc[...] = jnp.zeros((rows, V_WIDTH), F32)
    per_chunk = SEL_CHUNK // TQ
    n_plain = jnp.maximum(i - 1, 0)
    rem = n_plain % per_chunk

    @pl.when(rem > 0)
    def _():
        pos = (rem - per_chunk) * TQ
        s = _logits(q_aug, keys(ksel_ref, pos, SEL_CHUNK), lo_cols=-pos)
        _flash_step(s, keys(vsel_ref, pos, SEL_CHUNK), m_sc, acc_sc)

    def sel_body(c, carry):
        pos = (rem + c * per_chunk) * TQ
        _flash_step(_logits(q_aug, keys(ksel_ref, pos, SEL_CHUNK)), keys(vsel_ref, pos, SEL_CHUNK), m_sc, acc_sc)
        return carry

    lax.fori_loop(0, n_plain // per_chunk, sel_body, 0)
    pos = (i - 1) * TQ
    s = _logits(q_aug, keys(ksel_ref, pos, 2 * TQ), add=wa_ref[0, :, WINDOW - TQ:], lo_cols=-pos)
    _flash_step(s, keys(vsel_ref, pos, 2 * TQ), m_sc, acc_sc)
    acc = acc_sc[...]
    o_sel = acc[:, :HEAD_DIM] / acc[:, HEAD_DIM:HEAD_DIM + 1]

    pos = i * TQ - WINDOW
    s = _logits(qb, keys(kwin_ref, pos, WINDOW + TQ), add=wa_ref[0], lo_cols=-pos)
    p = jnp.exp(s - jnp.max(s, axis=-1, keepdims=True))
    acc = jnp.dot(p.astype(BF16), keys(vwin_ref, pos, WINDOW + TQ), preferred_element_type=F32)
    o_win = acc[:, :HEAD_DIM] / acc[:, HEAD_DIM:HEAD_DIM + 1]

    gate = gate_ref[0]
    outs = []
    for j in range(HEADS_PER_GROUP):
        sl = slice(j * TQ, (j + 1) * TQ)
        outs.append(gate[:, 3 * j:3 * j + 1] * o_cmp[sl] + gate[:, 3 * j + 1:3 * j + 2] * o_sel[sl]
                    + gate[:, 3 * j + 2:3 * j + 3] * o_win[sl])
    o_ref[0] = jnp.concatenate(outs, axis=1)


def _attention(z_q, z_gate, q_norm_g, kc, vc, ksel, vsel, kwin, vwin, rel_bias):
    b_, s_, _ = z_q.shape
    ncp = kc.shape[2]
    n_cmp = ncp - CMP_BLOCK // CMP_STRIDE + 1
    assert s_ // SEL_BLOCK <= SEL_FEATS and s_ % TQ == 0
    wa, bc = _bias_tiles(rel_bias, ncp)
    smap_t = jnp.asarray(_sel_mapping_t(ncp, n_cmp), BF16)
    rows = HEADS_PER_GROUP * TQ
    gw = HEADS_PER_GROUP * HEAD_DIM
    kv = lambda w: pl.BlockSpec((1, 1, s_ + KV_PAD, w), lambda b, g, i: (b, g, 0, 0))
    cm = pl.BlockSpec((1, 1, ncp, HEAD_DIM), lambda b, g, i: (b, g, 0, 0))
    return pl.pallas_call(
        _attn_kernel,
        grid=(b_, N_KV_GROUPS, s_ // TQ),
        in_specs=[
            pl.BlockSpec((1, TQ, gw), lambda b, g, i: (b, i, g)),
            pl.BlockSpec((1, TQ, LANES), lambda b, g, i: (b, i, g)),
            pl.BlockSpec((1, HEAD_DIM), lambda b, g, i: (0, 0)),
            cm, cm, kv(HEAD_DIM + SEL_FEATS), kv(V_WIDTH), kv(HEAD_DIM), kv(V_WIDTH),
            pl.BlockSpec((1, rows, WINDOW + TQ), lambda b, g, i: (g, 0, 0)),
            pl.BlockSpec((1, rows, ncp), lambda b, g, i: (g, 0, 0)),
            pl.BlockSpec((SEL_FEATS, ncp), lambda b, g, i: (0, 0)),
        ],
        out_specs=pl.BlockSpec((1, TQ, gw), lambda b, g, i: (b, i, g)),
        out_shape=jax.ShapeDtypeStruct((b_, s_, N_KV_GROUPS * gw), F32),
        scratch_shapes=[pltpu.VMEM((rows, LANES), F32), pltpu.VMEM((rows, V_WIDTH), F32)],
        compiler_params=pltpu.CompilerParams(
            dimension_semantics=("parallel", "parallel", "arbitrary"), vmem_limit_bytes=VMEM_LIMIT),
    )(z_q, z_gate, q_norm_g.reshape(1, HEAD_DIM), kc, vc, ksel, vsel, kwin, vwin, wa, bc, smap_t)


def _pool_tile(halo, tile, i, pw_ref, ps_ref):
    ts = tile.shape[0]
    ext = jnp.concatenate([jnp.where(i > 0, halo, 0.0), tile], axis=0)
    t = i * ts + lax.broadcasted_iota(jnp.int32, (ts, 1), 0)
    outs = []
    for gi, w in enumerate(POOL_WINDOWS):
        sl = slice(gi * POOL_GROUP_DIM, (gi + 1) * POOL_GROUP_DIM)
        run = ext[:, sl]
        span = 1
        while span < w:
            run = run + pltpu.roll(run, span, axis=0)
            span *= 2
        cnt = jnp.minimum(t + 1, w).astype(F32)
        pooled = run[POOL_HALO:] / cnt - tile[:, sl]
        outs.append(jnp.dot(pooled.astype(BF16), pw_ref[gi], preferred_element_type=F32))
    return jnp.concatenate(outs, axis=1) * ps_ref[...]


def _merge_kernel(x_ref, zp_ref, zh_ref, ya_ref, zm_ref, g1_ref, pw_ref, ps_ref, wbp_ref, wba_ref, wo_ref, o_ref):
    d_ = x_ref.shape[-1]
    y_pool = _pool_tile(zh_ref[0], zp_ref[0], pl.program_id(1), pw_ref, ps_ref)
    bp = jnp.dot(y_pool.astype(BF16), wbp_ref[...], preferred_element_type=F32)
    ba = jnp.dot(ya_ref[0].astype(BF16), wba_ref[...], preferred_element_type=F32)
    zm = zm_ref[0]
    mixed = jax.nn.sigmoid(zm[:, :d_]) * bp + jax.nn.sigmoid(zm[:, d_:]) * ba
    proj = jnp.dot(mixed.astype(BF16), wo_ref[...], preferred_element_type=F32)
    o_ref[0] = x_ref[0] + g1_ref[0] * proj


def _merge(x, z_pool, y_attn, z_merge, gate1, pool_w, pool_scale, wbp, wba, wo):
    b_, s_, d_ = x.shape
    row = lambda w: pl.BlockSpec((1, ROW_TILE, w), lambda b, i: (b, i, 0))
    full = lambda a: pl.BlockSpec(a.shape, lambda b, i: (0,) * a.ndim)
    halo_blocks = ROW_TILE // POOL_HALO
    halo = pl.BlockSpec((1, POOL_HALO, POOL_WIDTH), lambda b, i: (b, jnp.maximum(i * halo_blocks - 1, 0), 0))
    pool_scale = pool_scale.reshape(1, POOL_WIDTH)
    return pl.pallas_call(
        _merge_kernel,
        grid=(b_, s_ // ROW_TILE),
        in_specs=[row(d_), row(POOL_WIDTH), halo, row(Q_WIDTH), row(2 * d_),
                  pl.BlockSpec((1, 1, d_), lambda b, i: (b, 0, 0)),
                  full(pool_w), full(pool_scale), full(wbp), full(wba), full(wo)],
        out_specs=row(d_),
        out_shape=jax.ShapeDtypeStruct((b_, s_, d_), F32),
        compiler_params=pltpu.CompilerParams(
            dimension_semantics=("parallel", "parallel"), vmem_limit_bytes=VMEM_LIMIT),
    )(x, z_pool, z_pool, y_attn, z_merge, gate1, pool_w, pool_scale, wbp, wba, wo)


def _ada_kernel(c_ref, w_ref, b_ref, o_ref):
    c = c_ref[...]
    o_ref[...] = jnp.dot(jax.nn.silu(c).astype(BF16), w_ref[...].astype(BF16),
                         preferred_element_type=F32) + b_ref[...]


def _ada(c, w, b):
    b_, d_ = c.shape
    n = w.shape[1]
    return pl.pallas_call(
        _ada_kernel,
        grid=(n // d_,),
        in_specs=[pl.BlockSpec((b_, d_), lambda j: (0, 0)),
                  pl.BlockSpec((d_, d_), lambda j: (0, j)),
                  pl.BlockSpec((1, d_), lambda j: (0, j))],
        out_specs=pl.BlockSpec((b_, d_), lambda j: (0, j)),
        out_shape=jax.ShapeDtypeStruct((b_, n), F32),
        compiler_params=pltpu.CompilerParams(dimension_semantics=("parallel",), vmem_limit_bytes=VMEM_LIMIT),
    )(c, w, b.reshape(1, n))


_CAND_ROWS = PEER_TOPK + 7 * 8 + 8


def _topk_rows(x, k):
    n = x.shape[0]
    rid = lax.broadcasted_iota(jnp.int32, x.shape, 0)
    vals, idxs = [], []
    for _ in range(k):
        m = jnp.max(x, axis=0, keepdims=True)
        idx = jnp.min(jnp.where(x == m, rid, n), axis=0, keepdims=True)
        vals.append(m)
        idxs.append(idx)
        x = jnp.where(rid == idx, -jnp.inf, x)
    return jnp.concatenate(vals, axis=0), jnp.concatenate(idxs, axis=0)


def _pair_grid(r0, r1, combine):
    parts = [combine(r0[0:1], r1)]
    parts += [combine(r0[a:a + 1], r1[0:8]) for a in range(1, 8)]
    parts.append(combine(r0[8:16], r1[0:1]))
    return jnp.concatenate(parts, axis=0)


def _peer_route_kernel(x_ref, g_ref, sc_ref, sh_ref, w_ref, keys_ref, h_ref, eid_ref, gw_ref):
    x = x_ref[0]
    y = x * lax.rsqrt(jnp.mean(x * x, axis=-1, keepdims=True) + EPS)
    h = y * g_ref[...] * (1.0 + sc_ref[0]) + sh_ref[0]
    h_ref[0] = h
    qv = jnp.dot(h.astype(BF16), w_ref[...], preferred_element_type=F32).astype(BF16)
    half = PEER_QDIM // 2
    eids, gws = [], []
    for hd in range(PEER_HEADS):
        tops = []
        for p in range(2):
            c = hd * 2 + p
            sc = lax.dot_general(keys_ref[c], qv[:, c * half:(c + 1) * half], _NT,
                                 preferred_element_type=F32)
            tops.append(_topk_rows(sc, PEER_TOPK))
        (s0, i0), (s1, i1) = tops
        cand = _pair_grid(s0, s1, lambda a, b: a + b)
        cid = _pair_grid(i0, i1, lambda a, b: a * (PEER_KEYS * EXPERT_ROWS) + (b * EXPERT_ROWS + TABLE_PAD))
        rid = lax.broadcasted_iota(jnp.int32, cand.shape, 0)
        top_s, top_e = [], []
        for _ in range(PEER_TOPK):
            m = jnp.max(cand, axis=0, keepdims=True)
            idx = jnp.min(jnp.where(cand == m, rid, _CAND_ROWS), axis=0, keepdims=True)
            hit = rid == idx
            top_s.append(m)
            top_e.append(jnp.sum(jnp.where(hit, cid, 0), axis=0, keepdims=True))
            cand = jnp.where(hit, -jnp.inf, cand)
        top_s = jnp.concatenate(top_s, axis=0)
        e = jnp.exp(top_s - top_s[0:1])
        gws.append(e / jnp.sum(e, axis=0, keepdims=True))
        eids.append(jnp.concatenate(top_e, axis=0))
    eids = jnp.concatenate(eids, axis=0)
    for c in range(PEER_TILE // PEER_EXPERT_TILE):
        eid_ref[c] = eids[:, c * PEER_EXPERT_TILE:(c + 1) * PEER_EXPERT_TILE]
    gw_ref[...] = jnp.transpose(jnp.concatenate(gws, axis=0))


def _peer_route(x, g, scale, shift, w_bf16, keys_bf16):
    b_, s_, d_ = x.shape
    n = w_bf16.shape[1]
    nk = PEER_HEADS * PEER_TOPK
    per_tile = PEER_TILE // PEER_EXPERT_TILE
    tiles = s_ // PEER_TILE
    row = lambda w: pl.BlockSpec((1, PEER_TILE, w), lambda b, i: (b, i, 0))
    return pl.pallas_call(
        _peer_route_kernel,
        grid=(b_, s_ // PEER_TILE),
        in_specs=[
            row(d_),
            pl.BlockSpec((1, d_), lambda b, i: (0, 0)),
            pl.BlockSpec((1, 1, d_), lambda b, i: (b, 0, 0)),
            pl.BlockSpec((1, 1, d_), lambda b, i: (b, 0, 0)),
            pl.BlockSpec((d_, n), lambda b, i: (0, 0)),
            pl.BlockSpec(keys_bf16.shape, lambda b, i: (0, 0, 0)),
        ],
        out_specs=[row(d_),
                   pl.BlockSpec((per_tile, nk, PEER_EXPERT_TILE), lambda b, i: (b * tiles + i, 0, 0)),
                   pl.BlockSpec((PEER_TILE, nk), lambda b, i: (b * tiles + i, 0))],
        out_shape=[
            jax.ShapeDtypeStruct((b_, s_, d_), F32),
            jax.ShapeDtypeStruct((b_ * s_ // PEER_EXPERT_TILE, nk, PEER_EXPERT_TILE), jnp.int32),
            jax.ShapeDtypeStruct((b_ * s_, nk), F32),
        ],
        compiler_params=pltpu.CompilerParams(
            dimension_semantics=("parallel", "parallel"), vmem_limit_bytes=VMEM_LIMIT),
    )(x, g.reshape(1, d_), scale, shift, w_bf16, keys_bf16)


_HI_MASK = -65536


def _pack_table(t):
    e_, d_ = t.shape
    bits = lax.bitcast_convert_type(t.astype(jnp.bfloat16), jnp.uint16).astype(jnp.uint32)
    words = (bits[:, d_ // 2:] << 16) | bits[:, :d_ // 2]
    return lax.bitcast_convert_type(words, jnp.int32).reshape(e_ * EXPERT_ROWS, LANES)


def _expert_words(tab_vmem, row):
    return tab_vmem[pl.ds(pl.multiple_of(row, EXPERT_ROWS), EXPERT_ROWS), :]


def _expert_pair_words(tab_vmem, row_a, row_b, sub):
    top = tab_vmem[pl.ds(pl.multiple_of(row_a, EXPERT_ROWS), 2 * EXPERT_ROWS), :]
    bot = tab_vmem[pl.ds(pl.multiple_of(row_b - EXPERT_ROWS, EXPERT_ROWS), 2 * EXPERT_ROWS), :]
    return jnp.where(sub < EXPERT_ROWS, top, bot)


def _unpack(words):
    lo = lax.bitcast_convert_type(lax.shift_left(words, 16), F32)
    hi = lax.bitcast_convert_type(words & _HI_MASK, F32)
    return lo, hi


def _load_resident(j, tab_hbm, tab_vmem, eid_ref, idx_smem, sem):
    @pl.when(j == 0)
    def _():
        n_rows = tab_hbm.shape[0]
        zeros = jnp.zeros((TABLE_PAD, LANES), jnp.int32)
        tab_vmem[pl.ds(0, TABLE_PAD), :] = zeros
        tab_vmem[pl.ds(TABLE_PAD + n_rows, TABLE_PAD), :] = zeros
        cp = pltpu.make_async_copy(tab_hbm, tab_vmem.at[pl.ds(TABLE_PAD, n_rows)], sem)
        cp.start()
        cp.wait()

    copies = [pltpu.make_async_copy(eid_ref.at[0, k], idx_smem[k], sem) for k in range(len(idx_smem))]
    for cp in copies:
        cp.start()
    for cp in copies:
        cp.wait()


def _fold_pairs(q, sub):
    m2, m1 = (sub & 2) == 0, (sub & 1) == 0
    u = [jnp.where(m2, q[i], q[i + 2]) + jnp.where(m2, pltpu.roll(q[i], 6, axis=0), pltpu.roll(q[i + 2], 2, axis=0))
         for i in range(2)]
    return jnp.where(m1, u[0], u[1]) + jnp.where(m1, pltpu.roll(u[0], 7, axis=0), pltpu.roll(u[1], 1, axis=0))


def _peer_u_kernel(eid_ref, gw_ref, h_ref, tab_hbm, w_ref, *scratch):
    j = pl.program_id(0)
    nk = PEER_HEADS * PEER_TOPK
    idx_smem, (tab_vmem, acc_ref, sem) = scratch[:nk], scratch[nk:]
    half = D_MODEL // 2 // LANES
    _load_resident(j, tab_hbm, tab_vmem, eid_ref, idx_smem, sem)
    sub = lax.broadcasted_iota(jnp.int32, (2 * half, LANES), 0)
    lane = lax.broadcasted_iota(jnp.int32, (nk, PEER_EXPERT_TILE), 1)
    acc_ref[...] = jnp.zeros((nk, PEER_EXPERT_TILE), F32)

    def token(t):
        row = h_ref[pl.ds(t, 1), :]

        def seg(s):
            return jnp.broadcast_to(row[:, s * LANES:(s + 1) * LANES], (2 * half, LANES))

        def rows_of(first):
            out = seg(first + half - 1)
            for s in range(half - 2, -1, -1):
                out = jnp.where(sub % half == s, seg(first + s), out)
            return out

        x_lo, x_hi = rows_of(0), rows_of(half)
        folded = []
        for gi in range(nk // 8):
            q = []
            for i in range(4):
                k = gi * 8 + i
                words = _expert_pair_words(tab_vmem, idx_smem[k][t], idx_smem[k + 4][t], sub)
                lo, hi = _unpack(words)
                q.append(lo * x_lo + hi * x_hi)
            folded.append(_fold_pairs(q, sub))
        return jnp.sum(jnp.concatenate(folded, axis=0), axis=-1, keepdims=True)

    def body(tb, carry):
        acc = acc_ref[...]
        for u in range(PEER_TOKEN_UNROLL):
            t = tb * PEER_TOKEN_UNROLL + u
            acc = jnp.where(lane == t, token(t), acc)
        acc_ref[...] = acc
        return carry

    lax.fori_loop(0, PEER_EXPERT_TILE // PEER_TOKEN_UNROLL, body, 0)
    w_ref[...] = jax.nn.gelu(jnp.transpose(acc_ref[...])) * gw_ref[...]


def _peer_v_kernel(eid_ref, w_ref, x_ref, g2_ref, tab_hbm, o_ref, *scratch):
    j = pl.program_id(0)
    nk = PEER_HEADS * PEER_TOPK
    idx_smem, (wb_ref, tab_vmem, sem) = scratch[:nk], scratch[nk:]
    half = D_MODEL // 2 // LANES
    _load_resident(j, tab_hbm, tab_vmem, eid_ref, idx_smem, sem)
    n_acc = 4
    sub = lax.broadcasted_iota(jnp.int32, (2 * half, LANES), 0)

    def token(t, slot):
        wb_ref[slot] = jnp.transpose(jnp.broadcast_to(w_ref[pl.ds(t, 1), :], (nk, nk)))
        lo_acc = [jnp.zeros((2 * half, LANES), F32) for _ in range(n_acc)]
        hi_acc = [jnp.zeros((2 * half, LANES), F32) for _ in range(n_acc)]
        for p in range(nk // 2):
            k = 2 * p
            lo, hi = _unpack(_expert_pair_words(tab_vmem, idx_smem[k][t], idx_smem[k + 1][t], sub))
            wk = jnp.where(sub < half, wb_ref[slot, k:k + 1, :], wb_ref[slot, k + 1:k + 2, :])
            lo_acc[p % n_acc] = lo_acc[p % n_acc] + wk * lo
            hi_acc[p % n_acc] = hi_acc[p % n_acc] + wk * hi
        lo_sum, hi_sum = sum(lo_acc[1:], lo_acc[0]), sum(hi_acc[1:], hi_acc[0])
        ff = jnp.concatenate([lo_sum[:half] + lo_sum[half:], hi_sum[:half] + hi_sum[half:]], axis=0)
        ff_row = jnp.concatenate([ff[s:s + 1, :] for s in range(2 * half)], axis=1)
        o_ref[pl.ds(t, 1), :] = x_ref[pl.ds(t, 1), :] + g2_ref[0] * ff_row

    def body(tb, carry):
        for u in range(PEER_V_UNROLL):
            token(tb * PEER_V_UNROLL + u, u)
        return carry

    lax.fori_loop(0, PEER_EXPERT_TILE // PEER_V_UNROLL, body, 0)


def _peer_experts(eid, gw, h2, x1, gate2, u_tab, v_tab):
    b_, s_, d_ = x1.shape
    n_tok = b_ * s_
    nk = PEER_HEADS * PEER_TOPK
    tt = PEER_EXPERT_TILE
    nsteps = n_tok // tt
    steps_per_batch = s_ // tt
    tok = pl.BlockSpec((tt, d_), lambda j: (j, 0))
    sel = pl.BlockSpec((tt, nk), lambda j: (j, 0))
    slots = pl.BlockSpec((1, nk, tt), lambda j: (j, 0, 0))
    idx_scratch = [pltpu.SMEM((tt,), jnp.int32) for _ in range(nk)]
    params = pltpu.CompilerParams(dimension_semantics=("arbitrary",), vmem_limit_bytes=PEER_VMEM_LIMIT)
    eid3 = eid.reshape(nsteps, nk, tt)
    resident = (u_tab.shape[0] + 2 * TABLE_PAD, LANES)
    w = pl.pallas_call(
        _peer_u_kernel,
        grid=(nsteps,),
        in_specs=[slots, sel, tok, pl.BlockSpec(memory_space=pl.ANY)],
        out_specs=sel,
        out_shape=jax.ShapeDtypeStruct((n_tok, nk), F32),
        scratch_shapes=idx_scratch + [pltpu.VMEM(resident, jnp.int32), pltpu.VMEM((nk, tt), F32),
                                      pltpu.SemaphoreType.DMA],
        compiler_params=params,
    )(eid3, gw.reshape(n_tok, nk), h2.reshape(n_tok, d_), u_tab)
    out = pl.pallas_call(
        _peer_v_kernel,
        grid=(nsteps,),
        in_specs=[slots, sel, tok, pl.BlockSpec((1, 1, d_), lambda j: (j // steps_per_batch, 0, 0)),
                  pl.BlockSpec(memory_space=pl.ANY)],
        out_specs=tok,
        out_shape=jax.ShapeDtypeStruct((n_tok, d_), F32),
        scratch_shapes=idx_scratch + [pltpu.VMEM((PEER_V_UNROLL, nk, LANES), F32),
                                      pltpu.VMEM(resident, jnp.int32), pltpu.SemaphoreType.DMA],
        compiler_params=params,
    )(eid3, w, x1.reshape(n_tok, d_), gate2, v_tab)
    return out.reshape(b_, s_, d_)


def _permute_w_in(w):
    d_ = w.shape[0]
    kv_end = POOL_WIDTH + Q_WIDTH + 6 * KV_WIDTH
    per_group = HEADS_PER_GROUP * 3
    parts = [w[:, :kv_end], w[:, kv_end + GATE_COLS:]]
    for g in range(N_KV_GROUPS):
        parts.append(w[:, kv_end + g * per_group:kv_end + (g + 1) * per_group])
        parts.append(jnp.zeros((d_, LANES - per_group), w.dtype))
    return jnp.concatenate(parts, axis=1).astype(BF16)


def kernel(x, c, rel_bias, ada_w, ada_b, norm1_g, norm2_g, w_in, pool_w, pool_scale, cmp_pe_k, cmp_w1_k, cmp_w2_k, cmp_pe_v, cmp_w1_v, cmp_w2_v, q_norm_g, k_norm_g, w_branch_pool, w_branch_attn, w_out, peer_w_q, peer_sub_keys, peer_u, peer_v):
    b_, s_, d_ = x.shape
    l = 0
    ada = _ada(c, ada_w[l], ada_b[l]).reshape(b_, 6, 1, d_)
    shift1, scale1, gate1 = ada[:, 0], ada[:, 1], ada[:, 2]
    shift2, scale2, gate2 = ada[:, 3], ada[:, 4], ada[:, 5]

    (z_pool, z_q, ksel, vsel, kwin, vwin, kc_raw, vc_raw, z_merge, z_gate) = _in_proj(
        x, norm1_g[l], scale1, shift1, _permute_w_in(w_in[l]), k_norm_g[l])

    kc = _compress(kc_raw, cmp_pe_k[l], cmp_w1_k[l], cmp_w2_k[l], k_norm_g[l, 0], True)
    vc = _compress(vc_raw, cmp_pe_v[l], cmp_w1_v[l], cmp_w2_v[l], k_norm_g[l, 0], False)
    y_attn = _attention(z_q, z_gate, q_norm_g[l], kc, vc, ksel, vsel, kwin, vwin, rel_bias)

    x1 = _merge(x, z_pool, y_attn, z_merge, gate1, pool_w[l].astype(BF16), pool_scale[l],
                w_branch_pool[l].astype(BF16), w_branch_attn[l].astype(BF16), w_out[l].astype(BF16))

    keys = peer_sub_keys[l].reshape(PEER_HEADS * 2, PEER_KEYS, PEER_QDIM // 2).astype(BF16)
    h2, eid, gw = _peer_route(x1, norm2_g[l], scale2, shift2, peer_w_q[l].astype(BF16), keys)
    return _peer_experts(eid, gw, h2, x1, gate2, _pack_table(peer_u[l]), _pack_table(peer_v[l]))
```

```python
import math
from functools import partial

import jax
import jax.numpy as jnp
import numpy as np
from jax import lax
from jax.experimental import pallas as pl
from jax.experimental.pallas import tpu as pltpu

D_MODEL = 1024
POOL_WIDTH = 512
POOL_WINDOWS = (2, 4, 8, 16)
POOL_GROUP_DIM = 128
N_HEADS = 8
N_KV_GROUPS = 2
HEADS_PER_GROUP = 4
HEAD_DIM = 64
Q_WIDTH = 512
KV_WIDTH = 128
CMP_BLOCK = 32
CMP_STRIDE = 16
CMP_HIDDEN = 128
SEL_BLOCK = 64
SEL_TOPK = 16
WINDOW = 512
FORCE_SCORE = 1000.0
N_BUCKETS = 32
MAX_DISTANCE = 128
PEER_HEADS = 8
PEER_KEYS = 128
PEER_TOPK = 16
PEER_QDIM = 256
GATE_COLS = N_HEADS * 3
EPS = 1e-6
NEG_INF = -1e30

LANES = 128
VMEM_LIMIT = 48 * 1024 * 1024
ROW_TILE = 512
POOL_HALO = 16
TQ = 256
SEL_FEATS = 64
SEL_CHUNK = 512
V_WIDTH = 128
KV_PAD = 512
CMP_BAND_LO = 10
PEER_TILE = 256
PEER_EXPERT_TILE = 256
PEER_TOKEN_UNROLL = 16
PEER_V_UNROLL = 16
EXPERT_ROWS = D_MODEL // 2 // LANES
TABLE_PAD = 8
PEER_VMEM_LIMIT = 56 * 1024 * 1024

BF16 = jnp.bfloat16
F32 = jnp.float32
_NT = (((1,), (1,)), ((), ()))


def _rms_rows(x, g):
    return x * lax.rsqrt(jnp.mean(x * x, axis=-1, keepdims=True) + EPS) * g


def _in_proj_kernel(x_ref, g_ref, sc_ref, sh_ref, w_ref, kg_ref,
                    zpool_ref, zq_ref, ksel_ref, vsel_ref, kwin_ref, vwin_ref, kcr_ref, vcr_ref,
                    zmerge_ref, zgate_ref):
    j = pl.program_id(1)

    @pl.when(j == 0)
    def _():
        for ref in (ksel_ref, vsel_ref, kwin_ref, vwin_ref):
            ref[...] = jnp.zeros(ref.shape, ref.dtype)

    @pl.when(j > 0)
    def _():
        _in_proj_tile(j - 1, x_ref, g_ref, sc_ref, sh_ref, w_ref, kg_ref, zpool_ref, zq_ref, ksel_ref, vsel_ref,
                      kwin_ref, vwin_ref, kcr_ref, vcr_ref, zmerge_ref, zgate_ref)


def _in_proj_tile(i, x_ref, g_ref, sc_ref, sh_ref, w_ref, kg_ref,
                  zpool_ref, zq_ref, ksel_ref, vsel_ref, kwin_ref, vwin_ref, kcr_ref, vcr_ref,
                  zmerge_ref, zgate_ref):
    x = x_ref[0]
    y = x * lax.rsqrt(jnp.mean(x * x, axis=-1, keepdims=True) + EPS)
    h = y * g_ref[...] * (1.0 + sc_ref[0]) + sh_ref[0]
    z = jnp.dot(h.astype(BF16), w_ref[...], preferred_element_type=F32)
    ts = x.shape[0]
    zpool_ref[0] = z[:, :POOL_WIDTH]
    o = POOL_WIDTH
    zq_ref[0] = z[:, o:o + Q_WIDTH]
    o += Q_WIDTH
    pos = i * ts + lax.broadcasted_iota(jnp.int32, (ts, SEL_FEATS), 0)
    onehot = (pos // SEL_BLOCK == lax.broadcasted_iota(jnp.int32, (ts, SEL_FEATS), 1)).astype(BF16)
    one_col = (lax.broadcasted_iota(jnp.int32, (ts, V_WIDTH - HEAD_DIM), 1) == 0).astype(BF16)
    for g in range(N_KV_GROUPS):
        def col(k):
            return z[:, o + k * KV_WIDTH + g * HEAD_DIM:o + k * KV_WIDTH + (g + 1) * HEAD_DIM]
        kcr_ref[0, g] = col(0)
        vcr_ref[0, g] = col(1)
        ks = _rms_rows(col(2), kg_ref[1:2, :]).astype(BF16)
        ksel_ref[0, g] = jnp.concatenate([ks, onehot], axis=1)
        vsel_ref[0, g] = jnp.concatenate([col(3).astype(BF16), one_col], axis=1)
        kwin_ref[0, g] = _rms_rows(col(4), kg_ref[2:3, :]).astype(BF16)
        vwin_ref[0, g] = jnp.concatenate([col(5).astype(BF16), one_col], axis=1)
    o += 6 * KV_WIDTH
    zmerge_ref[0] = z[:, o:o + 2 * D_MODEL]
    o += 2 * D_MODEL
    zgate_ref[0] = jax.nn.sigmoid(z[:, o:o + N_KV_GROUPS * LANES])


def _in_proj(x, g, scale, shift, w_bf16, k_norm_g):
    b_, s_, d_ = x.shape
    n = w_bf16.shape[1]
    assert KV_PAD == ROW_TILE
    tile = lambda j: jnp.maximum(j - 1, 0)
    row = lambda w: pl.BlockSpec((1, ROW_TILE, w), lambda b, j: (b, tile(j), 0))
    grp = lambda w: pl.BlockSpec((1, N_KV_GROUPS, ROW_TILE, w), lambda b, j: (b, 0, tile(j), 0))
    padded = lambda w: pl.BlockSpec((1, N_KV_GROUPS, ROW_TILE, w), lambda b, j: (b, 0, j, 0))
    gshape = lambda w, dt: jax.ShapeDtypeStruct((b_, N_KV_GROUPS, s_, w), dt)
    pshape = lambda w, dt: jax.ShapeDtypeStruct((b_, N_KV_GROUPS, s_ + KV_PAD, w), dt)
    return pl.pallas_call(
        _in_proj_kernel,
        grid=(b_, s_ // ROW_TILE + 1),
        in_specs=[
            row(d_),
            pl.BlockSpec((1, d_), lambda b, i: (0, 0)),
            pl.BlockSpec((1, 1, d_), lambda b, i: (b, 0, 0)),
            pl.BlockSpec((1, 1, d_), lambda b, i: (b, 0, 0)),
            pl.BlockSpec((d_, n), lambda b, i: (0, 0)),
            pl.BlockSpec((3, HEAD_DIM), lambda b, i: (0, 0)),
        ],
        out_specs=[row(POOL_WIDTH), row(Q_WIDTH), padded(HEAD_DIM + SEL_FEATS), padded(V_WIDTH), padded(HEAD_DIM),
                   padded(V_WIDTH), grp(HEAD_DIM), grp(HEAD_DIM), row(2 * d_), row(N_KV_GROUPS * LANES)],
        out_shape=[
            jax.ShapeDtypeStruct((b_, s_, POOL_WIDTH), F32),
            jax.ShapeDtypeStruct((b_, s_, Q_WIDTH), F32),
            pshape(HEAD_DIM + SEL_FEATS, BF16), pshape(V_WIDTH, BF16), pshape(HEAD_DIM, BF16), pshape(V_WIDTH, BF16),
            gshape(HEAD_DIM, F32), gshape(HEAD_DIM, F32),
            jax.ShapeDtypeStruct((b_, s_, 2 * d_), F32),
            jax.ShapeDtypeStruct((b_, s_, N_KV_GROUPS * LANES), F32),
        ],
        compiler_params=pltpu.CompilerParams(
            dimension_semantics=("parallel", "arbitrary"), vmem_limit_bytes=VMEM_LIMIT),
    )(x, g.reshape(1, d_), scale, shift, w_bf16, k_norm_g)


def _compress_kernel(t_ref, pe_ref, w1_ref, w2_ref, g_ref, o_ref, *, normalize):
    half = CMP_STRIDE * HEAD_DIM
    n = t_ref.shape[2] // CMP_STRIDE
    w1 = w1_ref[...]
    a = jnp.zeros((n, CMP_HIDDEN), F32)
    b = jnp.zeros((n, CMP_HIDDEN), F32)
    for r in range(CMP_STRIDE):
        t_r = t_ref[0, 0, pl.ds(r, n, stride=CMP_STRIDE), :].astype(BF16)
        a = a + jnp.dot(t_r, w1[r * HEAD_DIM:(r + 1) * HEAD_DIM], preferred_element_type=F32)
        b = b + jnp.dot(t_r, w1[half + r * HEAD_DIM:half + (r + 1) * HEAD_DIM], preferred_element_type=F32)
    pe = jnp.dot(pe_ref[...], w1, preferred_element_type=F32)
    b_next = pltpu.roll(b, n - 1, axis=0)
    hid = jax.nn.gelu(a + b_next + pe)
    out = jnp.dot(hid.astype(BF16), w2_ref[...], preferred_element_type=F32)
    if normalize:
        out = _rms_rows(out, g_ref[...])
    o_ref[0, 0] = out.astype(BF16)


def _compress(t_raw, pe, w1, w2, g, normalize):
    b_, g_, s_, dh = t_raw.shape
    n_str = s_ // CMP_STRIDE
    return pl.pallas_call(
        partial(_compress_kernel, normalize=normalize),
        grid=(b_, g_),
        in_specs=[
            pl.BlockSpec((1, 1, s_, dh), lambda b, g: (b, g, 0, 0)),
            pl.BlockSpec((1, CMP_BLOCK * dh), lambda b, g: (0, 0)),
            pl.BlockSpec((CMP_BLOCK * dh, CMP_HIDDEN), lambda b, g: (0, 0)),
            pl.BlockSpec((CMP_HIDDEN, dh), lambda b, g: (0, 0)),
            pl.BlockSpec((1, dh), lambda b, g: (0, 0)),
        ],
        out_specs=pl.BlockSpec((1, 1, n_str, dh), lambda b, g: (b, g, 0, 0)),
        out_shape=jax.ShapeDtypeStruct((b_, g_, n_str, dh), BF16),
        compiler_params=pltpu.CompilerParams(
            dimension_semantics=("parallel", "parallel"), vmem_limit_bytes=VMEM_LIMIT),
    )(t_raw, pe.reshape(1, CMP_BLOCK * dh).astype(BF16), w1.astype(BF16), w2.astype(BF16), g.reshape(1, dh))


def _t5_bucket_np(rel):
    n = np.maximum(rel, 0)
    max_exact = N_BUCKETS // 2
    nf = np.maximum(n, 1).astype(np.float32)
    large = max_exact + (np.log(nf / max_exact) / math.log(MAX_DISTANCE / max_exact) * (N_BUCKETS - max_exact)).astype(np.int32)
    large = np.minimum(large, N_BUCKETS - 1)
    return np.where(n < max_exact, n, large)


def _bias_tiles(rel_bias, n_cmp_pad):
    far = rel_bias[N_BUCKETS - 1]
    ii = np.arange(TQ)[:, None]

    def lookup(dist):
        near = (dist >= 0) & (dist < MAX_DISTANCE)
        bucket = _t5_bucket_np(np.where(near, dist, MAX_DISTANCE))
        onehot = (jnp.asarray(bucket.astype(np.int8))[..., None] == jnp.arange(N_BUCKETS, dtype=jnp.int8)).astype(F32)
        b = jnp.dot(onehot, rel_bias - far, precision=lax.Precision.HIGHEST)
        b = jnp.transpose(b, (2, 0, 1))
        return b.reshape(N_KV_GROUPS, HEADS_PER_GROUP * dist.shape[0], dist.shape[1])

    jj = np.arange(TQ)[None, :]
    bt = jnp.stack([lookup(ii - jj), lookup(TQ + ii - jj)], axis=1)
    cc = np.arange(n_cmp_pad)[None, :]
    dist_c = ii - CMP_STRIDE * (cc - CMP_BAND_LO) - (CMP_BLOCK - 1)
    dist_c = np.where(cc <= CMP_BAND_LO + TQ // CMP_STRIDE, dist_c, -1)
    bc = lookup(dist_c)
    assert WINDOW >= 2 * TQ
    rows = HEADS_PER_GROUP * TQ
    i_r = np.tile(np.arange(TQ), HEADS_PER_GROUP)[:, None]
    first = jnp.asarray(np.where(jj > i_r, 0.0, NEG_INF), F32)
    last = jnp.where(jnp.asarray(jj <= i_r), bt[:, 0], NEG_INF)
    zeros = jnp.zeros((N_KV_GROUPS, rows, WINDOW - 2 * TQ), F32)
    wa = jnp.concatenate([jnp.broadcast_to(first, (N_KV_GROUPS, rows, TQ)), zeros, bt[:, 1], last], axis=-1)
    return wa, bc


def _sel_mapping_t(n_cmp_pad, n_cmp):
    m = np.zeros((SEL_FEATS, n_cmp_pad), np.float32)
    pos = np.arange(n_cmp)[:, None] * CMP_STRIDE + np.arange(CMP_BLOCK)[None, :]
    np.add.at(m, ((pos // SEL_BLOCK).ravel(), np.repeat(np.arange(n_cmp), CMP_BLOCK)), 1.0 / CMP_BLOCK)
    return m


def _logits(q, k, add=None, lo_cols=None):
    s = lax.dot_general(q, k, _NT, preferred_element_type=F32)
    if add is not None:
        s = s + add
    if lo_cols is not None:
        s = jnp.where(lax.broadcasted_iota(jnp.int32, s.shape, 1) < lo_cols, NEG_INF, s)
    return s


def _flash_step(s, v, m_sc, acc_sc):
    m_prev = m_sc[...]
    m_new = jnp.maximum(m_prev, jnp.max(s, axis=-1, keepdims=True))
    alpha = jnp.exp(m_prev - m_new)
    p = jnp.exp(s - jnp.concatenate([m_new] * (s.shape[1] // LANES), axis=1))
    acc_sc[...] = alpha * acc_sc[...] + jnp.dot(p.astype(BF16), v, preferred_element_type=F32)
    m_sc[...] = m_new


def _attn_kernel(zq_ref, gate_ref, qg_ref, kc_ref, vc_ref, ksel_ref, vsel_ref, kwin_ref, vwin_ref,
                 wa_ref, bc_ref, smap_ref, o_ref, m_sc, acc_sc):
    i = pl.program_id(2)
    rows = HEADS_PER_GROUP * TQ
    ncp = kc_ref.shape[2]
    scale = HEAD_DIM ** -0.5

    zq = zq_ref[0]
    qs = jnp.concatenate([zq[:, j * HEAD_DIM:(j + 1) * HEAD_DIM] for j in range(HEADS_PER_GROUP)], axis=0)
    qn = _rms_rows(qs, qg_ref[...]) * scale
    qb = qn.astype(BF16)

    lc = lax.dot_general(qb, kc_ref[0, 0], _NT, preferred_element_type=F32)
    lc = lc + pltpu.roll(bc_ref[0], (i * (TQ // CMP_STRIDE) + ncp - CMP_BAND_LO) % ncp, axis=1)
    t_c = i * TQ + lax.broadcasted_iota(jnp.int32, (rows, ncp), 0) % TQ
    n_c = lax.broadcasted_iota(jnp.int32, (rows, ncp), 1)
    valid_c = n_c * CMP_STRIDE + (CMP_BLOCK - 1) <= t_c
    lm = jnp.where(valid_c, lc, NEG_INF)
    e = jnp.where(valid_c, jnp.exp(lm - jnp.max(lm, axis=-1, keepdims=True)), 0.0)
    den = jnp.sum(e, axis=-1, keepdims=True)
    p_c = e / jnp.where(den > 0.0, den, 1.0)
    o_cmp = jnp.dot(p_c.astype(BF16), vc_ref[0, 0], preferred_element_type=F32)

    p_sum = p_c[0:TQ] + p_c[TQ:2 * TQ] + p_c[2 * TQ:3 * TQ] + p_c[3 * TQ:4 * TQ]
    p_hi = p_sum.astype(BF16)
    r1 = p_sum - p_hi.astype(F32)
    p_mid = r1.astype(BF16)
    p_lo = (r1 - p_mid.astype(F32)).astype(BF16)
    smap = smap_ref[...]
    imp = (lax.dot_general(smap, p_hi, _NT, preferred_element_type=F32)
           + lax.dot_general(smap, p_mid, _NT, preferred_element_type=F32)
           + lax.dot_general(smap, p_lo, _NT, preferred_element_type=F32))
    blk = lax.broadcasted_iota(jnp.int32, (SEL_FEATS, TQ), 0)
    t_s = i * TQ + lax.broadcasted_iota(jnp.int32, (SEL_FEATS, TQ), 1)
    cur = t_s // SEL_BLOCK
    forced = (blk == 0) | (blk == cur) | (blk == cur - 1)
    visible = blk * SEL_BLOCK <= t_s
    score = jnp.where(visible, imp + jnp.where(forced, FORCE_SCORE, 0.0), -1.0)
    rank = jnp.zeros((SEL_FEATS, TQ), jnp.int32)
    for sp in range(SEL_FEATS):
        row = score[sp:sp + 1, :]
        beats = (row > score) | ((row == score) & (blk > sp))
        rank = rank + beats.astype(jnp.int32)
    pen_t = jnp.where(rank < SEL_TOPK, 0.0, NEG_INF)
    pen = jnp.transpose(pen_t).astype(BF16)
    q_aug = jnp.concatenate([qb, jnp.concatenate([pen] * HEADS_PER_GROUP, axis=0)], axis=1)

    def keys(ref, pos, n):
        return ref[0, 0, pl.ds(pl.multiple_of(pos + KV_PAD, TQ), n), :]

    m_sc[...] = jnp.full((rows, LANES), NEG_INF, F32)
    acc_sc[...] = jnp.zeros((rows, V_WIDTH), F32)
    per_chunk = SEL_CHUNK // TQ
    n_plain = jnp.maximum(i - 1, 0)
    rem = n_plain % per_chunk

    @pl.when(rem > 0)
    def _():
        pos = (rem - per_chunk) * TQ
        s = _logits(q_aug, keys(ksel_ref, pos, SEL_CHUNK), lo_cols=-pos)
        _flash_step(s, keys(vsel_ref, pos, SEL_CHUNK), m_sc, acc_sc)

    def sel_body(c, carry):
        pos = (rem + c * per_chunk) * TQ
        _flash_step(_logits(q_aug, keys(ksel_ref, pos, SEL_CHUNK)), keys(vsel_ref, pos, SEL_CHUNK), m_sc, acc_sc)
        return carry

    lax.fori_loop(0, n_plain // per_chunk, sel_body, 0)
    pos = (i - 1) * TQ
    s = _logits(q_aug, keys(ksel_ref, pos, 2 * TQ), add=wa_ref[0, :, WINDOW - TQ:], lo_cols=-pos)
    _flash_step(s, keys(vsel_ref, pos, 2 * TQ), m_sc, acc_sc)
    acc = acc_sc[...]
    o_sel = acc[:, :HEAD_DIM] / acc[:, HEAD_DIM:HEAD_DIM + 1]

    pos = i * TQ - WINDOW
    s = _logits(qb, keys(kwin_ref, pos, WINDOW + TQ), add=wa_ref[0], lo_cols=-pos)
    p = jnp.exp(s - jnp.max(s, axis=-1, keepdims=True))
    acc = jnp.dot(p.astype(BF16), keys(vwin_ref, pos, WINDOW + TQ), preferred_element_type=F32)
    o_win = acc[:, :HEAD_DIM] / acc[:, HEAD_DIM:HEAD_DIM + 1]

    gate = gate_ref[0]
    outs = []
    for j in range(HEADS_PER_GROUP):
        sl = slice(j * TQ, (j + 1) * TQ)
        outs.append(gate[:, 3 * j:3 * j + 1] * o_cmp[sl] + gate[:, 3 * j + 1:3 * j + 2] * o_sel[sl]
                    + gate[:, 3 * j + 2:3 * j + 3] * o_win[sl])
    o_ref[0] = jnp.concatenate(outs, axis=1)


def _attention(z_q, z_gate, q_norm_g, kc, vc, ksel, vsel, kwin, vwin, rel_bias):
    b_, s_, _ = z_q.shape
    ncp = kc.shape[2]
    n_cmp = ncp - CMP_BLOCK // CMP_STRIDE + 1
    assert s_ // SEL_BLOCK <= SEL_FEATS and s_ % TQ == 0
    wa, bc = _bias_tiles(rel_bias, ncp)
    smap_t = jnp.asarray(_sel_mapping_t(ncp, n_cmp), BF16)
    rows = HEADS_PER_GROUP * TQ
    gw = HEADS_PER_GROUP * HEAD_DIM
    kv = lambda w: pl.BlockSpec((1, 1, s_ + KV_PAD, w), lambda b, g, i: (b, g, 0, 0))
    cm = pl.BlockSpec((1, 1, ncp, HEAD_DIM), lambda b, g, i: (b, g, 0, 0))
    return pl.pallas_call(
        _attn_kernel,
        grid=(b_, N_KV_GROUPS, s_ // TQ),
        in_specs=[
            pl.BlockSpec((1, TQ, gw), lambda b, g, i: (b, i, g)),
            pl.BlockSpec((1, TQ, LANES), lambda b, g, i: (b, i, g)),
            pl.BlockSpec((1, HEAD_DIM), lambda b, g, i: (0, 0)),
            cm, cm, kv(HEAD_DIM + SEL_FEATS), kv(V_WIDTH), kv(HEAD_DIM), kv(V_WIDTH),
            pl.BlockSpec((1, rows, WINDOW + TQ), lambda b, g, i: (g, 0, 0)),
            pl.BlockSpec((1, rows, ncp), lambda b, g, i: (g, 0, 0)),
            pl.BlockSpec((SEL_FEATS, ncp), lambda b, g, i: (0, 0)),
        ],
        out_specs=pl.BlockSpec((1, TQ, gw), lambda b, g, i: (b, i, g)),
        out_shape=jax.ShapeDtypeStruct((b_, s_, N_KV_GROUPS * gw), F32),
        scratch_shapes=[pltpu.VMEM((rows, LANES), F32), pltpu.VMEM((rows, V_WIDTH), F32)],
        compiler_params=pltpu.CompilerParams(
            dimension_semantics=("parallel", "parallel", "arbitrary"), vmem_limit_bytes=VMEM_LIMIT),
    )(z_q, z_gate, q_norm_g.reshape(1, HEAD_DIM), kc, vc, ksel, vsel, kwin, vwin, wa, bc, smap_t)


def _pool_tile(halo, tile, i, pw_ref, ps_ref):
    ts = tile.shape[0]
    ext = jnp.concatenate([jnp.where(i > 0, halo, 0.0), tile], axis=0)
    t = i * ts + lax.broadcasted_iota(jnp.int32, (ts, 1), 0)
    outs = []
    for gi, w in enumerate(POOL_WINDOWS):
        sl = slice(gi * POOL_GROUP_DIM, (gi + 1) * POOL_GROUP_DIM)
        run = ext[:, sl]
        span = 1
        while span < w:
            run = run + pltpu.roll(run, span, axis=0)
            span *= 2
        cnt = jnp.minimum(t + 1, w).astype(F32)
        pooled = run[POOL_HALO:] / cnt - tile[:, sl]
        outs.append(jnp.dot(pooled.astype(BF16), pw_ref[gi], preferred_element_type=F32))
    return jnp.concatenate(outs, axis=1) * ps_ref[...]


def _merge_kernel(x_ref, zp_ref, zh_ref, ya_ref, zm_ref, g1_ref, pw_ref, ps_ref, wbp_ref, wba_ref, wo_ref, o_ref):
    d_ = x_ref.shape[-1]
    y_pool = _pool_tile(zh_ref[0], zp_ref[0], pl.program_id(1), pw_ref, ps_ref)
    bp = jnp.dot(y_pool.astype(BF16), wbp_ref[...], preferred_element_type=F32)
    ba = jnp.dot(ya_ref[0].astype(BF16), wba_ref[...], preferred_element_type=F32)
    zm = zm_ref[0]
    mixed = jax.nn.sigmoid(zm[:, :d_]) * bp + jax.nn.sigmoid(zm[:, d_:]) * ba
    proj = jnp.dot(mixed.astype(BF16), wo_ref[...], preferred_element_type=F32)
    o_ref[0] = x_ref[0] + g1_ref[0] * proj


def _merge(x, z_pool, y_attn, z_merge, gate1, pool_w, pool_scale, wbp, wba, wo):
    b_, s_, d_ = x.shape
    row = lambda w: pl.BlockSpec((1, ROW_TILE, w), lambda b, i: (b, i, 0))
    full = lambda a: pl.BlockSpec(a.shape, lambda b, i: (0,) * a.ndim)
    halo_blocks = ROW_TILE // POOL_HALO
    halo = pl.BlockSpec((1, POOL_HALO, POOL_WIDTH), lambda b, i: (b, jnp.maximum(i * halo_blocks - 1, 0), 0))
    pool_scale = pool_scale.reshape(1, POOL_WIDTH)
    return pl.pallas_call(
        _merge_kernel,
        grid=(b_, s_ // ROW_TILE),
        in_specs=[row(d_), row(POOL_WIDTH), halo, row(Q_WIDTH), row(2 * d_),
                  pl.BlockSpec((1, 1, d_), lambda b, i: (b, 0, 0)),
                  full(pool_w), full(pool_scale), full(wbp), full(wba), full(wo)],
        out_specs=row(d_),
        out_shape=jax.ShapeDtypeStruct((b_, s_, d_), F32),
        compiler_params=pltpu.CompilerParams(
            dimension_semantics=("parallel", "parallel"), vmem_limit_bytes=VMEM_LIMIT),
    )(x, z_pool, z_pool, y_attn, z_merge, gate1, pool_w, pool_scale, wbp, wba, wo)


def _ada_kernel(c_ref, w_ref, b_ref, o_ref):
    c = c_ref[...]
    o_ref[...] = jnp.dot(jax.nn.silu(c).astype(BF16), w_ref[...].astype(BF16),
                         preferred_element_type=F32) + b_ref[...]


def _ada(c, w, b):
    b_, d_ = c.shape
    n = w.shape[1]
    return pl.pallas_call(
        _ada_kernel,
        grid=(n // d_,),
        in_specs=[pl.BlockSpec((b_, d_), lambda j: (0, 0)),
                  pl.BlockSpec((d_, d_), lambda j: (0, j)),
                  pl.BlockSpec((1, d_), lambda j: (0, j))],
        out_specs=pl.BlockSpec((b_, d_), lambda j: (0, j)),
        out_shape=jax.ShapeDtypeStruct((b_, n), F32),
        compiler_params=pltpu.CompilerParams(dimension_semantics=("parallel",), vmem_limit_bytes=VMEM_LIMIT),
    )(c, w, b.reshape(1, n))


_CAND_ROWS = PEER_TOPK + 7 * 8 + 8


def _topk_rows(x, k):
    n = x.shape[0]
    rid = lax.broadcasted_iota(jnp.int32, x.shape, 0)
    vals, idxs = [], []
    for _ in range(k):
        m = jnp.max(x, axis=0, keepdims=True)
        idx = jnp.min(jnp.where(x == m, rid, n), axis=0, keepdims=True)
        vals.append(m)
        idxs.append(idx)
        x = jnp.where(rid == idx, -jnp.inf, x)
    return jnp.concatenate(vals, axis=0), jnp.concatenate(idxs, axis=0)


def _pair_grid(r0, r1, combine):
    parts = [combine(r0[0:1], r1)]
    parts += [combine(r0[a:a + 1], r1[0:8]) for a in range(1, 8)]
    parts.append(combine(r0[8:16], r1[0:1]))
    return jnp.concatenate(parts, axis=0)


def _peer_route_kernel(x_ref, g_ref, sc_ref, sh_ref, w_ref, keys_ref, h_ref, eid_ref, gw_ref):
    x = x_ref[0]
    y = x * lax.rsqrt(jnp.mean(x * x, axis=-1, keepdims=True) + EPS)
    h = y * g_ref[...] * (1.0 + sc_ref[0]) + sh_ref[0]
    h_ref[0] = h
    qv = jnp.dot(h.astype(BF16), w_ref[...], preferred_element_type=F32).astype(BF16)
    half = PEER_QDIM // 2
    eids, gws = [], []
    for hd in range(PEER_HEADS):
        tops = []
        for p in range(2):
            c = hd * 2 + p
            sc = lax.dot_general(keys_ref[c], qv[:, c * half:(c + 1) * half], _NT,
                                 preferred_element_type=F32)
            tops.append(_topk_rows(sc, PEER_TOPK))
        (s0, i0), (s1, i1) = tops
        cand = _pair_grid(s0, s1, lambda a, b: a + b)
        cid = _pair_grid(i0, i1, lambda a, b: a * (PEER_KEYS * EXPERT_ROWS) + (b * EXPERT_ROWS + TABLE_PAD))
        rid = lax.broadcasted_iota(jnp.int32, cand.shape, 0)
        top_s, top_e = [], []
        for _ in range(PEER_TOPK):
            m = jnp.max(cand, axis=0, keepdims=True)
            idx = jnp.min(jnp.where(cand == m, rid, _CAND_ROWS), axis=0, keepdims=True)
            hit = rid == idx
            top_s.append(m)
            top_e.append(jnp.sum(jnp.where(hit, cid, 0), axis=0, keepdims=True))
            cand = jnp.where(hit, -jnp.inf, cand)
        top_s = jnp.concatenate(top_s, axis=0)
        e = jnp.exp(top_s - top_s[0:1])
        gws.append(e / jnp.sum(e, axis=0, keepdims=True))
        eids.append(jnp.concatenate(top_e, axis=0))
    eids = jnp.concatenate(eids, axis=0)
    for c in range(PEER_TILE // PEER_EXPERT_TILE):
        eid_ref[c] = eids[:, c * PEER_EXPERT_TILE:(c + 1) * PEER_EXPERT_TILE]
    gw_ref[...] = jnp.transpose(jnp.concatenate(gws, axis=0))


def _peer_route(x, g, scale, shift, w_bf16, keys_bf16):
    b_, s_, d_ = x.shape
    n = w_bf16.shape[1]
    nk = PEER_HEADS * PEER_TOPK
    per_tile = PEER_TILE // PEER_EXPERT_TILE
    tiles = s_ // PEER_TILE
    row = lambda w: pl.BlockSpec((1, PEER_TILE, w), lambda b, i: (b, i, 0))
    return pl.pallas_call(
        _peer_route_kernel,
        grid=(b_, s_ // PEER_TILE),
        in_specs=[
            row(d_),
            pl.BlockSpec((1, d_), lambda b, i: (0, 0)),
            pl.BlockSpec((1, 1, d_), lambda b, i: (b, 0, 0)),
            pl.BlockSpec((1, 1, d_), lambda b, i: (b, 0, 0)),
            pl.BlockSpec((d_, n), lambda b, i: (0, 0)),
            pl.BlockSpec(keys_bf16.shape, lambda b, i: (0, 0, 0)),
        ],
        out_specs=[row(d_),
                   pl.BlockSpec((per_tile, nk, PEER_EXPERT_TILE), lambda b, i: (b * tiles + i, 0, 0)),
                   pl.BlockSpec((PEER_TILE, nk), lambda b, i: (b * tiles + i, 0))],
        out_shape=[
            jax.ShapeDtypeStruct((b_, s_, d_), F32),
            jax.ShapeDtypeStruct((b_ * s_ // PEER_EXPERT_TILE, nk, PEER_EXPERT_TILE), jnp.int32),
            jax.ShapeDtypeStruct((b_ * s_, nk), F32),
        ],
        compiler_params=pltpu.CompilerParams(
            dimension_semantics=("parallel", "parallel"), vmem_limit_bytes=VMEM_LIMIT),
    )(x, g.reshape(1, d_), scale, shift, w_bf16, keys_bf16)


_HI_MASK = -65536


def _pack_table(t):
    e_, d_ = t.shape
    bits = lax.bitcast_convert_type(t.astype(jnp.bfloat16), jnp.uint16).astype(jnp.uint32)
    words = (bits[:, d_ // 2:] << 16) | bits[:, :d_ // 2]
    return lax.bitcast_convert_type(words, jnp.int32).reshape(e_ * EXPERT_ROWS, LANES)


def _expert_words(tab_vmem, row):
    return tab_vmem[pl.ds(pl.multiple_of(row, EXPERT_ROWS), EXPERT_ROWS), :]


def _expert_pair_words(tab_vmem, row_a, row_b, sub):
    top = tab_vmem[pl.ds(pl.multiple_of(row_a, EXPERT_ROWS), 2 * EXPERT_ROWS), :]
    bot = tab_vmem[pl.ds(pl.multiple_of(row_b - EXPERT_ROWS, EXPERT_ROWS), 2 * EXPERT_ROWS), :]
    return jnp.where(sub < EXPERT_ROWS, top, bot)


def _unpack(words):
    lo = lax.bitcast_convert_type(lax.shift_left(words, 16), F32)
    hi = lax.bitcast_convert_type(words & _HI_MASK, F32)
    return lo, hi


def _load_resident(j, tab_hbm, tab_vmem, eid_ref, idx_smem, sem):
    @pl.when(j == 0)
    def _():
        n_rows = tab_hbm.shape[0]
        zeros = jnp.zeros((TABLE_PAD, LANES), jnp.int32)
        tab_vmem[pl.ds(0, TABLE_PAD), :] = zeros
        tab_vmem[pl.ds(TABLE_PAD + n_rows, TABLE_PAD), :] = zeros
        cp = pltpu.make_async_copy(tab_hbm, tab_vmem.at[pl.ds(TABLE_PAD, n_rows)], sem)
        cp.start()
        cp.wait()

    copies = [pltpu.make_async_copy(eid_ref.at[0, k], idx_smem[k], sem) for k in range(len(idx_smem))]
    for cp in copies:
        cp.start()
    for cp in copies:
        cp.wait()


def _fold_pairs(q, sub):
    m2, m1 = (sub & 2) == 0, (sub & 1) == 0
    u = [jnp.where(m2, q[i], q[i + 2]) + jnp.where(m2, pltpu.roll(q[i], 6, axis=0), pltpu.roll(q[i + 2], 2, axis=0))
         for i in range(2)]
    return jnp.where(m1, u[0], u[1]) + jnp.where(m1, pltpu.roll(u[0], 7, axis=0), pltpu.roll(u[1], 1, axis=0))


def _peer_u_kernel(eid_ref, gw_ref, h_ref, tab_hbm, w_ref, *scratch):
    j = pl.program_id(0)
    nk = PEER_HEADS * PEER_TOPK
    idx_smem, (tab_vmem, acc_ref, sem) = scratch[:nk], scratch[nk:]
    half = D_MODEL // 2 // LANES
    _load_resident(j, tab_hbm, tab_vmem, eid_ref, idx_smem, sem)
    sub = lax.broadcasted_iota(jnp.int32, (2 * half, LANES), 0)
    lane = lax.broadcasted_iota(jnp.int32, (nk, PEER_EXPERT_TILE), 1)
    acc_ref[...] = jnp.zeros((nk, PEER_EXPERT_TILE), F32)

    def token(t):
        row = h_ref[pl.ds(t, 1), :]

        def seg(s):
            return jnp.broadcast_to(row[:, s * LANES:(s + 1) * LANES], (2 * half, LANES))

        def rows_of(first):
            out = seg(first + half - 1)
            for s in range(half - 2, -1, -1):
                out = jnp.where(sub % half == s, seg(first + s), out)
            return out

        x_lo, x_hi = rows_of(0), rows_of(half)
        folded = []
        for gi in range(nk // 8):
            q = []
            for i in range(4):
                k = gi * 8 + i
                words = _expert_pair_words(tab_vmem, idx_smem[k][t], idx_smem[k + 4][t], sub)
                lo, hi = _unpack(words)
                q.append(lo * x_lo + hi * x_hi)
            folded.append(_fold_pairs(q, sub))
        return jnp.sum(jnp.concatenate(folded, axis=0), axis=-1, keepdims=True)

    def body(tb, carry):
        acc = acc_ref[...]
        for u in range(PEER_TOKEN_UNROLL):
            t = tb * PEER_TOKEN_UNROLL + u
            acc = jnp.where(lane == t, token(t), acc)
        acc_ref[...] = acc
        return carry

    lax.fori_loop(0, PEER_EXPERT_TILE // PEER_TOKEN_UNROLL, body, 0)
    w_ref[...] = jax.nn.gelu(jnp.transpose(acc_ref[...])) * gw_ref[...]


def _peer_v_kernel(eid_ref, w_ref, x_ref, g2_ref, tab_hbm, o_ref, *scratch):
    j = pl.program_id(0)
    nk = PEER_HEADS * PEER_TOPK
    idx_smem, (wb_ref, tab_vmem, sem) = scratch[:nk], scratch[nk:]
    half = D_MODEL // 2 // LANES
    _load_resident(j, tab_hbm, tab_vmem, eid_ref, idx_smem, sem)
    n_acc = 4
    sub = lax.broadcasted_iota(jnp.int32, (2 * half, LANES), 0)

    def token(t, slot):
        wb_ref[slot] = jnp.transpose(jnp.broadcast_to(w_ref[pl.ds(t, 1), :], (nk, nk)))
        lo_acc = [jnp.zeros((2 * half, LANES), F32) for _ in range(n_acc)]
        hi_acc = [jnp.zeros((2 * half, LANES), F32) for _ in range(n_acc)]
        for p in range(nk // 2):
            k = 2 * p
            lo, hi = _unpack(_expert_pair_words(tab_vmem, idx_smem[k][t], idx_smem[k + 1][t], sub))
            wk = jnp.where(sub < half, wb_ref[slot, k:k + 1, :], wb_ref[slot, k + 1:k + 2, :])
            lo_acc[p % n_acc] = lo_acc[p % n_acc] + wk * lo
            hi_acc[p % n_acc] = hi_acc[p % n_acc] + wk * hi
        lo_sum, hi_sum = sum(lo_acc[1:], lo_acc[0]), sum(hi_acc[1:], hi_acc[0])
        ff = jnp.concatenate([lo_sum[:half] + lo_sum[half:], hi_sum[:half] + hi_sum[half:]], axis=0)
        ff_row = jnp.concatenate([ff[s:s + 1, :] for s in range(2 * half)], axis=1)
        o_ref[pl.ds(t, 1), :] = x_ref[pl.ds(t, 1), :] + g2_ref[0] * ff_row

    def body(tb, carry):
        for u in range(PEER_V_UNROLL):
            token(tb * PEER_V_UNROLL + u, u)
        return carry

    lax.fori_loop(0, PEER_EXPERT_TILE // PEER_V_UNROLL, body, 0)


def _peer_experts(eid, gw, h2, x1, gate2, u_tab, v_tab):
    b_, s_, d_ = x1.shape
    n_tok = b_ * s_
    nk = PEER_HEADS * PEER_TOPK
    tt = PEER_EXPERT_TILE
    nsteps = n_tok // tt
    steps_per_batch = s_ // tt
    tok = pl.BlockSpec((tt, d_), lambda j: (j, 0))
    sel = pl.BlockSpec((tt, nk), lambda j: (j, 0))
    slots = pl.BlockSpec((1, nk, tt), lambda j: (j, 0, 0))
    idx_scratch = [pltpu.SMEM((tt,), jnp.int32) for _ in range(nk)]
    params = pltpu.CompilerParams(dimension_semantics=("arbitrary",), vmem_limit_bytes=PEER_VMEM_LIMIT)
    eid3 = eid.reshape(nsteps, nk, tt)
    resident = (u_tab.shape[0] + 2 * TABLE_PAD, LANES)
    w = pl.pallas_call(
        _peer_u_kernel,
        grid=(nsteps,),
        in_specs=[slots, sel, tok, pl.BlockSpec(memory_space=pl.ANY)],
        out_specs=sel,
        out_shape=jax.ShapeDtypeStruct((n_tok, nk), F32),
        scratch_shapes=idx_scratch + [pltpu.VMEM(resident, jnp.int32), pltpu.VMEM((nk, tt), F32),
                                      pltpu.SemaphoreType.DMA],
        compiler_params=params,
    )(eid3, gw.reshape(n_tok, nk), h2.reshape(n_tok, d_), u_tab)
    out = pl.pallas_call(
        _peer_v_kernel,
        grid=(nsteps,),
        in_specs=[slots, sel, tok, pl.BlockSpec((1, 1, d_), lambda j: (j // steps_per_batch, 0, 0)),
                  pl.BlockSpec(memory_space=pl.ANY)],
        out_specs=tok,
        out_shape=jax.ShapeDtypeStruct((n_tok, d_), F32),
        scratch_shapes=idx_scratch + [pltpu.VMEM((PEER_V_UNROLL, nk, LANES), F32),
                                      pltpu.VMEM(resident, jnp.int32), pltpu.SemaphoreType.DMA],
        compiler_params=params,
    )(eid3, w, x1.reshape(n_tok, d_), gate2, v_tab)
    return out.reshape(b_, s_, d_)


def _permute_w_in(w):
    d_ = w.shape[0]
    kv_end = POOL_WIDTH + Q_WIDTH + 6 * KV_WIDTH
    per_group = HEADS_PER_GROUP * 3
    parts = [w[:, :kv_end], w[:, kv_end + GATE_COLS:]]
    for g in range(N_KV_GROUPS):
        parts.append(w[:, kv_end + g * per_group:kv_end + (g + 1) * per_group])
        parts.append(jnp.zeros((d_, LANES - per_group), w.dtype))
    return jnp.concatenate(parts, axis=1).astype(BF16)


def kernel(x, c, rel_bias, ada_w, ada_b, norm1_g, norm2_g, w_in, pool_w, pool_scale, cmp_pe_k, cmp_w1_k, cmp_w2_k, cmp_pe_v, cmp_w1_v, cmp_w2_v, q_norm_g, k_norm_g, w_branch_pool, w_branch_attn, w_out, peer_w_q, peer_sub_keys, peer_u, peer_v):
    b_, s_, d_ = x.shape
    l = 0
    ada = _ada(c, ada_w[l], ada_b[l]).reshape(b_, 6, 1, d_)
    shift1, scale1, gate1 = ada[:, 0], ada[:, 1], ada[:, 2]
    shift2, scale2, gate2 = ada[:, 3], ada[:, 4], ada[:, 5]

    (z_pool, z_q, ksel, vsel, kwin, vwin, kc_raw, vc_raw, z_merge, z_gate) = _in_proj(
        x, norm1_g[l], scale1, shift1, _permute_w_in(w_in[l]), k_norm_g[l])

    kc = _compress(kc_raw, cmp_pe_k[l], cmp_w1_k[l], cmp_w2_k[l], k_norm_g[l, 0], True)
    vc = _compress(vc_raw, cmp_pe_v[l], cmp_w1_v[l], cmp_w2_v[l], k_norm_g[l, 0], False)
    y_attn = _attention(z_q, z_gate, q_norm_g[l], kc, vc, ksel, vsel, kwin, vwin, rel_bias)

    x1 = _merge(x, z_pool, y_attn, z_merge, gate1, pool_w[l].astype(BF16), pool_scale[l],
                w_branch_pool[l].astype(BF16), w_branch_attn[l].astype(BF16), w_out[l].astype(BF16))

    keys = peer_sub_keys[l].reshape(PEER_HEADS * 2, PEER_KEYS, PEER_QDIM // 2).astype(BF16)
    h2, eid, gw = _peer_route(x1, norm2_g[l], scale2, shift2, peer_w_q[l].astype(BF16), keys)
    return _peer_experts(eid, gw, h2, x1, gate2, _pack_table(peer_u[l]), _pack_table(peer_v[l]))
```

```python
import math
from functools import partial

import jax
import jax.numpy as jnp
import numpy as np
from jax import lax
from jax.experimental import pallas as pl
from jax.experimental.pallas import tpu as pltpu

D_MODEL = 1024
POOL_WIDTH = 512
POOL_WINDOWS = (2, 4, 8, 16)
POOL_GROUP_DIM = 128
N_HEADS = 8
N_KV_GROUPS = 2
HEADS_PER_GROUP = 4
HEAD_DIM = 64
Q_WIDTH = 512
KV_WIDTH = 128
CMP_BLOCK = 32
CMP_STRIDE = 16
CMP_HIDDEN = 128
SEL_BLOCK = 64
SEL_TOPK = 16
WINDOW = 512
FORCE_SCORE = 1000.0
N_BUCKETS = 32
MAX_DISTANCE = 128
PEER_HEADS = 8
PEER_KEYS = 128
PEER_TOPK = 16
PEER_QDIM = 256
GATE_COLS = N_HEADS * 3
EPS = 1e-6
NEG_INF = -1e30

LANES = 128
VMEM_LIMIT = 48 * 1024 * 1024
ROW_TILE = 512
POOL_HALO = 16
TQ = 256
SEL_FEATS = 64
SEL_CHUNK = 512
V_WIDTH = 128
KV_PAD = 512
CMP_BAND_LO = 10
PEER_TILE = 256
PEER_EXPERT_TILE = 256
PEER_TOKEN_UNROLL = 16
PEER_V_UNROLL = 32
EXPERT_ROWS = D_MODEL // 2 // LANES
TABLE_PAD = 8
PEER_VMEM_LIMIT = 56 * 1024 * 1024

BF16 = jnp.bfloat16
F32 = jnp.float32
_NT = (((1,), (1,)), ((), ()))


def _rms_rows(x, g):
    return x * lax.rsqrt(jnp.mean(x * x, axis=-1, keepdims=True) + EPS) * g


def _in_proj_kernel(x_ref, g_ref, sc_ref, sh_ref, w_ref, kg_ref,
                    zpool_ref, zq_ref, ksel_ref, vsel_ref, kwin_ref, vwin_ref, kcr_ref, vcr_ref,
                    zmerge_ref, zgate_ref):
    j = pl.program_id(1)

    @pl.when(j == 0)
    def _():
        for ref in (ksel_ref, vsel_ref, kwin_ref, vwin_ref):
            ref[...] = jnp.zeros(ref.shape, ref.dtype)

    @pl.when(j > 0)
    def _():
        _in_proj_tile(j - 1, x_ref, g_ref, sc_ref, sh_ref, w_ref, kg_ref, zpool_ref, zq_ref, ksel_ref, vsel_ref,
                      kwin_ref, vwin_ref, kcr_ref, vcr_ref, zmerge_ref, zgate_ref)


def _in_proj_tile(i, x_ref, g_ref, sc_ref, sh_ref, w_ref, kg_ref,
                  zpool_ref, zq_ref, ksel_ref, vsel_ref, kwin_ref, vwin_ref, kcr_ref, vcr_ref,
                  zmerge_ref, zgate_ref):
    x = x_ref[0]
    y = x * lax.rsqrt(jnp.mean(x * x, axis=-1, keepdims=True) + EPS)
    h = y * g_ref[...] * (1.0 + sc_ref[0]) + sh_ref[0]
    z = jnp.dot(h.astype(BF16), w_ref[...], preferred_element_type=F32)
    ts = x.shape[0]
    zpool_ref[0] = z[:, :POOL_WIDTH]
    o = POOL_WIDTH
    zq_ref[0] = z[:, o:o + Q_WIDTH]
    o += Q_WIDTH
    pos = i * ts + lax.broadcasted_iota(jnp.int32, (ts, SEL_FEATS), 0)
    onehot = (pos // SEL_BLOCK == lax.broadcasted_iota(jnp.int32, (ts, SEL_FEATS), 1)).astype(BF16)
    one_col = (lax.broadcasted_iota(jnp.int32, (ts, V_WIDTH - HEAD_DIM), 1) == 0).astype(BF16)
    for g in range(N_KV_GROUPS):
        def col(k):
            return z[:, o + k * KV_WIDTH + g * HEAD_DIM:o + k * KV_WIDTH + (g + 1) * HEAD_DIM]
        kcr_ref[0, g] = col(0)
        vcr_ref[0, g] = col(1)
        ks = _rms_rows(col(2), kg_ref[1:2, :]).astype(BF16)
        ksel_ref[0, g] = jnp.concatenate([ks, onehot], axis=1)
        vsel_ref[0, g] = jnp.concatenate([col(3).astype(BF16), one_col], axis=1)
        kwin_ref[0, g] = _rms_rows(col(4), kg_ref[2:3, :]).astype(BF16)
        vwin_ref[0, g] = jnp.concatenate([col(5).astype(BF16), one_col], axis=1)
    o += 6 * KV_WIDTH
    zmerge_ref[0] = z[:, o:o + 2 * D_MODEL]
    o += 2 * D_MODEL
    zgate_ref[0] = jax.nn.sigmoid(z[:, o:o + N_KV_GROUPS * LANES])


def _in_proj(x, g, scale, shift, w_bf16, k_norm_g):
    b_, s_, d_ = x.shape
    n = w_bf16.shape[1]
    assert KV_PAD == ROW_TILE
    tile = lambda j: jnp.maximum(j - 1, 0)
    row = lambda w: pl.BlockSpec((1, ROW_TILE, w), lambda b, j: (b, tile(j), 0))
    grp = lambda w: pl.BlockSpec((1, N_KV_GROUPS, ROW_TILE, w), lambda b, j: (b, 0, tile(j), 0))
    padded = lambda w: pl.BlockSpec((1, N_KV_GROUPS, ROW_TILE, w), lambda b, j: (b, 0, j, 0))
    gshape = lambda w, dt: jax.ShapeDtypeStruct((b_, N_KV_GROUPS, s_, w), dt)
    pshape = lambda w, dt: jax.ShapeDtypeStruct((b_, N_KV_GROUPS, s_ + KV_PAD, w), dt)
    return pl.pallas_call(
        _in_proj_kernel,
        grid=(b_, s_ // ROW_TILE + 1),
        in_specs=[
            row(d_),
            pl.BlockSpec((1, d_), lambda b, i: (0, 0)),
            pl.BlockSpec((1, 1, d_), lambda b, i: (b, 0, 0)),
            pl.BlockSpec((1, 1, d_), lambda b, i: (b, 0, 0)),
            pl.BlockSpec((d_, n), lambda b, i: (0, 0)),
            pl.BlockSpec((3, HEAD_DIM), lambda b, i: (0, 0)),
        ],
        out_specs=[row(POOL_WIDTH), row(Q_WIDTH), padded(HEAD_DIM + SEL_FEATS), padded(V_WIDTH), padded(HEAD_DIM),
                   padded(V_WIDTH), grp(HEAD_DIM), grp(HEAD_DIM), row(2 * d_), row(N_KV_GROUPS * LANES)],
        out_shape=[
            jax.ShapeDtypeStruct((b_, s_, POOL_WIDTH), F32),
            jax.ShapeDtypeStruct((b_, s_, Q_WIDTH), F32),
            pshape(HEAD_DIM + SEL_FEATS, BF16), pshape(V_WIDTH, BF16), pshape(HEAD_DIM, BF16), pshape(V_WIDTH, BF16),
            gshape(HEAD_DIM, F32), gshape(HEAD_DIM, F32),
            jax.ShapeDtypeStruct((b_, s_, 2 * d_), F32),
            jax.ShapeDtypeStruct((b_, s_, N_KV_GROUPS * LANES), F32),
        ],
        compiler_params=pltpu.CompilerParams(
            dimension_semantics=("parallel", "arbitrary"), vmem_limit_bytes=VMEM_LIMIT),
    )(x, g.reshape(1, d_), scale, shift, w_bf16, k_norm_g)


def _compress_kernel(t_ref, pe_ref, w1_ref, w2_ref, g_ref, o_ref, *, normalize):
    half = CMP_STRIDE * HEAD_DIM
    n = t_ref.shape[2] // CMP_STRIDE
    w1 = w1_ref[...]
    a = jnp.zeros((n, CMP_HIDDEN), F32)
    b = jnp.zeros((n, CMP_HIDDEN), F32)
    for r in range(CMP_STRIDE):
        t_r = t_ref[0, 0, pl.ds(r, n, stride=CMP_STRIDE), :].astype(BF16)
        a = a + jnp.dot(t_r, w1[r * HEAD_DIM:(r + 1) * HEAD_DIM], preferred_element_type=F32)
        b = b + jnp.dot(t_r, w1[half + r * HEAD_DIM:half + (r + 1) * HEAD_DIM], preferred_element_type=F32)
    pe = jnp.dot(pe_ref[...], w1, preferred_element_type=F32)
    b_next = pltpu.roll(b, n - 1, axis=0)
    hid = jax.nn.gelu(a + b_next + pe)
    out = jnp.dot(hid.astype(BF16), w2_ref[...], preferred_element_type=F32)
    if normalize:
        out = _rms_rows(out, g_ref[...])
    o_ref[0, 0] = out.astype(BF16)


def _compress(t_raw, pe, w1, w2, g, normalize):
    b_, g_, s_, dh = t_raw.shape
    n_str = s_ // CMP_STRIDE
    return pl.pallas_call(
        partial(_compress_kernel, normalize=normalize),
        grid=(b_, g_),
        in_specs=[
            pl.BlockSpec((1, 1, s_, dh), lambda b, g: (b, g, 0, 0)),
            pl.BlockSpec((1, CMP_BLOCK * dh), lambda b, g: (0, 0)),
            pl.BlockSpec((CMP_BLOCK * dh, CMP_HIDDEN), lambda b, g: (0, 0)),
            pl.BlockSpec((CMP_HIDDEN, dh), lambda b, g: (0, 0)),
            pl.BlockSpec((1, dh), lambda b, g: (0, 0)),
        ],
        out_specs=pl.BlockSpec((1, 1, n_str, dh), lambda b, g: (b, g, 0, 0)),
        out_shape=jax.ShapeDtypeStruct((b_, g_, n_str, dh), BF16),
        compiler_params=pltpu.CompilerParams(
            dimension_semantics=("parallel", "parallel"), vmem_limit_bytes=VMEM_LIMIT),
    )(t_raw, pe.reshape(1, CMP_BLOCK * dh).astype(BF16), w1.astype(BF16), w2.astype(BF16), g.reshape(1, dh))


def _t5_bucket_np(rel):
    n = np.maximum(rel, 0)
    max_exact = N_BUCKETS // 2
    nf = np.maximum(n, 1).astype(np.float32)
    large = max_exact + (np.log(nf / max_exact) / math.log(MAX_DISTANCE / max_exact) * (N_BUCKETS - max_exact)).astype(np.int32)
    large = np.minimum(large, N_BUCKETS - 1)
    return np.where(n < max_exact, n, large)


def _bias_tiles(rel_bias, n_cmp_pad):
    far = rel_bias[N_BUCKETS - 1]
    ii = np.arange(TQ)[:, None]

    def lookup(dist):
        near = (dist >= 0) & (dist < MAX_DISTANCE)
        bucket = _t5_bucket_np(np.where(near, dist, MAX_DISTANCE))
        onehot = (jnp.asarray(bucket.astype(np.int8))[..., None] == jnp.arange(N_BUCKETS, dtype=jnp.int8)).astype(F32)
        b = jnp.dot(onehot, rel_bias - far, precision=lax.Precision.HIGHEST)
        b = jnp.transpose(b, (2, 0, 1))
        return b.reshape(N_KV_GROUPS, HEADS_PER_GROUP * dist.shape[0], dist.shape[1])

    jj = np.arange(TQ)[None, :]
    bt = jnp.stack([lookup(ii - jj), lookup(TQ + ii - jj)], axis=1)
    cc = np.arange(n_cmp_pad)[None, :]
    dist_c = ii - CMP_STRIDE * (cc - CMP_BAND_LO) - (CMP_BLOCK - 1)
    dist_c = np.where(cc <= CMP_BAND_LO + TQ // CMP_STRIDE, dist_c, -1)
    bc = lookup(dist_c)
    assert WINDOW >= 2 * TQ
    rows = HEADS_PER_GROUP * TQ
    i_r = np.tile(np.arange(TQ), HEADS_PER_GROUP)[:, None]
    first = jnp.asarray(np.where(jj > i_r, 0.0, NEG_INF), F32)
    last = jnp.where(jnp.asarray(jj <= i_r), bt[:, 0], NEG_INF)
    zeros = jnp.zeros((N_KV_GROUPS, rows, WINDOW - 2 * TQ), F32)
    wa = jnp.concatenate([jnp.broadcast_to(first, (N_KV_GROUPS, rows, TQ)), zeros, bt[:, 1], last], axis=-1)
    return wa, bc


def _sel_mapping_t(n_cmp_pad, n_cmp):
    m = np.zeros((SEL_FEATS, n_cmp_pad), np.float32)
    pos = np.arange(n_cmp)[:, None] * CMP_STRIDE + np.arange(CMP_BLOCK)[None, :]
    np.add.at(m, ((pos // SEL_BLOCK).ravel(), np.repeat(np.arange(n_cmp), CMP_BLOCK)), 1.0 / CMP_BLOCK)
    return m


def _logits(q, k, add=None, lo_cols=None):
    s = lax.dot_general(q, k, _NT, preferred_element_type=F32)
    if add is not None:
        s = s + add
    if lo_cols is not None:
        s = jnp.where(lax.broadcasted_iota(jnp.int32, s.shape, 1) < lo_cols, NEG_INF, s)
    return s


def _flash_step(s, v, m_sc, acc_sc):
    m_prev = m_sc[...]
    m_new = jnp.maximum(m_prev, jnp.max(s, axis=-1, keepdims=True))
    alpha = jnp.exp(m_prev - m_new)
    p = jnp.exp(s - jnp.concatenate([m_new] * (s.shape[1] // LANES), axis=1))
    acc_sc[...] = alpha * acc_sc[...] + jnp.dot(p.astype(BF16), v, preferred_element_type=F32)
    m_sc[...] = m_new


def _attn_kernel(zq_ref, gate_ref, qg_ref, kc_ref, vc_ref, ksel_ref, vsel_ref, kwin_ref, vwin_ref,
                 wa_ref, bc_ref, smap_ref, o_ref, m_sc, acc_sc):
    i = pl.program_id(2)
    rows = HEADS_PER_GROUP * TQ
    ncp = kc_ref.shape[2]
    scale = HEAD_DIM ** -0.5

    zq = zq_ref[0]
    qs = jnp.concatenate([zq[:, j * HEAD_DIM:(j + 1) * HEAD_DIM] for j in range(HEADS_PER_GROUP)], axis=0)
    qn = _rms_rows(qs, qg_ref[...]) * scale
    qb = qn.astype(BF16)

    lc = lax.dot_general(qb, kc_ref[0, 0], _NT, preferred_element_type=F32)
    lc = lc + pltpu.roll(bc_ref[0], (i * (TQ // CMP_STRIDE) + ncp - CMP_BAND_LO) % ncp, axis=1)
    t_c = i * TQ + lax.broadcasted_iota(jnp.int32, (rows, ncp), 0) % TQ
    n_c = lax.broadcasted_iota(jnp.int32, (rows, ncp), 1)
    valid_c = n_c * CMP_STRIDE + (CMP_BLOCK - 1) <= t_c
    lm = jnp.where(valid_c, lc, NEG_INF)
    e = jnp.where(valid_c, jnp.exp(lm - jnp.max(lm, axis=-1, keepdims=True)), 0.0)
    den = jnp.sum(e, axis=-1, keepdims=True)
    p_c = e / jnp.where(den > 0.0, den, 1.0)
    o_cmp = jnp.dot(p_c.astype(BF16), vc_ref[0, 0], preferred_element_type=F32)

    p_sum = p_c[0:TQ] + p_c[TQ:2 * TQ] + p_c[2 * TQ:3 * TQ] + p_c[3 * TQ:4 * TQ]
    p_hi = p_sum.astype(BF16)
    r1 = p_sum - p_hi.astype(F32)
    p_mid = r1.astype(BF16)
    p_lo = (r1 - p_mid.astype(F32)).astype(BF16)
    smap = smap_ref[...]
    imp = (lax.dot_general(smap, p_hi, _NT, preferred_element_type=F32)
           + lax.dot_general(smap, p_mid, _NT, preferred_element_type=F32)
           + lax.dot_general(smap, p_lo, _NT, preferred_element_type=F32))
    blk = lax.broadcasted_iota(jnp.int32, (SEL_FEATS, TQ), 0)
    t_s = i * TQ + lax.broadcasted_iota(jnp.int32, (SEL_FEATS, TQ), 1)
    cur = t_s // SEL_BLOCK
    forced = (blk == 0) | (blk == cur) | (blk == cur - 1)
    visible = blk * SEL_BLOCK <= t_s
    score = jnp.where(visible, imp + jnp.where(forced, FORCE_SCORE, 0.0), -1.0)
    rank = jnp.zeros((SEL_FEATS, TQ), jnp.int32)
    for sp in range(SEL_FEATS):
        row = score[sp:sp + 1, :]
        beats = (row > score) | ((row == score) & (blk > sp))
        rank = rank + beats.astype(jnp.int32)
    pen_t = jnp.where(rank < SEL_TOPK, 0.0, NEG_INF)
    pen = jnp.transpose(pen_t).astype(BF16)
    q_aug = jnp.concatenate([qb, jnp.concatenate([pen] * HEADS_PER_GROUP, axis=0)], axis=1)

    def keys(ref, pos, n):
        return ref[0, 0, pl.ds(pl.multiple_of(pos + KV_PAD, TQ), n), :]

    m_sc[...] = jnp.full((rows, LANES), NEG_INF, F32)
    acc_sc[...] = jnp.zeros((rows, V_WIDTH), F32)
    per_chunk = SEL_CHUNK // TQ
    n_plain = jnp.maximum(i - 1, 0)
    rem = n_plain % per_chunk

    @pl.when(rem > 0)
    def _():
        pos = (rem - per_chunk) * TQ
        s = _logits(q_aug, keys(ksel_ref, pos, SEL_CHUNK), lo_cols=-pos)
        _flash_step(s, keys(vsel_ref, pos, SEL_CHUNK), m_sc, acc_sc)

    def sel_body(c, carry):
        pos = (rem + c * per_chunk) * TQ
        _flash_step(_logits(q_aug, keys(ksel_ref, pos, SEL_CHUNK)), keys(vsel_ref, pos, SEL_CHUNK), m_sc, acc_sc)
        return carry

    lax.fori_loop(0, n_plain // per_chunk, sel_body, 0)
    pos = (i - 1) * TQ
    s = _logits(q_aug, keys(ksel_ref, pos, 2 * TQ), add=wa_ref[0, :, WINDOW - TQ:], lo_cols=-pos)
    _flash_step(s, keys(vsel_ref, pos, 2 * TQ), m_sc, acc_sc)
    acc = acc_sc[...]
    o_sel = acc[:, :HEAD_DIM] / acc[:, HEAD_DIM:HEAD_DIM + 1]

    pos = i * TQ - WINDOW
    s = _logits(qb, keys(kwin_ref, pos, WINDOW + TQ), add=wa_ref[0], lo_cols=-pos)
    p = jnp.exp(s - jnp.max(s, axis=-1, keepdims=True))
    acc = jnp.dot(p.astype(BF16), keys(vwin_ref, pos, WINDOW + TQ), preferred_element_type=F32)
    o_win = acc[:, :HEAD_DIM] / acc[:, HEAD_DIM:HEAD_DIM + 1]

    gate = gate_ref[0]
    outs = []
    for j in range(HEADS_PER_GROUP):
        sl = slice(j * TQ, (j + 1) * TQ)
        outs.append(gate[:, 3 * j:3 * j + 1] * o_cmp[sl] + gate[:, 3 * j + 1:3 * j + 2] * o_sel[sl]
                    + gate[:, 3 * j + 2:3 * j + 3] * o_win[sl])
    o_ref[0] = jnp.concatenate(outs, axis=1)


def _attention(z_q, z_gate, q_norm_g, kc, vc, ksel, vsel, kwin, vwin, rel_bias):
    b_, s_, _ = z_q.shape
    ncp = kc.shape[2]
    n_cmp = ncp - CMP_BLOCK // CMP_STRIDE + 1
    assert s_ // SEL_BLOCK <= SEL_FEATS and s_ % TQ == 0
    wa, bc = _bias_tiles(rel_bias, ncp)
    smap_t = jnp.asarray(_sel_mapping_t(ncp, n_cmp), BF16)
    rows = HEADS_PER_GROUP * TQ
    gw = HEADS_PER_GROUP * HEAD_DIM
    kv = lambda w: pl.BlockSpec((1, 1, s_ + KV_PAD, w), lambda b, g, i: (b, g, 0, 0))
    cm = pl.BlockSpec((1, 1, ncp, HEAD_DIM), lambda b, g, i: (b, g, 0, 0))
    return pl.pallas_call(
        _attn_kernel,
        grid=(b_, N_KV_GROUPS, s_ // TQ),
        in_specs=[
            pl.BlockSpec((1, TQ, gw), lambda b, g, i: (b, i, g)),
            pl.BlockSpec((1, TQ, LANES), lambda b, g, i: (b, i, g)),
            pl.BlockSpec((1, HEAD_DIM), lambda b, g, i: (0, 0)),
            cm, cm, kv(HEAD_DIM + SEL_FEATS), kv(V_WIDTH), kv(HEAD_DIM), kv(V_WIDTH),
            pl.BlockSpec((1, rows, WINDOW + TQ), lambda b, g, i: (g, 0, 0)),
            pl.BlockSpec((1, rows, ncp), lambda b, g, i: (g, 0, 0)),
            pl.BlockSpec((SEL_FEATS, ncp), lambda b, g, i: (0, 0)),
        ],
        out_specs=pl.BlockSpec((1, TQ, gw), lambda b, g, i: (b, i, g)),
        out_shape=jax.ShapeDtypeStruct((b_, s_, N_KV_GROUPS * gw), F32),
        scratch_shapes=[pltpu.VMEM((rows, LANES), F32), pltpu.VMEM((rows, V_WIDTH), F32)],
        compiler_params=pltpu.CompilerParams(
            dimension_semantics=("parallel", "parallel", "arbitrary"), vmem_limit_bytes=VMEM_LIMIT),
    )(z_q, z_gate, q_norm_g.reshape(1, HEAD_DIM), kc, vc, ksel, vsel, kwin, vwin, wa, bc, smap_t)


def _pool_tile(halo, tile, i, pw_ref, ps_ref):
    ts = tile.shape[0]
    ext = jnp.concatenate([jnp.where(i > 0, halo, 0.0), tile], axis=0)
    t = i * ts + lax.broadcasted_iota(jnp.int32, (ts, 1), 0)
    outs = []
    for gi, w in enumerate(POOL_WINDOWS):
        sl = slice(gi * POOL_GROUP_DIM, (gi + 1) * POOL_GROUP_DIM)
        run = ext[:, sl]
        span = 1
        while span < w:
            run = run + pltpu.roll(run, span, axis=0)
            span *= 2
        cnt = jnp.minimum(t + 1, w).astype(F32)
        pooled = run[POOL_HALO:] / cnt - tile[:, sl]
        outs.append(jnp.dot(pooled.astype(BF16), pw_ref[gi], preferred_element_type=F32))
    return jnp.concatenate(outs, axis=1) * ps_ref[...]


def _merge_kernel(x_ref, zp_ref, zh_ref, ya_ref, zm_ref, g1_ref, pw_ref, ps_ref, wbp_ref, wba_ref, wo_ref, o_ref):
    d_ = x_ref.shape[-1]
    y_pool = _pool_tile(zh_ref[0], zp_ref[0], pl.program_id(1), pw_ref, ps_ref)
    bp = jnp.dot(y_pool.astype(BF16), wbp_ref[...], preferred_element_type=F32)
    ba = jnp.dot(ya_ref[0].astype(BF16), wba_ref[...], preferred_element_type=F32)
    zm = zm_ref[0]
    mixed = jax.nn.sigmoid(zm[:, :d_]) * bp + jax.nn.sigmoid(zm[:, d_:]) * ba
    proj = jnp.dot(mixed.astype(BF16), wo_ref[...], preferred_element_type=F32)
    o_ref[0] = x_ref[0] + g1_ref[0] * proj


def _merge(x, z_pool, y_attn, z_merge, gate1, pool_w, pool_scale, wbp, wba, wo):
    b_, s_, d_ = x.shape
    row = lambda w: pl.BlockSpec((1, ROW_TILE, w), lambda b, i: (b, i, 0))
    full = lambda a: pl.BlockSpec(a.shape, lambda b, i: (0,) * a.ndim)
    halo_blocks = ROW_TILE // POOL_HALO
    halo = pl.BlockSpec((1, POOL_HALO, POOL_WIDTH), lambda b, i: (b, jnp.maximum(i * halo_blocks - 1, 0), 0))
    pool_scale = pool_scale.reshape(1, POOL_WIDTH)
    return pl.pallas_call(
        _merge_kernel,
        grid=(b_, s_ // ROW_TILE),
        in_specs=[row(d_), row(POOL_WIDTH), halo, row(Q_WIDTH), row(2 * d_),
                  pl.BlockSpec((1, 1, d_), lambda b, i: (b, 0, 0)),
                  full(pool_w), full(pool_scale), full(wbp), full(wba), full(wo)],
        out_specs=row(d_),
        out_shape=jax.ShapeDtypeStruct((b_, s_, d_), F32),
        compiler_params=pltpu.CompilerParams(
            dimension_semantics=("parallel", "parallel"), vmem_limit_bytes=VMEM_LIMIT),
    )(x, z_pool, z_pool, y_attn, z_merge, gate1, pool_w, pool_scale, wbp, wba, wo)


def _ada_kernel(c_ref, w_ref, b_ref, o_ref):
    c = c_ref[...]
    o_ref[...] = jnp.dot(jax.nn.silu(c).astype(BF16), w_ref[...].astype(BF16),
                         preferred_element_type=F32) + b_ref[...]


def _ada(c, w, b):
    b_, d_ = c.shape
    n = w.shape[1]
    return pl.pallas_call(
        _ada_kernel,
        grid=(n // d_,),
        in_specs=[pl.BlockSpec((b_, d_), lambda j: (0, 0)),
                  pl.BlockSpec((d_, d_), lambda j: (0, j)),
                  pl.BlockSpec((1, d_), lambda j: (0, j))],
        out_specs=pl.BlockSpec((b_, d_), lambda j: (0, j)),
        out_shape=jax.ShapeDtypeStruct((b_, n), F32),
        compiler_params=pltpu.CompilerParams(dimension_semantics=("parallel",), vmem_limit_bytes=VMEM_LIMIT),
    )(c, w, b.reshape(1, n))


_CAND_ROWS = PEER_TOPK + 7 * 8 + 8


def _topk_rows(x, k):
    n = x.shape[0]
    rid = lax.broadcasted_iota(jnp.int32, x.shape, 0)
    vals, idxs = [], []
    for _ in range(k):
        m = jnp.max(x, axis=0, keepdims=True)
        idx = jnp.min(jnp.where(x == m, rid, n), axis=0, keepdims=True)
        vals.append(m)
        idxs.append(idx)
        x = jnp.where(rid == idx, -jnp.inf, x)
    return jnp.concatenate(vals, axis=0), jnp.concatenate(idxs, axis=0)


def _pair_grid(r0, r1, combine):
    parts = [combine(r0[0:1], r1)]
    parts += [combine(r0[a:a + 1], r1[0:8]) for a in range(1, 8)]
    parts.append(combine(r0[8:16], r1[0:1]))
    return jnp.concatenate(parts, axis=0)


def _peer_route_kernel(x_ref, g_ref, sc_ref, sh_ref, w_ref, keys_ref, h_ref, eid_ref, gw_ref):
    x = x_ref[0]
    y = x * lax.rsqrt(jnp.mean(x * x, axis=-1, keepdims=True) + EPS)
    h = y * g_ref[...] * (1.0 + sc_ref[0]) + sh_ref[0]
    h_ref[0] = h
    qv = jnp.dot(h.astype(BF16), w_ref[...], preferred_element_type=F32).astype(BF16)
    half = PEER_QDIM // 2
    eids, gws = [], []
    for hd in range(PEER_HEADS):
        tops = []
        for p in range(2):
            c = hd * 2 + p
            sc = lax.dot_general(keys_ref[c], qv[:, c * half:(c + 1) * half], _NT,
                                 preferred_element_type=F32)
            tops.append(_topk_rows(sc, PEER_TOPK))
        (s0, i0), (s1, i1) = tops
        cand = _pair_grid(s0, s1, lambda a, b: a + b)
        cid = _pair_grid(i0, i1, lambda a, b: a * (PEER_KEYS * EXPERT_ROWS) + (b * EXPERT_ROWS + TABLE_PAD))
        rid = lax.broadcasted_iota(jnp.int32, cand.shape, 0)
        top_s, top_e = [], []
        for _ in range(PEER_TOPK):
            m = jnp.max(cand, axis=0, keepdims=True)
            idx = jnp.min(jnp.where(cand == m, rid, _CAND_ROWS), axis=0, keepdims=True)
            hit = rid == idx
            top_s.append(m)
            top_e.append(jnp.sum(jnp.where(hit, cid, 0), axis=0, keepdims=True))
            cand = jnp.where(hit, -jnp.inf, cand)
        top_s = jnp.concatenate(top_s, axis=0)
        e = jnp.exp(top_s - top_s[0:1])
        gws.append(e / jnp.sum(e, axis=0, keepdims=True))
        eids.append(jnp.concatenate(top_e, axis=0))
    eids = jnp.concatenate(eids, axis=0)
    for c in range(PEER_TILE // PEER_EXPERT_TILE):
        eid_ref[c] = eids[:, c * PEER_EXPERT_TILE:(c + 1) * PEER_EXPERT_TILE]
    gw_ref[...] = jnp.transpose(jnp.concatenate(gws, axis=0))


def _peer_route(x, g, scale, shift, w_bf16, keys_bf16):
    b_, s_, d_ = x.shape
    n = w_bf16.shape[1]
    nk = PEER_HEADS * PEER_TOPK
    per_tile = PEER_TILE // PEER_EXPERT_TILE
    tiles = s_ // PEER_TILE
    row = lambda w: pl.BlockSpec((1, PEER_TILE, w), lambda b, i: (b, i, 0))
    return pl.pallas_call(
        _peer_route_kernel,
        grid=(b_, s_ // PEER_TILE),
        in_specs=[
            row(d_),
            pl.BlockSpec((1, d_), lambda b, i: (0, 0)),
            pl.BlockSpec((1, 1, d_), lambda b, i: (b, 0, 0)),
            pl.BlockSpec((1, 1, d_), lambda b, i: (b, 0, 0)),
            pl.BlockSpec((d_, n), lambda b, i: (0, 0)),
            pl.BlockSpec(keys_bf16.shape, lambda b, i: (0, 0, 0)),
        ],
        out_specs=[row(d_),
                   pl.BlockSpec((per_tile, nk, PEER_EXPERT_TILE), lambda b, i: (b * tiles + i, 0, 0)),
                   pl.BlockSpec((PEER_TILE, nk), lambda b, i: (b * tiles + i, 0))],
        out_shape=[
            jax.ShapeDtypeStruct((b_, s_, d_), F32),
            jax.ShapeDtypeStruct((b_ * s_ // PEER_EXPERT_TILE, nk, PEER_EXPERT_TILE), jnp.int32),
            jax.ShapeDtypeStruct((b_ * s_, nk), F32),
        ],
        compiler_params=pltpu.CompilerParams(
            dimension_semantics=("parallel", "parallel"), vmem_limit_bytes=VMEM_LIMIT),
    )(x, g.reshape(1, d_), scale, shift, w_bf16, keys_bf16)


_HI_MASK = -65536


def _pack_table(t):
    e_, d_ = t.shape
    bits = lax.bitcast_convert_type(t.astype(jnp.bfloat16), jnp.uint16).astype(jnp.uint32)
    words = (bits[:, d_ // 2:] << 16) | bits[:, :d_ // 2]
    return lax.bitcast_convert_type(words, jnp.int32).reshape(e_ * EXPERT_ROWS, LANES)


def _expert_words(tab_vmem, row):
    return tab_vmem[pl.ds(pl.multiple_of(row, EXPERT_ROWS), EXPERT_ROWS), :]


def _expert_pair_words(tab_vmem, row_a, row_b, sub):
    top = tab_vmem[pl.ds(pl.multiple_of(row_a, EXPERT_ROWS), 2 * EXPERT_ROWS), :]
    bot = tab_vmem[pl.ds(pl.multiple_of(row_b - EXPERT_ROWS, EXPERT_ROWS), 2 * EXPERT_ROWS), :]
    return jnp.where(sub < EXPERT_ROWS, top, bot)


def _unpack(words):
    lo = lax.bitcast_convert_type(lax.shift_left(words, 16), F32)
    hi = lax.bitcast_convert_type(words & _HI_MASK, F32)
    return lo, hi


def _load_resident(j, tab_hbm, tab_vmem, eid_ref, idx_smem, sem):
    @pl.when(j == 0)
    def _():
        n_rows = tab_hbm.shape[0]
        zeros = jnp.zeros((TABLE_PAD, LANES), jnp.int32)
        tab_vmem[pl.ds(0, TABLE_PAD), :] = zeros
        tab_vmem[pl.ds(TABLE_PAD + n_rows, TABLE_PAD), :] = zeros
        cp = pltpu.make_async_copy(tab_hbm, tab_vmem.at[pl.ds(TABLE_PAD, n_rows)], sem)
        cp.start()
        cp.wait()

    copies = [pltpu.make_async_copy(eid_ref.at[0, k], idx_smem[k], sem) for k in range(len(idx_smem))]
    for cp in copies:
        cp.start()
    for cp in copies:
        cp.wait()


def _fold_pairs(q, sub):
    m2, m1 = (sub & 2) == 0, (sub & 1) == 0
    u = [jnp.where(m2, q[i], q[i + 2]) + jnp.where(m2, pltpu.roll(q[i], 6, axis=0), pltpu.roll(q[i + 2], 2, axis=0))
         for i in range(2)]
    return jnp.where(m1, u[0], u[1]) + jnp.where(m1, pltpu.roll(u[0], 7, axis=0), pltpu.roll(u[1], 1, axis=0))


def _peer_u_kernel(eid_ref, gw_ref, h_ref, tab_hbm, w_ref, *scratch):
    j = pl.program_id(0)
    nk = PEER_HEADS * PEER_TOPK
    idx_smem, (tab_vmem, acc_ref, sem) = scratch[:nk], scratch[nk:]
    half = D_MODEL // 2 // LANES
    _load_resident(j, tab_hbm, tab_vmem, eid_ref, idx_smem, sem)
    sub = lax.broadcasted_iota(jnp.int32, (2 * half, LANES), 0)
    lane = lax.broadcasted_iota(jnp.int32, (nk, PEER_EXPERT_TILE), 1)
    acc_ref[...] = jnp.zeros((nk, PEER_EXPERT_TILE), F32)

    def token(t):
        row = h_ref[pl.ds(t, 1), :]

        def seg(s):
            return jnp.broadcast_to(row[:, s * LANES:(s + 1) * LANES], (2 * half, LANES))

        def rows_of(first):
            out = seg(first + half - 1)
            for s in range(half - 2, -1, -1):
                out = jnp.where(sub % half == s, seg(first + s), out)
            return out

        x_lo, x_hi = rows_of(0), rows_of(half)
        folded = []
        for gi in range(nk // 8):
            q = []
            for i in range(4):
                k = gi * 8 + i
                words = _expert_pair_words(tab_vmem, idx_smem[k][t], idx_smem[k + 4][t], sub)
                lo, hi = _unpack(words)
                q.append(lo * x_lo + hi * x_hi)
            folded.append(_fold_pairs(q, sub))
        return jnp.sum(jnp.concatenate(folded, axis=0), axis=-1, keepdims=True)

    def body(tb, carry):
        acc = acc_ref[...]
        for u in range(PEER_TOKEN_UNROLL):
            t = tb * PEER_TOKEN_UNROLL + u
            acc = jnp.where(lane == t, token(t), acc)
        acc_ref[...] = acc
        return carry

    lax.fori_loop(0, PEER_EXPERT_TILE // PEER_TOKEN_UNROLL, body, 0)
    w_ref[...] = jax.nn.gelu(jnp.transpose(acc_ref[...])) * gw_ref[...]


def _peer_v_kernel(eid_ref, w_ref, x_ref, g2_ref, tab_hbm, o_ref, *scratch):
    j = pl.program_id(0)
    nk = PEER_HEADS * PEER_TOPK
    idx_smem, (wb_ref, tab_vmem, sem) = scratch[:nk], scratch[nk:]
    half = D_MODEL // 2 // LANES
    _load_resident(j, tab_hbm, tab_vmem, eid_ref, idx_smem, sem)
    n_acc = 4
    sub = lax.broadcasted_iota(jnp.int32, (2 * half, LANES), 0)

    def token(t, slot):
        wb_ref[slot] = jnp.transpose(jnp.broadcast_to(w_ref[pl.ds(t, 1), :], (nk, nk)))
        lo_acc = [jnp.zeros((2 * half, LANES), F32) for _ in range(n_acc)]
        hi_acc = [jnp.zeros((2 * half, LANES), F32) for _ in range(n_acc)]
        for p in range(nk // 2):
            k = 2 * p
            lo, hi = _unpack(_expert_pair_words(tab_vmem, idx_smem[k][t], idx_smem[k + 1][t], sub))
            wk = jnp.where(sub < half, wb_ref[slot, k:k + 1, :], wb_ref[slot, k + 1:k + 2, :])
            lo_acc[p % n_acc] = lo_acc[p % n_acc] + wk * lo
            hi_acc[p % n_acc] = hi_acc[p % n_acc] + wk * hi
        lo_sum, hi_sum = sum(lo_acc[1:], lo_acc[0]), sum(hi_acc[1:], hi_acc[0])
        ff = jnp.concatenate([lo_sum[:half] + lo_sum[half:], hi_sum[:half] + hi_sum[half:]], axis=0)
        ff_row = jnp.concatenate([ff[s:s + 1, :] for s in range(2 * half)], axis=1)
        o_ref[pl.ds(t, 1), :] = x_ref[pl.ds(t, 1), :] + g2_ref[0] * ff_row

    def body(tb, carry):
        for u in range(PEER_V_UNROLL):
            token(tb * PEER_V_UNROLL + u, u)
        return carry

    lax.fori_loop(0, PEER_EXPERT_TILE // PEER_V_UNROLL, body, 0)


def _peer_experts(eid, gw, h2, x1, gate2, u_tab, v_tab):
    b_, s_, d_ = x1.shape
    n_tok = b_ * s_
    nk = PEER_HEADS * PEER_TOPK
    tt = PEER_EXPERT_TILE
    nsteps = n_tok // tt
    steps_per_batch = s_ // tt
    tok = pl.BlockSpec((tt, d_), lambda j: (j, 0))
    sel = pl.BlockSpec((tt, nk), lambda j: (j, 0))
    slots = pl.BlockSpec((1, nk, tt), lambda j: (j, 0, 0))
    idx_scratch = [pltpu.SMEM((tt,), jnp.int32) for _ in range(nk)]
    params = pltpu.CompilerParams(dimension_semantics=("arbitrary",), vmem_limit_bytes=PEER_VMEM_LIMIT)
    eid3 = eid.reshape(nsteps, nk, tt)
    resident = (u_tab.shape[0] + 2 * TABLE_PAD, LANES)
    w = pl.pallas_call(
        _peer_u_kernel,
        grid=(nsteps,),
        in_specs=[slots, sel, tok, pl.BlockSpec(memory_space=pl.ANY)],
        out_specs=sel,
        out_shape=jax.ShapeDtypeStruct((n_tok, nk), F32),
        scratch_shapes=idx_scratch + [pltpu.VMEM(resident, jnp.int32), pltpu.VMEM((nk, tt), F32),
                                      pltpu.SemaphoreType.DMA],
        compiler_params=params,
    )(eid3, gw.reshape(n_tok, nk), h2.reshape(n_tok, d_), u_tab)
    out = pl.pallas_call(
        _peer_v_kernel,
        grid=(nsteps,),
        in_specs=[slots, sel, tok, pl.BlockSpec((1, 1, d_), lambda j: (j // steps_per_batch, 0, 0)),
                  pl.BlockSpec(memory_space=pl.ANY)],
        out_specs=tok,
        out_shape=jax.ShapeDtypeStruct((n_tok, d_), F32),
        scratch_shapes=idx_scratch + [pltpu.VMEM((PEER_V_UNROLL, nk, LANES), F32),
                                      pltpu.VMEM(resident, jnp.int32), pltpu.SemaphoreType.DMA],
        compiler_params=params,
    )(eid3, w, x1.reshape(n_tok, d_), gate2, v_tab)
    return out.reshape(b_, s_, d_)


def _permute_w_in(w):
    d_ = w.shape[0]
    kv_end = POOL_WIDTH + Q_WIDTH + 6 * KV_WIDTH
    per_group = HEADS_PER_GROUP * 3
    parts = [w[:, :kv_end], w[:, kv_end + GATE_COLS:]]
    for g in range(N_KV_GROUPS):
        parts.append(w[:, kv_end + g * per_group:kv_end + (g + 1) * per_group])
        parts.append(jnp.zeros((d_, LANES - per_group), w.dtype))
    return jnp.concatenate(parts, axis=1).astype(BF16)


def kernel(x, c, rel_bias, ada_w, ada_b, norm1_g, norm2_g, w_in, pool_w, pool_scale, cmp_pe_k, cmp_w1_k, cmp_w2_k, cmp_pe_v, cmp_w1_v, cmp_w2_v, q_norm_g, k_norm_g, w_branch_pool, w_branch_attn, w_out, peer_w_q, peer_sub_keys, peer_u, peer_v):
    b_, s_, d_ = x.shape
    l = 0
    ada = _ada(c, ada_w[l], ada_b[l]).reshape(b_, 6, 1, d_)
    shift1, scale1, gate1 = ada[:, 0], ada[:, 1], ada[:, 2]
    shift2, scale2, gate2 = ada[:, 3], ada[:, 4], ada[:, 5]

    (z_pool, z_q, ksel, vsel, kwin, vwin, kc_raw, vc_raw, z_merge, z_gate) = _in_proj(
        x, norm1_g[l], scale1, shift1, _permute_w_in(w_in[l]), k_norm_g[l])

    kc = _compress(kc_raw, cmp_pe_k[l], cmp_w1_k[l], cmp_w2_k[l], k_norm_g[l, 0], True)
    vc = _compress(vc_raw, cmp_pe_v[l], cmp_w1_v[l], cmp_w2_v[l], k_norm_g[l, 0], False)
    y_attn = _attention(z_q, z_gate, q_norm_g[l], kc, vc, ksel, vsel, kwin, vwin, rel_bias)

    x1 = _merge(x, z_pool, y_attn, z_merge, gate1, pool_w[l].astype(BF16), pool_scale[l],
                w_branch_pool[l].astype(BF16), w_branch_attn[l].astype(BF16), w_out[l].astype(BF16))

    keys = peer_sub_keys[l].reshape(PEER_HEADS * 2, PEER_KEYS, PEER_QDIM // 2).astype(BF16)
    h2, eid, gw = _peer_route(x1, norm2_g[l], scale2, shift2, peer_w_q[l].astype(BF16), keys)
    return _peer_experts(eid, gw, h2, x1, gate2, _pack_table(peer_u[l]), _pack_table(peer_v[l]))
```
